```python
import jax
import jax.numpy as jnp
from jax import lax
import numpy as np

D_MODEL = 2048
BATCH = 16
SEQ = 2048
DEPTH = 1

HGRN_HEAD_DIM = 128
HGRN_HEADS = D_MODEL // HGRN_HEAD_DIM
HGRN_WIDTH = HGRN_HEADS * HGRN_HEAD_DIM
HGRN_CHUNK = 32
ATTN_GROUPS = ((128, 1), (512, 4), (2048, 16))
ATTN_HEADS_PER_GROUP = 4
HEAD_DIM = 128
ATTN_QKV_WIDTH = len(ATTN_GROUPS) * 3 * ATTN_HEADS_PER_GROUP * HEAD_DIM
ATTN_OUT_WIDTH = ATTN_HEADS_PER_GROUP * HEAD_DIM
ROPE_THETA = 500000.0
ROPE_DIM = HEAD_DIM // 4
N_BRANCHES = 2
IN_COLS = 5 * HGRN_WIDTH + ATTN_QKV_WIDTH + N_BRANCHES * D_MODEL
D_FF = ((8 * D_MODEL // 3 + 255) // 256) * 256
DEEPNORM_ALPHA = (2.0 * DEPTH) ** 0.25
DEEPNORM_BETA = (8.0 * DEPTH) ** -0.25
LN_EPS = 1e-5
NEG_INF = -1e30

kernel_name = 'hybrid_hgrn2_dilated_attn_macaron_deepnorm'


def layer_norm(x, g, b):
    xf = x.astype(jnp.float32)
    mu = jnp.mean(xf, axis=-1, keepdims=True)
    var = jnp.mean(jnp.square(xf - mu), axis=-1, keepdims=True)
    return ((xf - mu) * lax.rsqrt(var + LN_EPS) * g + b).astype(x.dtype)


def swiglu(x, w_in, w_out):
    gate, up = jnp.split(x @ w_in, 2, axis=-1)
    return (jax.nn.silu(gate) * up) @ w_out


def partial_rope(t, pos):
    t = t.astype(jnp.float32)
    inv_freq = ROPE_THETA ** (-jnp.arange(0, ROPE_DIM, 2, dtype=jnp.float32) / ROPE_DIM)
    ang = pos.astype(jnp.float32)[:, None] * inv_freq
    cos = jnp.cos(ang)[None, :, None, None, :]
    sin = jnp.sin(ang)[None, :, None, None, :]
    t1, t2, rest = jnp.split(t, [ROPE_DIM // 2, ROPE_DIM], axis=-1)
    return jnp.concatenate([t1 * cos - t2 * sin, t2 * cos + t1 * sin, rest], axis=-1)


def dilated_window_attention(q, k, v, window, dilation):
    b_, s_, h_, dh = q.shape
    half = window // (2 * dilation)
    seg = s_ // dilation
    blk = half
    n_blk = -(-seg // blk)
    seg_p = n_blk * blk

    def to_residue(t):
        t = t.reshape(b_, seg, dilation, h_, dh).transpose(0, 2, 3, 1, 4)
        return jnp.pad(t, ((0, 0), (0, 0), (0, 0), (0, seg_p - seg), (0, 0)))

    def neighbours(t):
        t = jnp.pad(t, ((0, 0), (0, 0), (0, 0), (blk, blk), (0, 0)))
        t = t.reshape(b_, dilation, h_, n_blk + 2, blk, dh)
        return jnp.concatenate([t[:, :, :, :-2], t[:, :, :, 1:-1], t[:, :, :, 2:]], axis=4)

    qr = to_residue(q).reshape(b_, dilation, h_, n_blk, blk, dh)
    kr = neighbours(to_residue(k))
    vr = neighbours(to_residue(v))
    qi = jnp.arange(seg_p).reshape(n_blk, blk, 1)
    kj = (jnp.arange(n_blk)[:, None, None] - 1) * blk + jnp.arange(3 * blk)[None, None, :]
    valid = (jnp.abs(qi - kj) <= half) & (kj >= 0) & (kj < seg)
    s = jnp.einsum('brhnqe,brhnke->brhnqk', qr, kr).astype(jnp.float32) * (HEAD_DIM ** -0.5)
    s = jnp.where(valid, s, NEG_INF)
    m = jnp.max(s, axis=-1, keepdims=True)
    p = jnp.exp(s - m)
    denom = jnp.sum(p, axis=-1, keepdims=True)
    o = jnp.einsum('brhnqk,brhnke->brhnqe', p, vr.astype(jnp.float32)) / denom
    lse = (m + jnp.log(denom))[..., 0]
    o = o.reshape(b_, dilation, h_, seg_p, dh)[:, :, :, :seg]
    o = o.transpose(0, 3, 1, 2, 4).reshape(b_, s_, h_, dh)
    lse = lse.reshape(b_, dilation, h_, seg_p)[:, :, :, :seg]
    lse = lse.transpose(0, 3, 1, 2).reshape(b_, s_, h_)
    return o, lse


def dilated_attention_mixer(h_qkv):
    b_, s_, _ = h_qkv.shape
    qkv = h_qkv.reshape(b_, s_, len(ATTN_GROUPS), 3, ATTN_HEADS_PER_GROUP, HEAD_DIM)
    pos = jnp.arange(s_)
    q = partial_rope(qkv[:, :, :, 0], pos)
    k = partial_rope(qkv[:, :, :, 1], pos)
    v = qkv[:, :, :, 2]
    outs, lses = [], []
    for g, (window, dilation) in enumerate(ATTN_GROUPS):
        o_g, lse_g = dilated_window_attention(q[:, :, g], k[:, :, g], v[:, :, g], window, dilation)
        outs.append(o_g)
        lses.append(lse_g)
    w = jax.nn.softmax(jnp.stack(lses, axis=0), axis=0)
    o = jnp.sum(w[..., None] * jnp.stack(outs, axis=0), axis=0)
    return o.reshape(b_, s_, ATTN_OUT_WIDTH)


def hgrn2_chunk_scan(q, f, v):
    b_, s_, h_, dk = q.shape
    dv = v.shape[-1]
    n_chunks = s_ // HGRN_CHUNK

    def chunks(t):
        return t.reshape(b_, n_chunks, HGRN_CHUNK, h_, t.shape[-1]).transpose(0, 3, 1, 2, 4)

    qc, fc, vc = chunks(q), chunks(f), chunks(v)
    kc = 1.0 - fc
    cum = jnp.cumsum(jnp.log(fc), axis=3)
    cum_last = cum[:, :, :, -1:]
    q_dec = qc * jnp.exp(cum)
    k_dec = kc * jnp.exp(-cum)
    k_end = kc * jnp.exp(cum_last - cum)
    tril = jnp.tril(jnp.ones((HGRN_CHUNK, HGRN_CHUNK), dtype=bool))
    a = jnp.where(tril, jnp.einsum('bhncd,bhnsd->bhncs', q_dec, k_dec), 0.0)
    o_intra = jnp.einsum('bhncs,bhnsv->bhncv', a, vc)
    decay = jnp.exp(cum_last[:, :, :, 0])

    def step(state, inp):
        q_n, k_n, v_n, dec_n = inp
        o_n = jnp.einsum('bhcd,bhdv->bhcv', q_n, state)
        state = dec_n[..., None] * state + jnp.einsum('bhcd,bhcv->bhdv', k_n, v_n)
        return state, o_n

    xs = (jnp.moveaxis(q_dec, 2, 0), jnp.moveaxis(k_end, 2, 0),
          jnp.moveaxis(vc, 2, 0), jnp.moveaxis(decay, 2, 0))
    _, o_inter = lax.scan(step, jnp.zeros((b_, h_, dk, dv), jnp.float32), xs)
    o = o_intra + jnp.moveaxis(o_inter, 0, 2)
    return o.transpose(0, 2, 3, 1, 4).reshape(b_, s_, h_, dv)


def bidirectional_hgrn2(hq, hf_fwd, hf_bwd, hi, hog, lb_fwd, lb_bwd, layer, norm_g):
    b_, s_, _ = hq.shape

    def heads(t):
        return t.astype(jnp.float32).reshape(b_, s_, HGRN_HEADS, HGRN_HEAD_DIM)

    def forget(hf, lb_table):
        lb = jnp.cumsum(jax.nn.softmax(lb_table.astype(jnp.float32), axis=0), axis=0)[layer]
        return heads(lb + (1.0 - lb) * jax.nn.sigmoid(hf.astype(jnp.float32)))

    q = heads(jax.nn.silu(hq.astype(jnp.float32)))
    i = heads(hi)
    f_f = forget(hf_fwd, lb_fwd)
    f_b = forget(hf_bwd, lb_bwd)
    rev = lambda t: jnp.flip(t, axis=1)
    o = hgrn2_chunk_scan(q, f_f, i) + rev(hgrn2_chunk_scan(rev(q), rev(f_b), rev(i)))
    o = o * lax.rsqrt(jnp.mean(jnp.square(o), axis=-1, keepdims=True) + LN_EPS)
    return o.reshape(b_, s_, HGRN_WIDTH) * norm_g * jax.nn.silu(hog.astype(jnp.float32))


def hybrid_mixer(h, w_in, lb_fwd, lb_bwd, layer, hgrn_norm_g, w_a, w_b, w_out):
    proj = h @ w_in
    splits = np.cumsum([HGRN_WIDTH] * 5 + [ATTN_QKV_WIDTH]).tolist()
    hq, hf_fwd, hf_bwd, hi, hog, h_qkv, h_gate = jnp.split(proj, splits, axis=-1)
    y_a = bidirectional_hgrn2(hq, hf_fwd, hf_bwd, hi, hog, lb_fwd, lb_bwd, layer,
                              hgrn_norm_g).astype(h.dtype) @ w_a
    y_b = dilated_attention_mixer(h_qkv).astype(h.dtype) @ w_b
    g_a, g_b = jnp.split(jax.nn.sigmoid(h_gate), N_BRANCHES, axis=-1)
    return (g_a * y_a + g_b * y_b) @ w_out


def _fwd_setup_inputs(seed: int = 0) -> dict:
    key = jax.random.key(seed)
    ks = jax.random.split(key, 18)

    def normal(k, shape):
        return jax.random.normal(k, shape, jnp.float32)

    def dense(k, shape, scale=1.0):
        return normal(k, shape) * (shape[-2] ** -0.5) * scale

    def gain(k, shape):
        return 1.0 + 0.02 * normal(k, shape)

    def bias(k, shape):
        return 0.02 * normal(k, shape)

    return {
        'x': normal(ks[0], (BATCH, SEQ, D_MODEL)),
        'ffn1_w_in': dense(ks[1], (DEPTH, D_MODEL, 2 * D_FF)),
        'ffn1_w_out': dense(ks[2], (DEPTH, D_FF, D_MODEL), DEEPNORM_BETA),
        'ln1_g': gain(ks[3], (DEPTH, D_MODEL)),
        'ln1_b': bias(ks[4], (DEPTH, D_MODEL)),
        'mix_w_in': dense(ks[5], (DEPTH, D_MODEL, IN_COLS)),
        'hgrn_lb_fwd': 0.1 * normal(ks[6], (DEPTH + 1, HGRN_WIDTH)),
        'hgrn_lb_bwd': 0.1 * normal(ks[7], (DEPTH + 1, HGRN_WIDTH)),
        'hgrn_norm_g': gain(ks[8], (DEPTH, HGRN_WIDTH)),
        'w_branch_a': dense(ks[9], (DEPTH, HGRN_WIDTH, D_MODEL), DEEPNORM_BETA),
        'w_branch_b': dense(ks[10], (DEPTH, ATTN_OUT_WIDTH, D_MODEL), DEEPNORM_BETA),
        'mix_w_out': dense(ks[11], (DEPTH, D_MODEL, D_MODEL), DEEPNORM_BETA),
        'ln2_g': gain(ks[12], (DEPTH, D_MODEL)),
        'ln2_b': bias(ks[13], (DEPTH, D_MODEL)),
        'ffn2_w_in': dense(ks[14], (DEPTH, D_MODEL, 2 * D_FF)),
        'ffn2_w_out': dense(ks[15], (DEPTH, D_FF, D_MODEL), DEEPNORM_BETA),
        'ln3_g': gain(ks[16], (DEPTH, D_MODEL)),
        'ln3_b': bias(ks[17], (DEPTH, D_MODEL)),
    }


def _fwd_reference(x, ffn1_w_in, ffn1_w_out, ln1_g, ln1_b, mix_w_in, hgrn_lb_fwd, hgrn_lb_bwd,
              hgrn_norm_g, w_branch_a, w_branch_b, mix_w_out, ln2_g, ln2_b,
              ffn2_w_in, ffn2_w_out, ln3_g, ln3_b):
    h = x
    for layer in range(DEPTH):
        h = layer_norm(DEEPNORM_ALPHA * h + 0.5 * swiglu(h, ffn1_w_in[layer], ffn1_w_out[layer]),
                       ln1_g[layer], ln1_b[layer])
        mix = hybrid_mixer(h, mix_w_in[layer], hgrn_lb_fwd, hgrn_lb_bwd, layer, hgrn_norm_g[layer],
                           w_branch_a[layer], w_branch_b[layer], mix_w_out[layer])
        h = layer_norm(DEEPNORM_ALPHA * h + mix, ln2_g[layer], ln2_b[layer])
        h = layer_norm(DEEPNORM_ALPHA * h + 0.5 * swiglu(h, ffn2_w_in[layer], ffn2_w_out[layer]),
                       ln3_g[layer], ln3_b[layer])
    return h


import jax as _jax
import jax.numpy as _jnp

TWIN_FORMAT = 'train_step'
FWD_PARAMS = ['x', 'ffn1_w_in', 'ffn1_w_out', 'ln1_g', 'ln1_b', 'mix_w_in', 'hgrn_lb_fwd', 'hgrn_lb_bwd', 'hgrn_norm_g', 'w_branch_a', 'w_branch_b', 'mix_w_out', 'ln2_g', 'ln2_b', 'ffn2_w_in', 'ffn2_w_out', 'ln3_g', 'ln3_b']
TWIN_WEIGHTS = ['ffn1_w_in', 'ffn1_w_out', 'ln1_g', 'ln1_b', 'mix_w_in', 'hgrn_lb_fwd', 'hgrn_lb_bwd', 'hgrn_norm_g', 'w_branch_a', 'w_branch_b', 'mix_w_out', 'ln2_g', 'ln2_b', 'ffn2_w_in', 'ffn2_w_out', 'ln3_g', 'ln3_b']
TWIN_DIFF_INPUT = 'x'
TWIN_INPUTS = ['x', 'ffn1_w_in', 'ffn1_w_out', 'ln1_g', 'ln1_b', 'mix_w_in', 'hgrn_lb_fwd', 'hgrn_lb_bwd', 'hgrn_norm_g', 'w_branch_a', 'w_branch_b', 'mix_w_out', 'ln2_g', 'ln2_b', 'ffn2_w_in', 'ffn2_w_out', 'ln3_g', 'ln3_b', 'loss_target', 'm_ffn1_w_in', 'm_ffn1_w_out', 'm_ln1_g', 'm_ln1_b', 'm_mix_w_in', 'm_hgrn_lb_fwd', 'm_hgrn_lb_bwd', 'm_hgrn_norm_g', 'm_w_branch_a', 'm_w_branch_b', 'm_mix_w_out', 'm_ln2_g', 'm_ln2_b', 'm_ffn2_w_in', 'm_ffn2_w_out', 'm_ln3_g', 'm_ln3_b', 'v_ffn1_w_in', 'v_ffn1_w_out', 'v_ln1_g', 'v_ln1_b', 'v_mix_w_in', 'v_hgrn_lb_fwd', 'v_hgrn_lb_bwd', 'v_hgrn_norm_g', 'v_w_branch_a', 'v_w_branch_b', 'v_mix_w_out', 'v_ln2_g', 'v_ln2_b', 'v_ffn2_w_in', 'v_ffn2_w_out', 'v_ln3_g', 'v_ln3_b']
TWIN_OUTPUTS = ['loss', 'grad_x', 'grad_ffn1_w_in', 'grad_ffn1_w_out', 'grad_ln1_g', 'grad_ln1_b', 'grad_mix_w_in', 'grad_hgrn_lb_fwd', 'grad_hgrn_lb_bwd', 'grad_hgrn_norm_g', 'grad_w_branch_a', 'grad_w_branch_b', 'grad_mix_w_out', 'grad_ln2_g', 'grad_ln2_b', 'grad_ffn2_w_in', 'grad_ffn2_w_out', 'grad_ln3_g', 'grad_ln3_b', 'delta_ffn1_w_in', 'delta_ffn1_w_out', 'delta_ln1_g', 'delta_ln1_b', 'delta_mix_w_in', 'delta_hgrn_lb_fwd', 'delta_hgrn_lb_bwd', 'delta_hgrn_norm_g', 'delta_w_branch_a', 'delta_w_branch_b', 'delta_mix_w_out', 'delta_ln2_g', 'delta_ln2_b', 'delta_ffn2_w_in', 'delta_ffn2_w_out', 'delta_ln3_g', 'delta_ln3_b', 'new_m_ffn1_w_in', 'new_m_ffn1_w_out', 'new_m_ln1_g', 'new_m_ln1_b', 'new_m_mix_w_in', 'new_m_hgrn_lb_fwd', 'new_m_hgrn_lb_bwd', 'new_m_hgrn_norm_g', 'new_m_w_branch_a', 'new_m_w_branch_b', 'new_m_mix_w_out', 'new_m_ln2_g', 'new_m_ln2_b', 'new_m_ffn2_w_in', 'new_m_ffn2_w_out', 'new_m_ln3_g', 'new_m_ln3_b', 'new_v_ffn1_w_in', 'new_v_ffn1_w_out', 'new_v_ln1_g', 'new_v_ln1_b', 'new_v_mix_w_in', 'new_v_hgrn_lb_fwd', 'new_v_hgrn_lb_bwd', 'new_v_hgrn_norm_g', 'new_v_w_branch_a', 'new_v_w_branch_b', 'new_v_mix_w_out', 'new_v_ln2_g', 'new_v_ln2_b', 'new_v_ffn2_w_in', 'new_v_ffn2_w_out', 'new_v_ln3_g', 'new_v_ln3_b']
TWIN_LEAF_KINDS = {'loss': 'loss', 'grad_x': 'grad_x', 'grad_ffn1_w_in': 'grad_w', 'grad_ffn1_w_out': 'grad_w', 'grad_ln1_g': 'grad_w', 'grad_ln1_b': 'grad_w', 'grad_mix_w_in': 'grad_w', 'grad_hgrn_lb_fwd': 'grad_w', 'grad_hgrn_lb_bwd': 'grad_w', 'grad_hgrn_norm_g': 'grad_w', 'grad_w_branch_a': 'grad_w', 'grad_w_branch_b': 'grad_w', 'grad_mix_w_out': 'grad_w', 'grad_ln2_g': 'grad_w', 'grad_ln2_b': 'grad_w', 'grad_ffn2_w_in': 'grad_w', 'grad_ffn2_w_out': 'grad_w', 'grad_ln3_g': 'grad_w', 'grad_ln3_b': 'grad_w', 'delta_ffn1_w_in': 'delta_w', 'delta_ffn1_w_out': 'delta_w', 'delta_ln1_g': 'delta_w', 'delta_ln1_b': 'delta_w', 'delta_mix_w_in': 'delta_w', 'delta_hgrn_lb_fwd': 'delta_w', 'delta_hgrn_lb_bwd': 'delta_w', 'delta_hgrn_norm_g': 'delta_w', 'delta_w_branch_a': 'delta_w', 'delta_w_branch_b': 'delta_w', 'delta_mix_w_out': 'delta_w', 'delta_ln2_g': 'delta_w', 'delta_ln2_b': 'delta_w', 'delta_ffn2_w_in': 'delta_w', 'delta_ffn2_w_out': 'delta_w', 'delta_ln3_g': 'delta_w', 'delta_ln3_b': 'delta_w', 'new_m_ffn1_w_in': 'new_m', 'new_m_ffn1_w_out': 'new_m', 'new_m_ln1_g': 'new_m', 'new_m_ln1_b': 'new_m', 'new_m_mix_w_in': 'new_m', 'new_m_hgrn_lb_fwd': 'new_m', 'new_m_hgrn_lb_bwd': 'new_m', 'new_m_hgrn_norm_g': 'new_m', 'new_m_w_branch_a': 'new_m', 'new_m_w_branch_b': 'new_m', 'new_m_mix_w_out': 'new_m', 'new_m_ln2_g': 'new_m', 'new_m_ln2_b': 'new_m', 'new_m_ffn2_w_in': 'new_m', 'new_m_ffn2_w_out': 'new_m', 'new_m_ln3_g': 'new_m', 'new_m_ln3_b': 'new_m', 'new_v_ffn1_w_in': 'new_v', 'new_v_ffn1_w_out': 'new_v', 'new_v_ln1_g': 'new_v', 'new_v_ln1_b': 'new_v', 'new_v_mix_w_in': 'new_v', 'new_v_hgrn_lb_fwd': 'new_v', 'new_v_hgrn_lb_bwd': 'new_v', 'new_v_hgrn_norm_g': 'new_v', 'new_v_w_branch_a': 'new_v', 'new_v_w_branch_b': 'new_v', 'new_v_mix_w_out': 'new_v', 'new_v_ln2_g': 'new_v', 'new_v_ln2_b': 'new_v', 'new_v_ffn2_w_in': 'new_v', 'new_v_ffn2_w_out': 'new_v', 'new_v_ln3_g': 'new_v', 'new_v_ln3_b': 'new_v'}


def _forward(args):
    return _fwd_reference(*[args[k] for k in FWD_PARAMS])


def _output_shape():
    out = _jax.eval_shape(lambda: _forward(_fwd_setup_inputs(0)))
    return out.shape, out.dtype

N_MICROBATCH = 1
ADAM_LR = 0.001
ADAM_B1 = 0.9
ADAM_B2 = 0.999
ADAM_EPS = 1e-08
ADAM_WD = 0.01
ADAM_STEP = 10
PER_EXAMPLE_BATCH_AXIS = {'x': 0, 'loss_target': 0}
SHARED_INPUTS = []
_WEIGHT_DTYPES = {'ffn1_w_in': _jnp.float32, 'ffn1_w_out': _jnp.float32, 'ln1_g': _jnp.float32, 'ln1_b': _jnp.float32, 'mix_w_in': _jnp.float32, 'hgrn_lb_fwd': _jnp.float32, 'hgrn_lb_bwd': _jnp.float32, 'hgrn_norm_g': _jnp.float32, 'w_branch_a': _jnp.float32, 'w_branch_b': _jnp.float32, 'mix_w_out': _jnp.float32, 'ln2_g': _jnp.float32, 'ln2_b': _jnp.float32, 'ffn2_w_in': _jnp.float32, 'ffn2_w_out': _jnp.float32, 'ln3_g': _jnp.float32, 'ln3_b': _jnp.float32}
MOMENT_SCALE = {'ffn1_w_in': 8.150883e-03, 'ffn1_w_out': 2.235123e-02, 'ln1_g': 5.112703e-01, 'ln1_b': 2.394977e-01, 'mix_w_in': 4.335725e-03, 'hgrn_lb_fwd': 4.681414e-04, 'hgrn_lb_bwd': 4.712476e-04, 'hgrn_norm_g': 8.678691e-03, 'w_branch_a': 1.443126e-02, 'w_branch_b': 2.895948e-03, 'mix_w_out': 1.472170e-02, 'ln2_g': 5.148720e-01, 'ln2_b': 2.398328e-01, 'ffn2_w_in': 8.031736e-03, 'ffn2_w_out': 2.198688e-02, 'ln3_g': 1.601078e+01, 'ln3_b': 4.215977e-01}


def _to_microbatches(a, axis):
    t = _jnp.moveaxis(a, axis, 0)
    t = t.reshape((N_MICROBATCH, t.shape[0] // N_MICROBATCH) + t.shape[1:])
    return _jnp.moveaxis(t, 1, axis + 1)


def setup_inputs(seed: int = 0) -> dict:
    inp = _fwd_setup_inputs(seed)
    key = _jax.random.fold_in(_jax.random.key(seed), 7919)
    shape, _ = _output_shape()
    out = dict(inp)
    out["loss_target"] = _jax.random.normal(_jax.random.fold_in(key, 0), shape, _jnp.float32)
    for i, name in enumerate(TWIN_WEIGHTS):
        w = inp[name].astype(_jnp.float32)
        if MOMENT_SCALE is None:
            s = _jnp.sqrt(_jnp.mean(_jnp.square(w)) + 1e-30)
        else:
            s = MOMENT_SCALE[name]
        km, kv = _jax.random.split(_jax.random.fold_in(key, i + 1))
        out[name] = w
        out["m_" + name] = s * _jax.random.normal(km, w.shape, _jnp.float32)
        out["v_" + name] = (s * s) * _jax.random.uniform(kv, w.shape, _jnp.float32, 0.5, 1.5)
    if N_MICROBATCH > 1:
        for name, axis in PER_EXAMPLE_BATCH_AXIS.items():
            out[name] = _to_microbatches(out[name], axis)
    return {'x': out['x'], 'ffn1_w_in': out['ffn1_w_in'], 'ffn1_w_out': out['ffn1_w_out'], 'ln1_g': out['ln1_g'], 'ln1_b': out['ln1_b'], 'mix_w_in': out['mix_w_in'], 'hgrn_lb_fwd': out['hgrn_lb_fwd'], 'hgrn_lb_bwd': out['hgrn_lb_bwd'], 'hgrn_norm_g': out['hgrn_norm_g'], 'w_branch_a': out['w_branch_a'], 'w_branch_b': out['w_branch_b'], 'mix_w_out': out['mix_w_out'], 'ln2_g': out['ln2_g'], 'ln2_b': out['ln2_b'], 'ffn2_w_in': out['ffn2_w_in'], 'ffn2_w_out': out['ffn2_w_out'], 'ln3_g': out['ln3_g'], 'ln3_b': out['ln3_b'], 'loss_target': out['loss_target'], 'm_ffn1_w_in': out['m_ffn1_w_in'], 'm_ffn1_w_out': out['m_ffn1_w_out'], 'm_ln1_g': out['m_ln1_g'], 'm_ln1_b': out['m_ln1_b'], 'm_mix_w_in': out['m_mix_w_in'], 'm_hgrn_lb_fwd': out['m_hgrn_lb_fwd'], 'm_hgrn_lb_bwd': out['m_hgrn_lb_bwd'], 'm_hgrn_norm_g': out['m_hgrn_norm_g'], 'm_w_branch_a': out['m_w_branch_a'], 'm_w_branch_b': out['m_w_branch_b'], 'm_mix_w_out': out['m_mix_w_out'], 'm_ln2_g': out['m_ln2_g'], 'm_ln2_b': out['m_ln2_b'], 'm_ffn2_w_in': out['m_ffn2_w_in'], 'm_ffn2_w_out': out['m_ffn2_w_out'], 'm_ln3_g': out['m_ln3_g'], 'm_ln3_b': out['m_ln3_b'], 'v_ffn1_w_in': out['v_ffn1_w_in'], 'v_ffn1_w_out': out['v_ffn1_w_out'], 'v_ln1_g': out['v_ln1_g'], 'v_ln1_b': out['v_ln1_b'], 'v_mix_w_in': out['v_mix_w_in'], 'v_hgrn_lb_fwd': out['v_hgrn_lb_fwd'], 'v_hgrn_lb_bwd': out['v_hgrn_lb_bwd'], 'v_hgrn_norm_g': out['v_hgrn_norm_g'], 'v_w_branch_a': out['v_w_branch_a'], 'v_w_branch_b': out['v_w_branch_b'], 'v_mix_w_out': out['v_mix_w_out'], 'v_ln2_g': out['v_ln2_g'], 'v_ln2_b': out['v_ln2_b'], 'v_ffn2_w_in': out['v_ffn2_w_in'], 'v_ffn2_w_out': out['v_ffn2_w_out'], 'v_ln3_g': out['v_ln3_g'], 'v_ln3_b': out['v_ln3_b']}


def _loss(weights, diff, rest, loss_target):
    with _jax.named_scope("forward"):
        args = {**rest, TWIN_DIFF_INPUT: diff, **{k: w.astype(_WEIGHT_DTYPES[k]) for k, w in weights.items()}}
        y = _forward(args)
    with _jax.named_scope("loss_head"):
        err = _jnp.square(y.astype(_jnp.float32) - loss_target)
        return 0.5 * _jnp.sum(_jnp.mean(err, axis=-1)) if err.ndim else 0.5 * err


def _adamw(w, g, m, v):
    m = ADAM_B1 * m + (1.0 - ADAM_B1) * g
    v = ADAM_B2 * v + (1.0 - ADAM_B2) * _jnp.square(g)
    m_hat = m / (1.0 - ADAM_B1 ** ADAM_STEP)
    v_hat = v / (1.0 - ADAM_B2 ** ADAM_STEP)
    delta = -ADAM_LR * (m_hat / (_jnp.sqrt(v_hat) + ADAM_EPS) + ADAM_WD * w)
    return delta, m, v


def reference(x, ffn1_w_in, ffn1_w_out, ln1_g, ln1_b, mix_w_in, hgrn_lb_fwd, hgrn_lb_bwd, hgrn_norm_g, w_branch_a, w_branch_b, mix_w_out, ln2_g, ln2_b, ffn2_w_in, ffn2_w_out, ln3_g, ln3_b, loss_target, m_ffn1_w_in, m_ffn1_w_out, m_ln1_g, m_ln1_b, m_mix_w_in, m_hgrn_lb_fwd, m_hgrn_lb_bwd, m_hgrn_norm_g, m_w_branch_a, m_w_branch_b, m_mix_w_out, m_ln2_g, m_ln2_b, m_ffn2_w_in, m_ffn2_w_out, m_ln3_g, m_ln3_b, v_ffn1_w_in, v_ffn1_w_out, v_ln1_g, v_ln1_b, v_mix_w_in, v_hgrn_lb_fwd, v_hgrn_lb_bwd, v_hgrn_norm_g, v_w_branch_a, v_w_branch_b, v_mix_w_out, v_ln2_g, v_ln2_b, v_ffn2_w_in, v_ffn2_w_out, v_ln3_g, v_ln3_b):
    given = dict(x=x, ffn1_w_in=ffn1_w_in, ffn1_w_out=ffn1_w_out, ln1_g=ln1_g, ln1_b=ln1_b, mix_w_in=mix_w_in, hgrn_lb_fwd=hgrn_lb_fwd, hgrn_lb_bwd=hgrn_lb_bwd, hgrn_norm_g=hgrn_norm_g, w_branch_a=w_branch_a, w_branch_b=w_branch_b, mix_w_out=mix_w_out, ln2_g=ln2_g, ln2_b=ln2_b, ffn2_w_in=ffn2_w_in, ffn2_w_out=ffn2_w_out, ln3_g=ln3_g, ln3_b=ln3_b, loss_target=loss_target, m_ffn1_w_in=m_ffn1_w_in, m_ffn1_w_out=m_ffn1_w_out, m_ln1_g=m_ln1_g, m_ln1_b=m_ln1_b, m_mix_w_in=m_mix_w_in, m_hgrn_lb_fwd=m_hgrn_lb_fwd, m_hgrn_lb_bwd=m_hgrn_lb_bwd, m_hgrn_norm_g=m_hgrn_norm_g, m_w_branch_a=m_w_branch_a, m_w_branch_b=m_w_branch_b, m_mix_w_out=m_mix_w_out, m_ln2_g=m_ln2_g, m_ln2_b=m_ln2_b, m_ffn2_w_in=m_ffn2_w_in, m_ffn2_w_out=m_ffn2_w_out, m_ln3_g=m_ln3_g, m_ln3_b=m_ln3_b, v_ffn1_w_in=v_ffn1_w_in, v_ffn1_w_out=v_ffn1_w_out, v_ln1_g=v_ln1_g, v_ln1_b=v_ln1_b, v_mix_w_in=v_mix_w_in, v_hgrn_lb_fwd=v_hgrn_lb_fwd, v_hgrn_lb_bwd=v_hgrn_lb_bwd, v_hgrn_norm_g=v_hgrn_norm_g, v_w_branch_a=v_w_branch_a, v_w_branch_b=v_w_branch_b, v_mix_w_out=v_mix_w_out, v_ln2_g=v_ln2_g, v_ln2_b=v_ln2_b, v_ffn2_w_in=v_ffn2_w_in, v_ffn2_w_out=v_ffn2_w_out, v_ln3_g=v_ln3_g, v_ln3_b=v_ln3_b)
    weights = {n: given[n] for n in TWIN_WEIGHTS}
    shared = {n: given[n] for n in SHARED_INPUTS}
    per_example = {n: given[n] for n in ['x']}
    grad_fn = _jax.value_and_grad(_loss, argnums=(0, 1))

    def one_microbatch(ex, loss_target):
        ex = dict(ex)
        diff = ex.pop(TWIN_DIFF_INPUT)
        return grad_fn(weights, diff, {**shared, **ex}, loss_target)

    if N_MICROBATCH == 1:
        loss, (grad_w, grad_x) = one_microbatch(per_example, given["loss_target"])
    else:
        def body(carry, xs):
            loss_sum, grad_sum = carry
            l_k, (gw_k, gx_k) = one_microbatch(xs[0], xs[1])
            with _jax.named_scope("update"):
                return (loss_sum + l_k, _jax.tree.map(_jnp.add, grad_sum, gw_k)), gx_k

        init = (_jnp.zeros((), _jnp.float32), _jax.tree.map(_jnp.zeros_like, weights))
        (loss, grad_w), grad_x = _jax.lax.scan(body, init, (per_example, given["loss_target"]))
    with _jax.named_scope("update"):
        delta_w, new_m, new_v = {}, {}, {}
        for n in TWIN_WEIGHTS:
            delta_w[n], new_m[n], new_v[n] = _adamw(weights[n], grad_w[n], given["m_" + n], given["v_" + n])
    return (loss, grad_x, *[grad_w[n] for n in TWIN_WEIGHTS], *[delta_w[n] for n in TWIN_WEIGHTS],
            *[new_m[n] for n in TWIN_WEIGHTS], *[new_v[n] for n in TWIN_WEIGHTS])
```

```python
import functools

import numpy as np
import jax
import jax.numpy as jnp
from jax import lax
from jax.experimental import pallas as pl
from jax.experimental.pallas import tpu as pltpu

F32 = jnp.float32
BF16 = jnp.bfloat16

HEAD = 128
ATTN_GROUPS = ((128, 1), (512, 4), (2048, 16))
ATTN_HEADS = 4
N_GROUPS = len(ATTN_GROUPS)
QKV_W = N_GROUPS * 3 * ATTN_HEADS * HEAD
ATTN_OUT = ATTN_HEADS * HEAD
ROPE_THETA = 500000.0
ROPE_DIM = HEAD // 4
HGRN_CHUNK = 32
ALPHA = 2.0 ** 0.25
LN_EPS = 1e-5
NEG_INF = -1e30
ADAM_LR, ADAM_B1, ADAM_B2, ADAM_EPS, ADAM_WD, ADAM_STEP = 0.001, 0.9, 0.999, 1e-08, 0.01, 10

N_SHARD = 4
VMEM_LIMIT = 56 * 1024 * 1024

NN = ((1,), (0,))
NT = ((1,), (1,))
TN = ((0,), (0,))


def _dot(a, b, dims, precision=None):
    return lax.dot_general(a, b, (dims, ((), ())), preferred_element_type=F32, precision=precision)


def _tile(n, pref, mult=128):
    best = None
    for t in range(mult, min(n, pref) + 1, mult):
        if n % t == 0:
            best = t
    return n if best is None else best


def _params(n_parallel, n_arbitrary):
    return pltpu.CompilerParams(
        dimension_semantics=("parallel",) * n_parallel + ("arbitrary",) * n_arbitrary,
        vmem_limit_bytes=VMEM_LIMIT)


def _sigmoid(x):
    return 1.0 / (1.0 + jnp.exp(-x))


def _silu(x):
    return x * _sigmoid(x)


def _gemm(name, grid, ins, in_specs, outs, out_specs, accs, dot_fn, epi_fn):
    n_in, n_out, n_k = len(ins), len(outs), grid[-1]

    def body(*refs):
        in_refs, out_refs, acc_refs = refs[:n_in], refs[n_in:n_in + n_out], refs[n_in + n_out:]
        k = pl.program_id(len(grid) - 1)

        @pl.when(k == 0)
        def _():
            for a in acc_refs:
                a[...] = jnp.zeros(a.shape, F32)

        dot_fn(in_refs, acc_refs)

        @pl.when(k == n_k - 1)
        def _():
            epi_fn(in_refs, acc_refs, out_refs)

    res = pl.pallas_call(
        body, name=name, grid=grid, in_specs=in_specs, out_specs=out_specs, out_shape=outs,
        scratch_shapes=[pltpu.VMEM(s, F32) for s in accs],
        compiler_params=_params(len(grid) - 1, 1))(*ins)
    return res


def _sds(shape, dtype):
    return jax.ShapeDtypeStruct(shape, dtype)


def _ffn_up(xb, w3):
    t, d = xb.shape
    nf = w3.shape[2]
    f = 2 * nf
    tm, tk = _tile(t, 512, 8), _tile(d, 512)

    def dot_fn(r, acc):
        acc[0][...] += _dot(r[0][...], r[1][...], NN)
        acc[1][...] += _dot(r[0][...], r[2][...], NN)

    def epi_fn(r, acc, out):
        g, u = acc[0][...], acc[1][...]
        out[0][0] = g.astype(BF16)
        out[0][1] = u.astype(BF16)
        out[1][...] = (_silu(g) * u).astype(BF16)

    return _gemm(
        "ffn_up", (t // tm, 2, d // tk), [xb, w3, w3],
        [pl.BlockSpec((tm, tk), lambda i, j, k: (i, k)),
         pl.BlockSpec((None, tk, nf), lambda i, j, k: (j, k, 0)),
         pl.BlockSpec((None, tk, nf), lambda i, j, k: (j + 2, k, 0))],
        [_sds((2, t, f), BF16), _sds((t, f), BF16)],
        [pl.BlockSpec((2, tm, nf), lambda i, j, k: (0, i, j)),
         pl.BlockSpec((tm, nf), lambda i, j, k: (i, j))],
        [(tm, nf), (tm, nf)], dot_fn, epi_fn)


def _down_ln(a, w, resid, coef, g, b):
    t, kd = a.shape
    d = w.shape[1]
    tm, tk = _tile(t, 512, 8), _tile(kd, 512)

    def dot_fn(r, acc):
        acc[0][...] += _dot(r[0][...], r[1][...], NN)

    def epi_fn(r, acc, out):
        v = ALPHA * r[2][...] + coef * acc[0][...]
        mu = jnp.mean(v, axis=-1, keepdims=True)
        c = v - mu
        var = jnp.mean(c * c, axis=-1, keepdims=True)
        rstd = lax.rsqrt(var + LN_EPS)
        xhat = c * rstd
        h = xhat * r[3][...] + r[4][...]
        out[0][...] = h
        out[1][...] = h.astype(BF16)
        out[2][...] = xhat
        out[3][...] = rstd

    row = pl.BlockSpec((tm, d), lambda i, k: (i, 0))
    vec = pl.BlockSpec((1, d), lambda i, k: (0, 0))
    return _gemm(
        "down_ln", (t // tm, kd // tk), [a, w, resid, g, b],
        [pl.BlockSpec((tm, tk), lambda i, k: (i, k)), pl.BlockSpec((tk, d), lambda i, k: (k, 0)), row, vec, vec],
        [_sds((t, d), F32), _sds((t, d), BF16), _sds((t, d), F32), _sds((t, 1), F32)],
        [row, row, row, pl.BlockSpec((tm, 1), lambda i, k: (i, 0))],
        [(tm, d)], dot_fn, epi_fn)


def _mm_w3(a, w3, out_dtype):
    t, kd = a.shape
    n = w3.shape[2]
    tm, tk = _tile(t, 512, 8), _tile(kd, 256)

    def dot_fn(r, acc):
        acc[0][...] += _dot(r[0][...], r[1][...], NN)

    def epi_fn(r, acc, out):
        out[0][...] = acc[0][...].astype(out_dtype)

    return _gemm(
        "mm_w3", (t // tm, N_SHARD, kd // tk), [a, w3],
        [pl.BlockSpec((tm, tk), lambda i, j, k: (i, k)), pl.BlockSpec((None, tk, n), lambda i, j, k: (j, k, 0))],
        [_sds((t, N_SHARD * n), out_dtype)], [pl.BlockSpec((tm, n), lambda i, j, k: (i, j))],
        [(tm, n)], dot_fn, epi_fn)[0]


def _mm_w2(a, w, out_dtype):
    t, kd = a.shape
    n = w.shape[1]
    tm, tn, tk = _tile(t, 512, 8), _tile(n, 1024), _tile(kd, 512)

    def dot_fn(r, acc):
        acc[0][...] += _dot(r[0][...], r[1][...], NN)

    def epi_fn(r, acc, out):
        out[0][...] = acc[0][...].astype(out_dtype)

    return _gemm(
        "mm_w2", (t // tm, n // tn, kd // tk), [a, w],
        [pl.BlockSpec((tm, tk), lambda i, j, k: (i, k)), pl.BlockSpec((tk, tn), lambda i, j, k: (k, j))],
        [_sds((t, n), out_dtype)], [pl.BlockSpec((tm, tn), lambda i, j, k: (i, j))],
        [(tm, tn)], dot_fn, epi_fn)[0]


def _branch_gate(ob, wb3, ya, proj, gate_col):
    t, kd = ob.shape
    n = wb3.shape[2]
    d = N_SHARD * n
    tm = _tile(t, 512, 8)
    ga0, gb0 = gate_col // n, (gate_col + d) // n

    def dot_fn(r, acc):
        acc[0][...] += _dot(r[0][...], r[1][...], NN)

    def epi_fn(r, acc, out):
        yb = acc[0][...]
        out[0][...] = yb
        out[1][...] = (_sigmoid(r[3][...]) * r[2][...] + _sigmoid(r[4][...]) * yb).astype(BF16)

    blk = pl.BlockSpec((tm, n), lambda i, j, k: (i, j))
    return _gemm(
        "branch_gate", (t // tm, N_SHARD, 1), [ob, wb3, ya, proj, proj],
        [pl.BlockSpec((tm, kd), lambda i, j, k: (i, 0)), pl.BlockSpec((None, kd, n), lambda i, j, k: (j, 0, 0)), blk,
         pl.BlockSpec((tm, n), lambda i, j, k: (i, ga0 + j)), pl.BlockSpec((tm, n), lambda i, j, k: (i, gb0 + j))],
        [_sds((t, d), F32), _sds((t, d), BF16)], [blk, blk], [(tm, n)], dot_fn, epi_fn)


def _swiglu_bwd(dyb, w, u3):
    t, d = dyb.shape
    f = w.shape[0]
    tm, tr = _tile(t, 1024, 8), _tile(f, 512)

    def dot_fn(r, acc):
        acc[0][...] += _dot(r[0][...], r[1][...], NT)

    def epi_fn(r, acc, out):
        da = acc[0][...]
        g, u = r[2][0].astype(F32), r[2][1].astype(F32)
        s = _sigmoid(g)
        out[0][0] = (da * u * s * (1.0 + g * (1.0 - s))).astype(BF16)
        out[0][1] = (da * g * s).astype(BF16)

    ublk = pl.BlockSpec((2, tm, tr), lambda i, j, k: (0, i, j))
    return _gemm(
        "swiglu_bwd", (t // tm, f // tr, 1), [dyb, w, u3],
        [pl.BlockSpec((tm, d), lambda i, j, k: (i, 0)), pl.BlockSpec((tr, d), lambda i, j, k: (j, 0)), ublk],
        [_sds((2, t, f), BF16)], [ublk], [(tm, tr)], dot_fn, epi_fn)[0]


def _ffn_dx(du3, w3, resid):
    t = du3.shape[1]
    d, nf = w3.shape[1], w3.shape[2]
    tm, tr = _tile(t, 512, 8), _tile(d, 1024)

    def dot_fn(r, acc):
        acc[0][...] += _dot(r[0][...], r[1][...], NT)

    def epi_fn(r, acc, out):
        out[0][...] = acc[0][...] + r[2][...]

    blk = pl.BlockSpec((tm, tr), lambda i, j, k: (i, j))
    return _gemm(
        "ffn_dx", (t // tm, d // tr, N_SHARD), [du3, w3, resid],
        [pl.BlockSpec((None, tm, nf), lambda i, j, k: (k // 2, i, k % 2)),
         pl.BlockSpec((None, tr, nf), lambda i, j, k: (k, j, 0)), blk],
        [_sds((t, d), F32)], [blk], [(tm, tr)], dot_fn, epi_fn)[0]


def _nt_w3(dy, w3, resid):
    t = dy.shape[0]
    kd, n = w3.shape[1], w3.shape[2]
    tm, tr = _tile(t, 512, 8), _tile(kd, 1024)
    has_res = resid is not None

    def dot_fn(r, acc):
        acc[0][...] += _dot(r[0][...], r[1][...], NT)

    def epi_fn(r, acc, out):
        v = acc[0][...]
        if has_res:
            v = v + r[2][...]
        out[0][...] = v

    blk = pl.BlockSpec((tm, tr), lambda i, j, k: (i, j))
    return _gemm(
        "nt_w3", (t // tm, kd // tr, N_SHARD), [dy, w3] + ([resid] if has_res else []),
        [pl.BlockSpec((tm, n), lambda i, j, k: (i, k)), pl.BlockSpec((None, tr, n), lambda i, j, k: (k, j, 0))]
        + ([blk] if has_res else []),
        [_sds((t, kd), F32)], [blk], [(tm, tr)], dot_fn, epi_fn)[0]


def _nt_w2(dy, w, out_dtype):
    t, n = dy.shape
    r_ = w.shape[0]
    tm, tr, tk = _tile(t, 512, 8), _tile(r_, 1024), _tile(n, 1024)

    def dot_fn(r, acc):
        acc[0][...] += _dot(r[0][...], r[1][...], NT)

    def epi_fn(r, acc, out):
        out[0][...] = acc[0][...].astype(out_dtype)

    return _gemm(
        "nt_w2", (t // tm, r_ // tr, n // tk), [dy, w],
        [pl.BlockSpec((tm, tk), lambda i, j, k: (i, k)), pl.BlockSpec((tr, tk), lambda i, j, k: (j, k))],
        [_sds((t, r_), out_dtype)], [pl.BlockSpec((tm, tr), lambda i, j, k: (i, j))],
        [(tm, tr)], dot_fn, epi_fn)[0]


def _gate_bwd(dmix, w, proj, ya, yb, gate_col):
    t, n = dmix.shape
    d = w.shape[0]
    tm, tr = _tile(t, 512, 8), _tile(d, 512)
    ga0, gb0 = gate_col // tr, (gate_col + d) // tr
    nb = d // tr

    def dot_fn(r, acc):
        acc[0][...] += _dot(r[0][...], r[1][...], NT)

    def epi_fn(r, acc, out):
        dz = acc[0][...]
        ga, gb = _sigmoid(r[2][...]), _sigmoid(r[3][...])
        out[0][...] = (dz * ga).astype(BF16)
        out[1][...] = (dz * gb).astype(BF16)
        out[2][...] = (dz * r[4][...] * ga * (1.0 - ga)).astype(BF16)
        out[3][...] = (dz * r[5][...] * gb * (1.0 - gb)).astype(BF16)

    blk = pl.BlockSpec((tm, tr), lambda i, j, k: (i, j))
    return _gemm(
        "gate_bwd", (t // tm, nb, 1), [dmix, w, proj, proj, ya, yb],
        [pl.BlockSpec((tm, n), lambda i, j, k: (i, 0)), pl.BlockSpec((tr, n), lambda i, j, k: (j, 0)),
         pl.BlockSpec((tm, tr), lambda i, j, k: (i, ga0 + j)), pl.BlockSpec((tm, tr), lambda i, j, k: (i, gb0 + j)),
         blk, blk],
        [_sds((t, d), BF16), _sds((t, d), BF16), _sds((t, d), BF16), _sds((t, d), BF16)],
        [blk, blk, blk, blk], [(tm, tr)], dot_fn, epi_fn)


def _tn_w3(a, dy, dy_map, n):
    t, kd = a.shape
    tm, tkk = _tile(t, 512, 8), _tile(kd, 512)

    def dot_fn(r, acc):
        acc[0][...] += _dot(r[0][...], r[1][...], TN)

    def epi_fn(r, acc, out):
        out[0][...] = acc[0][...].astype(BF16)

    dy_block = (tm, n) if dy.ndim == 2 else (None, tm, n)
    return _gemm(
        "tn_w3", (kd // tkk, N_SHARD, t // tm), [a, dy],
        [pl.BlockSpec((tm, tkk), lambda i, j, m: (m, i)), pl.BlockSpec(dy_block, lambda i, j, m: dy_map(j, m))],
        [_sds((N_SHARD, kd, n), BF16)], [pl.BlockSpec((None, tkk, n), lambda i, j, m: (j, i, 0))],
        [(tkk, n)], dot_fn, epi_fn)[0]


def _tn_w2(a, dy):
    t, kd = a.shape
    n = dy.shape[1]
    tm, tkk, tn = _tile(t, 512, 8), _tile(kd, 512), _tile(n, 2048)

    def dot_fn(r, acc):
        acc[0][...] += _dot(r[0][...], r[1][...], TN)

    def epi_fn(r, acc, out):
        out[0][...] = acc[0][...].astype(BF16)

    return _gemm(
        "tn_w2", (kd // tkk, n // tn, t // tm), [a, dy],
        [pl.BlockSpec((tm, tkk), lambda i, j, m: (m, i)), pl.BlockSpec((tm, tn), lambda i, j, m: (m, j))],
        [_sds((kd, n), BF16)], [pl.BlockSpec((tkk, tn), lambda i, j, m: (i, j))],
        [(tkk, tn)], dot_fn, epi_fn)[0]


def _ln_bwd(dh_parts, xhat, rstd, g, coef, target=None):
    t, d = xhat.shape
    tm = _tile(t, 256, 8)
    n_parts = len(dh_parts)
    with_loss = target is not None
    ins = list(dh_parts) + [xhat, rstd, g] + ([target] if with_loss else [])

    def body(*refs):
        parts = refs[:n_parts]
        xh_ref, rstd_ref, g_ref = refs[n_parts:n_parts + 3]
        o = n_parts + 3 + (1 if with_loss else 0)
        dres_ref, dyb_ref, dg_ref, db_ref, loss_ref = refs[o:o + 5]
        i = pl.program_id(0)
        dh = parts[0][...]
        for p in parts[1:]:
            dh = dh + p[...]
        if with_loss:
            e = dh - refs[o - 1][...]
            part = 0.5 * jnp.sum(jnp.sum(e * e, axis=-1, keepdims=True) * (1.0 / d), axis=0, keepdims=True)
            dh = e * (1.0 / d)
        else:
            part = jnp.zeros((1, 1), F32)
        xh = xh_ref[...]
        dxh = dh * g_ref[...]
        m1 = jnp.mean(dxh, axis=-1, keepdims=True)
        m2 = jnp.mean(dxh * xh, axis=-1, keepdims=True)
        dv = rstd_ref[...] * (dxh - m1 - xh * m2)
        dres_ref[...] = ALPHA * dv
        dyb_ref[...] = (coef * dv).astype(BF16)

        @pl.when(i == 0)
        def _():
            dg_ref[...] = jnp.zeros(dg_ref.shape, F32)
            db_ref[...] = jnp.zeros(db_ref.shape, F32)
            loss_ref[...] = jnp.zeros(loss_ref.shape, F32)

        dg_ref[...] += jnp.sum(dh * xh, axis=0, keepdims=True)
        db_ref[...] += jnp.sum(dh, axis=0, keepdims=True)
        loss_ref[...] += jnp.broadcast_to(part, loss_ref.shape)

    row = pl.BlockSpec((tm, d), lambda i: (i, 0))
    vec = pl.BlockSpec((1, d), lambda i: (0, 0))
    return pl.pallas_call(
        body, name="ln_bwd", grid=(t // tm,),
        in_specs=[row] * n_parts + [row, pl.BlockSpec((tm, 1), lambda i: (i, 0)), vec] + ([row] if with_loss else []),
        out_specs=[row, row, vec, vec, pl.BlockSpec((1, 128), lambda i: (0, 0))],
        out_shape=[_sds((t, d), F32), _sds((t, d), BF16), _sds((1, d), F32), _sds((1, d), F32), _sds((1, 128), F32)],
        compiler_params=_params(0, 1))(*ins)


def _cast_bf16(x2d):
    t, d = x2d.shape
    tm = _tile(t, 512, 8)

    def body(x_ref, o_ref):
        o_ref[...] = x_ref[...].astype(BF16)

    row = pl.BlockSpec((tm, d), lambda i: (i, 0))
    return pl.pallas_call(body, name="cast_bf16", grid=(t // tm,), in_specs=[row], out_specs=row,
                          out_shape=_sds((t, d), BF16), compiler_params=_params(1, 0))(x2d)


def _lower_bound(table):
    t0, t1 = table[0:1, :], table[1:2, :]
    m = jnp.maximum(t0, t1)
    e0, e1 = jnp.exp(t0 - m), jnp.exp(t1 - m)
    return e0 / (e0 + e1)


def _hgrn_chunk(hq, hf, hi, lb, st, tri, mask):
    q = _silu(hq)
    f = lb + (1.0 - lb) * _sigmoid(hf)
    lg = jnp.log(f)
    cum = _dot(tri, lg, NN, precision=lax.Precision.HIGHEST)
    tot = jnp.sum(lg, axis=0, keepdims=True)
    kk = 1.0 - f
    qd = (q * jnp.exp(cum)).astype(BF16)
    kd = (kk * jnp.exp(-cum)).astype(BF16)
    ke = (kk * jnp.exp(tot - cum)).astype(BF16)
    vb = hi.astype(BF16)
    a = jnp.where(mask, _dot(qd, kd, NT), 0.0)
    o = _dot(a.astype(BF16), vb, NN) + _dot(qd, st.astype(BF16), NT)
    st_next = st * jnp.exp(tot) + _dot(vb, ke, TN)
    return o, st_next


def _tri_masks(c):
    r = lax.broadcasted_iota(jnp.int32, (c, c), 0)
    s = lax.broadcasted_iota(jnp.int32, (c, c), 1)
    lower, upper = r >= s, r <= s
    return lower, upper


def _hgrn_fwd(proj3, lb_f, lb_b, norm_g, d):
    b_, s_, _ = proj3.shape
    nh = d // HEAD
    c = HGRN_CHUNK
    n_chunks = s_ // c

    def body(hq_ref, hff_ref, hfb_ref, hi_ref, hog_ref, lbf_ref, lbb_ref, g_ref, oraw_ref, out_ref, of_ref, ob_ref):
        lower, upper = _tri_masks(c)
        tril, triu = lower.astype(F32), upper.astype(F32)
        lbf, lbb = _lower_bound(lbf_ref[...]), _lower_bound(lbb_ref[...])

        def step(j, carry):
            st_f, st_b = carry
            rf = pl.ds(pl.multiple_of(j * c, c), c)
            rb = pl.ds(pl.multiple_of((n_chunks - 1 - j) * c, c), c)
            o_f, st_f = _hgrn_chunk(hq_ref[rf, :], hff_ref[rf, :], hi_ref[rf, :], lbf, st_f, tril, lower)
            o_b, st_b = _hgrn_chunk(hq_ref[rb, :], hfb_ref[rb, :], hi_ref[rb, :], lbb, st_b, triu, upper)
            of_ref[rf, :] = o_f
            ob_ref[rb, :] = o_b
            return st_f, st_b

        z = jnp.zeros((HEAD, HEAD), F32)
        lax.fori_loop(0, n_chunks, step, (z, z))
        o = of_ref[...] + ob_ref[...]
        oraw_ref[...] = o
        on = o * lax.rsqrt(jnp.mean(o * o, axis=-1, keepdims=True) + LN_EPS)
        out_ref[...] = (on * g_ref[...] * _silu(hog_ref[...])).astype(BF16)

    def col(k):
        return pl.BlockSpec((None, s_, HEAD), lambda h, b: (b, 0, k * nh + h))

    tab = pl.BlockSpec((2, HEAD), lambda h, b: (0, h))
    oblk = pl.BlockSpec((None, s_, HEAD), lambda h, b: (b, 0, h))
    return pl.pallas_call(
        body, name="hgrn_fwd", grid=(nh, b_),
        in_specs=[col(0), col(1), col(2), col(3), col(4), tab, tab, pl.BlockSpec((1, HEAD), lambda h, b: (0, h))],
        out_specs=[oblk, oblk], out_shape=[_sds((b_, s_, d), F32), _sds((b_, s_, d), BF16)],
        scratch_shapes=[pltpu.VMEM((s_, HEAD), F32), pltpu.VMEM((s_, HEAD), F32)],
        compiler_params=_params(2, 0))(proj3, proj3, proj3, proj3, proj3, lb_f, lb_b, norm_g)


def _hgrn_bwd(proj3, lb_f, lb_b, norm_g, o_raw, do_a, d):
    b_, s_, _ = proj3.shape
    nh = d // HEAD
    c = HGRN_CHUNK
    n_chunks = s_ // c

    def body(hq_ref, hff_ref, hfb_ref, hi_ref, hog_ref, lbf_ref, lbb_ref, g_ref, oraw_ref, doa_ref,
             dq_ref, dff_ref, dfb_ref, di_ref, dog_ref, dlbf_ref, dlbb_ref, dg_ref,
             stf_ref, stb_ref, dor_ref, dq2_ref, di2_ref):
        b = pl.program_id(1)
        lower, upper = _tri_masks(c)
        tril, triu = lower.astype(F32), upper.astype(F32)
        tab_f, tab_b = lbf_ref[...], lbb_ref[...]
        lbf, lbb = _lower_bound(tab_f), _lower_bound(tab_b)

        o, doa, hog, g = oraw_ref[...], doa_ref[...], hog_ref[...], g_ref[...]
        rs = lax.rsqrt(jnp.mean(o * o, axis=-1, keepdims=True) + LN_EPS)
        on = o * rs
        sg = _sigmoid(hog)
        gate = hog * sg
        dog_ref[...] = (doa * on * g * sg * (1.0 + hog * (1.0 - sg))).astype(BF16)
        don = doa * g * gate
        dor_ref[...] = rs * (don - on * jnp.mean(don * on, axis=-1, keepdims=True))
        dg_part = jnp.sum(doa * on * gate, axis=0, keepdims=True)

        def fwd_step(j, carry):
            st_f, st_b = carry
            jb = n_chunks - 1 - j
            rf = pl.ds(pl.multiple_of(j * c, c), c)
            rb = pl.ds(pl.multiple_of(jb * c, c), c)
            stf_ref[j] = st_f
            stb_ref[jb] = st_b
            _, st_f = _hgrn_chunk(hq_ref[rf, :], hff_ref[rf, :], hi_ref[rf, :], lbf, st_f, tril, lower)
            _, st_b = _hgrn_chunk(hq_ref[rb, :], hfb_ref[rb, :], hi_ref[rb, :], lbb, st_b, triu, upper)
            return st_f, st_b

        z = jnp.zeros((HEAD, HEAD), F32)
        lax.fori_loop(0, n_chunks, fwd_step, (z, z))

        def bwd_step(j, carry):
            gs_f, gs_b, dl_f, dl_b = carry
            jf = n_chunks - 1 - j
            rf = pl.ds(pl.multiple_of(jf * c, c), c)
            rb = pl.ds(pl.multiple_of(j * c, c), c)
            _, vjp_f = jax.vjp(lambda a0, a1, a2, a3, a4: _hgrn_chunk(a0, a1, a2, a3, a4, tril, lower),
                               hq_ref[rf, :], hff_ref[rf, :], hi_ref[rf, :], lbf, stf_ref[jf])
            dq, df, di, dl, gs_f = vjp_f((dor_ref[rf, :], gs_f))
            dq_ref[rf, :] = dq.astype(BF16)
            dff_ref[rf, :] = df.astype(BF16)
            di_ref[rf, :] = di.astype(BF16)
            dl_f = dl_f + dl
            _, vjp_b = jax.vjp(lambda a0, a1, a2, a3, a4: _hgrn_chunk(a0, a1, a2, a3, a4, triu, upper),
                               hq_ref[rb, :], hfb_ref[rb, :], hi_ref[rb, :], lbb, stb_ref[j])
            dq, df, di, dl, gs_b = vjp_b((dor_ref[rb, :], gs_b))
            dq2_ref[rb, :] = dq
            dfb_ref[rb, :] = df.astype(BF16)
            di2_ref[rb, :] = di
            dl_b = dl_b + dl
            return gs_f, gs_b, dl_f, dl_b

        zl = jnp.zeros((1, HEAD), F32)
        _, _, dl_f, dl_b = lax.fori_loop(0, n_chunks, bwd_step, (z, z, zl, zl))
        dq_ref[...] = (dq_ref[...].astype(F32) + dq2_ref[...]).astype(BF16)
        di_ref[...] = (di_ref[...].astype(F32) + di2_ref[...]).astype(BF16)

        _, vjp_tf = jax.vjp(_lower_bound, tab_f)
        _, vjp_tb = jax.vjp(_lower_bound, tab_b)

        @pl.when(b == 0)
        def _():
            dlbf_ref[...] = jnp.zeros(dlbf_ref.shape, F32)
            dlbb_ref[...] = jnp.zeros(dlbb_ref.shape, F32)
            dg_ref[...] = jnp.zeros(dg_ref.shape, F32)

        dlbf_ref[...] += vjp_tf(dl_f)[0]
        dlbb_ref[...] += vjp_tb(dl_b)[0]
        dg_ref[...] += dg_part

    def col(k):
        return pl.BlockSpec((None, s_, HEAD), lambda h, b: (b, 0, k * nh + h))

    tab = pl.BlockSpec((2, HEAD), lambda h, b: (0, h))
    vec = pl.BlockSpec((1, HEAD), lambda h, b: (0, h))
    oblk = pl.BlockSpec((None, s_, HEAD), lambda h, b: (b, 0, h))
    seq = _sds((b_, s_, d), BF16)
    return pl.pallas_call(
        body, name="hgrn_bwd", grid=(nh, b_),
        in_specs=[col(0), col(1), col(2), col(3), col(4), tab, tab, vec, oblk, oblk],
        out_specs=[oblk, oblk, oblk, oblk, oblk, tab, tab, vec],
        out_shape=[seq, seq, seq, seq, seq, _sds((2, d), F32), _sds((2, d), F32), _sds((1, d), F32)],
        scratch_shapes=[pltpu.VMEM((n_chunks, HEAD, HEAD), F32), pltpu.VMEM((n_chunks, HEAD, HEAD), F32),
                        pltpu.VMEM((s_, HEAD), F32), pltpu.VMEM((s_, HEAD), F32), pltpu.VMEM((s_, HEAD), F32)],
        compiler_params=_params(1, 1))(proj3, proj3, proj3, proj3, proj3, lb_f, lb_b, norm_g, o_raw, do_a)


def _rope_tables(s_):
    inv = ROPE_THETA ** (-jnp.arange(0, ROPE_DIM, 2, dtype=F32) / ROPE_DIM)
    ang = jnp.arange(s_, dtype=F32)[:, None] * inv
    cos, sin = jnp.cos(ang), jnp.sin(ang)
    rest = HEAD - ROPE_DIM
    ctab = jnp.concatenate([cos, cos, jnp.ones((s_, rest), F32)], axis=1)
    stab = jnp.concatenate([-sin, sin, jnp.zeros((s_, rest), F32)], axis=1)
    half = ROPE_DIM // 2
    perm = np.zeros((HEAD, HEAD), np.float32)
    for i in range(half):
        perm[i + half, i] = 1.0
        perm[i, i + half] = 1.0
    return ctab, stab, jnp.asarray(perm)


def _attn_tile(qr, kr, v, cq, sq, ck, sk, perm, qi0, kj0, half):
    hi = lax.Precision.HIGHEST
    q = qr * cq + _dot(qr, perm, NN, precision=hi) * sq
    k = kr * ck + _dot(kr, perm, NN, precision=hi) * sk
    s = _dot(q.astype(BF16), k.astype(BF16), NT) * (HEAD ** -0.5)
    qi = qi0 + lax.broadcasted_iota(jnp.int32, s.shape, 0)
    kj = kj0 + lax.broadcasted_iota(jnp.int32, s.shape, 1)
    s = jnp.where(jnp.abs(qi - kj) <= half, s, NEG_INF)
    m = lax.stop_gradient(jnp.max(s, axis=-1, keepdims=True))
    p = jnp.exp(s - m)
    den = jnp.sum(p, axis=-1, keepdims=True)
    o = _dot(p.astype(BF16), v.astype(BF16), NN) / den
    lse = m + jnp.log(den)
    return o, jnp.broadcast_to(lse, o.shape)


def _attn_tiling(seg):
    tq = seg if seg <= 256 else 256
    kw = seg if seg <= 512 else 512
    tiles = []
    for i in range(seg // tq):
        ws = min(max(i * tq - (kw - tq) // 2, 0), seg - kw)
        tiles.append((i * tq, ws))
    return tq, kw, tiles


def _attn_specs(b_, dil, seg):
    qkv = pl.BlockSpec((None, None, 3, None, seg, HEAD), lambda b, r, h: (b, r, 0, h, 0, 0))
    tab = pl.BlockSpec((None, seg, HEAD), lambda b, r, h: (r, 0, 0))
    perm = pl.BlockSpec((HEAD, HEAD), lambda b, r, h: (0, 0))
    oblk = pl.BlockSpec((None, None, None, seg, HEAD), lambda b, r, h: (b, r, h, 0, 0))
    return qkv, tab, perm, oblk


def _attn_fwd(qkv_g, ctab, stab, perm, window, dil):
    b_, _, _, nh, seg, _ = qkv_g.shape
    half = window // (2 * dil)
    tq, kw, tiles = _attn_tiling(seg)

    def body(qkv_ref, c_ref, s_ref, p_ref, o_ref, l_ref):
        pm = p_ref[...]
        for q0, k0 in tiles:
            rq, rk = pl.ds(q0, tq), pl.ds(k0, kw)
            o, l = _attn_tile(qkv_ref[0, rq, :], qkv_ref[1, rk, :], qkv_ref[2, rk, :], c_ref[rq, :], s_ref[rq, :],
                              c_ref[rk, :], s_ref[rk, :], pm, q0, k0, half)
            o_ref[rq, :] = o
            l_ref[rq, :] = l

    qkv, tab, pspec, oblk = _attn_specs(b_, dil, seg)
    shp = _sds((b_, dil, nh, seg, HEAD), F32)
    return pl.pallas_call(
        body, name=f"attn_fwd_d{dil}", grid=(b_, dil, nh), in_specs=[qkv, tab, tab, pspec],
        out_specs=[oblk, oblk], out_shape=[shp, shp], compiler_params=_params(3, 0))(qkv_g, ctab, stab, perm)


def _attn_bwd(qkv_g, ctab, stab, perm, do, dl, window, dil):
    b_, _, _, nh, seg, _ = qkv_g.shape
    half = window // (2 * dil)
    tq, kw, tiles = _attn_tiling(seg)

    def body(qkv_ref, c_ref, s_ref, p_ref, do_ref, dl_ref, d_ref):
        pm = p_ref[...]
        d_ref[...] = jnp.zeros(d_ref.shape, F32)
        for q0, k0 in tiles:
            rq, rk = pl.ds(q0, tq), pl.ds(k0, kw)
            cq, sq, ck, sk = c_ref[rq, :], s_ref[rq, :], c_ref[rk, :], s_ref[rk, :]
            _, vjp = jax.vjp(lambda a, b, c: _attn_tile(a, b, c, cq, sq, ck, sk, pm, q0, k0, half),
                             qkv_ref[0, rq, :], qkv_ref[1, rk, :], qkv_ref[2, rk, :])
            dq, dk, dv = vjp((do_ref[rq, :], dl_ref[rq, :]))
            d_ref[0, rq, :] = dq
            d_ref[1, rk, :] += dk
            d_ref[2, rk, :] += dv

    qkv, tab, pspec, oblk = _attn_specs(b_, dil, seg)
    return pl.pallas_call(
        body, name=f"attn_bwd_d{dil}", grid=(b_, dil, nh), in_specs=[qkv, tab, tab, pspec, oblk, oblk],
        out_specs=qkv, out_shape=_sds(qkv_g.shape, F32), compiler_params=_params(3, 0))(
            qkv_g, ctab, stab, perm, do, dl)


def _combine(os_, ls_):
    m = jnp.maximum(jnp.maximum(ls_[0], ls_[1]), ls_[2])
    es = [jnp.exp(l - m) for l in ls_]
    return (es[0] * os_[0] + es[1] * os_[1] + es[2] * os_[2]) / (es[0] + es[1] + es[2])


def _combine_fwd(os_, ls_):
    t, w = os_[0].shape
    tm = _tile(t, 512, 8)

    def body(*refs):
        refs[6][...] = _combine([r[...] for r in refs[:3]], [r[...] for r in refs[3:6]]).astype(BF16)

    row = pl.BlockSpec((tm, w), lambda i: (i, 0))
    return pl.pallas_call(body, name="combine_fwd", grid=(t // tm,), in_specs=[row] * 6, out_specs=row,
                          out_shape=_sds((t, w), BF16), compiler_params=_params(1, 0))(*os_, *ls_)


def _combine_bwd(os_, ls_, dob):
    t, w = os_[0].shape
    tm = _tile(t, 512, 8)

    def body(*refs):
        _, vjp = jax.vjp(lambda *a: _combine(a[:3], a[3:]), *[r[...] for r in refs[:6]])
        for r, g in zip(refs[7:], vjp(refs[6][...])):
            r[...] = g

    row = pl.BlockSpec((tm, w), lambda i: (i, 0))
    return pl.pallas_call(body, name="combine_bwd", grid=(t // tm,), in_specs=[row] * 7, out_specs=[row] * 6,
                          out_shape=[_sds((t, w), F32)] * 6, compiler_params=_params(1, 0))(*os_, *ls_, dob)


def _to_residues(cols, b_, s_, dil):
    seg = s_ // dil
    return cols.reshape(b_, seg, dil, 3, ATTN_HEADS, HEAD).transpose(0, 2, 3, 4, 1, 5)


def _from_residues(a, b_, s_):
    if a.ndim == 6:
        return a.transpose(0, 4, 1, 2, 3, 5).reshape(b_ * s_, 3 * ATTN_HEADS * HEAD)
    return a.transpose(0, 3, 1, 2, 4).reshape(b_ * s_, ATTN_HEADS * HEAD)


def _local_step(x, target, w, p):
    b_, s_, d = x.shape
    t = b_ * s_
    x2 = x.reshape(t, d)
    gate_col = 5 * d + QKV_W

    xb = _cast_bf16(x2)
    u1, a1 = _ffn_up(xb, w["ffn1_in"])
    h1, h1b, xh1, rs1 = _down_ln(a1, w["ffn1_out"], x2, 0.5, p["ln1_g"], p["ln1_b"])

    proj = _mm_w3(h1b, w["mix_in"], F32)
    n_in = proj.shape[1]
    proj3 = proj.reshape(b_, s_, n_in)
    o_raw, oa = _hgrn_fwd(proj3, p["hgrn_lb_fwd"], p["hgrn_lb_bwd"], p["hgrn_norm_g"], d)
    oa2 = oa.reshape(t, d)

    ctab, stab, perm = _rope_tables(s_)
    qkv_gs, tabs, os_, ls_ = [], [], [], []
    for g, (window, dil) in enumerate(ATTN_GROUPS):
        c0 = 5 * d + g * 3 * ATTN_OUT
        qkv_g = _to_residues(proj[:, c0:c0 + 3 * ATTN_OUT], b_, s_, dil)
        seg = s_ // dil
        ct = ctab.reshape(seg, dil, HEAD).transpose(1, 0, 2)
        st = stab.reshape(seg, dil, HEAD).transpose(1, 0, 2)
        o_g, l_g = _attn_fwd(qkv_g, ct, st, perm, window, dil)
        qkv_gs.append(qkv_g)
        tabs.append((ct, st))
        os_.append(_from_residues(o_g, b_, s_))
        ls_.append(_from_residues(l_g, b_, s_))
    ob = _combine_fwd(os_, ls_)

    ya = _mm_w2(oa2, w["wa"], F32)
    yb, zb = _branch_gate(ob, w["wb"], ya, proj, gate_col)
    h2, h2b, xh2, rs2 = _down_ln(zb, w["mix_out"], h1, 1.0, p["ln2_g"], p["ln2_b"])

    u2, a2 = _ffn_up(h2b, w["ffn2_in"])
    h3, _, xh3, rs3 = _down_ln(a2, w["ffn2_out"], h2, 0.5, p["ln3_g"], p["ln3_b"])

    gw, gp = {}, {}
    nf = w["ffn2_in"].shape[2]

    def du_map(j, m):
        return (j // 2, m, j % 2)

    dres3, dy3, gp["ln3_g"], gp["ln3_b"], loss = _ln_bwd([h3], xh3, rs3, p["ln3_g"], 0.5, target.reshape(t, d))
    du2 = _swiglu_bwd(dy3, w["ffn2_out"], u2)
    gw["ffn2_out"] = _tn_w2(a2, dy3)
    dh2 = _ffn_dx(du2, w["ffn2_in"], dres3)
    gw["ffn2_in"] = _tn_w3(h2b, du2, du_map, nf)

    dres2, dmix, gp["ln2_g"], gp["ln2_b"], _ = _ln_bwd([dh2], xh2, rs2, p["ln2_g"], 1.0)
    dya, dyb, dpga, dpgb = _gate_bwd(dmix, w["mix_out"], proj, ya, yb, gate_col)
    gw["mix_out"] = _tn_w2(zb, dmix)
    do_a = _nt_w2(dya, w["wa"], F32)
    gw["wa"] = _tn_w2(oa2, dya)
    nb = w["wb"].shape[2]
    do_b = _nt_w3(dyb, w["wb"], None)
    gw["wb"] = _tn_w3(ob, dyb, lambda j, m: (m, j), nb)

    dq, dff, dfb, di, dog, gp["hgrn_lb_fwd"], gp["hgrn_lb_bwd"], gp["hgrn_norm_g"] = _hgrn_bwd(
        proj3, p["hgrn_lb_fwd"], p["hgrn_lb_bwd"], p["hgrn_norm_g"], o_raw, do_a.reshape(b_, s_, d), d)

    douts = _combine_bwd(os_, ls_, do_b)
    dqkv = []
    for g, (window, dil) in enumerate(ATTN_GROUPS):
        seg = s_ // dil
        do_g = douts[g].reshape(b_, seg, dil, ATTN_HEADS, HEAD).transpose(0, 2, 3, 1, 4)
        dl_g = douts[3 + g].reshape(b_, seg, dil, ATTN_HEADS, HEAD).transpose(0, 2, 3, 1, 4)
        dqkv_g = _attn_bwd(qkv_gs[g], tabs[g][0], tabs[g][1], perm, do_g, dl_g, window, dil)
        dqkv.append(_from_residues(dqkv_g, b_, s_).astype(BF16))

    dproj = jnp.concatenate(
        [a.reshape(t, d) for a in (dq, dff, dfb, di, dog)] + dqkv + [dpga, dpgb], axis=1)
    nm = w["mix_in"].shape[2]
    dh1 = _nt_w3(dproj, w["mix_in"], dres2)
    gw["mix_in"] = _tn_w3(h1b, dproj, lambda j, m: (m, j), nm)

    dres1, dy1, gp["ln1_g"], gp["ln1_b"], _ = _ln_bwd([dh1], xh1, rs1, p["ln1_g"], 0.5)
    du1 = _swiglu_bwd(dy1, w["ffn1_out"], u1)
    gw["ffn1_out"] = _tn_w2(a1, dy1)
    dx = _ffn_dx(du1, w["ffn1_in"], dres1)
    gw["ffn1_in"] = _tn_w3(xb, du1, du_map, nf)
    return loss, dx.reshape(b_, s_, d), gw, gp


MESH = pl.DeviceIdType.MESH
ANY = pl.BlockSpec(memory_space=pl.ANY)


def _place():
    x, y, c = lax.axis_index("x"), lax.axis_index("y"), lax.axis_index("c")
    chips = [(1 - x, y), (x, 1 - y), (1 - x, 1 - y)]
    return x, y, c, chips, (x, y, 1 - c)


def _half_rows(c, rows):
    hr = rows // 2
    return pl.ds(pl.multiple_of(c * hr, 16), hr)


def _remote(src, dst, send, recv, dev):
    return pltpu.make_async_remote_copy(src_ref=src, dst_ref=dst, send_sem=send, recv_sem=recv,
                                        device_id=dev, device_id_type=MESH)


def _gather_weights(shards):
    n = len(shards)

    def body(*refs):
        ins, outs = refs[:n], refs[n:2 * n]
        loc_sem, isend, irecv, fsend, frecv = refs[2 * n:]
        x, y, c, chips, sib = _place()
        s = 2 * x + y
        local, sends = [], []
        for i in range(n):
            mine = _half_rows(c, ins[i].shape[0])
            cp = pltpu.make_async_copy(ins[i], outs[i].at[s], loc_sem.at[i])
            cp.start()
            local.append(cp)
            for k, chip in enumerate(chips):
                cp = _remote(ins[i].at[mine], outs[i].at[s, mine], isend.at[i, k], irecv.at[i, k], (*chip, c))
                cp.start()
                sends.append(cp)
        for i in range(n):
            mine = _half_rows(c, ins[i].shape[0])
            for k, chip in enumerate(chips):
                blk = outs[i].at[2 * chip[0] + chip[1], mine]
                _remote(blk, blk, isend.at[i, k], irecv.at[i, k], (*chip, c)).wait_recv()
                cp = _remote(blk, blk, fsend.at[i, k], frecv.at[i, k], sib)
                cp.start()
                sends.append(cp)
        for i in range(n):
            other = _half_rows(1 - c, ins[i].shape[0])
            for k, chip in enumerate(chips):
                blk = outs[i].at[2 * chip[0] + chip[1], other]
                _remote(blk, blk, fsend.at[i, k], frecv.at[i, k], sib).wait_recv()
        for cp in sends:
            cp.wait_send()
        for cp in local:
            cp.wait()

    dma = pltpu.SemaphoreType.DMA
    return pl.pallas_call(
        body, name="gather_weights", in_specs=[ANY] * n, out_specs=[ANY] * n,
        out_shape=[_sds((N_SHARD,) + a.shape, a.dtype) for a in shards],
        scratch_shapes=[dma((n,)), dma((n, 3)), dma((n, 3)), dma((n, 3)), dma((n, 3))])(*shards)


def _swap_halves(grads):
    n = len(grads)

    def body(*refs):
        ins, outs, send, recv = refs[:n], refs[n:2 * n], refs[2 * n], refs[2 * n + 1]
        x, y, c, chips, sib = _place()
        cps = []
        for i in range(n):
            other = _half_rows(1 - c, ins[i].shape[1])
            cp = _remote(ins[i].at[:, other], outs[i], send.at[i], recv.at[i], sib)
            cp.start()
            cps.append(cp)
        for cp in cps:
            cp.wait()

    dma = pltpu.SemaphoreType.DMA
    return pl.pallas_call(
        body, name="swap_halves", in_specs=[ANY] * n, out_specs=[ANY] * n,
        out_shape=[_sds((N_SHARD, a.shape[1] // 2, a.shape[2]), a.dtype) for a in grads],
        scratch_shapes=[dma((n,)), dma((n,))])(*grads)


def _scatter_partials(parts):
    n = len(parts)

    def body(*refs):
        ins, outs, send, recv = refs[:n], refs[n:2 * n], refs[2 * n], refs[2 * n + 1]
        x, y, c, chips, sib = _place()
        cps = []
        for i in range(n):
            for k, chip in enumerate(chips):
                cp = _remote(ins[i].at[2 * chip[0] + chip[1]], outs[i].at[k], send.at[i, k], recv.at[i, k], (*chip, c))
                cp.start()
                cps.append(cp)
        for cp in cps:
            cp.wait()

    dma = pltpu.SemaphoreType.DMA
    return pl.pallas_call(
        body, name="scatter_partials", in_specs=[ANY] * n, out_specs=[ANY] * n,
        out_shape=[_sds((3,) + a.shape[1:], a.dtype) for a in parts],
        scratch_shapes=[dma((n, 3)), dma((n, 3))])(*parts)


def _join_halves(halves):
    n = len(halves)

    def body(*refs):
        ins, outs, loc, send, recv = refs[:n], refs[n:2 * n], refs[2 * n], refs[2 * n + 1], refs[2 * n + 2]
        x, y, c, chips, sib = _place()
        cps = []
        for i in range(n):
            mine = _half_rows(c, outs[i].shape[0])
            lc = pltpu.make_async_copy(ins[i], outs[i].at[mine], loc.at[i])
            lc.start()
            cp = _remote(ins[i], outs[i].at[mine], send.at[i], recv.at[i], sib)
            cp.start()
            cps += [lc, cp]
        for cp in cps:
            cp.wait()

    dma = pltpu.SemaphoreType.DMA
    return pl.pallas_call(
        body, name="join_halves", in_specs=[ANY] * n, out_specs=[ANY] * n,
        out_shape=[_sds((2 * a.shape[0], a.shape[1]), a.dtype) for a in halves],
        scratch_shapes=[dma((n,)), dma((n,)), dma((n,))])(*halves)


def _gather_rows(block):
    m_per, n = block.shape

    def body(x_ref, out_ref, send_sems, recv_sems, local_sem):
        x, y, c, chips, sibling = _place()
        me = (x, y, c)

        def rows(px, py, pc):
            return out_ref.at[pl.ds((4 * px + 2 * py + pc) * m_per, m_per), :]

        def copy(k, blk, to, src=None):
            return _remote(rows(*blk) if src is None else src, rows(*blk), send_sems.at[k], recv_sems.at[k], to)

        mine = pltpu.make_async_copy(x_ref, rows(*me), local_sem)
        mine.start()
        first = [copy(0, me, sibling, src=x_ref)]
        first += [copy(1 + j, me, (*chip, c), src=x_ref) for j, chip in enumerate(chips)]
        for cp in first:
            cp.start()
        passed = [copy(4 + j, (*chip, c), sibling) for j, chip in enumerate(chips)]
        for j, chip in enumerate(chips):
            copy(1 + j, (*chip, c), me).wait_recv()
            passed[j].start()
        copy(0, sibling, me).wait_recv()
        for j, chip in enumerate(chips):
            copy(4 + j, (*chip, 1 - c), me).wait_recv()
        for cp in first + passed:
            cp.wait_send()
        mine.wait()

    vmem = pl.BlockSpec(memory_space=pltpu.VMEM)
    dma = pltpu.SemaphoreType.DMA
    return pl.pallas_call(
        body, name="gather_rows", in_specs=[vmem], out_specs=vmem, out_shape=_sds((8 * m_per, n), block.dtype),
        scratch_shapes=[dma((7,)), dma((7,)), dma(())])(block)


def _row_tile(rows, cols):
    return _tile(rows, max(16, (1 << 20) // cols), 16)


def _sum_halves(grad, got, c_arr):
    _, hr, cols = got.shape
    tr = _row_tile(hr, cols)
    nb = hr // tr

    def body(c_ref, a_ref, b_ref, o_ref):
        o_ref[...] = (a_ref[...].astype(F32) + b_ref[...].astype(F32)).astype(BF16)

    blk = pl.BlockSpec((None, tr, cols), lambda s, i, c_ref: (s, i, 0))
    return pl.pallas_call(
        body, name="sum_halves",
        grid_spec=pltpu.PrefetchScalarGridSpec(
            num_scalar_prefetch=1, grid=(N_SHARD, nb),
            in_specs=[pl.BlockSpec((None, tr, cols), lambda s, i, c_ref: (s, c_ref[0] * nb + i, 0)), blk],
            out_specs=blk),
        out_shape=_sds(got.shape, BF16), compiler_params=_params(2, 0))(c_arr, grad, got)


def _sum_chips(part, got, s_arr):
    _, hr, cols = got.shape
    tr = _row_tile(hr, cols)

    def body(s_ref, a_ref, b_ref, o_ref):
        o_ref[...] = ((a_ref[...].astype(F32) + b_ref[0].astype(F32)) + b_ref[1].astype(F32)) + b_ref[2].astype(F32)

    return pl.pallas_call(
        body, name="sum_chips",
        grid_spec=pltpu.PrefetchScalarGridSpec(
            num_scalar_prefetch=1, grid=(hr // tr,),
            in_specs=[pl.BlockSpec((None, tr, cols), lambda i, s_ref: (s_ref[0], i, 0)),
                      pl.BlockSpec((3, tr, cols), lambda i, s_ref: (0, i, 0))],
            out_specs=pl.BlockSpec((tr, cols), lambda i, s_ref: (i, 0))),
        out_shape=_sds((hr, cols), F32), compiler_params=_params(1, 0))(s_arr, part, got)


def _adam_math(w, g, m, v):
    m = ADAM_B1 * m + (1.0 - ADAM_B1) * g
    v = ADAM_B2 * v + (1.0 - ADAM_B2) * (g * g)
    m_hat = m / (1.0 - ADAM_B1 ** ADAM_STEP)
    v_hat = v / (1.0 - ADAM_B2 ** ADAM_STEP)
    delta = -ADAM_LR * (m_hat / (jnp.sqrt(v_hat) + ADAM_EPS) + ADAM_WD * w)
    return delta, m, v


def _adamw(w, g, m, v):
    rows, cols = w.shape
    tr = _tile(rows, max(8, (1 << 19) // cols), 8)

    def body(w_ref, g_ref, m_ref, v_ref, go_ref, d_ref, mo_ref, vo_ref):
        g_ = g_ref[...]
        go_ref[...] = g_
        d_ref[...], mo_ref[...], vo_ref[...] = _adam_math(w_ref[...], g_, m_ref[...], v_ref[...])

    blk = pl.BlockSpec((tr, cols), lambda i: (i, 0))
    return pl.pallas_call(
        body, name="adamw", grid=(rows // tr,), in_specs=[blk] * 4, out_specs=[blk] * 4,
        out_shape=[_sds((rows, cols), F32)] * 4, compiler_params=_params(1, 0))(w, g, m, v)


def _adamw_small(gathered, w, m, v):
    rows, cols = w.shape

    def body(a_ref, w_ref, m_ref, v_ref, go_ref, d_ref, mo_ref, vo_ref):
        g_ = a_ref[pl.ds(0, rows), :]
        for k in range(1, 8):
            g_ = g_ + a_ref[pl.ds(k * rows, rows), :]
        go_ref[...] = g_
        d_ref[...], mo_ref[...], vo_ref[...] = _adam_math(w_ref[...], g_, m_ref[...], v_ref[...])

    vmem = pl.BlockSpec(memory_space=pltpu.VMEM)
    return pl.pallas_call(
        body, name="adamw_small", in_specs=[vmem] * 4, out_specs=[vmem] * 4,
        out_shape=[_sds((rows, cols), F32)] * 4)(gathered, w, m, v)


BIG = ("ffn1_w_in", "ffn1_w_out", "mix_w_in", "w_branch_a", "w_branch_b", "mix_w_out", "ffn2_w_in", "ffn2_w_out")
BIG_KEY = {"ffn1_w_in": "ffn1_in", "ffn1_w_out": "ffn1_out", "mix_w_in": "mix_in", "w_branch_a": "wa",
           "w_branch_b": "wb", "mix_w_out": "mix_out", "ffn2_w_in": "ffn2_in", "ffn2_w_out": "ffn2_out"}
SMALL = ("ln1_g", "ln1_b", "hgrn_lb_fwd", "hgrn_lb_bwd", "hgrn_norm_g", "ln2_g", "ln2_b", "ln3_g", "ln3_b")
ORDER = ("ffn1_w_in", "ffn1_w_out", "ln1_g", "ln1_b", "mix_w_in", "hgrn_lb_fwd", "hgrn_lb_bwd", "hgrn_norm_g",
         "w_branch_a", "w_branch_b", "mix_w_out", "ln2_g", "ln2_b", "ffn2_w_in", "ffn2_w_out", "ln3_g", "ln3_b")
SMALL_ROWS = 16


def _pack_small(d):
    rows = jnp.concatenate([d[k].reshape(-1, d[k].shape[-1]) for k in SMALL], axis=0)
    return jnp.pad(rows, ((0, SMALL_ROWS - rows.shape[0]), (0, 0)))


def _unpack_small(a, like):
    out, r = {}, 0
    for k in SMALL:
        n = like[k].shape[0]
        out[k] = a[r:r + n].reshape(like[k].shape)
        r += n
    return out


def kernel(x, ffn1_w_in, ffn1_w_out, ln1_g, ln1_b, mix_w_in, hgrn_lb_fwd, hgrn_lb_bwd, hgrn_norm_g, w_branch_a, w_branch_b, mix_w_out, ln2_g, ln2_b, ffn2_w_in, ffn2_w_out, ln3_g, ln3_b, loss_target, m_ffn1_w_in, m_ffn1_w_out, m_ln1_g, m_ln1_b, m_mix_w_in, m_hgrn_lb_fwd, m_hgrn_lb_bwd, m_hgrn_norm_g, m_w_branch_a, m_w_branch_b, m_mix_w_out, m_ln2_g, m_ln2_b, m_ffn2_w_in, m_ffn2_w_out, m_ln3_g, m_ln3_b, v_ffn1_w_in, v_ffn1_w_out, v_ln1_g, v_ln1_b, v_mix_w_in, v_hgrn_lb_fwd, v_hgrn_lb_bwd, v_hgrn_norm_g, v_w_branch_a, v_w_branch_b, v_mix_w_out, v_ln2_g, v_ln2_b, v_ffn2_w_in, v_ffn2_w_out, v_ln3_g, v_ln3_b):
    wts = dict(ffn1_w_in=ffn1_w_in, ffn1_w_out=ffn1_w_out, ln1_g=ln1_g, ln1_b=ln1_b, mix_w_in=mix_w_in,
               hgrn_lb_fwd=hgrn_lb_fwd, hgrn_lb_bwd=hgrn_lb_bwd, hgrn_norm_g=hgrn_norm_g, w_branch_a=w_branch_a,
               w_branch_b=w_branch_b, mix_w_out=mix_w_out, ln2_g=ln2_g, ln2_b=ln2_b, ffn2_w_in=ffn2_w_in,
               ffn2_w_out=ffn2_w_out, ln3_g=ln3_g, ln3_b=ln3_b)
    mom = dict(ffn1_w_in=m_ffn1_w_in, ffn1_w_out=m_ffn1_w_out, ln1_g=m_ln1_g, ln1_b=m_ln1_b, mix_w_in=m_mix_w_in,
               hgrn_lb_fwd=m_hgrn_lb_fwd, hgrn_lb_bwd=m_hgrn_lb_bwd, hgrn_norm_g=m_hgrn_norm_g,
               w_branch_a=m_w_branch_a, w_branch_b=m_w_branch_b, mix_w_out=m_mix_w_out, ln2_g=m_ln2_g, ln2_b=m_ln2_b,
               ffn2_w_in=m_ffn2_w_in, ffn2_w_out=m_ffn2_w_out, ln3_g=m_ln3_g, ln3_b=m_ln3_b)
    var = dict(ffn1_w_in=v_ffn1_w_in, ffn1_w_out=v_ffn1_w_out, ln1_g=v_ln1_g, ln1_b=v_ln1_b, mix_w_in=v_mix_w_in,
               hgrn_lb_fwd=v_hgrn_lb_fwd, hgrn_lb_bwd=v_hgrn_lb_bwd, hgrn_norm_g=v_hgrn_norm_g,
               w_branch_a=v_w_branch_a, w_branch_b=v_w_branch_b, mix_w_out=v_mix_w_out, ln2_g=v_ln2_g, ln2_b=v_ln2_b,
               ffn2_w_in=v_ffn2_w_in, ffn2_w_out=v_ffn2_w_out, ln3_g=v_ln3_g, ln3_b=v_ln3_b)
    c_arr = lax.axis_index("c").astype(jnp.int32).reshape(1)
    s_arr = (2 * lax.axis_index("x") + lax.axis_index("y")).astype(jnp.int32).reshape(1)

    shard2d = {k: wts[k].reshape(wts[k].shape[1:]) for k in BIG}
    full = _gather_weights([_cast_bf16(shard2d[k]) for k in BIG])
    w = {}
    for k, a in zip(BIG, full):
        row_sharded = k in ("ffn1_w_out", "ffn2_w_out", "w_branch_a", "mix_w_out")
        w[BIG_KEY[k]] = a.reshape(a.shape[0] * a.shape[1], a.shape[2]) if row_sharded else a
    p = {k: wts[k] for k in SMALL}

    loss, grad_x, gw, gp = _local_step(x, loss_target, w, p)
    loss = lax.psum(loss[0, 0], ("x", "y", "c"))

    g3 = [gw[BIG_KEY[k]].reshape((N_SHARD,) + shard2d[k].shape) for k in BIG]
    got = _swap_halves(g3)
    parts = [_sum_halves(a, b, c_arr) for a, b in zip(g3, got)]
    got = _scatter_partials(parts)
    halves = [_sum_chips(a, b, s_arr) for a, b in zip(parts, got)]
    grads = _join_halves(halves)

    out_g, out_d, out_m, out_v = {}, {}, {}, {}
    for k, g in zip(BIG, grads):
        shp = wts[k].shape
        res = _adamw(shard2d[k], g, mom[k].reshape(shp[1:]), var[k].reshape(shp[1:]))
        out_g[k], out_d[k], out_m[k], out_v[k] = [a.reshape(shp) for a in res]

    gathered = _gather_rows(_pack_small(gp))
    res = _adamw_small(gathered, _pack_small(wts), _pack_small(mom), _pack_small(var))
    for dst, a in zip((out_g, out_d, out_m, out_v), res):
        dst.update(_unpack_small(a, wts))

    return (loss, grad_x, *[out_g[k] for k in ORDER], *[out_d[k] for k in ORDER],
            *[out_m[k] for k in ORDER], *[out_v[k] for k in ORDER])
```

```python
import functools

import numpy as np
import jax
import jax.numpy as jnp
from jax import lax
from jax.experimental import pallas as pl
from jax.experimental.pallas import tpu as pltpu

F32 = jnp.float32
BF16 = jnp.bfloat16

HEAD = 128
ATTN_GROUPS = ((128, 1), (512, 4), (2048, 16))
ATTN_HEADS = 4
N_GROUPS = len(ATTN_GROUPS)
QKV_W = N_GROUPS * 3 * ATTN_HEADS * HEAD
ATTN_OUT = ATTN_HEADS * HEAD
ROPE_THETA = 500000.0
ROPE_DIM = HEAD // 4
HGRN_CHUNK = 32
HGRN_FWD_HEADS = 2
HGRN_BWD_HEADS = 1
HGRN_SUB = 4
HGRN_UNROLL = 2
ALPHA = 2.0 ** 0.25
LN_EPS = 1e-5
NEG_INF = -1e30
ADAM_LR, ADAM_B1, ADAM_B2, ADAM_EPS, ADAM_WD, ADAM_STEP = 0.001, 0.9, 0.999, 1e-08, 0.01, 10

N_SHARD = 4
VMEM_LIMIT = 56 * 1024 * 1024

NN = ((1,), (0,))
NT = ((1,), (1,))
TN = ((0,), (0,))


def _dot(a, b, dims, precision=None):
    return lax.dot_general(a, b, (dims, ((), ())), preferred_element_type=F32, precision=precision)


def _tile(n, pref, mult=128):
    best = None
    for t in range(mult, min(n, pref) + 1, mult):
        if n % t == 0:
            best = t
    return n if best is None else best


def _params(n_parallel, n_arbitrary):
    return pltpu.CompilerParams(
        dimension_semantics=("parallel",) * n_parallel + ("arbitrary",) * n_arbitrary,
        vmem_limit_bytes=VMEM_LIMIT)


def _sigmoid(x):
    return 1.0 / (1.0 + jnp.exp(-x))


def _silu(x):
    return x * _sigmoid(x)


def _gemm(name, grid, ins, in_specs, outs, out_specs, accs, dot_fn, epi_fn):
    n_in, n_out, n_k = len(ins), len(outs), grid[-1]

    def body(*refs):
        in_refs, out_refs, acc_refs = refs[:n_in], refs[n_in:n_in + n_out], refs[n_in + n_out:]
        k = pl.program_id(len(grid) - 1)

        @pl.when(k == 0)
        def _():
            for a in acc_refs:
                a[...] = jnp.zeros(a.shape, F32)

        dot_fn(in_refs, acc_refs)

        @pl.when(k == n_k - 1)
        def _():
            epi_fn(in_refs, acc_refs, out_refs)

    res = pl.pallas_call(
        body, name=name, grid=grid, in_specs=in_specs, out_specs=out_specs, out_shape=outs,
        scratch_shapes=[pltpu.VMEM(s, F32) for s in accs],
        compiler_params=_params(len(grid) - 1, 1))(*ins)
    return res


def _sds(shape, dtype):
    return jax.ShapeDtypeStruct(shape, dtype)


def _ffn_up(xb, w3):
    t, d = xb.shape
    nf = w3.shape[2]
    f = 2 * nf
    tm, tk = _tile(t, 512, 8), _tile(d, 512)

    def dot_fn(r, acc):
        acc[0][...] += _dot(r[0][...], r[1][...], NN)
        acc[1][...] += _dot(r[0][...], r[2][...], NN)

    def epi_fn(r, acc, out):
        g, u = acc[0][...], acc[1][...]
        out[0][0] = g.astype(BF16)
        out[0][1] = u.astype(BF16)
        out[1][...] = (_silu(g) * u).astype(BF16)

    return _gemm(
        "ffn_up", (t // tm, 2, d // tk), [xb, w3, w3],
        [pl.BlockSpec((tm, tk), lambda i, j, k: (i, k)),
         pl.BlockSpec((None, tk, nf), lambda i, j, k: (j, k, 0)),
         pl.BlockSpec((None, tk, nf), lambda i, j, k: (j + 2, k, 0))],
        [_sds((2, t, f), BF16), _sds((t, f), BF16)],
        [pl.BlockSpec((2, tm, nf), lambda i, j, k: (0, i, j)),
         pl.BlockSpec((tm, nf), lambda i, j, k: (i, j))],
        [(tm, nf), (tm, nf)], dot_fn, epi_fn)


def _down_ln(a, w, resid, coef, g, b):
    t, kd = a.shape
    d = w.shape[1]
    tm, tk = _tile(t, 512, 8), _tile(kd, 512)

    def dot_fn(r, acc):
        acc[0][...] += _dot(r[0][...], r[1][...], NN)

    def epi_fn(r, acc, out):
        v = ALPHA * r[2][...] + coef * acc[0][...]
        mu = jnp.mean(v, axis=-1, keepdims=True)
        c = v - mu
        var = jnp.mean(c * c, axis=-1, keepdims=True)
        rstd = lax.rsqrt(var + LN_EPS)
        xhat = c * rstd
        h = xhat * r[3][...] + r[4][...]
        out[0][...] = h
        out[1][...] = h.astype(BF16)
        out[2][...] = xhat
        out[3][...] = rstd

    row = pl.BlockSpec((tm, d), lambda i, k: (i, 0))
    vec = pl.BlockSpec((1, d), lambda i, k: (0, 0))
    return _gemm(
        "down_ln", (t // tm, kd // tk), [a, w, resid, g, b],
        [pl.BlockSpec((tm, tk), lambda i, k: (i, k)), pl.BlockSpec((tk, d), lambda i, k: (k, 0)), row, vec, vec],
        [_sds((t, d), F32), _sds((t, d), BF16), _sds((t, d), F32), _sds((t, 1), F32)],
        [row, row, row, pl.BlockSpec((tm, 1), lambda i, k: (i, 0))],
        [(tm, d)], dot_fn, epi_fn)


def _mm_w3(a, w3, out_dtype):
    t, kd = a.shape
    n = w3.shape[2]
    tm, tk = _tile(t, 512, 8), _tile(kd, 256)

    def dot_fn(r, acc):
        acc[0][...] += _dot(r[0][...], r[1][...], NN)

    def epi_fn(r, acc, out):
        out[0][...] = acc[0][...].astype(out_dtype)

    return _gemm(
        "mm_w3", (t // tm, N_SHARD, kd // tk), [a, w3],
        [pl.BlockSpec((tm, tk), lambda i, j, k: (i, k)), pl.BlockSpec((None, tk, n), lambda i, j, k: (j, k, 0))],
        [_sds((t, N_SHARD * n), out_dtype)], [pl.BlockSpec((tm, n), lambda i, j, k: (i, j))],
        [(tm, n)], dot_fn, epi_fn)[0]


def _mm_w2(a, w, out_dtype):
    t, kd = a.shape
    n = w.shape[1]
    tm, tn, tk = _tile(t, 512, 8), _tile(n, 1024), _tile(kd, 512)

    def dot_fn(r, acc):
        acc[0][...] += _dot(r[0][...], r[1][...], NN)

    def epi_fn(r, acc, out):
        out[0][...] = acc[0][...].astype(out_dtype)

    return _gemm(
        "mm_w2", (t // tm, n // tn, kd // tk), [a, w],
        [pl.BlockSpec((tm, tk), lambda i, j, k: (i, k)), pl.BlockSpec((tk, tn), lambda i, j, k: (k, j))],
        [_sds((t, n), out_dtype)], [pl.BlockSpec((tm, tn), lambda i, j, k: (i, j))],
        [(tm, tn)], dot_fn, epi_fn)[0]


def _branch_gate(ob, wb3, ya, proj, gate_col):
    t, kd = ob.shape
    n = wb3.shape[2]
    d = N_SHARD * n
    tm = _tile(t, 512, 8)
    ga0, gb0 = gate_col // n, (gate_col + d) // n

    def dot_fn(r, acc):
        acc[0][...] += _dot(r[0][...], r[1][...], NN)

    def epi_fn(r, acc, out):
        yb = acc[0][...]
        out[0][...] = yb
        out[1][...] = (_sigmoid(r[3][...]) * r[2][...] + _sigmoid(r[4][...]) * yb).astype(BF16)

    blk = pl.BlockSpec((tm, n), lambda i, j, k: (i, j))
    return _gemm(
        "branch_gate", (t // tm, N_SHARD, 1), [ob, wb3, ya, proj, proj],
        [pl.BlockSpec((tm, kd), lambda i, j, k: (i, 0)), pl.BlockSpec((None, kd, n), lambda i, j, k: (j, 0, 0)), blk,
         pl.BlockSpec((tm, n), lambda i, j, k: (i, ga0 + j)), pl.BlockSpec((tm, n), lambda i, j, k: (i, gb0 + j))],
        [_sds((t, d), F32), _sds((t, d), BF16)], [blk, blk], [(tm, n)], dot_fn, epi_fn)


def _swiglu_bwd(dyb, w, u3):
    t, d = dyb.shape
    f = w.shape[0]
    tm, tr = _tile(t, 1024, 8), _tile(f, 512)

    def dot_fn(r, acc):
        acc[0][...] += _dot(r[0][...], r[1][...], NT)

    def epi_fn(r, acc, out):
        da = acc[0][...]
        g, u = r[2][0].astype(F32), r[2][1].astype(F32)
        s = _sigmoid(g)
        out[0][0] = (da * u * s * (1.0 + g * (1.0 - s))).astype(BF16)
        out[0][1] = (da * g * s).astype(BF16)

    ublk = pl.BlockSpec((2, tm, tr), lambda i, j, k: (0, i, j))
    return _gemm(
        "swiglu_bwd", (t // tm, f // tr, 1), [dyb, w, u3],
        [pl.BlockSpec((tm, d), lambda i, j, k: (i, 0)), pl.BlockSpec((tr, d), lambda i, j, k: (j, 0)), ublk],
        [_sds((2, t, f), BF16)], [ublk], [(tm, tr)], dot_fn, epi_fn)[0]


def _ffn_dx(du3, w3, resid):
    t = du3.shape[1]
    d, nf = w3.shape[1], w3.shape[2]
    tm, tr = _tile(t, 512, 8), _tile(d, 1024)

    def dot_fn(r, acc):
        acc[0][...] += _dot(r[0][...], r[1][...], NT)

    def epi_fn(r, acc, out):
        out[0][...] = acc[0][...] + r[2][...]

    blk = pl.BlockSpec((tm, tr), lambda i, j, k: (i, j))
    return _gemm(
        "ffn_dx", (t // tm, d // tr, N_SHARD), [du3, w3, resid],
        [pl.BlockSpec((None, tm, nf), lambda i, j, k: (k // 2, i, k % 2)),
         pl.BlockSpec((None, tr, nf), lambda i, j, k: (k, j, 0)), blk],
        [_sds((t, d), F32)], [blk], [(tm, tr)], dot_fn, epi_fn)[0]


def _nt_w3(dy, w3, resid):
    t = dy.shape[0]
    kd, n = w3.shape[1], w3.shape[2]
    tm, tr = _tile(t, 512, 8), _tile(kd, 1024)
    has_res = resid is not None

    def dot_fn(r, acc):
        acc[0][...] += _dot(r[0][...], r[1][...], NT)

    def epi_fn(r, acc, out):
        v = acc[0][...]
        if has_res:
            v = v + r[2][...]
        out[0][...] = v

    blk = pl.BlockSpec((tm, tr), lambda i, j, k: (i, j))
    return _gemm(
        "nt_w3", (t // tm, kd // tr, N_SHARD), [dy, w3] + ([resid] if has_res else []),
        [pl.BlockSpec((tm, n), lambda i, j, k: (i, k)), pl.BlockSpec((None, tr, n), lambda i, j, k: (k, j, 0))]
        + ([blk] if has_res else []),
        [_sds((t, kd), F32)], [blk], [(tm, tr)], dot_fn, epi_fn)[0]


def _nt_w2(dy, w, out_dtype):
    t, n = dy.shape
    r_ = w.shape[0]
    tm, tr, tk = _tile(t, 512, 8), _tile(r_, 1024), _tile(n, 1024)

    def dot_fn(r, acc):
        acc[0][...] += _dot(r[0][...], r[1][...], NT)

    def epi_fn(r, acc, out):
        out[0][...] = acc[0][...].astype(out_dtype)

    return _gemm(
        "nt_w2", (t // tm, r_ // tr, n // tk), [dy, w],
        [pl.BlockSpec((tm, tk), lambda i, j, k: (i, k)), pl.BlockSpec((tr, tk), lambda i, j, k: (j, k))],
        [_sds((t, r_), out_dtype)], [pl.BlockSpec((tm, tr), lambda i, j, k: (i, j))],
        [(tm, tr)], dot_fn, epi_fn)[0]


def _gate_bwd(dmix, w, proj, ya, yb, gate_col):
    t, n = dmix.shape
    d = w.shape[0]
    tm, tr = _tile(t, 512, 8), _tile(d, 512)
    ga0, gb0 = gate_col // tr, (gate_col + d) // tr
    nb = d // tr

    def dot_fn(r, acc):
        acc[0][...] += _dot(r[0][...], r[1][...], NT)

    def epi_fn(r, acc, out):
        dz = acc[0][...]
        ga, gb = _sigmoid(r[2][...]), _sigmoid(r[3][...])
        out[0][...] = (dz * ga).astype(BF16)
        out[1][...] = (dz * gb).astype(BF16)
        out[2][...] = (dz * r[4][...] * ga * (1.0 - ga)).astype(BF16)
        out[3][...] = (dz * r[5][...] * gb * (1.0 - gb)).astype(BF16)

    blk = pl.BlockSpec((tm, tr), lambda i, j, k: (i, j))
    return _gemm(
        "gate_bwd", (t // tm, nb, 1), [dmix, w, proj, proj, ya, yb],
        [pl.BlockSpec((tm, n), lambda i, j, k: (i, 0)), pl.BlockSpec((tr, n), lambda i, j, k: (j, 0)),
         pl.BlockSpec((tm, tr), lambda i, j, k: (i, ga0 + j)), pl.BlockSpec((tm, tr), lambda i, j, k: (i, gb0 + j)),
         blk, blk],
        [_sds((t, d), BF16), _sds((t, d), BF16), _sds((t, d), BF16), _sds((t, d), BF16)],
        [blk, blk, blk, blk], [(tm, tr)], dot_fn, epi_fn)


def _tn_w3(a, dy, dy_map, n):
    t, kd = a.shape
    tm, tkk = _tile(t, 512, 8), _tile(kd, 512)

    def dot_fn(r, acc):
        acc[0][...] += _dot(r[0][...], r[1][...], TN)

    def epi_fn(r, acc, out):
        out[0][...] = acc[0][...].astype(BF16)

    dy_block = (tm, n) if dy.ndim == 2 else (None, tm, n)
    return _gemm(
        "tn_w3", (kd // tkk, N_SHARD, t // tm), [a, dy],
        [pl.BlockSpec((tm, tkk), lambda i, j, m: (m, i)), pl.BlockSpec(dy_block, lambda i, j, m: dy_map(j, m))],
        [_sds((N_SHARD, kd, n), BF16)], [pl.BlockSpec((None, tkk, n), lambda i, j, m: (j, i, 0))],
        [(tkk, n)], dot_fn, epi_fn)[0]


def _tn_w2(a, dy):
    t, kd = a.shape
    n = dy.shape[1]
    tm, tkk, tn = _tile(t, 512, 8), _tile(kd, 512), _tile(n, 2048)

    def dot_fn(r, acc):
        acc[0][...] += _dot(r[0][...], r[1][...], TN)

    def epi_fn(r, acc, out):
        out[0][...] = acc[0][...].astype(BF16)

    return _gemm(
        "tn_w2", (kd // tkk, n // tn, t // tm), [a, dy],
        [pl.BlockSpec((tm, tkk), lambda i, j, m: (m, i)), pl.BlockSpec((tm, tn), lambda i, j, m: (m, j))],
        [_sds((kd, n), BF16)], [pl.BlockSpec((tkk, tn), lambda i, j, m: (i, j))],
        [(tkk, tn)], dot_fn, epi_fn)[0]


def _ln_bwd(dh_parts, xhat, rstd, g, coef, target=None):
    t, d = xhat.shape
    tm = _tile(t, 256, 8)
    n_parts = len(dh_parts)
    with_loss = target is not None
    ins = list(dh_parts) + [xhat, rstd, g] + ([target] if with_loss else [])

    def body(*refs):
        parts = refs[:n_parts]
        xh_ref, rstd_ref, g_ref = refs[n_parts:n_parts + 3]
        o = n_parts + 3 + (1 if with_loss else 0)
        dres_ref, dyb_ref, dg_ref, db_ref, loss_ref = refs[o:o + 5]
        i = pl.program_id(0)
        dh = parts[0][...]
        for p in parts[1:]:
            dh = dh + p[...]
        if with_loss:
            e = dh - refs[o - 1][...]
            part = 0.5 * jnp.sum(jnp.sum(e * e, axis=-1, keepdims=True) * (1.0 / d), axis=0, keepdims=True)
            dh = e * (1.0 / d)
        else:
            part = jnp.zeros((1, 1), F32)
        xh = xh_ref[...]
        dxh = dh * g_ref[...]
        m1 = jnp.mean(dxh, axis=-1, keepdims=True)
        m2 = jnp.mean(dxh * xh, axis=-1, keepdims=True)
        dv = rstd_ref[...] * (dxh - m1 - xh * m2)
        dres_ref[...] = ALPHA * dv
        dyb_ref[...] = (coef * dv).astype(BF16)

        @pl.when(i == 0)
        def _():
            dg_ref[...] = jnp.zeros(dg_ref.shape, F32)
            db_ref[...] = jnp.zeros(db_ref.shape, F32)
            loss_ref[...] = jnp.zeros(loss_ref.shape, F32)

        dg_ref[...] += jnp.sum(dh * xh, axis=0, keepdims=True)
        db_ref[...] += jnp.sum(dh, axis=0, keepdims=True)
        loss_ref[...] += jnp.broadcast_to(part, loss_ref.shape)

    row = pl.BlockSpec((tm, d), lambda i: (i, 0))
    vec = pl.BlockSpec((1, d), lambda i: (0, 0))
    return pl.pallas_call(
        body, name="ln_bwd", grid=(t // tm,),
        in_specs=[row] * n_parts + [row, pl.BlockSpec((tm, 1), lambda i: (i, 0)), vec] + ([row] if with_loss else []),
        out_specs=[row, row, vec, vec, pl.BlockSpec((1, 128), lambda i: (0, 0))],
        out_shape=[_sds((t, d), F32), _sds((t, d), BF16), _sds((1, d), F32), _sds((1, d), F32), _sds((1, 128), F32)],
        compiler_params=_params(0, 1))(*ins)


def _cast_bf16(x2d):
    t, d = x2d.shape
    tm = _tile(t, 512, 8)

    def body(x_ref, o_ref):
        o_ref[...] = x_ref[...].astype(BF16)

    row = pl.BlockSpec((tm, d), lambda i: (i, 0))
    return pl.pallas_call(body, name="cast_bf16", grid=(t // tm,), in_specs=[row], out_specs=row,
                          out_shape=_sds((t, d), BF16), compiler_params=_params(1, 0))(x2d)


def _lower_bound(table):
    t0, t1 = table[0:1, :], table[1:2, :]
    m = jnp.maximum(t0, t1)
    e0, e1 = jnp.exp(t0 - m), jnp.exp(t1 - m)
    return e0 / (e0 + e1)


def _chunk_tri(rows, upper):
    r = lax.broadcasted_iota(jnp.int32, (rows, rows), 0)
    s = lax.broadcasted_iota(jnp.int32, (rows, rows), 1)
    shift = HGRN_CHUNK.bit_length() - 1
    same = lax.shift_right_logical(r, shift) == lax.shift_right_logical(s, shift)
    return same & ((r <= s) if upper else (r >= s))


def _tri_apply(x, upper):
    tri = _chunk_tri(x.shape[0], upper).astype(F32).astype(BF16)
    hi = x.astype(BF16)
    r1 = x - hi.astype(F32)
    mid = r1.astype(BF16)
    lo = (r1 - mid.astype(F32)).astype(BF16)
    return _dot(tri, hi, NN) + _dot(tri, mid, NN) + _dot(tri, lo, NN)


@functools.partial(jax.custom_vjp, nondiff_argnums=(1,))
def _chunk_cumsum(x, upper):
    return _tri_apply(x, upper)


def _chunk_cumsum_fwd(x, upper):
    return _tri_apply(x, upper), None


def _chunk_cumsum_bwd(upper, _, g):
    return (_tri_apply(g, not upper),)


_chunk_cumsum.defvjp(_chunk_cumsum_fwd, _chunk_cumsum_bwd)


def _hgrn_block(hq, hf, hi, lb, st, upper):
    c = HGRN_CHUNK
    rows = hq.shape[0]
    n_sub = rows // c
    q = _silu(hq)
    f = lb + (1.0 - lb) * _sigmoid(hf)
    lg = jnp.log(f)
    cum = _chunk_cumsum(lg, upper)
    tots = [jnp.sum(lg[n * c:(n + 1) * c], axis=0, keepdims=True) for n in range(n_sub)]
    totb = jnp.concatenate([jnp.broadcast_to(t, (c, HEAD)) for t in tots], axis=0)
    kk = 1.0 - f
    qd = (q * jnp.exp(cum)).astype(BF16)
    kd = (kk * jnp.exp(-cum)).astype(BF16)
    ke = (kk * jnp.exp(totb - cum)).astype(BF16)
    vb = hi.astype(BF16)
    a = jnp.where(_chunk_tri(rows, upper), _dot(qd, kd, NT), 0.0)
    o_intra = _dot(a.astype(BF16), vb, NN)
    o_inter = [None] * n_sub
    for n in (range(n_sub - 1, -1, -1) if upper else range(n_sub)):
        sl = slice(n * c, (n + 1) * c)
        o_inter[n] = _dot(qd[sl], st.astype(BF16), NT)
        st = st * jnp.exp(tots[n]) + _dot(vb[sl], ke[sl], TN)
    return o_intra + jnp.concatenate(o_inter, axis=0), st


def _hgrn_fwd(proj3, lb_f, lb_b, norm_g, d):
    b_, s_, _ = proj3.shape
    nh = d // HEAD
    hb = HGRN_FWD_HEADS
    wid = hb * HEAD
    rows = HGRN_CHUNK * HGRN_SUB
    n_blk = s_ // rows

    def body(hq_ref, hff_ref, hfb_ref, hi_ref, hog_ref, lbf_ref, lbb_ref, g_ref, oraw_ref, out_ref, of_ref, ob_ref):
        lbf, lbb = _lower_bound(lbf_ref[...]), _lower_bound(lbb_ref[...])

        def step(j, carry):
            sts = list(carry)
            rf = pl.ds(pl.multiple_of(j * rows, rows), rows)
            rb = pl.ds(pl.multiple_of((n_blk - 1 - j) * rows, rows), rows)
            for hh in range(hb):
                cs = slice(hh * HEAD, (hh + 1) * HEAD)
                o_f, sts[2 * hh] = _hgrn_block(hq_ref[rf, cs], hff_ref[rf, cs], hi_ref[rf, cs], lbf[:, cs],
                                               sts[2 * hh], False)
                o_b, sts[2 * hh + 1] = _hgrn_block(hq_ref[rb, cs], hfb_ref[rb, cs], hi_ref[rb, cs], lbb[:, cs],
                                                   sts[2 * hh + 1], True)
                of_ref[rf, cs] = o_f
                ob_ref[rb, cs] = o_b
            return tuple(sts)

        z = jnp.zeros((HEAD, HEAD), F32)
        lax.fori_loop(0, n_blk, step, (z,) * (2 * hb), unroll=HGRN_UNROLL)
        for hh in range(hb):
            cs = slice(hh * HEAD, (hh + 1) * HEAD)
            o = of_ref[:, cs] + ob_ref[:, cs]
            oraw_ref[:, cs] = o
            on = o * lax.rsqrt(jnp.mean(o * o, axis=-1, keepdims=True) + LN_EPS)
            out_ref[:, cs] = (on * g_ref[:, cs] * _silu(hog_ref[:, cs])).astype(BF16)

    def col(k):
        return pl.BlockSpec((None, s_, wid), lambda h, b: (b, 0, k * (nh // hb) + h))

    tab = pl.BlockSpec((2, wid), lambda h, b: (0, h))
    oblk = pl.BlockSpec((None, s_, wid), lambda h, b: (b, 0, h))
    return pl.pallas_call(
        body, name="hgrn_fwd", grid=(nh // hb, b_),
        in_specs=[col(0), col(1), col(2), col(3), col(4), tab, tab, pl.BlockSpec((1, wid), lambda h, b: (0, h))],
        out_specs=[oblk, oblk], out_shape=[_sds((b_, s_, d), F32), _sds((b_, s_, d), BF16)],
        scratch_shapes=[pltpu.VMEM((s_, wid), F32), pltpu.VMEM((s_, wid), F32)],
        compiler_params=_params(2, 0))(proj3, proj3, proj3, proj3, proj3, lb_f, lb_b, norm_g)


def _hgrn_bwd(proj3, lb_f, lb_b, norm_g, o_raw, do_a, d):
    b_, s_, _ = proj3.shape
    nh = d // HEAD
    hb = HGRN_BWD_HEADS
    wid = hb * HEAD
    rows = HGRN_CHUNK * HGRN_SUB
    n_blk = s_ // rows

    def body(hq_ref, hff_ref, hfb_ref, hi_ref, hog_ref, lbf_ref, lbb_ref, g_ref, oraw_ref, doa_ref,
             dq_ref, dff_ref, dfb_ref, di_ref, dog_ref, dlbf_ref, dlbb_ref, dg_ref,
             st_ref, dor_ref, dq2_ref, di2_ref):
        b = pl.program_id(1)
        tab_f, tab_b = lbf_ref[...], lbb_ref[...]
        lbf, lbb = _lower_bound(tab_f), _lower_bound(tab_b)

        dg_parts = []
        for hh in range(hb):
            cs = slice(hh * HEAD, (hh + 1) * HEAD)
            o, doa, hog, g = oraw_ref[:, cs], doa_ref[:, cs], hog_ref[:, cs], g_ref[:, cs]
            rs = lax.rsqrt(jnp.mean(o * o, axis=-1, keepdims=True) + LN_EPS)
            on = o * rs
            sg = _sigmoid(hog)
            gate = hog * sg
            dog_ref[:, cs] = (doa * on * g * sg * (1.0 + hog * (1.0 - sg))).astype(BF16)
            don = doa * g * gate
            dor_ref[:, cs] = rs * (don - on * jnp.mean(don * on, axis=-1, keepdims=True))
            dg_parts.append(jnp.sum(doa * on * gate, axis=0, keepdims=True))

        def fwd_step(j, carry):
            sts = list(carry)
            jb = n_blk - 1 - j
            rf = pl.ds(pl.multiple_of(j * rows, rows), rows)
            rb = pl.ds(pl.multiple_of(jb * rows, rows), rows)
            for hh in range(hb):
                cs = slice(hh * HEAD, (hh + 1) * HEAD)
                st_ref[2 * hh, j] = sts[2 * hh]
                st_ref[2 * hh + 1, jb] = sts[2 * hh + 1]
                _, sts[2 * hh] = _hgrn_block(hq_ref[rf, cs], hff_ref[rf, cs], hi_ref[rf, cs], lbf[:, cs],
                                             sts[2 * hh], False)
                _, sts[2 * hh + 1] = _hgrn_block(hq_ref[rb, cs], hfb_ref[rb, cs], hi_ref[rb, cs], lbb[:, cs],
                                                 sts[2 * hh + 1], True)
            return tuple(sts)

        z = jnp.zeros((HEAD, HEAD), F32)
        lax.fori_loop(0, n_blk, fwd_step, (z,) * (2 * hb), unroll=HGRN_UNROLL)

        def bwd_step(j, carry):
            gs, dls = list(carry[0]), list(carry[1])
            jf = n_blk - 1 - j
            rf = pl.ds(pl.multiple_of(jf * rows, rows), rows)
            rb = pl.ds(pl.multiple_of(j * rows, rows), rows)
            for hh in range(hb):
                cs = slice(hh * HEAD, (hh + 1) * HEAD)
                _, vjp_f = jax.vjp(lambda a0, a1, a2, a3, a4: _hgrn_block(a0, a1, a2, a3, a4, False),
                                   hq_ref[rf, cs], hff_ref[rf, cs], hi_ref[rf, cs], lbf[:, cs], st_ref[2 * hh, jf])
                dq, df, di, dl, gs[2 * hh] = vjp_f((dor_ref[rf, cs], gs[2 * hh]))
                dq_ref[rf, cs] = dq.astype(BF16)
                dff_ref[rf, cs] = df.astype(BF16)
                di_ref[rf, cs] = di.astype(BF16)
                dls[2 * hh] = dls[2 * hh] + dl
                _, vjp_b = jax.vjp(lambda a0, a1, a2, a3, a4: _hgrn_block(a0, a1, a2, a3, a4, True),
                                   hq_ref[rb, cs], hfb_ref[rb, cs], hi_ref[rb, cs], lbb[:, cs], st_ref[2 * hh + 1, j])
                dq, df, di, dl, gs[2 * hh + 1] = vjp_b((dor_ref[rb, cs], gs[2 * hh + 1]))
                dq2_ref[rb, cs] = dq
                dfb_ref[rb, cs] = df.astype(BF16)
                di2_ref[rb, cs] = di
                dls[2 * hh + 1] = dls[2 * hh + 1] + dl
            return tuple(gs), tuple(dls)

        zl = jnp.zeros((1, HEAD), F32)
        _, dls = lax.fori_loop(0, n_blk, bwd_step, ((z,) * (2 * hb), (zl,) * (2 * hb)), unroll=HGRN_UNROLL)
        dq_ref[...] = (dq_ref[...].astype(F32) + dq2_ref[...]).astype(BF16)
        di_ref[...] = (di_ref[...].astype(F32) + di2_ref[...]).astype(BF16)

        _, vjp_tf = jax.vjp(_lower_bound, tab_f)
        _, vjp_tb = jax.vjp(_lower_bound, tab_b)

        @pl.when(b == 0)
        def _():
            dlbf_ref[...] = jnp.zeros(dlbf_ref.shape, F32)
            dlbb_ref[...] = jnp.zeros(dlbb_ref.shape, F32)
            dg_ref[...] = jnp.zeros(dg_ref.shape, F32)

        dlbf_ref[...] += vjp_tf(jnp.concatenate(dls[0::2], axis=1))[0]
        dlbb_ref[...] += vjp_tb(jnp.concatenate(dls[1::2], axis=1))[0]
        dg_ref[...] += jnp.concatenate(dg_parts, axis=1)

    def col(k):
        return pl.BlockSpec((None, s_, wid), lambda h, b: (b, 0, k * (nh // hb) + h))

    tab = pl.BlockSpec((2, wid), lambda h, b: (0, h))
    vec = pl.BlockSpec((1, wid), lambda h, b: (0, h))
    oblk = pl.BlockSpec((None, s_, wid), lambda h, b: (b, 0, h))
    seq = _sds((b_, s_, d), BF16)
    return pl.pallas_call(
        body, name="hgrn_bwd", grid=(nh // hb, b_),
        in_specs=[col(0), col(1), col(2), col(3), col(4), tab, tab, vec, oblk, oblk],
        out_specs=[oblk, oblk, oblk, oblk, oblk, tab, tab, vec],
        out_shape=[seq, seq, seq, seq, seq, _sds((2, d), F32), _sds((2, d), F32), _sds((1, d), F32)],
        scratch_shapes=[pltpu.VMEM((2 * hb, n_blk, HEAD, HEAD), F32),
                        pltpu.VMEM((s_, wid), F32), pltpu.VMEM((s_, wid), F32), pltpu.VMEM((s_, wid), F32)],
        compiler_params=_params(1, 1))(proj3, proj3, proj3, proj3, proj3, lb_f, lb_b, norm_g, o_raw, do_a)


def _rope_tables(s_):
    inv = ROPE_THETA ** (-jnp.arange(0, ROPE_DIM, 2, dtype=F32) / ROPE_DIM)
    ang = jnp.arange(s_, dtype=F32)[:, None] * inv
    cos, sin = jnp.cos(ang), jnp.sin(ang)
    rest = HEAD - ROPE_DIM
    ctab = jnp.concatenate([cos, cos, jnp.ones((s_, rest), F32)], axis=1)
    stab = jnp.concatenate([-sin, sin, jnp.zeros((s_, rest), F32)], axis=1)
    half = ROPE_DIM // 2
    perm = np.zeros((HEAD, HEAD), np.float32)
    for i in range(half):
        perm[i + half, i] = 1.0
        perm[i, i + half] = 1.0
    return ctab, stab, jnp.asarray(perm)


def _attn_tile(qr, kr, v, cq, sq, ck, sk, perm, qi0, kj0, half):
    hi = lax.Precision.HIGHEST
    q = qr * cq + _dot(qr, perm, NN, precision=hi) * sq
    k = kr * ck + _dot(kr, perm, NN, precision=hi) * sk
    s = _dot(q.astype(BF16), k.astype(BF16), NT) * (HEAD ** -0.5)
    qi = qi0 + lax.broadcasted_iota(jnp.int32, s.shape, 0)
    kj = kj0 + lax.broadcasted_iota(jnp.int32, s.shape, 1)
    s = jnp.where(jnp.abs(qi - kj) <= half, s, NEG_INF)
    m = lax.stop_gradient(jnp.max(s, axis=-1, keepdims=True))
    p = jnp.exp(s - m)
    den = jnp.sum(p, axis=-1, keepdims=True)
    o = _dot(p.astype(BF16), v.astype(BF16), NN) / den
    lse = m + jnp.log(den)
    return o, jnp.broadcast_to(lse, o.shape)


def _attn_tiling(seg):
    tq = seg if seg <= 256 else 256
    kw = seg if seg <= 512 else 512
    tiles = []
    for i in range(seg // tq):
        ws = min(max(i * tq - (kw - tq) // 2, 0), seg - kw)
        tiles.append((i * tq, ws))
    return tq, kw, tiles


def _attn_specs(b_, dil, seg):
    qkv = pl.BlockSpec((None, None, 3, None, seg, HEAD), lambda b, r, h: (b, r, 0, h, 0, 0))
    tab = pl.BlockSpec((None, seg, HEAD), lambda b, r, h: (r, 0, 0))
    perm = pl.BlockSpec((HEAD, HEAD), lambda b, r, h: (0, 0))
    oblk = pl.BlockSpec((None, None, None, seg, HEAD), lambda b, r, h: (b, r, h, 0, 0))
    return qkv, tab, perm, oblk


def _attn_fwd(qkv_g, ctab, stab, perm, window, dil):
    b_, _, _, nh, seg, _ = qkv_g.shape
    half = window // (2 * dil)
    tq, kw, tiles = _attn_tiling(seg)

    def body(qkv_ref, c_ref, s_ref, p_ref, o_ref, l_ref):
        pm = p_ref[...]
        for q0, k0 in tiles:
            rq, rk = pl.ds(q0, tq), pl.ds(k0, kw)
            o, l = _attn_tile(qkv_ref[0, rq, :], qkv_ref[1, rk, :], qkv_ref[2, rk, :], c_ref[rq, :], s_ref[rq, :],
                              c_ref[rk, :], s_ref[rk, :], pm, q0, k0, half)
            o_ref[rq, :] = o
            l_ref[rq, :] = l

    qkv, tab, pspec, oblk = _attn_specs(b_, dil, seg)
    shp = _sds((b_, dil, nh, seg, HEAD), F32)
    return pl.pallas_call(
        body, name=f"attn_fwd_d{dil}", grid=(b_, dil, nh), in_specs=[qkv, tab, tab, pspec],
        out_specs=[oblk, oblk], out_shape=[shp, shp], compiler_params=_params(3, 0))(qkv_g, ctab, stab, perm)


def _attn_bwd(qkv_g, ctab, stab, perm, do, dl, window, dil):
    b_, _, _, nh, seg, _ = qkv_g.shape
    half = window // (2 * dil)
    tq, kw, tiles = _attn_tiling(seg)

    def body(qkv_ref, c_ref, s_ref, p_ref, do_ref, dl_ref, d_ref):
        pm = p_ref[...]
        d_ref[...] = jnp.zeros(d_ref.shape, F32)
        for q0, k0 in tiles:
            rq, rk = pl.ds(q0, tq), pl.ds(k0, kw)
            cq, sq, ck, sk = c_ref[rq, :], s_ref[rq, :], c_ref[rk, :], s_ref[rk, :]
            _, vjp = jax.vjp(lambda a, b, c: _attn_tile(a, b, c, cq, sq, ck, sk, pm, q0, k0, half),
                             qkv_ref[0, rq, :], qkv_ref[1, rk, :], qkv_ref[2, rk, :])
            dq, dk, dv = vjp((do_ref[rq, :], dl_ref[rq, :]))
            d_ref[0, rq, :] = dq
            d_ref[1, rk, :] += dk
            d_ref[2, rk, :] += dv

    qkv, tab, pspec, oblk = _attn_specs(b_, dil, seg)
    return pl.pallas_call(
        body, name=f"attn_bwd_d{dil}", grid=(b_, dil, nh), in_specs=[qkv, tab, tab, pspec, oblk, oblk],
        out_specs=qkv, out_shape=_sds(qkv_g.shape, F32), compiler_params=_params(3, 0))(
            qkv_g, ctab, stab, perm, do, dl)


def _combine(os_, ls_):
    m = jnp.maximum(jnp.maximum(ls_[0], ls_[1]), ls_[2])
    es = [jnp.exp(l - m) for l in ls_]
    return (es[0] * os_[0] + es[1] * os_[1] + es[2] * os_[2]) / (es[0] + es[1] + es[2])


def _combine_fwd(os_, ls_):
    t, w = os_[0].shape
    tm = _tile(t, 512, 8)

    def body(*refs):
        refs[6][...] = _combine([r[...] for r in refs[:3]], [r[...] for r in refs[3:6]]).astype(BF16)

    row = pl.BlockSpec((tm, w), lambda i: (i, 0))
    return pl.pallas_call(body, name="combine_fwd", grid=(t // tm,), in_specs=[row] * 6, out_specs=row,
                          out_shape=_sds((t, w), BF16), compiler_params=_params(1, 0))(*os_, *ls_)


def _combine_bwd(os_, ls_, dob):
    t, w = os_[0].shape
    tm = _tile(t, 512, 8)

    def body(*refs):
        _, vjp = jax.vjp(lambda *a: _combine(a[:3], a[3:]), *[r[...] for r in refs[:6]])
        for r, g in zip(refs[7:], vjp(refs[6][...])):
            r[...] = g

    row = pl.BlockSpec((tm, w), lambda i: (i, 0))
    return pl.pallas_call(body, name="combine_bwd", grid=(t // tm,), in_specs=[row] * 7, out_specs=[row] * 6,
                          out_shape=[_sds((t, w), F32)] * 6, compiler_params=_params(1, 0))(*os_, *ls_, dob)


def _to_residues(cols, b_, s_, dil):
    seg = s_ // dil
    return cols.reshape(b_, seg, dil, 3, ATTN_HEADS, HEAD).transpose(0, 2, 3, 4, 1, 5)


def _from_residues(a, b_, s_):
    if a.ndim == 6:
        return a.transpose(0, 4, 1, 2, 3, 5).reshape(b_ * s_, 3 * ATTN_HEADS * HEAD)
    return a.transpose(0, 3, 1, 2, 4).reshape(b_ * s_, ATTN_HEADS * HEAD)


def _local_step(x, target, w, p):
    b_, s_, d = x.shape
    t = b_ * s_
    x2 = x.reshape(t, d)
    gate_col = 5 * d + QKV_W

    xb = _cast_bf16(x2)
    u1, a1 = _ffn_up(xb, w["ffn1_in"])
    h1, h1b, xh1, rs1 = _down_ln(a1, w["ffn1_out"], x2, 0.5, p["ln1_g"], p["ln1_b"])

    proj = _mm_w3(h1b, w["mix_in"], F32)
    n_in = proj.shape[1]
    proj3 = proj.reshape(b_, s_, n_in)
    o_raw, oa = _hgrn_fwd(proj3, p["hgrn_lb_fwd"], p["hgrn_lb_bwd"], p["hgrn_norm_g"], d)
    oa2 = oa.reshape(t, d)

    ctab, stab, perm = _rope_tables(s_)
    qkv_gs, tabs, os_, ls_ = [], [], [], []
    for g, (window, dil) in enumerate(ATTN_GROUPS):
        c0 = 5 * d + g * 3 * ATTN_OUT
        qkv_g = _to_residues(proj[:, c0:c0 + 3 * ATTN_OUT], b_, s_, dil)
        seg = s_ // dil
        ct = ctab.reshape(seg, dil, HEAD).transpose(1, 0, 2)
        st = stab.reshape(seg, dil, HEAD).transpose(1, 0, 2)
        o_g, l_g = _attn_fwd(qkv_g, ct, st, perm, window, dil)
        qkv_gs.append(qkv_g)
        tabs.append((ct, st))
        os_.append(_from_residues(o_g, b_, s_))
        ls_.append(_from_residues(l_g, b_, s_))
    ob = _combine_fwd(os_, ls_)

    ya = _mm_w2(oa2, w["wa"], F32)
    yb, zb = _branch_gate(ob, w["wb"], ya, proj, gate_col)
    h2, h2b, xh2, rs2 = _down_ln(zb, w["mix_out"], h1, 1.0, p["ln2_g"], p["ln2_b"])

    u2, a2 = _ffn_up(h2b, w["ffn2_in"])
    h3, _, xh3, rs3 = _down_ln(a2, w["ffn2_out"], h2, 0.5, p["ln3_g"], p["ln3_b"])

    gw, gp = {}, {}
    nf = w["ffn2_in"].shape[2]

    def du_map(j, m):
        return (j // 2, m, j % 2)

    dres3, dy3, gp["ln3_g"], gp["ln3_b"], loss = _ln_bwd([h3], xh3, rs3, p["ln3_g"], 0.5, target.reshape(t, d))
    du2 = _swiglu_bwd(dy3, w["ffn2_out"], u2)
    gw["ffn2_out"] = _tn_w2(a2, dy3)
    dh2 = _ffn_dx(du2, w["ffn2_in"], dres3)
    gw["ffn2_in"] = _tn_w3(h2b, du2, du_map, nf)

    dres2, dmix, gp["ln2_g"], gp["ln2_b"], _ = _ln_bwd([dh2], xh2, rs2, p["ln2_g"], 1.0)
    dya, dyb, dpga, dpgb = _gate_bwd(dmix, w["mix_out"], proj, ya, yb, gate_col)
    gw["mix_out"] = _tn_w2(zb, dmix)
    do_a = _nt_w2(dya, w["wa"], F32)
    gw["wa"] = _tn_w2(oa2, dya)
    nb = w["wb"].shape[2]
    do_b = _nt_w3(dyb, w["wb"], None)
    gw["wb"] = _tn_w3(ob, dyb, lambda j, m: (m, j), nb)

    dq, dff, dfb, di, dog, gp["hgrn_lb_fwd"], gp["hgrn_lb_bwd"], gp["hgrn_norm_g"] = _hgrn_bwd(
        proj3, p["hgrn_lb_fwd"], p["hgrn_lb_bwd"], p["hgrn_norm_g"], o_raw, do_a.reshape(b_, s_, d), d)

    douts = _combine_bwd(os_, ls_, do_b)
    dqkv = []
    for g, (window, dil) in enumerate(ATTN_GROUPS):
        seg = s_ // dil
        do_g = douts[g].reshape(b_, seg, dil, ATTN_HEADS, HEAD).transpose(0, 2, 3, 1, 4)
        dl_g = douts[3 + g].reshape(b_, seg, dil, ATTN_HEADS, HEAD).transpose(0, 2, 3, 1, 4)
        dqkv_g = _attn_bwd(qkv_gs[g], tabs[g][0], tabs[g][1], perm, do_g, dl_g, window, dil)
        dqkv.append(_from_residues(dqkv_g, b_, s_).astype(BF16))

    dproj = jnp.concatenate(
        [a.reshape(t, d) for a in (dq, dff, dfb, di, dog)] + dqkv + [dpga, dpgb], axis=1)
    nm = w["mix_in"].shape[2]
    dh1 = _nt_w3(dproj, w["mix_in"], dres2)
    gw["mix_in"] = _tn_w3(h1b, dproj, lambda j, m: (m, j), nm)

    dres1, dy1, gp["ln1_g"], gp["ln1_b"], _ = _ln_bwd([dh1], xh1, rs1, p["ln1_g"], 0.5)
    du1 = _swiglu_bwd(dy1, w["ffn1_out"], u1)
    gw["ffn1_out"] = _tn_w2(a1, dy1)
    dx = _ffn_dx(du1, w["ffn1_in"], dres1)
    gw["ffn1_in"] = _tn_w3(xb, du1, du_map, nf)
    return loss, dx.reshape(b_, s_, d), gw, gp


MESH = pl.DeviceIdType.MESH
ANY = pl.BlockSpec(memory_space=pl.ANY)


def _place():
    x, y, c = lax.axis_index("x"), lax.axis_index("y"), lax.axis_index("c")
    chips = [(1 - x, y), (x, 1 - y), (1 - x, 1 - y)]
    return x, y, c, chips, (x, y, 1 - c)


def _half_rows(c, rows):
    hr = rows // 2
    return pl.ds(pl.multiple_of(c * hr, 16), hr)


def _remote(src, dst, send, recv, dev):
    return pltpu.make_async_remote_copy(src_ref=src, dst_ref=dst, send_sem=send, recv_sem=recv,
                                        device_id=dev, device_id_type=MESH)


def _gather_weights(fulls):
    n = len(fulls)

    def body(*refs):
        bufs = refs[n:2 * n]
        isend, irecv, fsend, frecv = refs[2 * n:]
        x, y, c, chips, sib = _place()
        s = 2 * x + y
        sends = []
        for i in range(n):
            mine = _half_rows(c, bufs[i].shape[1])
            for k, chip in enumerate(chips):
                blk = bufs[i].at[s, mine]
                cp = _remote(blk, blk, isend.at[i, k], irecv.at[i, k], (*chip, c))
                cp.start()
                sends.append(cp)
        for i in range(n):
            mine = _half_rows(c, bufs[i].shape[1])
            for k, chip in enumerate(chips):
                blk = bufs[i].at[2 * chip[0] + chip[1], mine]
                _remote(blk, blk, isend.at[i, k], irecv.at[i, k], (*chip, c)).wait_recv()
                cp = _remote(blk, blk, fsend.at[i, k], frecv.at[i, k], sib)
                cp.start()
                sends.append(cp)
        for i in range(n):
            other = _half_rows(1 - c, bufs[i].shape[1])
            for k, chip in enumerate(chips):
                blk = bufs[i].at[2 * chip[0] + chip[1], other]
                _remote(blk, blk, fsend.at[i, k], frecv.at[i, k], sib).wait_recv()
        for cp in sends:
            cp.wait_send()

    dma = pltpu.SemaphoreType.DMA
    return pl.pallas_call(
        body, name="gather_weights", in_specs=[ANY] * n, out_specs=[ANY] * n,
        out_shape=[_sds(a.shape, a.dtype) for a in fulls], input_output_aliases={i: i for i in range(n)},
        scratch_shapes=[dma((n, 3)), dma((n, 3)), dma((n, 3)), dma((n, 3))])(*fulls)


def _swap_halves(grads):
    n = len(grads)

    def body(*refs):
        ins, outs, send, recv = refs[:n], refs[n:2 * n], refs[2 * n], refs[2 * n + 1]
        x, y, c, chips, sib = _place()
        cps = []
        for i in range(n):
            other = _half_rows(1 - c, ins[i].shape[1])
            cp = _remote(ins[i].at[:, other], outs[i], send.at[i], recv.at[i], sib)
            cp.start()
            cps.append(cp)
        for cp in cps:
            cp.wait()

    dma = pltpu.SemaphoreType.DMA
    return pl.pallas_call(
        body, name="swap_halves", in_specs=[ANY] * n, out_specs=[ANY] * n,
        out_shape=[_sds((N_SHARD, a.shape[1] // 2, a.shape[2]), a.dtype) for a in grads],
        scratch_shapes=[dma((n,)), dma((n,))])(*grads)


def _scatter_partials(parts):
    n = len(parts)

    def body(*refs):
        ins, outs, send, recv = refs[:n], refs[n:2 * n], refs[2 * n], refs[2 * n + 1]
        x, y, c, chips, sib = _place()
        cps = []
        for i in range(n):
            for k, chip in enumerate(chips):
                cp = _remote(ins[i].at[2 * chip[0] + chip[1]], outs[i].at[k], send.at[i, k], recv.at[i, k], (*chip, c))
                cp.start()
                cps.append(cp)
        for cp in cps:
            cp.wait()

    dma = pltpu.SemaphoreType.DMA
    return pl.pallas_call(
        body, name="scatter_partials", in_specs=[ANY] * n, out_specs=[ANY] * n,
        out_shape=[_sds((3,) + a.shape[1:], a.dtype) for a in parts],
        scratch_shapes=[dma((n, 3)), dma((n, 3))])(*parts)


def _join_halves(grads):
    n = len(grads)

    def body(*refs):
        bufs, send, recv = refs[n:2 * n], refs[2 * n], refs[2 * n + 1]
        x, y, c, chips, sib = _place()
        cps = []
        for i in range(n):
            blk = bufs[i].at[_half_rows(c, bufs[i].shape[0])]
            other = bufs[i].at[_half_rows(1 - c, bufs[i].shape[0])]
            cp = _remote(blk, blk, send.at[i], recv.at[i], sib)
            cp.start()
            cps.append((cp, _remote(other, other, send.at[i], recv.at[i], sib)))
        for cp, got in cps:
            cp.wait_send()
            got.wait_recv()

    dma = pltpu.SemaphoreType.DMA
    return pl.pallas_call(
        body, name="join_halves", in_specs=[ANY] * n, out_specs=[ANY] * n,
        out_shape=[_sds(a.shape, a.dtype) for a in grads], input_output_aliases={i: i for i in range(n)},
        scratch_shapes=[dma((n,)), dma((n,))])(*grads)


def _gather_rows(block):
    m_per, n = block.shape

    def body(x_ref, out_ref, send_sems, recv_sems, local_sem):
        x, y, c, chips, sibling = _place()
        me = (x, y, c)

        def rows(px, py, pc):
            return out_ref.at[pl.ds((4 * px + 2 * py + pc) * m_per, m_per), :]

        def copy(k, blk, to, src=None):
            return _remote(rows(*blk) if src is None else src, rows(*blk), send_sems.at[k], recv_sems.at[k], to)

        mine = pltpu.make_async_copy(x_ref, rows(*me), local_sem)
        mine.start()
        first = [copy(0, me, sibling, src=x_ref)]
        first += [copy(1 + j, me, (*chip, c), src=x_ref) for j, chip in enumerate(chips)]
        for cp in first:
            cp.start()
        passed = [copy(4 + j, (*chip, c), sibling) for j, chip in enumerate(chips)]
        for j, chip in enumerate(chips):
            copy(1 + j, (*chip, c), me).wait_recv()
            passed[j].start()
        copy(0, sibling, me).wait_recv()
        for j, chip in enumerate(chips):
            copy(4 + j, (*chip, 1 - c), me).wait_recv()
        for cp in first + passed:
            cp.wait_send()
        mine.wait()

    vmem = pl.BlockSpec(memory_space=pltpu.VMEM)
    dma = pltpu.SemaphoreType.DMA
    return pl.pallas_call(
        body, name="gather_rows", in_specs=[vmem], out_specs=vmem, out_shape=_sds((8 * m_per, n), block.dtype),
        scratch_shapes=[dma((7,)), dma((7,)), dma(())])(block)


def _row_tile(rows, cols):
    return _tile(rows, max(16, (1 << 20) // cols), 16)


def _sum_halves(grad, got, c_arr):
    _, hr, cols = got.shape
    tr = _row_tile(hr, cols)
    nb = hr // tr

    def body(c_ref, a_ref, b_ref, o_ref):
        o_ref[...] = (a_ref[...].astype(F32) + b_ref[...].astype(F32)).astype(BF16)

    blk = pl.BlockSpec((None, tr, cols), lambda s, i, c_ref: (s, i, 0))
    return pl.pallas_call(
        body, name="sum_halves",
        grid_spec=pltpu.PrefetchScalarGridSpec(
            num_scalar_prefetch=1, grid=(N_SHARD, nb),
            in_specs=[pl.BlockSpec((None, tr, cols), lambda s, i, c_ref: (s, c_ref[0] * nb + i, 0)), blk],
            out_specs=blk),
        out_shape=_sds(got.shape, BF16), compiler_params=_params(2, 0))(c_arr, grad, got)


def _sum_chips(part, got, sc_arr):
    _, hr, cols = got.shape
    tr = _row_tile(hr, cols)
    nb = hr // tr

    def body(s_ref, a_ref, b_ref, o_ref):
        o_ref[...] = ((a_ref[...].astype(F32) + b_ref[0].astype(F32)) + b_ref[1].astype(F32)) + b_ref[2].astype(F32)

    return pl.pallas_call(
        body, name="sum_chips",
        grid_spec=pltpu.PrefetchScalarGridSpec(
            num_scalar_prefetch=1, grid=(nb,),
            in_specs=[pl.BlockSpec((None, tr, cols), lambda i, s_ref: (s_ref[0], i, 0)),
                      pl.BlockSpec((3, tr, cols), lambda i, s_ref: (0, i, 0))],
            out_specs=pl.BlockSpec((tr, cols), lambda i, s_ref: (s_ref[1] * nb + i, 0))),
        out_shape=_sds((2 * hr, cols), F32), compiler_params=_params(1, 0))(sc_arr, part, got)


def _cast_into_slot(x2d, sc_arr):
    rows, cols = x2d.shape
    tr = _row_tile(rows, cols)

    def body(s_ref, x_ref, o_ref):
        o_ref[...] = x_ref[...].astype(BF16)

    return pl.pallas_call(
        body, name="cast_into_slot",
        grid_spec=pltpu.PrefetchScalarGridSpec(
            num_scalar_prefetch=1, grid=(rows // tr,),
            in_specs=[pl.BlockSpec((tr, cols), lambda i, s_ref: (i, 0))],
            out_specs=pl.BlockSpec((None, tr, cols), lambda i, s_ref: (s_ref[0], i, 0))),
        out_shape=_sds((N_SHARD, rows, cols), BF16), compiler_params=_params(1, 0))(sc_arr, x2d)


def _adam_math(w, g, m, v):
    m = ADAM_B1 * m + (1.0 - ADAM_B1) * g
    v = ADAM_B2 * v + (1.0 - ADAM_B2) * (g * g)
    m_hat = m / (1.0 - ADAM_B1 ** ADAM_STEP)
    v_hat = v / (1.0 - ADAM_B2 ** ADAM_STEP)
    delta = -ADAM_LR * (m_hat / (jnp.sqrt(v_hat) + ADAM_EPS) + ADAM_WD * w)
    return delta, m, v


def _adamw(w, g, m, v):
    rows, cols = w.shape
    tr = _tile(rows, max(8, (1 << 19) // cols), 8)

    def body(w_ref, g_ref, m_ref, v_ref, go_ref, d_ref, mo_ref, vo_ref):
        g_ = g_ref[...]
        go_ref[...] = g_
        d_ref[...], mo_ref[...], vo_ref[...] = _adam_math(w_ref[...], g_, m_ref[...], v_ref[...])

    blk = pl.BlockSpec((tr, cols), lambda i: (i, 0))
    return pl.pallas_call(
        body, name="adamw", grid=(rows // tr,), in_specs=[blk] * 4, out_specs=[blk] * 4,
        out_shape=[_sds((rows, cols), F32)] * 4, compiler_params=_params(1, 0))(w, g, m, v)


def _adamw_small(gathered, w, m, v):
    rows, cols = w.shape

    def body(a_ref, w_ref, m_ref, v_ref, go_ref, d_ref, mo_ref, vo_ref):
        g_ = a_ref[pl.ds(0, rows), :]
        for k in range(1, 8):
            g_ = g_ + a_ref[pl.ds(k * rows, rows), :]
        go_ref[...] = g_
        d_ref[...], mo_ref[...], vo_ref[...] = _adam_math(w_ref[...], g_, m_ref[...], v_ref[...])

    vmem = pl.BlockSpec(memory_space=pltpu.VMEM)
    return pl.pallas_call(
        body, name="adamw_small", in_specs=[vmem] * 4, out_specs=[vmem] * 4,
        out_shape=[_sds((rows, cols), F32)] * 4)(gathered, w, m, v)


BIG = ("ffn1_w_in", "ffn1_w_out", "mix_w_in", "w_branch_a", "w_branch_b", "mix_w_out", "ffn2_w_in", "ffn2_w_out")
BIG_KEY = {"ffn1_w_in": "ffn1_in", "ffn1_w_out": "ffn1_out", "mix_w_in": "mix_in", "w_branch_a": "wa",
           "w_branch_b": "wb", "mix_w_out": "mix_out", "ffn2_w_in": "ffn2_in", "ffn2_w_out": "ffn2_out"}
SMALL = ("ln1_g", "ln1_b", "hgrn_lb_fwd", "hgrn_lb_bwd", "hgrn_norm_g", "ln2_g", "ln2_b", "ln3_g", "ln3_b")
ORDER = ("ffn1_w_in", "ffn1_w_out", "ln1_g", "ln1_b", "mix_w_in", "hgrn_lb_fwd", "hgrn_lb_bwd", "hgrn_norm_g",
         "w_branch_a", "w_branch_b", "mix_w_out", "ln2_g", "ln2_b", "ffn2_w_in", "ffn2_w_out", "ln3_g", "ln3_b")
SMALL_ROWS = 16


def _pack_small(d):
    rows = jnp.concatenate([d[k].reshape(-1, d[k].shape[-1]) for k in SMALL], axis=0)
    return jnp.pad(rows, ((0, SMALL_ROWS - rows.shape[0]), (0, 0)))


def _unpack_small(a, like):
    out, r = {}, 0
    for k in SMALL:
        n = like[k].shape[0]
        out[k] = a[r:r + n].reshape(like[k].shape)
        r += n
    return out


def kernel(x, ffn1_w_in, ffn1_w_out, ln1_g, ln1_b, mix_w_in, hgrn_lb_fwd, hgrn_lb_bwd, hgrn_norm_g, w_branch_a, w_branch_b, mix_w_out, ln2_g, ln2_b, ffn2_w_in, ffn2_w_out, ln3_g, ln3_b, loss_target, m_ffn1_w_in, m_ffn1_w_out, m_ln1_g, m_ln1_b, m_mix_w_in, m_hgrn_lb_fwd, m_hgrn_lb_bwd, m_hgrn_norm_g, m_w_branch_a, m_w_branch_b, m_mix_w_out, m_ln2_g, m_ln2_b, m_ffn2_w_in, m_ffn2_w_out, m_ln3_g, m_ln3_b, v_ffn1_w_in, v_ffn1_w_out, v_ln1_g, v_ln1_b, v_mix_w_in, v_hgrn_lb_fwd, v_hgrn_lb_bwd, v_hgrn_norm_g, v_w_branch_a, v_w_branch_b, v_mix_w_out, v_ln2_g, v_ln2_b, v_ffn2_w_in, v_ffn2_w_out, v_ln3_g, v_ln3_b):
    wts = dict(ffn1_w_in=ffn1_w_in, ffn1_w_out=ffn1_w_out, ln1_g=ln1_g, ln1_b=ln1_b, mix_w_in=mix_w_in,
               hgrn_lb_fwd=hgrn_lb_fwd, hgrn_lb_bwd=hgrn_lb_bwd, hgrn_norm_g=hgrn_norm_g, w_branch_a=w_branch_a,
               w_branch_b=w_branch_b, mix_w_out=mix_w_out, ln2_g=ln2_g, ln2_b=ln2_b, ffn2_w_in=ffn2_w_in,
               ffn2_w_out=ffn2_w_out, ln3_g=ln3_g, ln3_b=ln3_b)
    mom = dict(ffn1_w_in=m_ffn1_w_in, ffn1_w_out=m_ffn1_w_out, ln1_g=m_ln1_g, ln1_b=m_ln1_b, mix_w_in=m_mix_w_in,
               hgrn_lb_fwd=m_hgrn_lb_fwd, hgrn_lb_bwd=m_hgrn_lb_bwd, hgrn_norm_g=m_hgrn_norm_g,
               w_branch_a=m_w_branch_a, w_branch_b=m_w_branch_b, mix_w_out=m_mix_w_out, ln2_g=m_ln2_g, ln2_b=m_ln2_b,
               ffn2_w_in=m_ffn2_w_in, ffn2_w_out=m_ffn2_w_out, ln3_g=m_ln3_g, ln3_b=m_ln3_b)
    var = dict(ffn1_w_in=v_ffn1_w_in, ffn1_w_out=v_ffn1_w_out, ln1_g=v_ln1_g, ln1_b=v_ln1_b, mix_w_in=v_mix_w_in,
               hgrn_lb_fwd=v_hgrn_lb_fwd, hgrn_lb_bwd=v_hgrn_lb_bwd, hgrn_norm_g=v_hgrn_norm_g,
               w_branch_a=v_w_branch_a, w_branch_b=v_w_branch_b, mix_w_out=v_mix_w_out, ln2_g=v_ln2_g, ln2_b=v_ln2_b,
               ffn2_w_in=v_ffn2_w_in, ffn2_w_out=v_ffn2_w_out, ln3_g=v_ln3_g, ln3_b=v_ln3_b)
    c_arr = lax.axis_index("c").astype(jnp.int32).reshape(1)
    shard = (2 * lax.axis_index("x") + lax.axis_index("y")).astype(jnp.int32)
    sc_arr = jnp.stack([shard, lax.axis_index("c").astype(jnp.int32)])

    shard2d = {k: wts[k].reshape(wts[k].shape[1:]) for k in BIG}
    full = _gather_weights([_cast_into_slot(shard2d[k], sc_arr) for k in BIG])
    w = {}
    for k, a in zip(BIG, full):
        row_sharded = k in ("ffn1_w_out", "ffn2_w_out", "w_branch_a", "mix_w_out")
        w[BIG_KEY[k]] = a.reshape(a.shape[0] * a.shape[1], a.shape[2]) if row_sharded else a
    p = {k: wts[k] for k in SMALL}

    loss, grad_x, gw, gp = _local_step(x, loss_target, w, p)
    loss = lax.psum(loss[0, 0], ("x", "y", "c"))

    g3 = [gw[BIG_KEY[k]].reshape((N_SHARD,) + shard2d[k].shape) for k in BIG]
    got = _swap_halves(g3)
    parts = [_sum_halves(a, b, c_arr) for a, b in zip(g3, got)]
    got = _scatter_partials(parts)
    grads = _join_halves([_sum_chips(a, b, sc_arr) for a, b in zip(parts, got)])

    out_g, out_d, out_m, out_v = {}, {}, {}, {}
    for k, g in zip(BIG, grads):
        shp = wts[k].shape
        res = _adamw(shard2d[k], g, mom[k].reshape(shp[1:]), var[k].reshape(shp[1:]))
        out_g[k], out_d[k], out_m[k], out_v[k] = [a.reshape(shp) for a in res]

    gathered = _gather_rows(_pack_small(gp))
    res = _adamw_small(gathered, _pack_small(wts), _pack_small(mom), _pack_small(var))
    for dst, a in zip((out_g, out_d, out_m, out_v), res):
        dst.update(_unpack_small(a, wts))

    return (loss, grad_x, *[out_g[k] for k in ORDER], *[out_d[k] for k in ORDER],
            *[out_m[k] for k in ORDER], *[out_v[k] for k in ORDER])
```

```python
import functools

import numpy as np
import jax
import jax.numpy as jnp
from jax import lax
from jax.experimental import pallas as pl
from jax.experimental.pallas import tpu as pltpu

F32 = jnp.float32
BF16 = jnp.bfloat16

HEAD = 128
ATTN_GROUPS = ((128, 1), (512, 4), (2048, 16))
ATTN_HEADS = 4
N_GROUPS = len(ATTN_GROUPS)
QKV_W = N_GROUPS * 3 * ATTN_HEADS * HEAD
ATTN_OUT = ATTN_HEADS * HEAD
ROPE_THETA = 500000.0
ROPE_DIM = HEAD // 4
HGRN_CHUNK = 32
HGRN_FWD_HEADS = 2
HGRN_BWD_HEADS = 1
HGRN_SUB = 4
HGRN_UNROLL = 2
ALPHA = 2.0 ** 0.25
LN_EPS = 1e-5
NEG_INF = -1e30
ADAM_LR, ADAM_B1, ADAM_B2, ADAM_EPS, ADAM_WD, ADAM_STEP = 0.001, 0.9, 0.999, 1e-08, 0.01, 10

N_SHARD = 4
VMEM_LIMIT = 56 * 1024 * 1024

NN = ((1,), (0,))
NT = ((1,), (1,))
TN = ((0,), (0,))


def _dot(a, b, dims, precision=None):
    return lax.dot_general(a, b, (dims, ((), ())), preferred_element_type=F32, precision=precision)


def _tile(n, pref, mult=128):
    best = None
    for t in range(mult, min(n, pref) + 1, mult):
        if n % t == 0:
            best = t
    return n if best is None else best


def _params(n_parallel, n_arbitrary):
    return pltpu.CompilerParams(
        dimension_semantics=("parallel",) * n_parallel + ("arbitrary",) * n_arbitrary,
        vmem_limit_bytes=VMEM_LIMIT)


def _sigmoid(x):
    return 1.0 / (1.0 + jnp.exp(-x))


def _silu(x):
    return x * _sigmoid(x)


class _Exchange:
    def __init__(self, ins, outs, aliases, sems, start, finish):
        self.ins, self.outs, self.aliases, self.sems = list(ins), list(outs), dict(aliases), list(sems)
        self.start, self.finish = start, finish


def _run(body, name, grid, ins, in_specs, outs, out_specs, scratch, n_arbitrary, side=None):
    n_in, n_out, n_scr = len(ins), len(outs), len(scratch)
    if side is None:
        return pl.pallas_call(
            body, name=name, grid=grid, in_specs=in_specs, out_specs=out_specs, out_shape=outs,
            scratch_shapes=scratch, compiler_params=_params(len(grid) - n_arbitrary, n_arbitrary))(*ins)
    s_in, s_out = len(side.ins), len(side.outs)
    i1 = n_in + s_in
    o1 = i1 + n_out
    o2 = o1 + s_out
    c1 = o2 + n_scr

    def wrapped(*refs):
        s_refs = (refs[n_in:i1], refs[o1:o2], refs[c1:])
        ids = [pl.program_id(a) for a in range(len(grid))]
        first = functools.reduce(jnp.logical_and, [i == 0 for i in ids])
        last = functools.reduce(jnp.logical_and, [i == g - 1 for i, g in zip(ids, grid)])

        @pl.when(first)
        def _():
            side.start(*s_refs)

        body(*refs[:n_in], *refs[i1:o1], *refs[o2:c1])

        @pl.when(last)
        def _():
            side.finish(*s_refs)

    res = pl.pallas_call(
        wrapped, name=name, grid=grid, in_specs=list(in_specs) + [ANY] * s_in,
        out_specs=list(out_specs) + [ANY] * s_out, out_shape=list(outs) + side.outs,
        scratch_shapes=list(scratch) + side.sems,
        input_output_aliases={n_in + a: n_out + b for a, b in side.aliases.items()},
        compiler_params=_params(0, len(grid)))(*ins, *side.ins)
    return res[:n_out], res[n_out:]


def _gemm(name, grid, ins, in_specs, outs, out_specs, accs, dot_fn, epi_fn, side=None):
    n_in, n_out, n_k = len(ins), len(outs), grid[-1]

    def body(*refs):
        in_refs, out_refs, acc_refs = refs[:n_in], refs[n_in:n_in + n_out], refs[n_in + n_out:]
        k = pl.program_id(len(grid) - 1)

        @pl.when(k == 0)
        def _():
            for a in acc_refs:
                a[...] = jnp.zeros(a.shape, F32)

        dot_fn(in_refs, acc_refs)

        @pl.when(k == n_k - 1)
        def _():
            epi_fn(in_refs, acc_refs, out_refs)

    return _run(body, name, grid, ins, in_specs, outs, out_specs, [pltpu.VMEM(s, F32) for s in accs], 1, side)


def _sds(shape, dtype):
    return jax.ShapeDtypeStruct(shape, dtype)


def _ffn_up(xb, w3, side=None):
    t, d = xb.shape
    nf = w3.shape[2]
    f = 2 * nf
    tm, tk = _tile(t, 512, 8), _tile(d, 512)

    def dot_fn(r, acc):
        acc[0][...] += _dot(r[0][...], r[1][...], NN)
        acc[1][...] += _dot(r[0][...], r[2][...], NN)

    def epi_fn(r, acc, out):
        g, u = acc[0][...], acc[1][...]
        out[0][0] = g.astype(BF16)
        out[0][1] = u.astype(BF16)
        out[1][...] = (_silu(g) * u).astype(BF16)

    return _gemm(
        "ffn_up", (t // tm, 2, d // tk), [xb, w3, w3],
        [pl.BlockSpec((tm, tk), lambda i, j, k: (i, k)),
         pl.BlockSpec((None, tk, nf), lambda i, j, k: (j, k, 0)),
         pl.BlockSpec((None, tk, nf), lambda i, j, k: (j + 2, k, 0))],
        [_sds((2, t, f), BF16), _sds((t, f), BF16)],
        [pl.BlockSpec((2, tm, nf), lambda i, j, k: (0, i, j)),
         pl.BlockSpec((tm, nf), lambda i, j, k: (i, j))],
        [(tm, nf), (tm, nf)], dot_fn, epi_fn, side)


def _down_ln(a, w, resid, coef, g, b):
    t, kd = a.shape
    d = w.shape[1]
    tm, tk = _tile(t, 512, 8), _tile(kd, 512)

    def dot_fn(r, acc):
        acc[0][...] += _dot(r[0][...], r[1][...], NN)

    def epi_fn(r, acc, out):
        v = ALPHA * r[2][...] + coef * acc[0][...]
        mu = jnp.mean(v, axis=-1, keepdims=True)
        c = v - mu
        var = jnp.mean(c * c, axis=-1, keepdims=True)
        rstd = lax.rsqrt(var + LN_EPS)
        xhat = c * rstd
        h = xhat * r[3][...] + r[4][...]
        out[0][...] = h
        out[1][...] = h.astype(BF16)
        out[2][...] = xhat
        out[3][...] = rstd

    row = pl.BlockSpec((tm, d), lambda i, k: (i, 0))
    vec = pl.BlockSpec((1, d), lambda i, k: (0, 0))
    return _gemm(
        "down_ln", (t // tm, kd // tk), [a, w, resid, g, b],
        [pl.BlockSpec((tm, tk), lambda i, k: (i, k)), pl.BlockSpec((tk, d), lambda i, k: (k, 0)), row, vec, vec],
        [_sds((t, d), F32), _sds((t, d), BF16), _sds((t, d), F32), _sds((t, 1), F32)],
        [row, row, row, pl.BlockSpec((tm, 1), lambda i, k: (i, 0))],
        [(tm, d)], dot_fn, epi_fn)


def _only(res, side):
    return res[0] if side is None else (res[0][0], res[1])


def _mm_w3(a, w3, out_dtype, side=None):
    t, kd = a.shape
    n = w3.shape[2]
    tm, tk = _tile(t, 512, 8), _tile(kd, 256)

    def dot_fn(r, acc):
        acc[0][...] += _dot(r[0][...], r[1][...], NN)

    def epi_fn(r, acc, out):
        out[0][...] = acc[0][...].astype(out_dtype)

    return _only(_gemm(
        "mm_w3", (t // tm, N_SHARD, kd // tk), [a, w3],
        [pl.BlockSpec((tm, tk), lambda i, j, k: (i, k)), pl.BlockSpec((None, tk, n), lambda i, j, k: (j, k, 0))],
        [_sds((t, N_SHARD * n), out_dtype)], [pl.BlockSpec((tm, n), lambda i, j, k: (i, j))],
        [(tm, n)], dot_fn, epi_fn, side), side)


def _mm_w2(a, w, out_dtype):
    t, kd = a.shape
    n = w.shape[1]
    tm, tn, tk = _tile(t, 512, 8), _tile(n, 1024), _tile(kd, 512)

    def dot_fn(r, acc):
        acc[0][...] += _dot(r[0][...], r[1][...], NN)

    def epi_fn(r, acc, out):
        out[0][...] = acc[0][...].astype(out_dtype)

    return _gemm(
        "mm_w2", (t // tm, n // tn, kd // tk), [a, w],
        [pl.BlockSpec((tm, tk), lambda i, j, k: (i, k)), pl.BlockSpec((tk, tn), lambda i, j, k: (k, j))],
        [_sds((t, n), out_dtype)], [pl.BlockSpec((tm, tn), lambda i, j, k: (i, j))],
        [(tm, tn)], dot_fn, epi_fn)[0]


def _branch_gate(ob, wb3, ya, proj, gate_col):
    t, kd = ob.shape
    n = wb3.shape[2]
    d = N_SHARD * n
    tm = _tile(t, 512, 8)
    ga0, gb0 = gate_col // n, (gate_col + d) // n

    def dot_fn(r, acc):
        acc[0][...] += _dot(r[0][...], r[1][...], NN)

    def epi_fn(r, acc, out):
        yb = acc[0][...]
        out[0][...] = yb
        out[1][...] = (_sigmoid(r[3][...]) * r[2][...] + _sigmoid(r[4][...]) * yb).astype(BF16)

    blk = pl.BlockSpec((tm, n), lambda i, j, k: (i, j))
    return _gemm(
        "branch_gate", (t // tm, N_SHARD, 1), [ob, wb3, ya, proj, proj],
        [pl.BlockSpec((tm, kd), lambda i, j, k: (i, 0)), pl.BlockSpec((None, kd, n), lambda i, j, k: (j, 0, 0)), blk,
         pl.BlockSpec((tm, n), lambda i, j, k: (i, ga0 + j)), pl.BlockSpec((tm, n), lambda i, j, k: (i, gb0 + j))],
        [_sds((t, d), F32), _sds((t, d), BF16)], [blk, blk], [(tm, n)], dot_fn, epi_fn)


def _swiglu_bwd(dyb, w, u3):
    t, d = dyb.shape
    f = w.shape[0]
    tm, tr = _tile(t, 1024, 8), _tile(f, 512)

    def dot_fn(r, acc):
        acc[0][...] += _dot(r[0][...], r[1][...], NT)

    def epi_fn(r, acc, out):
        da = acc[0][...]
        g, u = r[2][0].astype(F32), r[2][1].astype(F32)
        s = _sigmoid(g)
        out[0][0] = (da * u * s * (1.0 + g * (1.0 - s))).astype(BF16)
        out[0][1] = (da * g * s).astype(BF16)

    ublk = pl.BlockSpec((2, tm, tr), lambda i, j, k: (0, i, j))
    return _gemm(
        "swiglu_bwd", (t // tm, f // tr, 1), [dyb, w, u3],
        [pl.BlockSpec((tm, d), lambda i, j, k: (i, 0)), pl.BlockSpec((tr, d), lambda i, j, k: (j, 0)), ublk],
        [_sds((2, t, f), BF16)], [ublk], [(tm, tr)], dot_fn, epi_fn)[0]


def _ffn_dx(du3, w3, resid, side=None):
    t = du3.shape[1]
    d, nf = w3.shape[1], w3.shape[2]
    tm, tr = _tile(t, 512, 8), _tile(d, 1024)

    def dot_fn(r, acc):
        acc[0][...] += _dot(r[0][...], r[1][...], NT)

    def epi_fn(r, acc, out):
        out[0][...] = acc[0][...] + r[2][...]

    blk = pl.BlockSpec((tm, tr), lambda i, j, k: (i, j))
    return _only(_gemm(
        "ffn_dx", (t // tm, d // tr, N_SHARD), [du3, w3, resid],
        [pl.BlockSpec((None, tm, nf), lambda i, j, k: (k // 2, i, k % 2)),
         pl.BlockSpec((None, tr, nf), lambda i, j, k: (k, j, 0)), blk],
        [_sds((t, d), F32)], [blk], [(tm, tr)], dot_fn, epi_fn, side), side)


def _nt_w3(dy, w3, resid, side=None):
    t = dy.shape[0]
    kd, n = w3.shape[1], w3.shape[2]
    tm, tr = _tile(t, 512, 8), _tile(kd, 1024)
    has_res = resid is not None

    def dot_fn(r, acc):
        acc[0][...] += _dot(r[0][...], r[1][...], NT)

    def epi_fn(r, acc, out):
        v = acc[0][...]
        if has_res:
            v = v + r[2][...]
        out[0][...] = v

    blk = pl.BlockSpec((tm, tr), lambda i, j, k: (i, j))
    return _only(_gemm(
        "nt_w3", (t // tm, kd // tr, N_SHARD), [dy, w3] + ([resid] if has_res else []),
        [pl.BlockSpec((tm, n), lambda i, j, k: (i, k)), pl.BlockSpec((None, tr, n), lambda i, j, k: (k, j, 0))]
        + ([blk] if has_res else []),
        [_sds((t, kd), F32)], [blk], [(tm, tr)], dot_fn, epi_fn, side), side)


def _nt_w2(dy, w, out_dtype):
    t, n = dy.shape
    r_ = w.shape[0]
    tm, tr, tk = _tile(t, 512, 8), _tile(r_, 1024), _tile(n, 1024)

    def dot_fn(r, acc):
        acc[0][...] += _dot(r[0][...], r[1][...], NT)

    def epi_fn(r, acc, out):
        out[0][...] = acc[0][...].astype(out_dtype)

    return _gemm(
        "nt_w2", (t // tm, r_ // tr, n // tk), [dy, w],
        [pl.BlockSpec((tm, tk), lambda i, j, k: (i, k)), pl.BlockSpec((tr, tk), lambda i, j, k: (j, k))],
        [_sds((t, r_), out_dtype)], [pl.BlockSpec((tm, tr), lambda i, j, k: (i, j))],
        [(tm, tr)], dot_fn, epi_fn)[0]


def _gate_bwd(dmix, w, proj, ya, yb, gate_col):
    t, n = dmix.shape
    d = w.shape[0]
    tm, tr = _tile(t, 512, 8), _tile(d, 512)
    ga0, gb0 = gate_col // tr, (gate_col + d) // tr
    nb = d // tr

    def dot_fn(r, acc):
        acc[0][...] += _dot(r[0][...], r[1][...], NT)

    def epi_fn(r, acc, out):
        dz = acc[0][...]
        ga, gb = _sigmoid(r[2][...]), _sigmoid(r[3][...])
        out[0][...] = (dz * ga).astype(BF16)
        out[1][...] = (dz * gb).astype(BF16)
        out[2][...] = (dz * r[4][...] * ga * (1.0 - ga)).astype(BF16)
        out[3][...] = (dz * r[5][...] * gb * (1.0 - gb)).astype(BF16)

    blk = pl.BlockSpec((tm, tr), lambda i, j, k: (i, j))
    return _gemm(
        "gate_bwd", (t // tm, nb, 1), [dmix, w, proj, proj, ya, yb],
        [pl.BlockSpec((tm, n), lambda i, j, k: (i, 0)), pl.BlockSpec((tr, n), lambda i, j, k: (j, 0)),
         pl.BlockSpec((tm, tr), lambda i, j, k: (i, ga0 + j)), pl.BlockSpec((tm, tr), lambda i, j, k: (i, gb0 + j)),
         blk, blk],
        [_sds((t, d), BF16), _sds((t, d), BF16), _sds((t, d), BF16), _sds((t, d), BF16)],
        [blk, blk, blk, blk], [(tm, tr)], dot_fn, epi_fn)


def _tn_w3(a, dy, dy_map, n):
    t, kd = a.shape
    tm, tkk = _tile(t, 512, 8), _tile(kd, 512)

    def dot_fn(r, acc):
        acc[0][...] += _dot(r[0][...], r[1][...], TN)

    def epi_fn(r, acc, out):
        out[0][...] = acc[0][...].astype(BF16)

    dy_block = (tm, n) if dy.ndim == 2 else (None, tm, n)
    return _gemm(
        "tn_w3", (kd // tkk, N_SHARD, t // tm), [a, dy],
        [pl.BlockSpec((tm, tkk), lambda i, j, m: (m, i)), pl.BlockSpec(dy_block, lambda i, j, m: dy_map(j, m))],
        [_sds((N_SHARD, kd, n), BF16)], [pl.BlockSpec((None, tkk, n), lambda i, j, m: (j, i, 0))],
        [(tkk, n)], dot_fn, epi_fn)[0]


def _tn_w2(a, dy):
    t, kd = a.shape
    n = dy.shape[1]
    tm, tkk, tn = _tile(t, 512, 8), _tile(kd, 512), _tile(n, 2048)

    def dot_fn(r, acc):
        acc[0][...] += _dot(r[0][...], r[1][...], TN)

    def epi_fn(r, acc, out):
        out[0][...] = acc[0][...].astype(BF16)

    return _gemm(
        "tn_w2", (kd // tkk, n // tn, t // tm), [a, dy],
        [pl.BlockSpec((tm, tkk), lambda i, j, m: (m, i)), pl.BlockSpec((tm, tn), lambda i, j, m: (m, j))],
        [_sds((kd, n), BF16)], [pl.BlockSpec((tkk, tn), lambda i, j, m: (i, j))],
        [(tkk, tn)], dot_fn, epi_fn)[0]


def _ln_bwd(dh_parts, xhat, rstd, g, coef, target=None):
    t, d = xhat.shape
    tm = _tile(t, 256, 8)
    n_parts = len(dh_parts)
    with_loss = target is not None
    ins = list(dh_parts) + [xhat, rstd, g] + ([target] if with_loss else [])

    def body(*refs):
        parts = refs[:n_parts]
        xh_ref, rstd_ref, g_ref = refs[n_parts:n_parts + 3]
        o = n_parts + 3 + (1 if with_loss else 0)
        dres_ref, dyb_ref, dg_ref, db_ref, loss_ref = refs[o:o + 5]
        i = pl.program_id(0)
        dh = parts[0][...]
        for p in parts[1:]:
            dh = dh + p[...]
        if with_loss:
            e = dh - refs[o - 1][...]
            part = 0.5 * jnp.sum(jnp.sum(e * e, axis=-1, keepdims=True) * (1.0 / d), axis=0, keepdims=True)
            dh = e * (1.0 / d)
        else:
            part = jnp.zeros((1, 1), F32)
        xh = xh_ref[...]
        dxh = dh * g_ref[...]
        m1 = jnp.mean(dxh, axis=-1, keepdims=True)
        m2 = jnp.mean(dxh * xh, axis=-1, keepdims=True)
        dv = rstd_ref[...] * (dxh - m1 - xh * m2)
        dres_ref[...] = ALPHA * dv
        dyb_ref[...] = (coef * dv).astype(BF16)

        @pl.when(i == 0)
        def _():
            dg_ref[...] = jnp.zeros(dg_ref.shape, F32)
            db_ref[...] = jnp.zeros(db_ref.shape, F32)
            loss_ref[...] = jnp.zeros(loss_ref.shape, F32)

        dg_ref[...] += jnp.sum(dh * xh, axis=0, keepdims=True)
        db_ref[...] += jnp.sum(dh, axis=0, keepdims=True)
        loss_ref[...] += jnp.broadcast_to(part, loss_ref.shape)

    row = pl.BlockSpec((tm, d), lambda i: (i, 0))
    vec = pl.BlockSpec((1, d), lambda i: (0, 0))
    return pl.pallas_call(
        body, name="ln_bwd", grid=(t // tm,),
        in_specs=[row] * n_parts + [row, pl.BlockSpec((tm, 1), lambda i: (i, 0)), vec] + ([row] if with_loss else []),
        out_specs=[row, row, vec, vec, pl.BlockSpec((1, 128), lambda i: (0, 0))],
        out_shape=[_sds((t, d), F32), _sds((t, d), BF16), _sds((1, d), F32), _sds((1, d), F32), _sds((1, 128), F32)],
        compiler_params=_params(0, 1))(*ins)


def _cast_bf16(x2d):
    t, d = x2d.shape
    tm = _tile(t, 512, 8)

    def body(x_ref, o_ref):
        o_ref[...] = x_ref[...].astype(BF16)

    row = pl.BlockSpec((tm, d), lambda i: (i, 0))
    return pl.pallas_call(body, name="cast_bf16", grid=(t // tm,), in_specs=[row], out_specs=row,
                          out_shape=_sds((t, d), BF16), compiler_params=_params(1, 0))(x2d)


def _lower_bound(table):
    t0, t1 = table[0:1, :], table[1:2, :]
    m = jnp.maximum(t0, t1)
    e0, e1 = jnp.exp(t0 - m), jnp.exp(t1 - m)
    return e0 / (e0 + e1)


def _chunk_tri(rows, upper):
    r = lax.broadcasted_iota(jnp.int32, (rows, rows), 0)
    s = lax.broadcasted_iota(jnp.int32, (rows, rows), 1)
    shift = HGRN_CHUNK.bit_length() - 1
    same = lax.shift_right_logical(r, shift) == lax.shift_right_logical(s, shift)
    return same & ((r <= s) if upper else (r >= s))


def _tri_apply(x, upper):
    tri = _chunk_tri(x.shape[0], upper).astype(F32).astype(BF16)
    hi = x.astype(BF16)
    r1 = x - hi.astype(F32)
    mid = r1.astype(BF16)
    lo = (r1 - mid.astype(F32)).astype(BF16)
    return _dot(tri, hi, NN) + _dot(tri, mid, NN) + _dot(tri, lo, NN)


@functools.partial(jax.custom_vjp, nondiff_argnums=(1,))
def _chunk_cumsum(x, upper):
    return _tri_apply(x, upper)


def _chunk_cumsum_fwd(x, upper):
    return _tri_apply(x, upper), None


def _chunk_cumsum_bwd(upper, _, g):
    return (_tri_apply(g, not upper),)


_chunk_cumsum.defvjp(_chunk_cumsum_fwd, _chunk_cumsum_bwd)


def _hgrn_block(hq, hf, hi, lb, st, upper):
    c = HGRN_CHUNK
    rows = hq.shape[0]
    n_sub = rows // c
    q = _silu(hq)
    f = lb + (1.0 - lb) * _sigmoid(hf)
    lg = jnp.log(f)
    cum = _chunk_cumsum(lg, upper)
    tots = [jnp.sum(lg[n * c:(n + 1) * c], axis=0, keepdims=True) for n in range(n_sub)]
    totb = jnp.concatenate([jnp.broadcast_to(t, (c, HEAD)) for t in tots], axis=0)
    kk = 1.0 - f
    qd = (q * jnp.exp(cum)).astype(BF16)
    kd = (kk * jnp.exp(-cum)).astype(BF16)
    ke = (kk * jnp.exp(totb - cum)).astype(BF16)
    vb = hi.astype(BF16)
    a = jnp.where(_chunk_tri(rows, upper), _dot(qd, kd, NT), 0.0)
    o_intra = _dot(a.astype(BF16), vb, NN)
    o_inter = [None] * n_sub
    for n in (range(n_sub - 1, -1, -1) if upper else range(n_sub)):
        sl = slice(n * c, (n + 1) * c)
        o_inter[n] = _dot(qd[sl], st.astype(BF16), NT)
        st = st * jnp.exp(tots[n]) + _dot(vb[sl], ke[sl], TN)
    return o_intra + jnp.concatenate(o_inter, axis=0), st


def _hgrn_fwd(proj3, lb_f, lb_b, norm_g, d):
    b_, s_, _ = proj3.shape
    nh = d // HEAD
    hb = HGRN_FWD_HEADS
    wid = hb * HEAD
    rows = HGRN_CHUNK * HGRN_SUB
    n_blk = s_ // rows

    def body(hq_ref, hff_ref, hfb_ref, hi_ref, hog_ref, lbf_ref, lbb_ref, g_ref, oraw_ref, out_ref, of_ref, ob_ref):
        lbf, lbb = _lower_bound(lbf_ref[...]), _lower_bound(lbb_ref[...])

        def step(j, carry):
            sts = list(carry)
            rf = pl.ds(pl.multiple_of(j * rows, rows), rows)
            rb = pl.ds(pl.multiple_of((n_blk - 1 - j) * rows, rows), rows)
            for hh in range(hb):
                cs = slice(hh * HEAD, (hh + 1) * HEAD)
                o_f, sts[2 * hh] = _hgrn_block(hq_ref[rf, cs], hff_ref[rf, cs], hi_ref[rf, cs], lbf[:, cs],
                                               sts[2 * hh], False)
                o_b, sts[2 * hh + 1] = _hgrn_block(hq_ref[rb, cs], hfb_ref[rb, cs], hi_ref[rb, cs], lbb[:, cs],
                                                   sts[2 * hh + 1], True)
                of_ref[rf, cs] = o_f
                ob_ref[rb, cs] = o_b
            return tuple(sts)

        z = jnp.zeros((HEAD, HEAD), F32)
        lax.fori_loop(0, n_blk, step, (z,) * (2 * hb), unroll=HGRN_UNROLL)
        for hh in range(hb):
            cs = slice(hh * HEAD, (hh + 1) * HEAD)
            o = of_ref[:, cs] + ob_ref[:, cs]
            oraw_ref[:, cs] = o
            on = o * lax.rsqrt(jnp.mean(o * o, axis=-1, keepdims=True) + LN_EPS)
            out_ref[:, cs] = (on * g_ref[:, cs] * _silu(hog_ref[:, cs])).astype(BF16)

    def col(k):
        return pl.BlockSpec((None, s_, wid), lambda h, b: (b, 0, k * (nh // hb) + h))

    tab = pl.BlockSpec((2, wid), lambda h, b: (0, h))
    oblk = pl.BlockSpec((None, s_, wid), lambda h, b: (b, 0, h))
    return pl.pallas_call(
        body, name="hgrn_fwd", grid=(nh // hb, b_),
        in_specs=[col(0), col(1), col(2), col(3), col(4), tab, tab, pl.BlockSpec((1, wid), lambda h, b: (0, h))],
        out_specs=[oblk, oblk], out_shape=[_sds((b_, s_, d), F32), _sds((b_, s_, d), BF16)],
        scratch_shapes=[pltpu.VMEM((s_, wid), F32), pltpu.VMEM((s_, wid), F32)],
        compiler_params=_params(2, 0))(proj3, proj3, proj3, proj3, proj3, lb_f, lb_b, norm_g)


def _hgrn_bwd(proj3, lb_f, lb_b, norm_g, o_raw, do_a, d, side=None):
    b_, s_, _ = proj3.shape
    nh = d // HEAD
    hb = HGRN_BWD_HEADS
    wid = hb * HEAD
    rows = HGRN_CHUNK * HGRN_SUB
    n_blk = s_ // rows

    def body(hq_ref, hff_ref, hfb_ref, hi_ref, hog_ref, lbf_ref, lbb_ref, g_ref, oraw_ref, doa_ref,
             dq_ref, dff_ref, dfb_ref, di_ref, dog_ref, dlbf_ref, dlbb_ref, dg_ref,
             st_ref, dor_ref, dq2_ref, di2_ref):
        b = pl.program_id(1)
        tab_f, tab_b = lbf_ref[...], lbb_ref[...]
        lbf, lbb = _lower_bound(tab_f), _lower_bound(tab_b)

        dg_parts = []
        for hh in range(hb):
            cs = slice(hh * HEAD, (hh + 1) * HEAD)
            o, doa, hog, g = oraw_ref[:, cs], doa_ref[:, cs], hog_ref[:, cs], g_ref[:, cs]
            rs = lax.rsqrt(jnp.mean(o * o, axis=-1, keepdims=True) + LN_EPS)
            on = o * rs
            sg = _sigmoid(hog)
            gate = hog * sg
            dog_ref[:, cs] = (doa * on * g * sg * (1.0 + hog * (1.0 - sg))).astype(BF16)
            don = doa * g * gate
            dor_ref[:, cs] = rs * (don - on * jnp.mean(don * on, axis=-1, keepdims=True))
            dg_parts.append(jnp.sum(doa * on * gate, axis=0, keepdims=True))

        def fwd_step(j, carry):
            sts = list(carry)
            jb = n_blk - 1 - j
            rf = pl.ds(pl.multiple_of(j * rows, rows), rows)
            rb = pl.ds(pl.multiple_of(jb * rows, rows), rows)
            for hh in range(hb):
                cs = slice(hh * HEAD, (hh + 1) * HEAD)
                st_ref[2 * hh, j] = sts[2 * hh]
                st_ref[2 * hh + 1, jb] = sts[2 * hh + 1]
                _, sts[2 * hh] = _hgrn_block(hq_ref[rf, cs], hff_ref[rf, cs], hi_ref[rf, cs], lbf[:, cs],
                                             sts[2 * hh], False)
                _, sts[2 * hh + 1] = _hgrn_block(hq_ref[rb, cs], hfb_ref[rb, cs], hi_ref[rb, cs], lbb[:, cs],
                                                 sts[2 * hh + 1], True)
            return tuple(sts)

        z = jnp.zeros((HEAD, HEAD), F32)
        lax.fori_loop(0, n_blk, fwd_step, (z,) * (2 * hb), unroll=HGRN_UNROLL)

        def bwd_step(j, carry):
            gs, dls = list(carry[0]), list(carry[1])
            jf = n_blk - 1 - j
            rf = pl.ds(pl.multiple_of(jf * rows, rows), rows)
            rb = pl.ds(pl.multiple_of(j * rows, rows), rows)
            for hh in range(hb):
                cs = slice(hh * HEAD, (hh + 1) * HEAD)
                _, vjp_f = jax.vjp(lambda a0, a1, a2, a3, a4: _hgrn_block(a0, a1, a2, a3, a4, False),
                                   hq_ref[rf, cs], hff_ref[rf, cs], hi_ref[rf, cs], lbf[:, cs], st_ref[2 * hh, jf])
                dq, df, di, dl, gs[2 * hh] = vjp_f((dor_ref[rf, cs], gs[2 * hh]))
                dq_ref[rf, cs] = dq.astype(BF16)
                dff_ref[rf, cs] = df.astype(BF16)
                di_ref[rf, cs] = di.astype(BF16)
                dls[2 * hh] = dls[2 * hh] + dl
                _, vjp_b = jax.vjp(lambda a0, a1, a2, a3, a4: _hgrn_block(a0, a1, a2, a3, a4, True),
                                   hq_ref[rb, cs], hfb_ref[rb, cs], hi_ref[rb, cs], lbb[:, cs], st_ref[2 * hh + 1, j])
                dq, df, di, dl, gs[2 * hh + 1] = vjp_b((dor_ref[rb, cs], gs[2 * hh + 1]))
                dq2_ref[rb, cs] = dq
                dfb_ref[rb, cs] = df.astype(BF16)
                di2_ref[rb, cs] = di
                dls[2 * hh + 1] = dls[2 * hh + 1] + dl
            return tuple(gs), tuple(dls)

        zl = jnp.zeros((1, HEAD), F32)
        _, dls = lax.fori_loop(0, n_blk, bwd_step, ((z,) * (2 * hb), (zl,) * (2 * hb)), unroll=HGRN_UNROLL)
        dq_ref[...] = (dq_ref[...].astype(F32) + dq2_ref[...]).astype(BF16)
        di_ref[...] = (di_ref[...].astype(F32) + di2_ref[...]).astype(BF16)

        _, vjp_tf = jax.vjp(_lower_bound, tab_f)
        _, vjp_tb = jax.vjp(_lower_bound, tab_b)

        @pl.when(b == 0)
        def _():
            dlbf_ref[...] = jnp.zeros(dlbf_ref.shape, F32)
            dlbb_ref[...] = jnp.zeros(dlbb_ref.shape, F32)
            dg_ref[...] = jnp.zeros(dg_ref.shape, F32)

        dlbf_ref[...] += vjp_tf(jnp.concatenate(dls[0::2], axis=1))[0]
        dlbb_ref[...] += vjp_tb(jnp.concatenate(dls[1::2], axis=1))[0]
        dg_ref[...] += jnp.concatenate(dg_parts, axis=1)

    def col(k):
        return pl.BlockSpec((None, s_, wid), lambda h, b: (b, 0, k * (nh // hb) + h))

    tab = pl.BlockSpec((2, wid), lambda h, b: (0, h))
    vec = pl.BlockSpec((1, wid), lambda h, b: (0, h))
    oblk = pl.BlockSpec((None, s_, wid), lambda h, b: (b, 0, h))
    seq = _sds((b_, s_, d), BF16)
    return _run(
        body, "hgrn_bwd", (nh // hb, b_), [proj3, proj3, proj3, proj3, proj3, lb_f, lb_b, norm_g, o_raw, do_a],
        [col(0), col(1), col(2), col(3), col(4), tab, tab, vec, oblk, oblk],
        [seq, seq, seq, seq, seq, _sds((2, d), F32), _sds((2, d), F32), _sds((1, d), F32)],
        [oblk, oblk, oblk, oblk, oblk, tab, tab, vec],
        [pltpu.VMEM((2 * hb, n_blk, HEAD, HEAD), F32),
         pltpu.VMEM((s_, wid), F32), pltpu.VMEM((s_, wid), F32), pltpu.VMEM((s_, wid), F32)], 1, side)


def _rope_tables(s_):
    inv = ROPE_THETA ** (-jnp.arange(0, ROPE_DIM, 2, dtype=F32) / ROPE_DIM)
    ang = jnp.arange(s_, dtype=F32)[:, None] * inv
    cos, sin = jnp.cos(ang), jnp.sin(ang)
    rest = HEAD - ROPE_DIM
    ctab = jnp.concatenate([cos, cos, jnp.ones((s_, rest), F32)], axis=1)
    stab = jnp.concatenate([-sin, sin, jnp.zeros((s_, rest), F32)], axis=1)
    half = ROPE_DIM // 2
    perm = np.zeros((HEAD, HEAD), np.float32)
    for i in range(half):
        perm[i + half, i] = 1.0
        perm[i, i + half] = 1.0
    return ctab, stab, jnp.asarray(perm)


def _attn_tile(qr, kr, v, cq, sq, ck, sk, perm, qi0, kj0, half):
    hi = lax.Precision.HIGHEST
    q = qr * cq + _dot(qr, perm, NN, precision=hi) * sq
    k = kr * ck + _dot(kr, perm, NN, precision=hi) * sk
    s = _dot(q.astype(BF16), k.astype(BF16), NT) * (HEAD ** -0.5)
    qi = qi0 + lax.broadcasted_iota(jnp.int32, s.shape, 0)
    kj = kj0 + lax.broadcasted_iota(jnp.int32, s.shape, 1)
    s = jnp.where(jnp.abs(qi - kj) <= half, s, NEG_INF)
    m = lax.stop_gradient(jnp.max(s, axis=-1, keepdims=True))
    p = jnp.exp(s - m)
    den = jnp.sum(p, axis=-1, keepdims=True)
    o = _dot(p.astype(BF16), v.astype(BF16), NN) / den
    lse = m + jnp.log(den)
    return o, jnp.broadcast_to(lse, o.shape)


def _attn_tiling(seg):
    tq = seg if seg <= 256 else 256
    kw = seg if seg <= 512 else 512
    tiles = []
    for i in range(seg // tq):
        ws = min(max(i * tq - (kw - tq) // 2, 0), seg - kw)
        tiles.append((i * tq, ws))
    return tq, kw, tiles


def _attn_specs(b_, dil, seg):
    qkv = pl.BlockSpec((None, None, 3, None, seg, HEAD), lambda b, r, h: (b, r, 0, h, 0, 0))
    tab = pl.BlockSpec((None, seg, HEAD), lambda b, r, h: (r, 0, 0))
    perm = pl.BlockSpec((HEAD, HEAD), lambda b, r, h: (0, 0))
    oblk = pl.BlockSpec((None, None, None, seg, HEAD), lambda b, r, h: (b, r, h, 0, 0))
    return qkv, tab, perm, oblk


def _attn_fwd(qkv_g, ctab, stab, perm, window, dil):
    b_, _, _, nh, seg, _ = qkv_g.shape
    half = window // (2 * dil)
    tq, kw, tiles = _attn_tiling(seg)

    def body(qkv_ref, c_ref, s_ref, p_ref, o_ref, l_ref):
        pm = p_ref[...]
        for q0, k0 in tiles:
            rq, rk = pl.ds(q0, tq), pl.ds(k0, kw)
            o, l = _attn_tile(qkv_ref[0, rq, :], qkv_ref[1, rk, :], qkv_ref[2, rk, :], c_ref[rq, :], s_ref[rq, :],
                              c_ref[rk, :], s_ref[rk, :], pm, q0, k0, half)
            o_ref[rq, :] = o
            l_ref[rq, :] = l

    qkv, tab, pspec, oblk = _attn_specs(b_, dil, seg)
    shp = _sds((b_, dil, nh, seg, HEAD), F32)
    return pl.pallas_call(
        body, name=f"attn_fwd_d{dil}", grid=(b_, dil, nh), in_specs=[qkv, tab, tab, pspec],
        out_specs=[oblk, oblk], out_shape=[shp, shp], compiler_params=_params(3, 0))(qkv_g, ctab, stab, perm)


def _attn_bwd(qkv_g, ctab, stab, perm, do, dl, window, dil):
    b_, _, _, nh, seg, _ = qkv_g.shape
    half = window // (2 * dil)
    tq, kw, tiles = _attn_tiling(seg)

    def body(qkv_ref, c_ref, s_ref, p_ref, do_ref, dl_ref, d_ref):
        pm = p_ref[...]
        d_ref[...] = jnp.zeros(d_ref.shape, F32)
        for q0, k0 in tiles:
            rq, rk = pl.ds(q0, tq), pl.ds(k0, kw)
            cq, sq, ck, sk = c_ref[rq, :], s_ref[rq, :], c_ref[rk, :], s_ref[rk, :]
            _, vjp = jax.vjp(lambda a, b, c: _attn_tile(a, b, c, cq, sq, ck, sk, pm, q0, k0, half),
                             qkv_ref[0, rq, :], qkv_ref[1, rk, :], qkv_ref[2, rk, :])
            dq, dk, dv = vjp((do_ref[rq, :], dl_ref[rq, :]))
            d_ref[0, rq, :] = dq
            d_ref[1, rk, :] += dk
            d_ref[2, rk, :] += dv

    qkv, tab, pspec, oblk = _attn_specs(b_, dil, seg)
    return pl.pallas_call(
        body, name=f"attn_bwd_d{dil}", grid=(b_, dil, nh), in_specs=[qkv, tab, tab, pspec, oblk, oblk],
        out_specs=qkv, out_shape=_sds(qkv_g.shape, F32), compiler_params=_params(3, 0))(
            qkv_g, ctab, stab, perm, do, dl)


def _combine(os_, ls_):
    m = jnp.maximum(jnp.maximum(ls_[0], ls_[1]), ls_[2])
    es = [jnp.exp(l - m) for l in ls_]
    return (es[0] * os_[0] + es[1] * os_[1] + es[2] * os_[2]) / (es[0] + es[1] + es[2])


def _combine_fwd(os_, ls_):
    t, w = os_[0].shape
    tm = _tile(t, 512, 8)

    def body(*refs):
        refs[6][...] = _combine([r[...] for r in refs[:3]], [r[...] for r in refs[3:6]]).astype(BF16)

    row = pl.BlockSpec((tm, w), lambda i: (i, 0))
    return pl.pallas_call(body, name="combine_fwd", grid=(t // tm,), in_specs=[row] * 6, out_specs=row,
                          out_shape=_sds((t, w), BF16), compiler_params=_params(1, 0))(*os_, *ls_)


def _combine_bwd(os_, ls_, dob):
    t, w = os_[0].shape
    tm = _tile(t, 512, 8)

    def body(*refs):
        _, vjp = jax.vjp(lambda *a: _combine(a[:3], a[3:]), *[r[...] for r in refs[:6]])
        for r, g in zip(refs[7:], vjp(refs[6][...])):
            r[...] = g

    row = pl.BlockSpec((tm, w), lambda i: (i, 0))
    return pl.pallas_call(body, name="combine_bwd", grid=(t // tm,), in_specs=[row] * 7, out_specs=[row] * 6,
                          out_shape=[_sds((t, w), F32)] * 6, compiler_params=_params(1, 0))(*os_, *ls_, dob)


def _to_residues(cols, b_, s_, dil):
    seg = s_ // dil
    return cols.reshape(b_, seg, dil, 3, ATTN_HEADS, HEAD).transpose(0, 2, 3, 4, 1, 5)


def _from_residues(a, b_, s_):
    if a.ndim == 6:
        return a.transpose(0, 4, 1, 2, 3, 5).reshape(b_ * s_, 3 * ATTN_HEADS * HEAD)
    return a.transpose(0, 3, 1, 2, 4).reshape(b_ * s_, ATTN_HEADS * HEAD)


ROW_SHARDED = ("ffn1_out", "ffn2_out", "wa", "mix_out")


def _local_step(x, target, fulls, p, sc_arr):
    b_, s_, d = x.shape
    t = b_ * s_
    x2 = x.reshape(t, d)
    gate_col = 5 * d + QKV_W
    c_arr = sc_arr[1:2]
    w = {}

    def arrived(keys, arrays):
        for k, a in zip(keys, arrays):
            w[k] = a.reshape(a.shape[0] * a.shape[1], a.shape[2]) if k in ROW_SHARDED else a

    def reduce_begin(keys):
        g3 = [gw[k].reshape(fulls[k].shape) for k in keys]
        return [_sum_halves(a, b, c_arr) for a, b in zip(g3, _swap_halves(g3))]

    def reduce_end(keys, parts, got):
        for k, a, b in zip(keys, parts, got):
            reduced[k] = _sum_chips(a, b, sc_arr)

    arrived(["ffn1_in"], _gather_weights([fulls["ffn1_in"]]))
    xb = _cast_bf16(x2)
    keys = ["ffn1_out", "mix_in"]
    (u1, a1), got = _ffn_up(xb, w["ffn1_in"], _gather_exchange([fulls[k] for k in keys]))
    arrived(keys, got)
    h1, h1b, xh1, rs1 = _down_ln(a1, w["ffn1_out"], x2, 0.5, p["ln1_g"], p["ln1_b"])

    keys = ["wa", "wb", "mix_out", "ffn2_in", "ffn2_out"]
    proj, got = _mm_w3(h1b, w["mix_in"], F32, _gather_exchange([fulls[k] for k in keys]))
    arrived(keys, got)
    n_in = proj.shape[1]
    proj3 = proj.reshape(b_, s_, n_in)
    o_raw, oa = _hgrn_fwd(proj3, p["hgrn_lb_fwd"], p["hgrn_lb_bwd"], p["hgrn_norm_g"], d)
    oa2 = oa.reshape(t, d)

    ctab, stab, perm = _rope_tables(s_)
    qkv_gs, tabs, os_, ls_ = [], [], [], []
    for g, (window, dil) in enumerate(ATTN_GROUPS):
        c0 = 5 * d + g * 3 * ATTN_OUT
        qkv_g = _to_residues(proj[:, c0:c0 + 3 * ATTN_OUT], b_, s_, dil)
        seg = s_ // dil
        ct = ctab.reshape(seg, dil, HEAD).transpose(1, 0, 2)
        st = stab.reshape(seg, dil, HEAD).transpose(1, 0, 2)
        o_g, l_g = _attn_fwd(qkv_g, ct, st, perm, window, dil)
        qkv_gs.append(qkv_g)
        tabs.append((ct, st))
        os_.append(_from_residues(o_g, b_, s_))
        ls_.append(_from_residues(l_g, b_, s_))
    ob = _combine_fwd(os_, ls_)

    ya = _mm_w2(oa2, w["wa"], F32)
    yb, zb = _branch_gate(ob, w["wb"], ya, proj, gate_col)
    h2, h2b, xh2, rs2 = _down_ln(zb, w["mix_out"], h1, 1.0, p["ln2_g"], p["ln2_b"])

    u2, a2 = _ffn_up(h2b, w["ffn2_in"])
    h3, _, xh3, rs3 = _down_ln(a2, w["ffn2_out"], h2, 0.5, p["ln3_g"], p["ln3_b"])

    gw, gp, reduced = {}, {}, {}
    nf = w["ffn2_in"].shape[2]

    def du_map(j, m):
        return (j // 2, m, j % 2)

    dres3, dy3, gp["ln3_g"], gp["ln3_b"], loss = _ln_bwd([h3], xh3, rs3, p["ln3_g"], 0.5, target.reshape(t, d))
    du2 = _swiglu_bwd(dy3, w["ffn2_out"], u2)
    gw["ffn2_out"] = _tn_w2(a2, dy3)
    gw["ffn2_in"] = _tn_w3(h2b, du2, du_map, nf)
    parts_a = reduce_begin(["ffn2_out", "ffn2_in"])
    dh2, got = _ffn_dx(du2, w["ffn2_in"], dres3, _scatter_exchange(parts_a[:1]))
    reduce_end(["ffn2_out"], parts_a[:1], got)

    dres2, dmix, gp["ln2_g"], gp["ln2_b"], _ = _ln_bwd([dh2], xh2, rs2, p["ln2_g"], 1.0)
    dya, dyb, dpga, dpgb = _gate_bwd(dmix, w["mix_out"], proj, ya, yb, gate_col)
    gw["mix_out"] = _tn_w2(zb, dmix)
    do_a = _nt_w2(dya, w["wa"], F32)
    gw["wa"] = _tn_w2(oa2, dya)
    nb = w["wb"].shape[2]
    do_b = _nt_w3(dyb, w["wb"], None)
    gw["wb"] = _tn_w3(ob, dyb, lambda j, m: (m, j), nb)
    keys = ["ffn2_in", "mix_out", "wa", "wb"]
    parts_b = parts_a[1:] + reduce_begin(keys[1:])

    (dq, dff, dfb, di, dog, gp["hgrn_lb_fwd"], gp["hgrn_lb_bwd"], gp["hgrn_norm_g"]), got = _hgrn_bwd(
        proj3, p["hgrn_lb_fwd"], p["hgrn_lb_bwd"], p["hgrn_norm_g"], o_raw, do_a.reshape(b_, s_, d), d,
        _scatter_exchange(parts_b))
    reduce_end(keys, parts_b, got)

    douts = _combine_bwd(os_, ls_, do_b)
    dqkv = []
    for g, (window, dil) in enumerate(ATTN_GROUPS):
        seg = s_ // dil
        do_g = douts[g].reshape(b_, seg, dil, ATTN_HEADS, HEAD).transpose(0, 2, 3, 1, 4)
        dl_g = douts[3 + g].reshape(b_, seg, dil, ATTN_HEADS, HEAD).transpose(0, 2, 3, 1, 4)
        dqkv_g = _attn_bwd(qkv_gs[g], tabs[g][0], tabs[g][1], perm, do_g, dl_g, window, dil)
        dqkv.append(_from_residues(dqkv_g, b_, s_).astype(BF16))

    dproj = jnp.concatenate(
        [a.reshape(t, d) for a in (dq, dff, dfb, di, dog)] + dqkv + [dpga, dpgb], axis=1)
    nm = w["mix_in"].shape[2]
    gw["mix_in"] = _tn_w3(h1b, dproj, lambda j, m: (m, j), nm)
    parts_c = reduce_begin(["mix_in"])
    dh1, got = _nt_w3(dproj, w["mix_in"], dres2, _scatter_exchange(parts_c))
    reduce_end(["mix_in"], parts_c, got)

    dres1, dy1, gp["ln1_g"], gp["ln1_b"], _ = _ln_bwd([dh1], xh1, rs1, p["ln1_g"], 0.5)
    du1 = _swiglu_bwd(dy1, w["ffn1_out"], u1)
    gw["ffn1_out"] = _tn_w2(a1, dy1)
    gw["ffn1_in"] = _tn_w3(xb, du1, du_map, nf)
    keys = ["ffn1_out", "ffn1_in"]
    parts_d = reduce_begin(keys)
    dx, got = _ffn_dx(du1, w["ffn1_in"], dres1, _scatter_exchange(parts_d))
    reduce_end(keys, parts_d, got)

    keys = list(reduced)
    grads = dict(zip(keys, _join_halves([reduced[k] for k in keys])))
    return loss, dx.reshape(b_, s_, d), grads, gp


MESH = pl.DeviceIdType.MESH
ANY = pl.BlockSpec(memory_space=pl.ANY)


def _place():
    x, y, c = lax.axis_index("x"), lax.axis_index("y"), lax.axis_index("c")
    chips = [(1 - x, y), (x, 1 - y), (1 - x, 1 - y)]
    return x, y, c, chips, (x, y, 1 - c)


def _half_rows(c, rows):
    hr = rows // 2
    return pl.ds(pl.multiple_of(c * hr, 16), hr)


def _remote(src, dst, send, recv, dev):
    return pltpu.make_async_remote_copy(src_ref=src, dst_ref=dst, send_sem=send, recv_sem=recv,
                                        device_id=dev, device_id_type=MESH)


def _gather_weights(fulls):
    n = len(fulls)

    def body(*refs):
        _gather_start(refs[n:2 * n], refs[2 * n:])
        _gather_finish(refs[n:2 * n], refs[2 * n:])

    return pl.pallas_call(
        body, name="gather_weights", in_specs=[ANY] * n, out_specs=[ANY] * n,
        out_shape=[_sds(a.shape, a.dtype) for a in fulls], input_output_aliases={i: i for i in range(n)},
        scratch_shapes=_gather_sems(n))(*fulls)


def _gather_sems(n):
    return [pltpu.SemaphoreType.DMA((n, 3)) for _ in range(4)]


def _gather_start(bufs, sems):
    isend, irecv = sems[0], sems[1]
    x, y, c, chips, sib = _place()
    for i, buf in enumerate(bufs):
        blk = buf.at[2 * x + y, _half_rows(c, buf.shape[1])]
        for k, chip in enumerate(chips):
            _remote(blk, blk, isend.at[i, k], irecv.at[i, k], (*chip, c)).start()


def _gather_finish(bufs, sems):
    isend, irecv, fsend, frecv = sems
    x, y, c, chips, sib = _place()
    for i, buf in enumerate(bufs):
        mine = _half_rows(c, buf.shape[1])
        for k, chip in enumerate(chips):
            blk = buf.at[2 * chip[0] + chip[1], mine]
            _remote(blk, blk, isend.at[i, k], irecv.at[i, k], (*chip, c)).wait_recv()
            _remote(blk, blk, fsend.at[i, k], frecv.at[i, k], sib).start()
    for i, buf in enumerate(bufs):
        mine, other = _half_rows(c, buf.shape[1]), _half_rows(1 - c, buf.shape[1])
        own = buf.at[2 * x + y, mine]
        for k, chip in enumerate(chips):
            got = buf.at[2 * chip[0] + chip[1], other]
            _remote(got, got, fsend.at[i, k], frecv.at[i, k], sib).wait_recv()
            _remote(own, own, isend.at[i, k], irecv.at[i, k], (*chip, c)).wait_send()
            blk = buf.at[2 * chip[0] + chip[1], mine]
            _remote(blk, blk, fsend.at[i, k], frecv.at[i, k], sib).wait_send()


def _gather_exchange(fulls):
    n = len(fulls)
    return _Exchange(fulls, [_sds(a.shape, a.dtype) for a in fulls], {i: i for i in range(n)}, _gather_sems(n),
                     lambda ins, outs, sems: _gather_start(outs, sems),
                     lambda ins, outs, sems: _gather_finish(outs, sems))


def _swap_halves(grads):
    n = len(grads)

    def body(*refs):
        ins, outs, send, recv = refs[:n], refs[n:2 * n], refs[2 * n], refs[2 * n + 1]
        x, y, c, chips, sib = _place()
        cps = []
        for i in range(n):
            other = _half_rows(1 - c, ins[i].shape[1])
            cp = _remote(ins[i].at[:, other], outs[i], send.at[i], recv.at[i], sib)
            cp.start()
            cps.append(cp)
        for cp in cps:
            cp.wait()

    dma = pltpu.SemaphoreType.DMA
    return pl.pallas_call(
        body, name="swap_halves", in_specs=[ANY] * n, out_specs=[ANY] * n,
        out_shape=[_sds((N_SHARD, a.shape[1] // 2, a.shape[2]), a.dtype) for a in grads],
        scratch_shapes=[dma((n,)), dma((n,))])(*grads)


def _scatter_partials(parts):
    side = _scatter_exchange(parts)
    n = len(parts)

    def body(*refs):
        side.start(refs[:n], refs[n:2 * n], refs[2 * n:])
        side.finish(refs[:n], refs[n:2 * n], refs[2 * n:])

    return pl.pallas_call(
        body, name="scatter_partials", in_specs=[ANY] * n, out_specs=[ANY] * n, out_shape=side.outs,
        scratch_shapes=side.sems)(*parts)


def _scatter_copies(ins, outs, sems):
    x, y, c, chips, sib = _place()
    return [_remote(a.at[2 * chip[0] + chip[1]], b.at[k], sems[0].at[i, k], sems[1].at[i, k], (*chip, c))
            for i, (a, b) in enumerate(zip(ins, outs)) for k, chip in enumerate(chips)]


def _scatter_exchange(parts):
    n = len(parts)

    def start(ins, outs, sems):
        for cp in _scatter_copies(ins, outs, sems):
            cp.start()

    def finish(ins, outs, sems):
        for cp in _scatter_copies(ins, outs, sems):
            cp.wait()

    return _Exchange(parts, [_sds((3,) + a.shape[1:], a.dtype) for a in parts], {},
                     [pltpu.SemaphoreType.DMA((n, 3)) for _ in range(2)], start, finish)


def _join_halves(grads):
    n = len(grads)

    def body(*refs):
        bufs, send, recv = refs[n:2 * n], refs[2 * n], refs[2 * n + 1]
        x, y, c, chips, sib = _place()
        cps = []
        for i in range(n):
            blk = bufs[i].at[_half_rows(c, bufs[i].shape[0])]
            other = bufs[i].at[_half_rows(1 - c, bufs[i].shape[0])]
            cp = _remote(blk, blk, send.at[i], recv.at[i], sib)
            cp.start()
            cps.append((cp, _remote(other, other, send.at[i], recv.at[i], sib)))
        for cp, got in cps:
            cp.wait_send()
            got.wait_recv()

    dma = pltpu.SemaphoreType.DMA
    return pl.pallas_call(
        body, name="join_halves", in_specs=[ANY] * n, out_specs=[ANY] * n,
        out_shape=[_sds(a.shape, a.dtype) for a in grads], input_output_aliases={i: i for i in range(n)},
        scratch_shapes=[dma((n,)), dma((n,))])(*grads)


def _gather_rows(block):
    m_per, n = block.shape

    def body(x_ref, out_ref, send_sems, recv_sems, local_sem):
        x, y, c, chips, sibling = _place()
        me = (x, y, c)

        def rows(px, py, pc):
            return out_ref.at[pl.ds((4 * px + 2 * py + pc) * m_per, m_per), :]

        def copy(k, blk, to, src=None):
            return _remote(rows(*blk) if src is None else src, rows(*blk), send_sems.at[k], recv_sems.at[k], to)

        mine = pltpu.make_async_copy(x_ref, rows(*me), local_sem)
        mine.start()
        first = [copy(0, me, sibling, src=x_ref)]
        first += [copy(1 + j, me, (*chip, c), src=x_ref) for j, chip in enumerate(chips)]
        for cp in first:
            cp.start()
        passed = [copy(4 + j, (*chip, c), sibling) for j, chip in enumerate(chips)]
        for j, chip in enumerate(chips):
            copy(1 + j, (*chip, c), me).wait_recv()
            passed[j].start()
        copy(0, sibling, me).wait_recv()
        for j, chip in enumerate(chips):
            copy(4 + j, (*chip, 1 - c), me).wait_recv()
        for cp in first + passed:
            cp.wait_send()
        mine.wait()

    vmem = pl.BlockSpec(memory_space=pltpu.VMEM)
    dma = pltpu.SemaphoreType.DMA
    return pl.pallas_call(
        body, name="gather_rows", in_specs=[vmem], out_specs=vmem, out_shape=_sds((8 * m_per, n), block.dtype),
        scratch_shapes=[dma((7,)), dma((7,)), dma(())])(block)


def _row_tile(rows, cols):
    return _tile(rows, max(16, (1 << 20) // cols), 16)


def _sum_halves(grad, got, c_arr):
    _, hr, cols = got.shape
    tr = _row_tile(hr, cols)
    nb = hr // tr

    def body(c_ref, a_ref, b_ref, o_ref):
        o_ref[...] = (a_ref[...].astype(F32) + b_ref[...].astype(F32)).astype(BF16)

    blk = pl.BlockSpec((None, tr, cols), lambda s, i, c_ref: (s, i, 0))
    return pl.pallas_call(
        body, name="sum_halves",
        grid_spec=pltpu.PrefetchScalarGridSpec(
            num_scalar_prefetch=1, grid=(N_SHARD, nb),
            in_specs=[pl.BlockSpec((None, tr, cols), lambda s, i, c_ref: (s, c_ref[0] * nb + i, 0)), blk],
            out_specs=blk),
        out_shape=_sds(got.shape, BF16), compiler_params=_params(2, 0))(c_arr, grad, got)


def _sum_chips(part, got, sc_arr):
    _, hr, cols = got.shape
    tr = _row_tile(hr, cols)
    nb = hr // tr

    def body(s_ref, a_ref, b_ref, o_ref):
        o_ref[...] = ((a_ref[...].astype(F32) + b_ref[0].astype(F32)) + b_ref[1].astype(F32)) + b_ref[2].astype(F32)

    return pl.pallas_call(
        body, name="sum_chips",
        grid_spec=pltpu.PrefetchScalarGridSpec(
            num_scalar_prefetch=1, grid=(nb,),
            in_specs=[pl.BlockSpec((None, tr, cols), lambda i, s_ref: (s_ref[0], i, 0)),
                      pl.BlockSpec((3, tr, cols), lambda i, s_ref: (0, i, 0))],
            out_specs=pl.BlockSpec((tr, cols), lambda i, s_ref: (s_ref[1] * nb + i, 0))),
        out_shape=_sds((2 * hr, cols), F32), compiler_params=_params(1, 0))(sc_arr, part, got)


def _cast_into_slot(x2d, sc_arr):
    rows, cols = x2d.shape
    tr = _row_tile(rows, cols)

    def body(s_ref, x_ref, o_ref):
        o_ref[...] = x_ref[...].astype(BF16)

    return pl.pallas_call(
        body, name="cast_into_slot",
        grid_spec=pltpu.PrefetchScalarGridSpec(
            num_scalar_prefetch=1, grid=(rows // tr,),
            in_specs=[pl.BlockSpec((tr, cols), lambda i, s_ref: (i, 0))],
            out_specs=pl.BlockSpec((None, tr, cols), lambda i, s_ref: (s_ref[0], i, 0))),
        out_shape=_sds((N_SHARD, rows, cols), BF16), compiler_params=_params(1, 0))(sc_arr, x2d)


def _adam_math(w, g, m, v):
    m = ADAM_B1 * m + (1.0 - ADAM_B1) * g
    v = ADAM_B2 * v + (1.0 - ADAM_B2) * (g * g)
    m_hat = m / (1.0 - ADAM_B1 ** ADAM_STEP)
    v_hat = v / (1.0 - ADAM_B2 ** ADAM_STEP)
    delta = -ADAM_LR * (m_hat / (jnp.sqrt(v_hat) + ADAM_EPS) + ADAM_WD * w)
    return delta, m, v


def _adamw(w, g, m, v):
    rows, cols = w.shape
    tr = _tile(rows, max(8, (1 << 19) // cols), 8)

    def body(w_ref, g_ref, m_ref, v_ref, go_ref, d_ref, mo_ref, vo_ref):
        g_ = g_ref[...]
        go_ref[...] = g_
        d_ref[...], mo_ref[...], vo_ref[...] = _adam_math(w_ref[...], g_, m_ref[...], v_ref[...])

    blk = pl.BlockSpec((tr, cols), lambda i: (i, 0))
    return pl.pallas_call(
        body, name="adamw", grid=(rows // tr,), in_specs=[blk] * 4, out_specs=[blk] * 4,
        out_shape=[_sds((rows, cols), F32)] * 4, compiler_params=_params(1, 0))(w, g, m, v)


def _adamw_small(gathered, w, m, v):
    rows, cols = w.shape

    def body(a_ref, w_ref, m_ref, v_ref, go_ref, d_ref, mo_ref, vo_ref):
        g_ = a_ref[pl.ds(0, rows), :]
        for k in range(1, 8):
            g_ = g_ + a_ref[pl.ds(k * rows, rows), :]
        go_ref[...] = g_
        d_ref[...], mo_ref[...], vo_ref[...] = _adam_math(w_ref[...], g_, m_ref[...], v_ref[...])

    vmem = pl.BlockSpec(memory_space=pltpu.VMEM)
    return pl.pallas_call(
        body, name="adamw_small", in_specs=[vmem] * 4, out_specs=[vmem] * 4,
        out_shape=[_sds((rows, cols), F32)] * 4)(gathered, w, m, v)


BIG = ("ffn1_w_in", "ffn1_w_out", "mix_w_in", "w_branch_a", "w_branch_b", "mix_w_out", "ffn2_w_in", "ffn2_w_out")
BIG_KEY = {"ffn1_w_in": "ffn1_in", "ffn1_w_out": "ffn1_out", "mix_w_in": "mix_in", "w_branch_a": "wa",
           "w_branch_b": "wb", "mix_w_out": "mix_out", "ffn2_w_in": "ffn2_in", "ffn2_w_out": "ffn2_out"}
SMALL = ("ln1_g", "ln1_b", "hgrn_lb_fwd", "hgrn_lb_bwd", "hgrn_norm_g", "ln2_g", "ln2_b", "ln3_g", "ln3_b")
ORDER = ("ffn1_w_in", "ffn1_w_out", "ln1_g", "ln1_b", "mix_w_in", "hgrn_lb_fwd", "hgrn_lb_bwd", "hgrn_norm_g",
         "w_branch_a", "w_branch_b", "mix_w_out", "ln2_g", "ln2_b", "ffn2_w_in", "ffn2_w_out", "ln3_g", "ln3_b")
SMALL_ROWS = 16


def _pack_small(d):
    rows = jnp.concatenate([d[k].reshape(-1, d[k].shape[-1]) for k in SMALL], axis=0)
    return jnp.pad(rows, ((0, SMALL_ROWS - rows.shape[0]), (0, 0)))


def _unpack_small(a, like):
    out, r = {}, 0
    for k in SMALL:
        n = like[k].shape[0]
        out[k] = a[r:r + n].reshape(like[k].shape)
        r += n
    return out


def kernel(x, ffn1_w_in, ffn1_w_out, ln1_g, ln1_b, mix_w_in, hgrn_lb_fwd, hgrn_lb_bwd, hgrn_norm_g, w_branch_a, w_branch_b, mix_w_out, ln2_g, ln2_b, ffn2_w_in, ffn2_w_out, ln3_g, ln3_b, loss_target, m_ffn1_w_in, m_ffn1_w_out, m_ln1_g, m_ln1_b, m_mix_w_in, m_hgrn_lb_fwd, m_hgrn_lb_bwd, m_hgrn_norm_g, m_w_branch_a, m_w_branch_b, m_mix_w_out, m_ln2_g, m_ln2_b, m_ffn2_w_in, m_ffn2_w_out, m_ln3_g, m_ln3_b, v_ffn1_w_in, v_ffn1_w_out, v_ln1_g, v_ln1_b, v_mix_w_in, v_hgrn_lb_fwd, v_hgrn_lb_bwd, v_hgrn_norm_g, v_w_branch_a, v_w_branch_b, v_mix_w_out, v_ln2_g, v_ln2_b, v_ffn2_w_in, v_ffn2_w_out, v_ln3_g, v_ln3_b):
    wts = dict(ffn1_w_in=ffn1_w_in, ffn1_w_out=ffn1_w_out, ln1_g=ln1_g, ln1_b=ln1_b, mix_w_in=mix_w_in,
               hgrn_lb_fwd=hgrn_lb_fwd, hgrn_lb_bwd=hgrn_lb_bwd, hgrn_norm_g=hgrn_norm_g, w_branch_a=w_branch_a,
               w_branch_b=w_branch_b, mix_w_out=mix_w_out, ln2_g=ln2_g, ln2_b=ln2_b, ffn2_w_in=ffn2_w_in,
               ffn2_w_out=ffn2_w_out, ln3_g=ln3_g, ln3_b=ln3_b)
    mom = dict(ffn1_w_in=m_ffn1_w_in, ffn1_w_out=m_ffn1_w_out, ln1_g=m_ln1_g, ln1_b=m_ln1_b, mix_w_in=m_mix_w_in,
               hgrn_lb_fwd=m_hgrn_lb_fwd, hgrn_lb_bwd=m_hgrn_lb_bwd, hgrn_norm_g=m_hgrn_norm_g,
               w_branch_a=m_w_branch_a, w_branch_b=m_w_branch_b, mix_w_out=m_mix_w_out, ln2_g=m_ln2_g, ln2_b=m_ln2_b,
               ffn2_w_in=m_ffn2_w_in, ffn2_w_out=m_ffn2_w_out, ln3_g=m_ln3_g, ln3_b=m_ln3_b)
    var = dict(ffn1_w_in=v_ffn1_w_in, ffn1_w_out=v_ffn1_w_out, ln1_g=v_ln1_g, ln1_b=v_ln1_b, mix_w_in=v_mix_w_in,
               hgrn_lb_fwd=v_hgrn_lb_fwd, hgrn_lb_bwd=v_hgrn_lb_bwd, hgrn_norm_g=v_hgrn_norm_g,
               w_branch_a=v_w_branch_a, w_branch_b=v_w_branch_b, mix_w_out=v_mix_w_out, ln2_g=v_ln2_g, ln2_b=v_ln2_b,
               ffn2_w_in=v_ffn2_w_in, ffn2_w_out=v_ffn2_w_out, ln3_g=v_ln3_g, ln3_b=v_ln3_b)
    shard = (2 * lax.axis_index("x") + lax.axis_index("y")).astype(jnp.int32)
    sc_arr = jnp.stack([shard, lax.axis_index("c").astype(jnp.int32)])

    shard2d = {k: wts[k].reshape(wts[k].shape[1:]) for k in BIG}
    fulls = {BIG_KEY[k]: _cast_into_slot(shard2d[k], sc_arr) for k in BIG}
    p = {k: wts[k] for k in SMALL}

    loss, grad_x, grads, gp = _local_step(x, loss_target, fulls, p, sc_arr)
    loss = lax.psum(loss[0, 0], ("x", "y", "c"))

    out_g, out_d, out_m, out_v = {}, {}, {}, {}
    for k in BIG:
        g = grads[BIG_KEY[k]]
        shp = wts[k].shape
        res = _adamw(shard2d[k], g, mom[k].reshape(shp[1:]), var[k].reshape(shp[1:]))
        out_g[k], out_d[k], out_m[k], out_v[k] = [a.reshape(shp) for a in res]

    gathered = _gather_rows(_pack_small(gp))
    res = _adamw_small(gathered, _pack_small(wts), _pack_small(mom), _pack_small(var))
    for dst, a in zip((out_g, out_d, out_m, out_v), res):
        dst.update(_unpack_small(a, wts))

    return (loss, grad_x, *[out_g[k] for k in ORDER], *[out_d[k] for k in ORDER],
            *[out_m[k] for k in ORDER], *[out_v[k] for k in ORDER])
```

```python
import functools

import numpy as np
import jax
import jax.numpy as jnp
from jax import lax
from jax.experimental import pallas as pl
from jax.experimental.pallas import tpu as pltpu

F32 = jnp.float32
BF16 = jnp.bfloat16

HEAD = 128
ATTN_GROUPS = ((128, 1), (512, 4), (2048, 16))
ATTN_HEADS = 4
N_GROUPS = len(ATTN_GROUPS)
QKV_W = N_GROUPS * 3 * ATTN_HEADS * HEAD
ATTN_OUT = ATTN_HEADS * HEAD
ROPE_THETA = 500000.0
ROPE_DIM = HEAD // 4
HGRN_CHUNK = 32
HGRN_FWD_HEADS = 2
HGRN_BWD_HEADS = 2
HGRN_SUB = 4
ALPHA = 2.0 ** 0.25
LN_EPS = 1e-5
NEG_INF = -1e30
ADAM_LR, ADAM_B1, ADAM_B2, ADAM_EPS, ADAM_WD, ADAM_STEP = 0.001, 0.9, 0.999, 1e-08, 0.01, 10

N_SHARD = 4
VMEM_LIMIT = 56 * 1024 * 1024

NN = ((1,), (0,))
NT = ((1,), (1,))
TN = ((0,), (0,))


def _dot(a, b, dims, precision=None):
    return lax.dot_general(a, b, (dims, ((), ())), preferred_element_type=F32, precision=precision)


def _tile(n, pref, mult=128):
    best = None
    for t in range(mult, min(n, pref) + 1, mult):
        if n % t == 0:
            best = t
    return n if best is None else best


def _params(n_parallel, n_arbitrary):
    return pltpu.CompilerParams(
        dimension_semantics=("parallel",) * n_parallel + ("arbitrary",) * n_arbitrary,
        vmem_limit_bytes=VMEM_LIMIT)


def _sigmoid(x):
    return 1.0 / (1.0 + jnp.exp(-x))


def _silu(x):
    return x * _sigmoid(x)


class _Exchange:
    def __init__(self, ins, outs, aliases, sems, start, finish):
        self.ins, self.outs, self.aliases, self.sems = list(ins), list(outs), dict(aliases), list(sems)
        self.start, self.finish = start, finish


def _run(body, name, grid, ins, in_specs, outs, out_specs, scratch, n_arbitrary, side=None):
    n_in, n_out, n_scr = len(ins), len(outs), len(scratch)
    if side is None:
        return pl.pallas_call(
            body, name=name, grid=grid, in_specs=in_specs, out_specs=out_specs, out_shape=outs,
            scratch_shapes=scratch, compiler_params=_params(len(grid) - n_arbitrary, n_arbitrary))(*ins)
    s_in, s_out = len(side.ins), len(side.outs)
    i1 = n_in + s_in
    o1 = i1 + n_out
    o2 = o1 + s_out
    c1 = o2 + n_scr

    def wrapped(*refs):
        s_refs = (refs[n_in:i1], refs[o1:o2], refs[c1:])
        ids = [pl.program_id(a) for a in range(len(grid))]
        first = functools.reduce(jnp.logical_and, [i == 0 for i in ids])
        last = functools.reduce(jnp.logical_and, [i == g - 1 for i, g in zip(ids, grid)])

        @pl.when(first)
        def _():
            side.start(*s_refs)

        body(*refs[:n_in], *refs[i1:o1], *refs[o2:c1])

        @pl.when(last)
        def _():
            side.finish(*s_refs)

    res = pl.pallas_call(
        wrapped, name=name, grid=grid, in_specs=list(in_specs) + [ANY] * s_in,
        out_specs=list(out_specs) + [ANY] * s_out, out_shape=list(outs) + side.outs,
        scratch_shapes=list(scratch) + side.sems,
        input_output_aliases={n_in + a: n_out + b for a, b in side.aliases.items()},
        compiler_params=_params(0, len(grid)))(*ins, *side.ins)
    return res[:n_out], res[n_out:]


def _gemm(name, grid, ins, in_specs, outs, out_specs, accs, dot_fn, epi_fn, side=None):
    n_in, n_out, n_k = len(ins), len(outs), grid[-1]

    def body(*refs):
        in_refs, out_refs, acc_refs = refs[:n_in], refs[n_in:n_in + n_out], refs[n_in + n_out:]
        k = pl.program_id(len(grid) - 1)

        @pl.when(k == 0)
        def _():
            for a in acc_refs:
                a[...] = jnp.zeros(a.shape, F32)

        dot_fn(in_refs, acc_refs)

        @pl.when(k == n_k - 1)
        def _():
            epi_fn(in_refs, acc_refs, out_refs)

    return _run(body, name, grid, ins, in_specs, outs, out_specs, [pltpu.VMEM(s, F32) for s in accs], 1, side)


def _sds(shape, dtype):
    return jax.ShapeDtypeStruct(shape, dtype)


def _ffn_up(xb, w3, side=None):
    t, d = xb.shape
    nf = w3.shape[2]
    f = 2 * nf
    tm, tk = _tile(t, 512, 8), _tile(d, 512)

    def dot_fn(r, acc):
        acc[0][...] += _dot(r[0][...], r[1][...], NN)
        acc[1][...] += _dot(r[0][...], r[2][...], NN)

    def epi_fn(r, acc, out):
        g, u = acc[0][...], acc[1][...]
        out[0][0] = g.astype(BF16)
        out[0][1] = u.astype(BF16)
        out[1][...] = (_silu(g) * u).astype(BF16)

    return _gemm(
        "ffn_up", (t // tm, 2, d // tk), [xb, w3, w3],
        [pl.BlockSpec((tm, tk), lambda i, j, k: (i, k)),
         pl.BlockSpec((None, tk, nf), lambda i, j, k: (j, k, 0)),
         pl.BlockSpec((None, tk, nf), lambda i, j, k: (j + 2, k, 0))],
        [_sds((2, t, f), BF16), _sds((t, f), BF16)],
        [pl.BlockSpec((2, tm, nf), lambda i, j, k: (0, i, j)),
         pl.BlockSpec((tm, nf), lambda i, j, k: (i, j))],
        [(tm, nf), (tm, nf)], dot_fn, epi_fn, side)


def _down_ln(a, w, resid, coef, g, b):
    t, kd = a.shape
    d = w.shape[1]
    tm, tk = _tile(t, 512, 8), _tile(kd, 512)

    def dot_fn(r, acc):
        acc[0][...] += _dot(r[0][...], r[1][...], NN)

    def epi_fn(r, acc, out):
        v = ALPHA * r[2][...] + coef * acc[0][...]
        mu = jnp.mean(v, axis=-1, keepdims=True)
        c = v - mu
        var = jnp.mean(c * c, axis=-1, keepdims=True)
        rstd = lax.rsqrt(var + LN_EPS)
        xhat = c * rstd
        h = xhat * r[3][...] + r[4][...]
        out[0][...] = h
        out[1][...] = h.astype(BF16)
        out[2][...] = xhat
        out[3][...] = rstd

    row = pl.BlockSpec((tm, d), lambda i, k: (i, 0))
    vec = pl.BlockSpec((1, d), lambda i, k: (0, 0))
    return _gemm(
        "down_ln", (t // tm, kd // tk), [a, w, resid, g, b],
        [pl.BlockSpec((tm, tk), lambda i, k: (i, k)), pl.BlockSpec((tk, d), lambda i, k: (k, 0)), row, vec, vec],
        [_sds((t, d), F32), _sds((t, d), BF16), _sds((t, d), F32), _sds((t, 1), F32)],
        [row, row, row, pl.BlockSpec((tm, 1), lambda i, k: (i, 0))],
        [(tm, d)], dot_fn, epi_fn)


def _only(res, side):
    return res[0] if side is None else (res[0][0], res[1])


def _mm_w3(a, w3, out_dtype, side=None):
    t, kd = a.shape
    n = w3.shape[2]
    tm, tk = _tile(t, 512, 8), _tile(kd, 256)

    def dot_fn(r, acc):
        acc[0][...] += _dot(r[0][...], r[1][...], NN)

    def epi_fn(r, acc, out):
        out[0][...] = acc[0][...].astype(out_dtype)

    return _only(_gemm(
        "mm_w3", (t // tm, N_SHARD, kd // tk), [a, w3],
        [pl.BlockSpec((tm, tk), lambda i, j, k: (i, k)), pl.BlockSpec((None, tk, n), lambda i, j, k: (j, k, 0))],
        [_sds((t, N_SHARD * n), out_dtype)], [pl.BlockSpec((tm, n), lambda i, j, k: (i, j))],
        [(tm, n)], dot_fn, epi_fn, side), side)


def _mm_w2(a, w, out_dtype):
    t, kd = a.shape
    n = w.shape[1]
    tm, tn, tk = _tile(t, 512, 8), _tile(n, 1024), _tile(kd, 512)

    def dot_fn(r, acc):
        acc[0][...] += _dot(r[0][...], r[1][...], NN)

    def epi_fn(r, acc, out):
        out[0][...] = acc[0][...].astype(out_dtype)

    return _gemm(
        "mm_w2", (t // tm, n // tn, kd // tk), [a, w],
        [pl.BlockSpec((tm, tk), lambda i, j, k: (i, k)), pl.BlockSpec((tk, tn), lambda i, j, k: (k, j))],
        [_sds((t, n), out_dtype)], [pl.BlockSpec((tm, tn), lambda i, j, k: (i, j))],
        [(tm, tn)], dot_fn, epi_fn)[0]


def _branch_gate(ob, wb3, ya, proj, gate_col):
    t, kd = ob.shape
    n = wb3.shape[2]
    d = N_SHARD * n
    tm = _tile(t, 512, 8)
    ga0, gb0 = gate_col // n, (gate_col + d) // n

    def dot_fn(r, acc):
        acc[0][...] += _dot(r[0][...], r[1][...], NN)

    def epi_fn(r, acc, out):
        yb = acc[0][...]
        out[0][...] = yb
        out[1][...] = (_sigmoid(r[3][...]) * r[2][...] + _sigmoid(r[4][...]) * yb).astype(BF16)

    blk = pl.BlockSpec((tm, n), lambda i, j, k: (i, j))
    return _gemm(
        "branch_gate", (t // tm, N_SHARD, 1), [ob, wb3, ya, proj, proj],
        [pl.BlockSpec((tm, kd), lambda i, j, k: (i, 0)), pl.BlockSpec((None, kd, n), lambda i, j, k: (j, 0, 0)), blk,
         pl.BlockSpec((tm, n), lambda i, j, k: (i, ga0 + j)), pl.BlockSpec((tm, n), lambda i, j, k: (i, gb0 + j))],
        [_sds((t, d), F32), _sds((t, d), BF16)], [blk, blk], [(tm, n)], dot_fn, epi_fn)


def _swiglu_bwd(dyb, w, u3):
    t, d = dyb.shape
    f = w.shape[0]
    tm, tr = _tile(t, 1024, 8), _tile(f, 512)

    def dot_fn(r, acc):
        acc[0][...] += _dot(r[0][...], r[1][...], NT)

    def epi_fn(r, acc, out):
        da = acc[0][...]
        g, u = r[2][0].astype(F32), r[2][1].astype(F32)
        s = _sigmoid(g)
        out[0][0] = (da * u * s * (1.0 + g * (1.0 - s))).astype(BF16)
        out[0][1] = (da * g * s).astype(BF16)

    ublk = pl.BlockSpec((2, tm, tr), lambda i, j, k: (0, i, j))
    return _gemm(
        "swiglu_bwd", (t // tm, f // tr, 1), [dyb, w, u3],
        [pl.BlockSpec((tm, d), lambda i, j, k: (i, 0)), pl.BlockSpec((tr, d), lambda i, j, k: (j, 0)), ublk],
        [_sds((2, t, f), BF16)], [ublk], [(tm, tr)], dot_fn, epi_fn)[0]


def _ffn_dx(du3, w3, resid, side=None):
    t = du3.shape[1]
    d, nf = w3.shape[1], w3.shape[2]
    tm, tr = _tile(t, 512, 8), _tile(d, 1024)

    def dot_fn(r, acc):
        acc[0][...] += _dot(r[0][...], r[1][...], NT)

    def epi_fn(r, acc, out):
        out[0][...] = acc[0][...] + r[2][...]

    blk = pl.BlockSpec((tm, tr), lambda i, j, k: (i, j))
    return _only(_gemm(
        "ffn_dx", (t // tm, d // tr, N_SHARD), [du3, w3, resid],
        [pl.BlockSpec((None, tm, nf), lambda i, j, k: (k // 2, i, k % 2)),
         pl.BlockSpec((None, tr, nf), lambda i, j, k: (k, j, 0)), blk],
        [_sds((t, d), F32)], [blk], [(tm, tr)], dot_fn, epi_fn, side), side)


def _nt_w3(dy, w3, resid, side=None):
    t = dy.shape[0]
    kd, n = w3.shape[1], w3.shape[2]
    tm, tr = _tile(t, 512, 8), _tile(kd, 1024)
    has_res = resid is not None

    def dot_fn(r, acc):
        acc[0][...] += _dot(r[0][...], r[1][...], NT)

    def epi_fn(r, acc, out):
        v = acc[0][...]
        if has_res:
            v = v + r[2][...]
        out[0][...] = v

    blk = pl.BlockSpec((tm, tr), lambda i, j, k: (i, j))
    return _only(_gemm(
        "nt_w3", (t // tm, kd // tr, N_SHARD), [dy, w3] + ([resid] if has_res else []),
        [pl.BlockSpec((tm, n), lambda i, j, k: (i, k)), pl.BlockSpec((None, tr, n), lambda i, j, k: (k, j, 0))]
        + ([blk] if has_res else []),
        [_sds((t, kd), F32)], [blk], [(tm, tr)], dot_fn, epi_fn, side), side)


def _nt_w2(dy, w, out_dtype):
    t, n = dy.shape
    r_ = w.shape[0]
    tm, tr, tk = _tile(t, 512, 8), _tile(r_, 1024), _tile(n, 1024)

    def dot_fn(r, acc):
        acc[0][...] += _dot(r[0][...], r[1][...], NT)

    def epi_fn(r, acc, out):
        out[0][...] = acc[0][...].astype(out_dtype)

    return _gemm(
        "nt_w2", (t // tm, r_ // tr, n // tk), [dy, w],
        [pl.BlockSpec((tm, tk), lambda i, j, k: (i, k)), pl.BlockSpec((tr, tk), lambda i, j, k: (j, k))],
        [_sds((t, r_), out_dtype)], [pl.BlockSpec((tm, tr), lambda i, j, k: (i, j))],
        [(tm, tr)], dot_fn, epi_fn)[0]


def _gate_bwd(dmix, w, proj, ya, yb, gate_col):
    t, n = dmix.shape
    d = w.shape[0]
    tm, tr = _tile(t, 512, 8), _tile(d, 512)
    ga0, gb0 = gate_col // tr, (gate_col + d) // tr
    nb = d // tr

    def dot_fn(r, acc):
        acc[0][...] += _dot(r[0][...], r[1][...], NT)

    def epi_fn(r, acc, out):
        dz = acc[0][...]
        ga, gb = _sigmoid(r[2][...]), _sigmoid(r[3][...])
        out[0][...] = (dz * ga).astype(BF16)
        out[1][...] = (dz * gb).astype(BF16)
        out[2][...] = (dz * r[4][...] * ga * (1.0 - ga)).astype(BF16)
        out[3][...] = (dz * r[5][...] * gb * (1.0 - gb)).astype(BF16)

    blk = pl.BlockSpec((tm, tr), lambda i, j, k: (i, j))
    return _gemm(
        "gate_bwd", (t // tm, nb, 1), [dmix, w, proj, proj, ya, yb],
        [pl.BlockSpec((tm, n), lambda i, j, k: (i, 0)), pl.BlockSpec((tr, n), lambda i, j, k: (j, 0)),
         pl.BlockSpec((tm, tr), lambda i, j, k: (i, ga0 + j)), pl.BlockSpec((tm, tr), lambda i, j, k: (i, gb0 + j)),
         blk, blk],
        [_sds((t, d), BF16), _sds((t, d), BF16), _sds((t, d), BF16), _sds((t, d), BF16)],
        [blk, blk, blk, blk], [(tm, tr)], dot_fn, epi_fn)


def _tn_w3(a, dy, dy_map, n):
    t, kd = a.shape
    tm, tkk = _tile(t, 512, 8), _tile(kd, 512)

    def dot_fn(r, acc):
        acc[0][...] += _dot(r[0][...], r[1][...], TN)

    def epi_fn(r, acc, out):
        out[0][...] = acc[0][...].astype(BF16)

    dy_block = (tm, n) if dy.ndim == 2 else (None, tm, n)
    return _gemm(
        "tn_w3", (kd // tkk, N_SHARD, t // tm), [a, dy],
        [pl.BlockSpec((tm, tkk), lambda i, j, m: (m, i)), pl.BlockSpec(dy_block, lambda i, j, m: dy_map(j, m))],
        [_sds((N_SHARD, kd, n), BF16)], [pl.BlockSpec((None, tkk, n), lambda i, j, m: (j, i, 0))],
        [(tkk, n)], dot_fn, epi_fn)[0]


def _tn_w2(a, dy):
    t, kd = a.shape
    n = dy.shape[1]
    tm, tkk, tn = _tile(t, 512, 8), _tile(kd, 512), _tile(n, 2048)

    def dot_fn(r, acc):
        acc[0][...] += _dot(r[0][...], r[1][...], TN)

    def epi_fn(r, acc, out):
        out[0][...] = acc[0][...].astype(BF16)

    return _gemm(
        "tn_w2", (kd // tkk, n // tn, t // tm), [a, dy],
        [pl.BlockSpec((tm, tkk), lambda i, j, m: (m, i)), pl.BlockSpec((tm, tn), lambda i, j, m: (m, j))],
        [_sds((kd, n), BF16)], [pl.BlockSpec((tkk, tn), lambda i, j, m: (i, j))],
        [(tkk, tn)], dot_fn, epi_fn)[0]


def _ln_bwd(dh_parts, xhat, rstd, g, coef, target=None):
    t, d = xhat.shape
    tm = _tile(t, 256, 8)
    n_parts = len(dh_parts)
    with_loss = target is not None
    ins = list(dh_parts) + [xhat, rstd, g] + ([target] if with_loss else [])

    def body(*refs):
        parts = refs[:n_parts]
        xh_ref, rstd_ref, g_ref = refs[n_parts:n_parts + 3]
        o = n_parts + 3 + (1 if with_loss else 0)
        dres_ref, dyb_ref, dg_ref, db_ref, loss_ref = refs[o:o + 5]
        i = pl.program_id(0)
        dh = parts[0][...]
        for p in parts[1:]:
            dh = dh + p[...]
        if with_loss:
            e = dh - refs[o - 1][...]
            part = 0.5 * jnp.sum(jnp.sum(e * e, axis=-1, keepdims=True) * (1.0 / d), axis=0, keepdims=True)
            dh = e * (1.0 / d)
        else:
            part = jnp.zeros((1, 1), F32)
        xh = xh_ref[...]
        dxh = dh * g_ref[...]
        m1 = jnp.mean(dxh, axis=-1, keepdims=True)
        m2 = jnp.mean(dxh * xh, axis=-1, keepdims=True)
        dv = rstd_ref[...] * (dxh - m1 - xh * m2)
        dres_ref[...] = ALPHA * dv
        dyb_ref[...] = (coef * dv).astype(BF16)

        @pl.when(i == 0)
        def _():
            dg_ref[...] = jnp.zeros(dg_ref.shape, F32)
            db_ref[...] = jnp.zeros(db_ref.shape, F32)
            loss_ref[...] = jnp.zeros(loss_ref.shape, F32)

        dg_ref[...] += jnp.sum(dh * xh, axis=0, keepdims=True)
        db_ref[...] += jnp.sum(dh, axis=0, keepdims=True)
        loss_ref[...] += jnp.broadcast_to(part, loss_ref.shape)

    row = pl.BlockSpec((tm, d), lambda i: (i, 0))
    vec = pl.BlockSpec((1, d), lambda i: (0, 0))
    return pl.pallas_call(
        body, name="ln_bwd", grid=(t // tm,),
        in_specs=[row] * n_parts + [row, pl.BlockSpec((tm, 1), lambda i: (i, 0)), vec] + ([row] if with_loss else []),
        out_specs=[row, row, vec, vec, pl.BlockSpec((1, 128), lambda i: (0, 0))],
        out_shape=[_sds((t, d), F32), _sds((t, d), BF16), _sds((1, d), F32), _sds((1, d), F32), _sds((1, 128), F32)],
        compiler_params=_params(0, 1))(*ins)


def _cast_bf16(x2d):
    t, d = x2d.shape
    tm = _tile(t, 512, 8)

    def body(x_ref, o_ref):
        o_ref[...] = x_ref[...].astype(BF16)

    row = pl.BlockSpec((tm, d), lambda i: (i, 0))
    return pl.pallas_call(body, name="cast_bf16", grid=(t // tm,), in_specs=[row], out_specs=row,
                          out_shape=_sds((t, d), BF16), compiler_params=_params(1, 0))(x2d)


def _lower_bound(table):
    t0, t1 = table[0:1, :], table[1:2, :]
    m = jnp.maximum(t0, t1)
    e0, e1 = jnp.exp(t0 - m), jnp.exp(t1 - m)
    return e0 / (e0 + e1)


def _chunk_tri(rows, upper):
    r = lax.broadcasted_iota(jnp.int32, (rows, rows), 0)
    s = lax.broadcasted_iota(jnp.int32, (rows, rows), 1)
    shift = HGRN_CHUNK.bit_length() - 1
    same = lax.shift_right_logical(r, shift) == lax.shift_right_logical(s, shift)
    return same & ((r <= s) if upper else (r >= s))


def _tri_apply(x, upper):
    tri = _chunk_tri(x.shape[0], upper).astype(F32).astype(BF16)
    hi = x.astype(BF16)
    r1 = x - hi.astype(F32)
    mid = r1.astype(BF16)
    lo = (r1 - mid.astype(F32)).astype(BF16)
    return _dot(tri, hi, NN) + _dot(tri, mid, NN) + _dot(tri, lo, NN)


@functools.partial(jax.custom_vjp, nondiff_argnums=(1,))
def _chunk_cumsum(x, upper):
    return _tri_apply(x, upper)


def _chunk_cumsum_fwd(x, upper):
    return _tri_apply(x, upper), None


def _chunk_cumsum_bwd(upper, _, g):
    return (_tri_apply(g, not upper),)


_chunk_cumsum.defvjp(_chunk_cumsum_fwd, _chunk_cumsum_bwd)


def _hgrn_blocks(units, uppers):
    c = HGRN_CHUNK
    rows = units[0][0].shape[0]
    n_sub = rows // c
    ids = range(len(units))
    chunk = lax.shift_right_logical(lax.broadcasted_iota(jnp.int32, (rows, HEAD), 0), c.bit_length() - 1)
    zero = jnp.zeros((rows, HEAD), BF16)

    def expand(a):
        return jnp.concatenate([jnp.where(chunk == n, a, zero) for n in range(n_sub)], axis=1)

    fs = [u[3] + (1.0 - u[3]) * _sigmoid(u[1]) for u in units]
    lgs = [jnp.log(f) for f in fs]
    cums = [_chunk_cumsum(lgs[i], uppers[i]) for i in ids]
    tots = [[jnp.sum(lg[n * c:(n + 1) * c], axis=0, keepdims=True) for n in range(n_sub)] for lg in lgs]
    qd, kd, ke, vb = [], [], [], []
    for i in ids:
        totb = jnp.concatenate([jnp.broadcast_to(t, (c, HEAD)) for t in tots[i]], axis=0)
        kk = 1.0 - fs[i]
        qd.append((_silu(units[i][0]) * jnp.exp(cums[i])).astype(BF16))
        kd.append((kk * jnp.exp(-cums[i])).astype(BF16))
        ke.append((kk * jnp.exp(totb - cums[i])).astype(BF16))
        vb.append(units[i][2].astype(BF16))
    scores = [_dot(qd[i], kd[i], NT) for i in ids]
    kvs = [_dot(vb[i], expand(ke[i]), TN) for i in ids]
    outs = []
    for i in ids:
        a = jnp.where(_chunk_tri(rows, uppers[i]), scores[i], 0.0).astype(BF16)
        st = units[i][4]
        entering = [None] * n_sub
        for n in (range(n_sub - 1, -1, -1) if uppers[i] else range(n_sub)):
            entering[n] = st.astype(BF16)
            st = st * jnp.exp(tots[i][n]) + kvs[i][:, n * HEAD:(n + 1) * HEAD]
        outs.append((a, jnp.concatenate(entering, axis=1), st))
    res = []
    for i in ids:
        a, entering, st = outs[i]
        res.append((_dot(a, vb[i], NN) + _dot(expand(qd[i]), entering, NT), st))
    return res


def _hgrn_fwd(proj3, lb_f, lb_b, norm_g, d):
    b_, s_, _ = proj3.shape
    nh = d // HEAD
    hb = HGRN_FWD_HEADS
    wid = hb * HEAD
    rows = HGRN_CHUNK * HGRN_SUB
    n_blk = s_ // rows

    def body(hq_ref, hff_ref, hfb_ref, hi_ref, hog_ref, lbf_ref, lbb_ref, g_ref, oraw_ref, out_ref, of_ref, ob_ref):
        lbf, lbb = _lower_bound(lbf_ref[...]), _lower_bound(lbb_ref[...])

        def step(j, sts):
            rf = pl.ds(pl.multiple_of(j * rows, rows), rows)
            rb = pl.ds(pl.multiple_of((n_blk - 1 - j) * rows, rows), rows)
            units = []
            for hh in range(hb):
                cs = slice(hh * HEAD, (hh + 1) * HEAD)
                units.append((hq_ref[rf, cs], hff_ref[rf, cs], hi_ref[rf, cs], lbf[:, cs], sts[2 * hh]))
                units.append((hq_ref[rb, cs], hfb_ref[rb, cs], hi_ref[rb, cs], lbb[:, cs], sts[2 * hh + 1]))
            res = _hgrn_blocks(units, [False, True] * hb)
            for hh in range(hb):
                cs = slice(hh * HEAD, (hh + 1) * HEAD)
                of_ref[rf, cs] = res[2 * hh][0]
                ob_ref[rb, cs] = res[2 * hh + 1][0]
            return tuple(r[1] for r in res)

        z = jnp.zeros((HEAD, HEAD), F32)
        lax.fori_loop(0, n_blk, step, (z,) * (2 * hb))
        for hh in range(hb):
            cs = slice(hh * HEAD, (hh + 1) * HEAD)
            o = of_ref[:, cs] + ob_ref[:, cs]
            oraw_ref[:, cs] = o
            on = o * lax.rsqrt(jnp.mean(o * o, axis=-1, keepdims=True) + LN_EPS)
            out_ref[:, cs] = (on * g_ref[:, cs] * _silu(hog_ref[:, cs])).astype(BF16)

    def col(k):
        return pl.BlockSpec((None, s_, wid), lambda h, b: (b, 0, k * (nh // hb) + h))

    tab = pl.BlockSpec((2, wid), lambda h, b: (0, h))
    oblk = pl.BlockSpec((None, s_, wid), lambda h, b: (b, 0, h))
    return pl.pallas_call(
        body, name="hgrn_fwd", grid=(nh // hb, b_),
        in_specs=[col(0), col(1), col(2), col(3), col(4), tab, tab, pl.BlockSpec((1, wid), lambda h, b: (0, h))],
        out_specs=[oblk, oblk], out_shape=[_sds((b_, s_, d), F32), _sds((b_, s_, d), BF16)],
        scratch_shapes=[pltpu.VMEM((s_, wid), F32), pltpu.VMEM((s_, wid), F32)],
        compiler_params=_params(2, 0))(proj3, proj3, proj3, proj3, proj3, lb_f, lb_b, norm_g)


def _hgrn_bwd(proj3, lb_f, lb_b, norm_g, o_raw, do_a, d, side=None):
    b_, s_, _ = proj3.shape
    nh = d // HEAD
    hb = HGRN_BWD_HEADS
    wid = hb * HEAD
    rows = HGRN_CHUNK * HGRN_SUB
    n_blk = s_ // rows

    def body(hq_ref, hff_ref, hfb_ref, hi_ref, hog_ref, lbf_ref, lbb_ref, g_ref, oraw_ref, doa_ref,
             dq_ref, dff_ref, dfb_ref, di_ref, dog_ref, dlbf_ref, dlbb_ref, dg_ref,
             st_ref, dor_ref, dq2_ref, di2_ref):
        b = pl.program_id(1)
        tab_f, tab_b = lbf_ref[...], lbb_ref[...]
        lbf, lbb = _lower_bound(tab_f), _lower_bound(tab_b)

        dg_parts = []
        for hh in range(hb):
            cs = slice(hh * HEAD, (hh + 1) * HEAD)
            o, doa, hog, g = oraw_ref[:, cs], doa_ref[:, cs], hog_ref[:, cs], g_ref[:, cs]
            rs = lax.rsqrt(jnp.mean(o * o, axis=-1, keepdims=True) + LN_EPS)
            on = o * rs
            sg = _sigmoid(hog)
            gate = hog * sg
            dog_ref[:, cs] = (doa * on * g * sg * (1.0 + hog * (1.0 - sg))).astype(BF16)
            don = doa * g * gate
            dor_ref[:, cs] = rs * (don - on * jnp.mean(don * on, axis=-1, keepdims=True))
            dg_parts.append(jnp.sum(doa * on * gate, axis=0, keepdims=True))

        def fwd_step(j, carry):
            jb = n_blk - 1 - j
            rf = pl.ds(pl.multiple_of(j * rows, rows), rows)
            rb = pl.ds(pl.multiple_of(jb * rows, rows), rows)
            units = []
            for hh in range(hb):
                cs = slice(hh * HEAD, (hh + 1) * HEAD)
                st_ref[2 * hh, j] = carry[2 * hh]
                st_ref[2 * hh + 1, jb] = carry[2 * hh + 1]
                units.append((hq_ref[rf, cs], hff_ref[rf, cs], hi_ref[rf, cs], lbf[:, cs], carry[2 * hh]))
                units.append((hq_ref[rb, cs], hfb_ref[rb, cs], hi_ref[rb, cs], lbb[:, cs], carry[2 * hh + 1]))
            return tuple(r[1] for r in _hgrn_blocks(units, uppers))

        uppers = [False, True] * hb
        z = jnp.zeros((HEAD, HEAD), F32)
        lax.fori_loop(0, n_blk, fwd_step, (z,) * (2 * hb))

        def bwd_step(j, carry):
            gs, dls = carry
            jf = n_blk - 1 - j
            rf = pl.ds(pl.multiple_of(jf * rows, rows), rows)
            rb = pl.ds(pl.multiple_of(j * rows, rows), rows)
            flat, cots = [], []
            for hh in range(hb):
                cs = slice(hh * HEAD, (hh + 1) * HEAD)
                flat += [hq_ref[rf, cs], hff_ref[rf, cs], hi_ref[rf, cs], lbf[:, cs], st_ref[2 * hh, jf],
                         hq_ref[rb, cs], hfb_ref[rb, cs], hi_ref[rb, cs], lbb[:, cs], st_ref[2 * hh + 1, j]]
                cots += [(dor_ref[rf, cs], gs[2 * hh]), (dor_ref[rb, cs], gs[2 * hh + 1])]
            _, vjp = jax.vjp(lambda *a: _hgrn_blocks([a[5 * i:5 * i + 5] for i in range(2 * hb)], uppers), *flat)
            grads = vjp(cots)
            gs, dls = list(gs), list(dls)
            for hh in range(hb):
                cs = slice(hh * HEAD, (hh + 1) * HEAD)
                dq, df, di, dl, gs[2 * hh] = grads[10 * hh:10 * hh + 5]
                dq_ref[rf, cs] = dq.astype(BF16)
                dff_ref[rf, cs] = df.astype(BF16)
                di_ref[rf, cs] = di.astype(BF16)
                dls[2 * hh] = dls[2 * hh] + dl
                dq, df, di, dl, gs[2 * hh + 1] = grads[10 * hh + 5:10 * hh + 10]
                dq2_ref[rb, cs] = dq
                dfb_ref[rb, cs] = df.astype(BF16)
                di2_ref[rb, cs] = di
                dls[2 * hh + 1] = dls[2 * hh + 1] + dl
            return tuple(gs), tuple(dls)

        zl = jnp.zeros((1, HEAD), F32)
        _, dls = lax.fori_loop(0, n_blk, bwd_step, ((z,) * (2 * hb), (zl,) * (2 * hb)))
        dq_ref[...] = (dq_ref[...].astype(F32) + dq2_ref[...]).astype(BF16)
        di_ref[...] = (di_ref[...].astype(F32) + di2_ref[...]).astype(BF16)

        _, vjp_tf = jax.vjp(_lower_bound, tab_f)
        _, vjp_tb = jax.vjp(_lower_bound, tab_b)

        @pl.when(b == 0)
        def _():
            dlbf_ref[...] = jnp.zeros(dlbf_ref.shape, F32)
            dlbb_ref[...] = jnp.zeros(dlbb_ref.shape, F32)
            dg_ref[...] = jnp.zeros(dg_ref.shape, F32)

        dlbf_ref[...] += vjp_tf(jnp.concatenate(dls[0::2], axis=1))[0]
        dlbb_ref[...] += vjp_tb(jnp.concatenate(dls[1::2], axis=1))[0]
        dg_ref[...] += jnp.concatenate(dg_parts, axis=1)

    def col(k):
        return pl.BlockSpec((None, s_, wid), lambda h, b: (b, 0, k * (nh // hb) + h))

    tab = pl.BlockSpec((2, wid), lambda h, b: (0, h))
    vec = pl.BlockSpec((1, wid), lambda h, b: (0, h))
    oblk = pl.BlockSpec((None, s_, wid), lambda h, b: (b, 0, h))
    seq = _sds((b_, s_, d), BF16)
    return _run(
        body, "hgrn_bwd", (nh // hb, b_), [proj3, proj3, proj3, proj3, proj3, lb_f, lb_b, norm_g, o_raw, do_a],
        [col(0), col(1), col(2), col(3), col(4), tab, tab, vec, oblk, oblk],
        [seq, seq, seq, seq, seq, _sds((2, d), F32), _sds((2, d), F32), _sds((1, d), F32)],
        [oblk, oblk, oblk, oblk, oblk, tab, tab, vec],
        [pltpu.VMEM((2 * hb, n_blk, HEAD, HEAD), F32),
         pltpu.VMEM((s_, wid), F32), pltpu.VMEM((s_, wid), F32), pltpu.VMEM((s_, wid), F32)], 1, side)


def _rope_tables(s_):
    inv = ROPE_THETA ** (-jnp.arange(0, ROPE_DIM, 2, dtype=F32) / ROPE_DIM)
    ang = jnp.arange(s_, dtype=F32)[:, None] * inv
    cos, sin = jnp.cos(ang), jnp.sin(ang)
    rest = HEAD - ROPE_DIM
    ctab = jnp.concatenate([cos, cos, jnp.ones((s_, rest), F32)], axis=1)
    stab = jnp.concatenate([-sin, sin, jnp.zeros((s_, rest), F32)], axis=1)
    half = ROPE_DIM // 2
    perm = np.zeros((HEAD, HEAD), np.float32)
    for i in range(half):
        perm[i + half, i] = 1.0
        perm[i, i + half] = 1.0
    return ctab, stab, jnp.asarray(perm)


def _attn_tile(qr, kr, v, cq, sq, ck, sk, perm, qi0, kj0, half):
    hi = lax.Precision.HIGHEST
    q = qr * cq + _dot(qr, perm, NN, precision=hi) * sq
    k = kr * ck + _dot(kr, perm, NN, precision=hi) * sk
    s = _dot(q.astype(BF16), k.astype(BF16), NT) * (HEAD ** -0.5)
    qi = qi0 + lax.broadcasted_iota(jnp.int32, s.shape, 0)
    kj = kj0 + lax.broadcasted_iota(jnp.int32, s.shape, 1)
    s = jnp.where(jnp.abs(qi - kj) <= half, s, NEG_INF)
    m = lax.stop_gradient(jnp.max(s, axis=-1, keepdims=True))
    p = jnp.exp(s - m)
    den = jnp.sum(p, axis=-1, keepdims=True)
    o = _dot(p.astype(BF16), v.astype(BF16), NN) / den
    lse = m + jnp.log(den)
    return o, jnp.broadcast_to(lse, o.shape)


def _attn_tiling(seg):
    tq = seg if seg <= 256 else 256
    kw = seg if seg <= 512 else 512
    tiles = []
    for i in range(seg // tq):
        ws = min(max(i * tq - (kw - tq) // 2, 0), seg - kw)
        tiles.append((i * tq, ws))
    return tq, kw, tiles


def _attn_specs(b_, dil, seg):
    qkv = pl.BlockSpec((None, None, 3, None, seg, HEAD), lambda b, r, h: (b, r, 0, h, 0, 0))
    tab = pl.BlockSpec((None, seg, HEAD), lambda b, r, h: (r, 0, 0))
    perm = pl.BlockSpec((HEAD, HEAD), lambda b, r, h: (0, 0))
    oblk = pl.BlockSpec((None, None, None, seg, HEAD), lambda b, r, h: (b, r, h, 0, 0))
    return qkv, tab, perm, oblk


def _attn_fwd(qkv_g, ctab, stab, perm, window, dil):
    b_, _, _, nh, seg, _ = qkv_g.shape
    half = window // (2 * dil)
    tq, kw, tiles = _attn_tiling(seg)

    def body(qkv_ref, c_ref, s_ref, p_ref, o_ref, l_ref):
        pm = p_ref[...]
        for q0, k0 in tiles:
            rq, rk = pl.ds(q0, tq), pl.ds(k0, kw)
            o, l = _attn_tile(qkv_ref[0, rq, :], qkv_ref[1, rk, :], qkv_ref[2, rk, :], c_ref[rq, :], s_ref[rq, :],
                              c_ref[rk, :], s_ref[rk, :], pm, q0, k0, half)
            o_ref[rq, :] = o
            l_ref[rq, :] = l

    qkv, tab, pspec, oblk = _attn_specs(b_, dil, seg)
    shp = _sds((b_, dil, nh, seg, HEAD), F32)
    return pl.pallas_call(
        body, name=f"attn_fwd_d{dil}", grid=(b_, dil, nh), in_specs=[qkv, tab, tab, pspec],
        out_specs=[oblk, oblk], out_shape=[shp, shp], compiler_params=_params(3, 0))(qkv_g, ctab, stab, perm)


def _attn_bwd(qkv_g, ctab, stab, perm, do, dl, window, dil):
    b_, _, _, nh, seg, _ = qkv_g.shape
    half = window // (2 * dil)
    tq, kw, tiles = _attn_tiling(seg)

    def body(qkv_ref, c_ref, s_ref, p_ref, do_ref, dl_ref, d_ref):
        pm = p_ref[...]
        d_ref[...] = jnp.zeros(d_ref.shape, F32)
        for q0, k0 in tiles:
            rq, rk = pl.ds(q0, tq), pl.ds(k0, kw)
            cq, sq, ck, sk = c_ref[rq, :], s_ref[rq, :], c_ref[rk, :], s_ref[rk, :]
            _, vjp = jax.vjp(lambda a, b, c: _attn_tile(a, b, c, cq, sq, ck, sk, pm, q0, k0, half),
                             qkv_ref[0, rq, :], qkv_ref[1, rk, :], qkv_ref[2, rk, :])
            dq, dk, dv = vjp((do_ref[rq, :], dl_ref[rq, :]))
            d_ref[0, rq, :] = dq
            d_ref[1, rk, :] += dk
            d_ref[2, rk, :] += dv

    qkv, tab, pspec, oblk = _attn_specs(b_, dil, seg)
    return pl.pallas_call(
        body, name=f"attn_bwd_d{dil}", grid=(b_, dil, nh), in_specs=[qkv, tab, tab, pspec, oblk, oblk],
        out_specs=qkv, out_shape=_sds(qkv_g.shape, F32), compiler_params=_params(3, 0))(
            qkv_g, ctab, stab, perm, do, dl)


def _combine(os_, ls_):
    m = jnp.maximum(jnp.maximum(ls_[0], ls_[1]), ls_[2])
    es = [jnp.exp(l - m) for l in ls_]
    return (es[0] * os_[0] + es[1] * os_[1] + es[2] * os_[2]) / (es[0] + es[1] + es[2])


def _combine_fwd(os_, ls_):
    t, w = os_[0].shape
    tm = _tile(t, 512, 8)

    def body(*refs):
        refs[6][...] = _combine([r[...] for r in refs[:3]], [r[...] for r in refs[3:6]]).astype(BF16)

    row = pl.BlockSpec((tm, w), lambda i: (i, 0))
    return pl.pallas_call(body, name="combine_fwd", grid=(t // tm,), in_specs=[row] * 6, out_specs=row,
                          out_shape=_sds((t, w), BF16), compiler_params=_params(1, 0))(*os_, *ls_)


def _combine_bwd(os_, ls_, dob):
    t, w = os_[0].shape
    tm = _tile(t, 512, 8)

    def body(*refs):
        _, vjp = jax.vjp(lambda *a: _combine(a[:3], a[3:]), *[r[...] for r in refs[:6]])
        for r, g in zip(refs[7:], vjp(refs[6][...])):
            r[...] = g

    row = pl.BlockSpec((tm, w), lambda i: (i, 0))
    return pl.pallas_call(body, name="combine_bwd", grid=(t // tm,), in_specs=[row] * 7, out_specs=[row] * 6,
                          out_shape=[_sds((t, w), F32)] * 6, compiler_params=_params(1, 0))(*os_, *ls_, dob)


def _to_residues(cols, b_, s_, dil):
    seg = s_ // dil
    return cols.reshape(b_, seg, dil, 3, ATTN_HEADS, HEAD).transpose(0, 2, 3, 4, 1, 5)


def _from_residues(a, b_, s_):
    if a.ndim == 6:
        return a.transpose(0, 4, 1, 2, 3, 5).reshape(b_ * s_, 3 * ATTN_HEADS * HEAD)
    return a.transpose(0, 3, 1, 2, 4).reshape(b_ * s_, ATTN_HEADS * HEAD)


ROW_SHARDED = ("ffn1_out", "ffn2_out", "wa", "mix_out")


def _local_step(x, target, fulls, p, sc_arr):
    b_, s_, d = x.shape
    t = b_ * s_
    x2 = x.reshape(t, d)
    gate_col = 5 * d + QKV_W
    c_arr = sc_arr[1:2]
    w = {}

    def arrived(keys, arrays):
        for k, a in zip(keys, arrays):
            w[k] = a.reshape(a.shape[0] * a.shape[1], a.shape[2]) if k in ROW_SHARDED else a

    def reduce_begin(keys):
        g3 = [gw[k].reshape(fulls[k].shape) for k in keys]
        return [_sum_halves(a, b, c_arr) for a, b in zip(g3, _swap_halves(g3))]

    def reduce_end(keys, parts, got):
        for k, a, b in zip(keys, parts, got):
            reduced[k] = _sum_chips(a, b, sc_arr)

    arrived(["ffn1_in"], _gather_weights([fulls["ffn1_in"]]))
    xb = _cast_bf16(x2)
    keys = ["ffn1_out", "mix_in"]
    (u1, a1), got = _ffn_up(xb, w["ffn1_in"], _gather_exchange([fulls[k] for k in keys]))
    arrived(keys, got)
    h1, h1b, xh1, rs1 = _down_ln(a1, w["ffn1_out"], x2, 0.5, p["ln1_g"], p["ln1_b"])

    keys = ["wa", "wb", "mix_out", "ffn2_in", "ffn2_out"]
    proj, got = _mm_w3(h1b, w["mix_in"], F32, _gather_exchange([fulls[k] for k in keys]))
    arrived(keys, got)
    n_in = proj.shape[1]
    proj3 = proj.reshape(b_, s_, n_in)
    o_raw, oa = _hgrn_fwd(proj3, p["hgrn_lb_fwd"], p["hgrn_lb_bwd"], p["hgrn_norm_g"], d)
    oa2 = oa.reshape(t, d)

    ctab, stab, perm = _rope_tables(s_)
    qkv_gs, tabs, os_, ls_ = [], [], [], []
    for g, (window, dil) in enumerate(ATTN_GROUPS):
        c0 = 5 * d + g * 3 * ATTN_OUT
        qkv_g = _to_residues(proj[:, c0:c0 + 3 * ATTN_OUT], b_, s_, dil)
        seg = s_ // dil
        ct = ctab.reshape(seg, dil, HEAD).transpose(1, 0, 2)
        st = stab.reshape(seg, dil, HEAD).transpose(1, 0, 2)
        o_g, l_g = _attn_fwd(qkv_g, ct, st, perm, window, dil)
        qkv_gs.append(qkv_g)
        tabs.append((ct, st))
        os_.append(_from_residues(o_g, b_, s_))
        ls_.append(_from_residues(l_g, b_, s_))
    ob = _combine_fwd(os_, ls_)

    ya = _mm_w2(oa2, w["wa"], F32)
    yb, zb = _branch_gate(ob, w["wb"], ya, proj, gate_col)
    h2, h2b, xh2, rs2 = _down_ln(zb, w["mix_out"], h1, 1.0, p["ln2_g"], p["ln2_b"])

    u2, a2 = _ffn_up(h2b, w["ffn2_in"])
    h3, _, xh3, rs3 = _down_ln(a2, w["ffn2_out"], h2, 0.5, p["ln3_g"], p["ln3_b"])

    gw, gp, reduced = {}, {}, {}
    nf = w["ffn2_in"].shape[2]

    def du_map(j, m):
        return (j // 2, m, j % 2)

    dres3, dy3, gp["ln3_g"], gp["ln3_b"], loss = _ln_bwd([h3], xh3, rs3, p["ln3_g"], 0.5, target.reshape(t, d))
    du2 = _swiglu_bwd(dy3, w["ffn2_out"], u2)
    gw["ffn2_out"] = _tn_w2(a2, dy3)
    gw["ffn2_in"] = _tn_w3(h2b, du2, du_map, nf)
    parts_a = reduce_begin(["ffn2_out", "ffn2_in"])
    dh2, got = _ffn_dx(du2, w["ffn2_in"], dres3, _scatter_exchange(parts_a[:1]))
    reduce_end(["ffn2_out"], parts_a[:1], got)

    dres2, dmix, gp["ln2_g"], gp["ln2_b"], _ = _ln_bwd([dh2], xh2, rs2, p["ln2_g"], 1.0)
    dya, dyb, dpga, dpgb = _gate_bwd(dmix, w["mix_out"], proj, ya, yb, gate_col)
    gw["mix_out"] = _tn_w2(zb, dmix)
    do_a = _nt_w2(dya, w["wa"], F32)
    gw["wa"] = _tn_w2(oa2, dya)
    nb = w["wb"].shape[2]
    do_b = _nt_w3(dyb, w["wb"], None)
    gw["wb"] = _tn_w3(ob, dyb, lambda j, m: (m, j), nb)
    keys = ["ffn2_in", "mix_out", "wa", "wb"]
    parts_b = parts_a[1:] + reduce_begin(keys[1:])

    (dq, dff, dfb, di, dog, gp["hgrn_lb_fwd"], gp["hgrn_lb_bwd"], gp["hgrn_norm_g"]), got = _hgrn_bwd(
        proj3, p["hgrn_lb_fwd"], p["hgrn_lb_bwd"], p["hgrn_norm_g"], o_raw, do_a.reshape(b_, s_, d), d,
        _scatter_exchange(parts_b))
    reduce_end(keys, parts_b, got)

    douts = _combine_bwd(os_, ls_, do_b)
    dqkv = []
    for g, (window, dil) in enumerate(ATTN_GROUPS):
        seg = s_ // dil
        do_g = douts[g].reshape(b_, seg, dil, ATTN_HEADS, HEAD).transpose(0, 2, 3, 1, 4)
        dl_g = douts[3 + g].reshape(b_, seg, dil, ATTN_HEADS, HEAD).transpose(0, 2, 3, 1, 4)
        dqkv_g = _attn_bwd(qkv_gs[g], tabs[g][0], tabs[g][1], perm, do_g, dl_g, window, dil)
        dqkv.append(_from_residues(dqkv_g, b_, s_).astype(BF16))

    dproj = jnp.concatenate(
        [a.reshape(t, d) for a in (dq, dff, dfb, di, dog)] + dqkv + [dpga, dpgb], axis=1)
    nm = w["mix_in"].shape[2]
    gw["mix_in"] = _tn_w3(h1b, dproj, lambda j, m: (m, j), nm)
    parts_c = reduce_begin(["mix_in"])
    dh1, got = _nt_w3(dproj, w["mix_in"], dres2, _scatter_exchange(parts_c))
    reduce_end(["mix_in"], parts_c, got)

    dres1, dy1, gp["ln1_g"], gp["ln1_b"], _ = _ln_bwd([dh1], xh1, rs1, p["ln1_g"], 0.5)
    du1 = _swiglu_bwd(dy1, w["ffn1_out"], u1)
    gw["ffn1_out"] = _tn_w2(a1, dy1)
    gw["ffn1_in"] = _tn_w3(xb, du1, du_map, nf)
    keys = ["ffn1_out", "ffn1_in"]
    parts_d = reduce_begin(keys)
    dx, got = _ffn_dx(du1, w["ffn1_in"], dres1, _scatter_exchange(parts_d))
    reduce_end(keys, parts_d, got)

    keys = list(reduced)
    grads = dict(zip(keys, _join_halves([reduced[k] for k in keys])))
    return loss, dx.reshape(b_, s_, d), grads, gp


MESH = pl.DeviceIdType.MESH
ANY = pl.BlockSpec(memory_space=pl.ANY)


def _place():
    x, y, c = lax.axis_index("x"), lax.axis_index("y"), lax.axis_index("c")
    chips = [(1 - x, y), (x, 1 - y), (1 - x, 1 - y)]
    return x, y, c, chips, (x, y, 1 - c)


def _half_rows(c, rows):
    hr = rows // 2
    return pl.ds(pl.multiple_of(c * hr, 16), hr)


def _remote(src, dst, send, recv, dev):
    return pltpu.make_async_remote_copy(src_ref=src, dst_ref=dst, send_sem=send, recv_sem=recv,
                                        device_id=dev, device_id_type=MESH)


def _gather_weights(fulls):
    n = len(fulls)

    def body(*refs):
        _gather_start(refs[n:2 * n], refs[2 * n:])
        _gather_finish(refs[n:2 * n], refs[2 * n:])

    return pl.pallas_call(
        body, name="gather_weights", in_specs=[ANY] * n, out_specs=[ANY] * n,
        out_shape=[_sds(a.shape, a.dtype) for a in fulls], input_output_aliases={i: i for i in range(n)},
        scratch_shapes=_gather_sems(n))(*fulls)


def _gather_sems(n):
    return [pltpu.SemaphoreType.DMA((n, 3)) for _ in range(4)]


def _gather_start(bufs, sems):
    isend, irecv = sems[0], sems[1]
    x, y, c, chips, sib = _place()
    for i, buf in enumerate(bufs):
        blk = buf.at[2 * x + y, _half_rows(c, buf.shape[1])]
        for k, chip in enumerate(chips):
            _remote(blk, blk, isend.at[i, k], irecv.at[i, k], (*chip, c)).start()


def _gather_finish(bufs, sems):
    isend, irecv, fsend, frecv = sems
    x, y, c, chips, sib = _place()
    for i, buf in enumerate(bufs):
        mine = _half_rows(c, buf.shape[1])
        for k, chip in enumerate(chips):
            blk = buf.at[2 * chip[0] + chip[1], mine]
            _remote(blk, blk, isend.at[i, k], irecv.at[i, k], (*chip, c)).wait_recv()
            _remote(blk, blk, fsend.at[i, k], frecv.at[i, k], sib).start()
    for i, buf in enumerate(bufs):
        mine, other = _half_rows(c, buf.shape[1]), _half_rows(1 - c, buf.shape[1])
        own = buf.at[2 * x + y, mine]
        for k, chip in enumerate(chips):
            got = buf.at[2 * chip[0] + chip[1], other]
            _remote(got, got, fsend.at[i, k], frecv.at[i, k], sib).wait_recv()
            _remote(own, own, isend.at[i, k], irecv.at[i, k], (*chip, c)).wait_send()
            blk = buf.at[2 * chip[0] + chip[1], mine]
            _remote(blk, blk, fsend.at[i, k], frecv.at[i, k], sib).wait_send()


def _gather_exchange(fulls):
    n = len(fulls)
    return _Exchange(fulls, [_sds(a.shape, a.dtype) for a in fulls], {i: i for i in range(n)}, _gather_sems(n),
                     lambda ins, outs, sems: _gather_start(outs, sems),
                     lambda ins, outs, sems: _gather_finish(outs, sems))


def _swap_halves(grads):
    n = len(grads)

    def body(*refs):
        ins, outs, send, recv = refs[:n], refs[n:2 * n], refs[2 * n], refs[2 * n + 1]
        x, y, c, chips, sib = _place()
        cps = []
        for i in range(n):
            other = _half_rows(1 - c, ins[i].shape[1])
            cp = _remote(ins[i].at[:, other], outs[i], send.at[i], recv.at[i], sib)
            cp.start()
            cps.append(cp)
        for cp in cps:
            cp.wait()

    dma = pltpu.SemaphoreType.DMA
    return pl.pallas_call(
        body, name="swap_halves", in_specs=[ANY] * n, out_specs=[ANY] * n,
        out_shape=[_sds((N_SHARD, a.shape[1] // 2, a.shape[2]), a.dtype) for a in grads],
        scratch_shapes=[dma((n,)), dma((n,))])(*grads)


def _scatter_partials(parts):
    side = _scatter_exchange(parts)
    n = len(parts)

    def body(*refs):
        side.start(refs[:n], refs[n:2 * n], refs[2 * n:])
        side.finish(refs[:n], refs[n:2 * n], refs[2 * n:])

    return pl.pallas_call(
        body, name="scatter_partials", in_specs=[ANY] * n, out_specs=[ANY] * n, out_shape=side.outs,
        scratch_shapes=side.sems)(*parts)


def _scatter_copies(ins, outs, sems):
    x, y, c, chips, sib = _place()
    return [_remote(a.at[2 * chip[0] + chip[1]], b.at[k], sems[0].at[i, k], sems[1].at[i, k], (*chip, c))
            for i, (a, b) in enumerate(zip(ins, outs)) for k, chip in enumerate(chips)]


def _scatter_exchange(parts):
    n = len(parts)

    def start(ins, outs, sems):
        for cp in _scatter_copies(ins, outs, sems):
            cp.start()

    def finish(ins, outs, sems):
        for cp in _scatter_copies(ins, outs, sems):
            cp.wait()

    return _Exchange(parts, [_sds((3,) + a.shape[1:], a.dtype) for a in parts], {},
                     [pltpu.SemaphoreType.DMA((n, 3)) for _ in range(2)], start, finish)


def _join_halves(grads):
    n = len(grads)

    def body(*refs):
        bufs, send, recv = refs[n:2 * n], refs[2 * n], refs[2 * n + 1]
        x, y, c, chips, sib = _place()
        cps = []
        for i in range(n):
            blk = bufs[i].at[_half_rows(c, bufs[i].shape[0])]
            other = bufs[i].at[_half_rows(1 - c, bufs[i].shape[0])]
            cp = _remote(blk, blk, send.at[i], recv.at[i], sib)
            cp.start()
            cps.append((cp, _remote(other, other, send.at[i], recv.at[i], sib)))
        for cp, got in cps:
            cp.wait_send()
            got.wait_recv()

    dma = pltpu.SemaphoreType.DMA
    return pl.pallas_call(
        body, name="join_halves", in_specs=[ANY] * n, out_specs=[ANY] * n,
        out_shape=[_sds(a.shape, a.dtype) for a in grads], input_output_aliases={i: i for i in range(n)},
        scratch_shapes=[dma((n,)), dma((n,))])(*grads)


def _gather_rows(block):
    m_per, n = block.shape

    def body(x_ref, out_ref, send_sems, recv_sems, local_sem):
        x, y, c, chips, sibling = _place()
        me = (x, y, c)

        def rows(px, py, pc):
            return out_ref.at[pl.ds((4 * px + 2 * py + pc) * m_per, m_per), :]

        def copy(k, blk, to, src=None):
            return _remote(rows(*blk) if src is None else src, rows(*blk), send_sems.at[k], recv_sems.at[k], to)

        mine = pltpu.make_async_copy(x_ref, rows(*me), local_sem)
        mine.start()
        first = [copy(0, me, sibling, src=x_ref)]
        first += [copy(1 + j, me, (*chip, c), src=x_ref) for j, chip in enumerate(chips)]
        for cp in first:
            cp.start()
        passed = [copy(4 + j, (*chip, c), sibling) for j, chip in enumerate(chips)]
        for j, chip in enumerate(chips):
            copy(1 + j, (*chip, c), me).wait_recv()
            passed[j].start()
        copy(0, sibling, me).wait_recv()
        for j, chip in enumerate(chips):
            copy(4 + j, (*chip, 1 - c), me).wait_recv()
        for cp in first + passed:
            cp.wait_send()
        mine.wait()

    vmem = pl.BlockSpec(memory_space=pltpu.VMEM)
    dma = pltpu.SemaphoreType.DMA
    return pl.pallas_call(
        body, name="gather_rows", in_specs=[vmem], out_specs=vmem, out_shape=_sds((8 * m_per, n), block.dtype),
        scratch_shapes=[dma((7,)), dma((7,)), dma(())])(block)


def _row_tile(rows, cols):
    return _tile(rows, max(16, (1 << 20) // cols), 16)


def _sum_halves(grad, got, c_arr):
    _, hr, cols = got.shape
    tr = _row_tile(hr, cols)
    nb = hr // tr

    def body(c_ref, a_ref, b_ref, o_ref):
        o_ref[...] = (a_ref[...].astype(F32) + b_ref[...].astype(F32)).astype(BF16)

    blk = pl.BlockSpec((None, tr, cols), lambda s, i, c_ref: (s, i, 0))
    return pl.pallas_call(
        body, name="sum_halves",
        grid_spec=pltpu.PrefetchScalarGridSpec(
            num_scalar_prefetch=1, grid=(N_SHARD, nb),
            in_specs=[pl.BlockSpec((None, tr, cols), lambda s, i, c_ref: (s, c_ref[0] * nb + i, 0)), blk],
            out_specs=blk),
        out_shape=_sds(got.shape, BF16), compiler_params=_params(2, 0))(c_arr, grad, got)


def _sum_chips(part, got, sc_arr):
    _, hr, cols = got.shape
    tr = _row_tile(hr, cols)
    nb = hr // tr

    def body(s_ref, a_ref, b_ref, o_ref):
        o_ref[...] = ((a_ref[...].astype(F32) + b_ref[0].astype(F32)) + b_ref[1].astype(F32)) + b_ref[2].astype(F32)

    return pl.pallas_call(
        body, name="sum_chips",
        grid_spec=pltpu.PrefetchScalarGridSpec(
            num_scalar_prefetch=1, grid=(nb,),
            in_specs=[pl.BlockSpec((None, tr, cols), lambda i, s_ref: (s_ref[0], i, 0)),
                      pl.BlockSpec((3, tr, cols), lambda i, s_ref: (0, i, 0))],
            out_specs=pl.BlockSpec((tr, cols), lambda i, s_ref: (s_ref[1] * nb + i, 0))),
        out_shape=_sds((2 * hr, cols), F32), compiler_params=_params(1, 0))(sc_arr, part, got)


def _cast_into_slot(x2d, sc_arr):
    rows, cols = x2d.shape
    tr = _row_tile(rows, cols)

    def body(s_ref, x_ref, o_ref):
        o_ref[...] = x_ref[...].astype(BF16)

    return pl.pallas_call(
        body, name="cast_into_slot",
        grid_spec=pltpu.PrefetchScalarGridSpec(
            num_scalar_prefetch=1, grid=(rows // tr,),
            in_specs=[pl.BlockSpec((tr, cols), lambda i, s_ref: (i, 0))],
            out_specs=pl.BlockSpec((None, tr, cols), lambda i, s_ref: (s_ref[0], i, 0))),
        out_shape=_sds((N_SHARD, rows, cols), BF16), compiler_params=_params(1, 0))(sc_arr, x2d)


def _adam_math(w, g, m, v):
    m = ADAM_B1 * m + (1.0 - ADAM_B1) * g
    v = ADAM_B2 * v + (1.0 - ADAM_B2) * (g * g)
    m_hat = m / (1.0 - ADAM_B1 ** ADAM_STEP)
    v_hat = v / (1.0 - ADAM_B2 ** ADAM_STEP)
    delta = -ADAM_LR * (m_hat / (jnp.sqrt(v_hat) + ADAM_EPS) + ADAM_WD * w)
    return delta, m, v


def _adamw(w, g, m, v):
    rows, cols = w.shape
    tr = _tile(rows, max(8, (1 << 19) // cols), 8)

    def body(w_ref, g_ref, m_ref, v_ref, go_ref, d_ref, mo_ref, vo_ref):
        g_ = g_ref[...]
        go_ref[...] = g_
        d_ref[...], mo_ref[...], vo_ref[...] = _adam_math(w_ref[...], g_, m_ref[...], v_ref[...])

    blk = pl.BlockSpec((tr, cols), lambda i: (i, 0))
    return pl.pallas_call(
        body, name="adamw", grid=(rows // tr,), in_specs=[blk] * 4, out_specs=[blk] * 4,
        out_shape=[_sds((rows, cols), F32)] * 4, compiler_params=_params(1, 0))(w, g, m, v)


def _adamw_small(gathered, w, m, v):
    rows, cols = w.shape

    def body(a_ref, w_ref, m_ref, v_ref, go_ref, d_ref, mo_ref, vo_ref):
        g_ = a_ref[pl.ds(0, rows), :]
        for k in range(1, 8):
            g_ = g_ + a_ref[pl.ds(k * rows, rows), :]
        go_ref[...] = g_
        d_ref[...], mo_ref[...], vo_ref[...] = _adam_math(w_ref[...], g_, m_ref[...], v_ref[...])

    vmem = pl.BlockSpec(memory_space=pltpu.VMEM)
    return pl.pallas_call(
        body, name="adamw_small", in_specs=[vmem] * 4, out_specs=[vmem] * 4,
        out_shape=[_sds((rows, cols), F32)] * 4)(gathered, w, m, v)


BIG = ("ffn1_w_in", "ffn1_w_out", "mix_w_in", "w_branch_a", "w_branch_b", "mix_w_out", "ffn2_w_in", "ffn2_w_out")
BIG_KEY = {"ffn1_w_in": "ffn1_in", "ffn1_w_out": "ffn1_out", "mix_w_in": "mix_in", "w_branch_a": "wa",
           "w_branch_b": "wb", "mix_w_out": "mix_out", "ffn2_w_in": "ffn2_in", "ffn2_w_out": "ffn2_out"}
SMALL = ("ln1_g", "ln1_b", "hgrn_lb_fwd", "hgrn_lb_bwd", "hgrn_norm_g", "ln2_g", "ln2_b", "ln3_g", "ln3_b")
ORDER = ("ffn1_w_in", "ffn1_w_out", "ln1_g", "ln1_b", "mix_w_in", "hgrn_lb_fwd", "hgrn_lb_bwd", "hgrn_norm_g",
         "w_branch_a", "w_branch_b", "mix_w_out", "ln2_g", "ln2_b", "ffn2_w_in", "ffn2_w_out", "ln3_g", "ln3_b")
SMALL_ROWS = 16


def _pack_small(d):
    rows = jnp.concatenate([d[k].reshape(-1, d[k].shape[-1]) for k in SMALL], axis=0)
    return jnp.pad(rows, ((0, SMALL_ROWS - rows.shape[0]), (0, 0)))


def _unpack_small(a, like):
    out, r = {}, 0
    for k in SMALL:
        n = like[k].shape[0]
        out[k] = a[r:r + n].reshape(like[k].shape)
        r += n
    return out


def kernel(x, ffn1_w_in, ffn1_w_out, ln1_g, ln1_b, mix_w_in, hgrn_lb_fwd, hgrn_lb_bwd, hgrn_norm_g, w_branch_a, w_branch_b, mix_w_out, ln2_g, ln2_b, ffn2_w_in, ffn2_w_out, ln3_g, ln3_b, loss_target, m_ffn1_w_in, m_ffn1_w_out, m_ln1_g, m_ln1_b, m_mix_w_in, m_hgrn_lb_fwd, m_hgrn_lb_bwd, m_hgrn_norm_g, m_w_branch_a, m_w_branch_b, m_mix_w_out, m_ln2_g, m_ln2_b, m_ffn2_w_in, m_ffn2_w_out, m_ln3_g, m_ln3_b, v_ffn1_w_in, v_ffn1_w_out, v_ln1_g, v_ln1_b, v_mix_w_in, v_hgrn_lb_fwd, v_hgrn_lb_bwd, v_hgrn_norm_g, v_w_branch_a, v_w_branch_b, v_mix_w_out, v_ln2_g, v_ln2_b, v_ffn2_w_in, v_ffn2_w_out, v_ln3_g, v_ln3_b):
    wts = dict(ffn1_w_in=ffn1_w_in, ffn1_w_out=ffn1_w_out, ln1_g=ln1_g, ln1_b=ln1_b, mix_w_in=mix_w_in,
               hgrn_lb_fwd=hgrn_lb_fwd, hgrn_lb_bwd=hgrn_lb_bwd, hgrn_norm_g=hgrn_norm_g, w_branch_a=w_branch_a,
               w_branch_b=w_branch_b, mix_w_out=mix_w_out, ln2_g=ln2_g, ln2_b=ln2_b, ffn2_w_in=ffn2_w_in,
               ffn2_w_out=ffn2_w_out, ln3_g=ln3_g, ln3_b=ln3_b)
    mom = dict(ffn1_w_in=m_ffn1_w_in, ffn1_w_out=m_ffn1_w_out, ln1_g=m_ln1_g, ln1_b=m_ln1_b, mix_w_in=m_mix_w_in,
               hgrn_lb_fwd=m_hgrn_lb_fwd, hgrn_lb_bwd=m_hgrn_lb_bwd, hgrn_norm_g=m_hgrn_norm_g,
               w_branch_a=m_w_branch_a, w_branch_b=m_w_branch_b, mix_w_out=m_mix_w_out, ln2_g=m_ln2_g, ln2_b=m_ln2_b,
               ffn2_w_in=m_ffn2_w_in, ffn2_w_out=m_ffn2_w_out, ln3_g=m_ln3_g, ln3_b=m_ln3_b)
    var = dict(ffn1_w_in=v_ffn1_w_in, ffn1_w_out=v_ffn1_w_out, ln1_g=v_ln1_g, ln1_b=v_ln1_b, mix_w_in=v_mix_w_in,
               hgrn_lb_fwd=v_hgrn_lb_fwd, hgrn_lb_bwd=v_hgrn_lb_bwd, hgrn_norm_g=v_hgrn_norm_g,
               w_branch_a=v_w_branch_a, w_branch_b=v_w_branch_b, mix_w_out=v_mix_w_out, ln2_g=v_ln2_g, ln2_b=v_ln2_b,
               ffn2_w_in=v_ffn2_w_in, ffn2_w_out=v_ffn2_w_out, ln3_g=v_ln3_g, ln3_b=v_ln3_b)
    shard = (2 * lax.axis_index("x") + lax.axis_index("y")).astype(jnp.int32)
    sc_arr = jnp.stack([shard, lax.axis_index("c").astype(jnp.int32)])

    shard2d = {k: wts[k].reshape(wts[k].shape[1:]) for k in BIG}
    fulls = {BIG_KEY[k]: _cast_into_slot(shard2d[k], sc_arr) for k in BIG}
    p = {k: wts[k] for k in SMALL}

    loss, grad_x, grads, gp = _local_step(x, loss_target, fulls, p, sc_arr)
    loss = lax.psum(loss[0, 0], ("x", "y", "c"))

    out_g, out_d, out_m, out_v = {}, {}, {}, {}
    for k in BIG:
        g = grads[BIG_KEY[k]]
        shp = wts[k].shape
        res = _adamw(shard2d[k], g, mom[k].reshape(shp[1:]), var[k].reshape(shp[1:]))
        out_g[k], out_d[k], out_m[k], out_v[k] = [a.reshape(shp) for a in res]

    gathered = _gather_rows(_pack_small(gp))
    res = _adamw_small(gathered, _pack_small(wts), _pack_small(mom), _pack_small(var))
    for dst, a in zip((out_g, out_d, out_m, out_v), res):
        dst.update(_unpack_small(a, wts))

    return (loss, grad_x, *[out_g[k] for k in ORDER], *[out_d[k] for k in ORDER],
            *[out_m[k] for k in ORDER], *[out_v[k] for k in ORDER])
```

```python
import functools

import numpy as np
import jax
import jax.numpy as jnp
from jax import lax
from jax.experimental import pallas as pl
from jax.experimental.pallas import tpu as pltpu

F32 = jnp.float32
BF16 = jnp.bfloat16

HEAD = 128
ATTN_GROUPS = ((128, 1), (512, 4), (2048, 16))
ATTN_HEADS = 4
N_GROUPS = len(ATTN_GROUPS)
QKV_W = N_GROUPS * 3 * ATTN_HEADS * HEAD
ATTN_OUT = ATTN_HEADS * HEAD
ROPE_THETA = 500000.0
ROPE_DIM = HEAD // 4
HGRN_CHUNK = 32
HGRN_FWD_HEADS = 2
HGRN_BWD_HEADS = 2
HGRN_SUB = 4
ALPHA = 2.0 ** 0.25
LN_EPS = 1e-5
NEG_INF = -1e30
ADAM_LR, ADAM_B1, ADAM_B2, ADAM_EPS, ADAM_WD, ADAM_STEP = 0.001, 0.9, 0.999, 1e-08, 0.01, 10

N_SHARD = 4
VMEM_LIMIT = 56 * 1024 * 1024

NN = ((1,), (0,))
NT = ((1,), (1,))
TN = ((0,), (0,))


def _dot(a, b, dims, precision=None):
    return lax.dot_general(a, b, (dims, ((), ())), preferred_element_type=F32, precision=precision)


def _tile(n, pref, mult=128):
    best = None
    for t in range(mult, min(n, pref) + 1, mult):
        if n % t == 0:
            best = t
    return n if best is None else best


def _params(n_parallel, n_arbitrary):
    return pltpu.CompilerParams(
        dimension_semantics=("parallel",) * n_parallel + ("arbitrary",) * n_arbitrary,
        vmem_limit_bytes=VMEM_LIMIT)


def _sigmoid(x):
    return 1.0 / (1.0 + jnp.exp(-x))


def _silu(x):
    return x * _sigmoid(x)


class _Exchange:
    def __init__(self, ins, outs, aliases, sems, start, finish):
        self.ins, self.outs, self.aliases, self.sems = list(ins), list(outs), dict(aliases), list(sems)
        self.start, self.finish = start, finish


def _run(body, name, grid, ins, in_specs, outs, out_specs, scratch, n_arbitrary, side=None):
    n_in, n_out, n_scr = len(ins), len(outs), len(scratch)
    if side is None:
        return pl.pallas_call(
            body, name=name, grid=grid, in_specs=in_specs, out_specs=out_specs, out_shape=outs,
            scratch_shapes=scratch, compiler_params=_params(len(grid) - n_arbitrary, n_arbitrary))(*ins)
    s_in, s_out = len(side.ins), len(side.outs)
    i1 = n_in + s_in
    o1 = i1 + n_out
    o2 = o1 + s_out
    c1 = o2 + n_scr

    def wrapped(*refs):
        s_refs = (refs[n_in:i1], refs[o1:o2], refs[c1:])
        ids = [pl.program_id(a) for a in range(len(grid))]
        first = functools.reduce(jnp.logical_and, [i == 0 for i in ids])
        last = functools.reduce(jnp.logical_and, [i == g - 1 for i, g in zip(ids, grid)])

        @pl.when(first)
        def _():
            side.start(*s_refs)

        body(*refs[:n_in], *refs[i1:o1], *refs[o2:c1])

        @pl.when(last)
        def _():
            side.finish(*s_refs)

    res = pl.pallas_call(
        wrapped, name=name, grid=grid, in_specs=list(in_specs) + [ANY] * s_in,
        out_specs=list(out_specs) + [ANY] * s_out, out_shape=list(outs) + side.outs,
        scratch_shapes=list(scratch) + side.sems,
        input_output_aliases={n_in + a: n_out + b for a, b in side.aliases.items()},
        compiler_params=_params(0, len(grid)))(*ins, *side.ins)
    return res[:n_out], res[n_out:]


def _gemm(name, grid, ins, in_specs, outs, out_specs, accs, dot_fn, epi_fn, side=None):
    n_in, n_out, n_k = len(ins), len(outs), grid[-1]

    def body(*refs):
        in_refs, out_refs, acc_refs = refs[:n_in], refs[n_in:n_in + n_out], refs[n_in + n_out:]
        k = pl.program_id(len(grid) - 1)

        @pl.when(k == 0)
        def _():
            for a in acc_refs:
                a[...] = jnp.zeros(a.shape, F32)

        dot_fn(in_refs, acc_refs)

        @pl.when(k == n_k - 1)
        def _():
            epi_fn(in_refs, acc_refs, out_refs)

    return _run(body, name, grid, ins, in_specs, outs, out_specs, [pltpu.VMEM(s, F32) for s in accs], 1, side)


def _sds(shape, dtype):
    return jax.ShapeDtypeStruct(shape, dtype)


def _ffn_up(xb, w3, side=None):
    t, d = xb.shape
    nf = w3.shape[2]
    f = 2 * nf
    tm = _tile(t, 512, 8)
    tn = nf // 2 if nf % 256 == 0 else nf
    ns = nf // tn

    def dot_fn(r, acc):
        acc[0][...] += _dot(r[0][...], r[1][...], NN)
        acc[1][...] += _dot(r[0][...], r[2][...], NN)

    def epi_fn(r, acc, out):
        g, u = acc[0][...], acc[1][...]
        out[0][0] = g.astype(BF16)
        out[0][1] = u.astype(BF16)
        out[1][...] = (_silu(g) * u).astype(BF16)

    return _gemm(
        "ffn_up", (t // tm, 2, ns, 1), [xb, w3, w3],
        [pl.BlockSpec((tm, d), lambda i, j, h, k: (i, 0)),
         pl.BlockSpec((None, d, tn), lambda i, j, h, k: (j, 0, h)),
         pl.BlockSpec((None, d, tn), lambda i, j, h, k: (j + 2, 0, h))],
        [_sds((2, t, f), BF16), _sds((t, f), BF16)],
        [pl.BlockSpec((2, tm, tn), lambda i, j, h, k: (0, i, j * ns + h)),
         pl.BlockSpec((tm, tn), lambda i, j, h, k: (i, j * ns + h))],
        [(tm, tn), (tm, tn)], dot_fn, epi_fn, side)


def _down_ln(a, w, resid, coef, g, b, side=None):
    t, kd = a.shape
    d = w.shape[1]
    tm, tk = _tile(t, 256, 8), _tile(kd, 1408)

    def dot_fn(r, acc):
        acc[0][...] += _dot(r[0][...], r[1][...], NN)

    def epi_fn(r, acc, out):
        v = ALPHA * r[2][...] + coef * acc[0][...]
        mu = jnp.mean(v, axis=-1, keepdims=True)
        c = v - mu
        var = jnp.mean(c * c, axis=-1, keepdims=True)
        rstd = lax.rsqrt(var + LN_EPS)
        xhat = c * rstd
        h = xhat * r[3][...] + r[4][...]
        out[0][...] = h
        out[1][...] = h.astype(BF16)
        out[2][...] = xhat
        out[3][...] = rstd

    row = pl.BlockSpec((tm, d), lambda i, k: (i, 0))
    vec = pl.BlockSpec((1, d), lambda i, k: (0, 0))
    return _gemm(
        "down_ln", (t // tm, kd // tk), [a, w, resid, g, b],
        [pl.BlockSpec((tm, tk), lambda i, k: (i, k)), pl.BlockSpec((tk, d), lambda i, k: (k, 0)), row, vec, vec],
        [_sds((t, d), F32), _sds((t, d), BF16), _sds((t, d), F32), _sds((t, 1), F32)],
        [row, row, row, pl.BlockSpec((tm, 1), lambda i, k: (i, 0))],
        [(tm, d)], dot_fn, epi_fn, side)


def _only(res, side):
    return res[0] if side is None else (res[0][0], res[1])


def _mm_w3(a, w3, side=None):
    t, kd = a.shape
    n = w3.shape[2]
    tm, tk = _tile(t, 512, 8), _tile(kd, 1024)

    def body(a_ref, w_ref, o_ref):
        k = pl.program_id(2)
        p = _dot(a_ref[...], w_ref[...], NN)

        @pl.when(k == 0)
        def _():
            o_ref[...] = p

        @pl.when(k > 0)
        def _():
            o_ref[...] += p

    return _only(_run(
        body, "mm_w3", (t // tm, N_SHARD, kd // tk), [a, w3],
        [pl.BlockSpec((tm, tk), lambda i, j, k: (i, k)), pl.BlockSpec((None, tk, n), lambda i, j, k: (j, k, 0))],
        [_sds((t, N_SHARD * n), F32)], [pl.BlockSpec((tm, n), lambda i, j, k: (i, j))], [], 1, side), side)


def _mm_w2(a, w, out_dtype):
    t, kd = a.shape
    n = w.shape[1]
    tm, tn, tk = _tile(t, 512, 8), _tile(n, 1024), _tile(kd, 512)

    def dot_fn(r, acc):
        acc[0][...] += _dot(r[0][...], r[1][...], NN)

    def epi_fn(r, acc, out):
        out[0][...] = acc[0][...].astype(out_dtype)

    return _gemm(
        "mm_w2", (t // tm, n // tn, kd // tk), [a, w],
        [pl.BlockSpec((tm, tk), lambda i, j, k: (i, k)), pl.BlockSpec((tk, tn), lambda i, j, k: (k, j))],
        [_sds((t, n), out_dtype)], [pl.BlockSpec((tm, tn), lambda i, j, k: (i, j))],
        [(tm, tn)], dot_fn, epi_fn)[0]


def _branch_gate(ob, wb3, ya, proj, gate_col):
    t, kd = ob.shape
    n = wb3.shape[2]
    d = N_SHARD * n
    tm = _tile(t, 512, 8)
    ga0, gb0 = gate_col // n, (gate_col + d) // n

    def dot_fn(r, acc):
        acc[0][...] += _dot(r[0][...], r[1][...], NN)

    def epi_fn(r, acc, out):
        yb = acc[0][...]
        out[0][...] = yb
        out[1][...] = (_sigmoid(r[3][...]) * r[2][...] + _sigmoid(r[4][...]) * yb).astype(BF16)

    blk = pl.BlockSpec((tm, n), lambda i, j, k: (i, j))
    return _gemm(
        "branch_gate", (t // tm, N_SHARD, 1), [ob, wb3, ya, proj, proj],
        [pl.BlockSpec((tm, kd), lambda i, j, k: (i, 0)), pl.BlockSpec((None, kd, n), lambda i, j, k: (j, 0, 0)), blk,
         pl.BlockSpec((tm, n), lambda i, j, k: (i, ga0 + j)), pl.BlockSpec((tm, n), lambda i, j, k: (i, gb0 + j))],
        [_sds((t, d), F32), _sds((t, d), BF16)], [blk, blk], [(tm, n)], dot_fn, epi_fn)


def _swiglu_bwd(dyb, w, u3):
    t, d = dyb.shape
    f = w.shape[0]
    tm, tr = _tile(t, 1024, 8), _tile(f, 512)

    def dot_fn(r, acc):
        acc[0][...] += _dot(r[0][...], r[1][...], NT)

    def epi_fn(r, acc, out):
        da = acc[0][...]
        g, u = r[2][0].astype(F32), r[2][1].astype(F32)
        s = _sigmoid(g)
        out[0][0] = (da * u * s * (1.0 + g * (1.0 - s))).astype(BF16)
        out[0][1] = (da * g * s).astype(BF16)

    ublk = pl.BlockSpec((2, tm, tr), lambda i, j, k: (0, i, j))
    return _gemm(
        "swiglu_bwd", (t // tm, f // tr, 1), [dyb, w, u3],
        [pl.BlockSpec((tm, d), lambda i, j, k: (i, 0)), pl.BlockSpec((tr, d), lambda i, j, k: (j, 0)), ublk],
        [_sds((2, t, f), BF16)], [ublk], [(tm, tr)], dot_fn, epi_fn)[0]


def _ffn_dx(du3, w3, resid, side=None):
    t = du3.shape[1]
    d, nf = w3.shape[1], w3.shape[2]
    tm, tr = _tile(t, 512, 8), _tile(d, 1024)

    def dot_fn(r, acc):
        acc[0][...] += _dot(r[0][...], r[1][...], NT)

    def epi_fn(r, acc, out):
        out[0][...] = acc[0][...] + r[2][...]

    blk = pl.BlockSpec((tm, tr), lambda i, j, k: (i, j))
    return _only(_gemm(
        "ffn_dx", (t // tm, d // tr, N_SHARD), [du3, w3, resid],
        [pl.BlockSpec((None, tm, nf), lambda i, j, k: (k // 2, i, k % 2)),
         pl.BlockSpec((None, tr, nf), lambda i, j, k: (k, j, 0)), blk],
        [_sds((t, d), F32)], [blk], [(tm, tr)], dot_fn, epi_fn, side), side)


def _nt_w3(dy, w3, resid, side=None):
    t = dy.shape[0]
    kd, n = w3.shape[1], w3.shape[2]
    tm, tr = _tile(t, 512, 8), _tile(kd, 1024)
    has_res = resid is not None

    def dot_fn(r, acc):
        acc[0][...] += _dot(r[0][...], r[1][...], NT)

    def epi_fn(r, acc, out):
        v = acc[0][...]
        if has_res:
            v = v + r[2][...]
        out[0][...] = v

    blk = pl.BlockSpec((tm, tr), lambda i, j, k: (i, j))
    return _only(_gemm(
        "nt_w3", (t // tm, kd // tr, N_SHARD), [dy, w3] + ([resid] if has_res else []),
        [pl.BlockSpec((tm, n), lambda i, j, k: (i, k)), pl.BlockSpec((None, tr, n), lambda i, j, k: (k, j, 0))]
        + ([blk] if has_res else []),
        [_sds((t, kd), F32)], [blk], [(tm, tr)], dot_fn, epi_fn, side), side)


def _nt_w2(dy, w, out_dtype):
    t, n = dy.shape
    r_ = w.shape[0]
    tm, tr, tk = _tile(t, 512, 8), _tile(r_, 1024), _tile(n, 1024)

    def dot_fn(r, acc):
        acc[0][...] += _dot(r[0][...], r[1][...], NT)

    def epi_fn(r, acc, out):
        out[0][...] = acc[0][...].astype(out_dtype)

    return _gemm(
        "nt_w2", (t // tm, r_ // tr, n // tk), [dy, w],
        [pl.BlockSpec((tm, tk), lambda i, j, k: (i, k)), pl.BlockSpec((tr, tk), lambda i, j, k: (j, k))],
        [_sds((t, r_), out_dtype)], [pl.BlockSpec((tm, tr), lambda i, j, k: (i, j))],
        [(tm, tr)], dot_fn, epi_fn)[0]


def _gate_bwd(dmix, w, proj, ya, yb, gate_col):
    t, n = dmix.shape
    d = w.shape[0]
    tm, tr = _tile(t, 512, 8), _tile(d, 512)
    ga0, gb0 = gate_col // tr, (gate_col + d) // tr
    nb = d // tr

    def dot_fn(r, acc):
        acc[0][...] += _dot(r[0][...], r[1][...], NT)

    def epi_fn(r, acc, out):
        dz = acc[0][...]
        ga, gb = _sigmoid(r[2][...]), _sigmoid(r[3][...])
        out[0][...] = (dz * ga).astype(BF16)
        out[1][...] = (dz * gb).astype(BF16)
        out[2][...] = (dz * r[4][...] * ga * (1.0 - ga)).astype(BF16)
        out[3][...] = (dz * r[5][...] * gb * (1.0 - gb)).astype(BF16)

    blk = pl.BlockSpec((tm, tr), lambda i, j, k: (i, j))
    return _gemm(
        "gate_bwd", (t // tm, nb, 1), [dmix, w, proj, proj, ya, yb],
        [pl.BlockSpec((tm, n), lambda i, j, k: (i, 0)), pl.BlockSpec((tr, n), lambda i, j, k: (j, 0)),
         pl.BlockSpec((tm, tr), lambda i, j, k: (i, ga0 + j)), pl.BlockSpec((tm, tr), lambda i, j, k: (i, gb0 + j)),
         blk, blk],
        [_sds((t, d), BF16), _sds((t, d), BF16), _sds((t, d), BF16), _sds((t, d), BF16)],
        [blk, blk, blk, blk], [(tm, tr)], dot_fn, epi_fn)


def _tn_w3(a, dy, dy_map, n, side=None):
    t, kd = a.shape
    tm, tkk = _tile(t, (6 << 20) // n, 8), _tile(kd, 512)

    def dot_fn(r, acc):
        acc[0][...] += _dot(r[0][...], r[1][...], TN)

    def epi_fn(r, acc, out):
        out[0][...] = acc[0][...].astype(BF16)

    dy_block = (tm, n) if dy.ndim == 2 else (None, tm, n)
    return _only(_gemm(
        "tn_w3", (kd // tkk, N_SHARD, t // tm), [a, dy],
        [pl.BlockSpec((tm, tkk), lambda i, j, m: (m, i)), pl.BlockSpec(dy_block, lambda i, j, m: dy_map(j, m))],
        [_sds((N_SHARD, kd, n), BF16)], [pl.BlockSpec((None, tkk, n), lambda i, j, m: (j, i, 0))],
        [(tkk, n)], dot_fn, epi_fn, side), side)


def _tn_w2(a, dy):
    t, kd = a.shape
    n = dy.shape[1]
    tkk, tn = _tile(kd, 512), _tile(n, 2048)
    tm = _tile(t, (4 << 20) // tn, 8)

    def dot_fn(r, acc):
        acc[0][...] += _dot(r[0][...], r[1][...], TN)

    def epi_fn(r, acc, out):
        out[0][...] = acc[0][...].astype(BF16)

    return _gemm(
        "tn_w2", (kd // tkk, n // tn, t // tm), [a, dy],
        [pl.BlockSpec((tm, tkk), lambda i, j, m: (m, i)), pl.BlockSpec((tm, tn), lambda i, j, m: (m, j))],
        [_sds((kd, n), BF16)], [pl.BlockSpec((tkk, tn), lambda i, j, m: (i, j))],
        [(tkk, tn)], dot_fn, epi_fn)[0]


def _ln_bwd(dh_parts, xhat, rstd, g, coef, target=None):
    t, d = xhat.shape
    tm = _tile(t, 256, 8)
    n_parts = len(dh_parts)
    with_loss = target is not None
    ins = list(dh_parts) + [xhat, rstd, g] + ([target] if with_loss else [])

    def body(*refs):
        parts = refs[:n_parts]
        xh_ref, rstd_ref, g_ref = refs[n_parts:n_parts + 3]
        o = n_parts + 3 + (1 if with_loss else 0)
        dres_ref, dyb_ref, dg_ref, db_ref, loss_ref = refs[o:o + 5]
        i = pl.program_id(0)
        dh = parts[0][...]
        for p in parts[1:]:
            dh = dh + p[...]
        if with_loss:
            e = dh - refs[o - 1][...]
            part = 0.5 * jnp.sum(jnp.sum(e * e, axis=-1, keepdims=True) * (1.0 / d), axis=0, keepdims=True)
            dh = e * (1.0 / d)
        else:
            part = jnp.zeros((1, 1), F32)
        xh = xh_ref[...]
        dxh = dh * g_ref[...]
        m1 = jnp.mean(dxh, axis=-1, keepdims=True)
        m2 = jnp.mean(dxh * xh, axis=-1, keepdims=True)
        dv = rstd_ref[...] * (dxh - m1 - xh * m2)
        dres_ref[...] = ALPHA * dv
        dyb_ref[...] = (coef * dv).astype(BF16)

        @pl.when(i == 0)
        def _():
            dg_ref[...] = jnp.zeros(dg_ref.shape, F32)
            db_ref[...] = jnp.zeros(db_ref.shape, F32)
            loss_ref[...] = jnp.zeros(loss_ref.shape, F32)

        dg_ref[...] += jnp.sum(dh * xh, axis=0, keepdims=True)
        db_ref[...] += jnp.sum(dh, axis=0, keepdims=True)
        loss_ref[...] += jnp.broadcast_to(part, loss_ref.shape)

    row = pl.BlockSpec((tm, d), lambda i: (i, 0))
    vec = pl.BlockSpec((1, d), lambda i: (0, 0))
    return pl.pallas_call(
        body, name="ln_bwd", grid=(t // tm,),
        in_specs=[row] * n_parts + [row, pl.BlockSpec((tm, 1), lambda i: (i, 0)), vec] + ([row] if with_loss else []),
        out_specs=[row, row, vec, vec, pl.BlockSpec((1, 128), lambda i: (0, 0))],
        out_shape=[_sds((t, d), F32), _sds((t, d), BF16), _sds((1, d), F32), _sds((1, d), F32), _sds((1, 128), F32)],
        compiler_params=_params(0, 1))(*ins)


def _cast_bf16(x2d):
    t, d = x2d.shape
    tm = _tile(t, 512, 8)

    def body(x_ref, o_ref):
        o_ref[...] = x_ref[...].astype(BF16)

    row = pl.BlockSpec((tm, d), lambda i: (i, 0))
    return pl.pallas_call(body, name="cast_bf16", grid=(t // tm,), in_specs=[row], out_specs=row,
                          out_shape=_sds((t, d), BF16), compiler_params=_params(1, 0))(x2d)


def _lower_bound(table):
    t0, t1 = table[0:1, :], table[1:2, :]
    m = jnp.maximum(t0, t1)
    e0, e1 = jnp.exp(t0 - m), jnp.exp(t1 - m)
    return e0 / (e0 + e1)


def _chunk_tri(rows, upper):
    r = lax.broadcasted_iota(jnp.int32, (rows, rows), 0)
    s = lax.broadcasted_iota(jnp.int32, (rows, rows), 1)
    shift = HGRN_CHUNK.bit_length() - 1
    same = lax.shift_right_logical(r, shift) == lax.shift_right_logical(s, shift)
    return same & ((r <= s) if upper else (r >= s))


def _tri_apply(x, upper):
    tri = _chunk_tri(x.shape[0], upper).astype(F32).astype(BF16)
    hi = x.astype(BF16)
    r1 = x - hi.astype(F32)
    mid = r1.astype(BF16)
    lo = (r1 - mid.astype(F32)).astype(BF16)
    return _dot(tri, hi, NN) + _dot(tri, mid, NN) + _dot(tri, lo, NN)


@functools.partial(jax.custom_vjp, nondiff_argnums=(1,))
def _chunk_cumsum(x, upper):
    return _tri_apply(x, upper)


def _chunk_cumsum_fwd(x, upper):
    return _tri_apply(x, upper), None


def _chunk_cumsum_bwd(upper, _, g):
    return (_tri_apply(g, not upper),)


_chunk_cumsum.defvjp(_chunk_cumsum_fwd, _chunk_cumsum_bwd)


def _hgrn_blocks(units, uppers):
    c = HGRN_CHUNK
    rows = units[0][0].shape[0]
    n_sub = rows // c
    ids = range(len(units))
    chunk = lax.shift_right_logical(lax.broadcasted_iota(jnp.int32, (rows, HEAD), 0), c.bit_length() - 1)
    zero = jnp.zeros((rows, HEAD), BF16)

    def expand(a):
        return jnp.concatenate([jnp.where(chunk == n, a, zero) for n in range(n_sub)], axis=1)

    fs = [u[3] + (1.0 - u[3]) * _sigmoid(u[1]) for u in units]
    lgs = [jnp.log(f) for f in fs]
    cums = [_chunk_cumsum(lgs[i], uppers[i]) for i in ids]
    tots = [[jnp.sum(lg[n * c:(n + 1) * c], axis=0, keepdims=True) for n in range(n_sub)] for lg in lgs]
    qd, kd, ke, vb = [], [], [], []
    for i in ids:
        totb = jnp.concatenate([jnp.broadcast_to(t, (c, HEAD)) for t in tots[i]], axis=0)
        kk = 1.0 - fs[i]
        qd.append((_silu(units[i][0]) * jnp.exp(cums[i])).astype(BF16))
        kd.append((kk * jnp.exp(-cums[i])).astype(BF16))
        ke.append((kk * jnp.exp(totb - cums[i])).astype(BF16))
        vb.append(units[i][2].astype(BF16))
    scores = [_dot(qd[i], kd[i], NT) for i in ids]
    kvs = [_dot(vb[i], expand(ke[i]), TN) for i in ids]
    outs = []
    for i in ids:
        a = jnp.where(_chunk_tri(rows, uppers[i]), scores[i], 0.0).astype(BF16)
        st = units[i][4]
        entering = [None] * n_sub
        for n in (range(n_sub - 1, -1, -1) if uppers[i] else range(n_sub)):
            entering[n] = st.astype(BF16)
            st = st * jnp.exp(tots[i][n]) + kvs[i][:, n * HEAD:(n + 1) * HEAD]
        outs.append((a, jnp.concatenate(entering, axis=1), st))
    res = []
    for i in ids:
        a, entering, st = outs[i]
        res.append((_dot(a, vb[i], NN) + _dot(expand(qd[i]), entering, NT), st))
    return res


def _hgrn_fwd(proj3, lb_f, lb_b, norm_g, d):
    b_, s_, _ = proj3.shape
    nh = d // HEAD
    hb = HGRN_FWD_HEADS
    wid = hb * HEAD
    rows = HGRN_CHUNK * HGRN_SUB
    n_blk = s_ // rows

    def body(hq_ref, hff_ref, hfb_ref, hi_ref, hog_ref, lbf_ref, lbb_ref, g_ref, oraw_ref, out_ref, of_ref, ob_ref):
        lbf, lbb = _lower_bound(lbf_ref[...]), _lower_bound(lbb_ref[...])

        def step(j, sts):
            rf = pl.ds(pl.multiple_of(j * rows, rows), rows)
            rb = pl.ds(pl.multiple_of((n_blk - 1 - j) * rows, rows), rows)
            units = []
            for hh in range(hb):
                cs = slice(hh * HEAD, (hh + 1) * HEAD)
                units.append((hq_ref[rf, cs], hff_ref[rf, cs], hi_ref[rf, cs], lbf[:, cs], sts[2 * hh]))
                units.append((hq_ref[rb, cs], hfb_ref[rb, cs], hi_ref[rb, cs], lbb[:, cs], sts[2 * hh + 1]))
            res = _hgrn_blocks(units, [False, True] * hb)
            for hh in range(hb):
                cs = slice(hh * HEAD, (hh + 1) * HEAD)
                of_ref[rf, cs] = res[2 * hh][0]
                ob_ref[rb, cs] = res[2 * hh + 1][0]
            return tuple(r[1] for r in res)

        z = jnp.zeros((HEAD, HEAD), F32)
        lax.fori_loop(0, n_blk, step, (z,) * (2 * hb))
        for hh in range(hb):
            cs = slice(hh * HEAD, (hh + 1) * HEAD)
            o = of_ref[:, cs] + ob_ref[:, cs]
            oraw_ref[:, cs] = o
            on = o * lax.rsqrt(jnp.mean(o * o, axis=-1, keepdims=True) + LN_EPS)
            out_ref[:, cs] = (on * g_ref[:, cs] * _silu(hog_ref[:, cs])).astype(BF16)

    def col(k):
        return pl.BlockSpec((None, s_, wid), lambda h, b: (b, 0, k * (nh // hb) + h))

    tab = pl.BlockSpec((2, wid), lambda h, b: (0, h))
    oblk = pl.BlockSpec((None, s_, wid), lambda h, b: (b, 0, h))
    return pl.pallas_call(
        body, name="hgrn_fwd", grid=(nh // hb, b_),
        in_specs=[col(0), col(1), col(2), col(3), col(4), tab, tab, pl.BlockSpec((1, wid), lambda h, b: (0, h))],
        out_specs=[oblk, oblk], out_shape=[_sds((b_, s_, d), F32), _sds((b_, s_, d), BF16)],
        scratch_shapes=[pltpu.VMEM((s_, wid), F32), pltpu.VMEM((s_, wid), F32)],
        compiler_params=_params(2, 0))(proj3, proj3, proj3, proj3, proj3, lb_f, lb_b, norm_g)


def _hgrn_bwd(proj3, lb_f, lb_b, norm_g, o_raw, do_a, d, side=None):
    b_, s_, _ = proj3.shape
    nh = d // HEAD
    hb = HGRN_BWD_HEADS
    wid = hb * HEAD
    rows = HGRN_CHUNK * HGRN_SUB
    n_blk = s_ // rows

    def body(hq_ref, hff_ref, hfb_ref, hi_ref, hog_ref, lbf_ref, lbb_ref, g_ref, oraw_ref, doa_ref,
             dq_ref, dff_ref, dfb_ref, di_ref, dog_ref, dlbf_ref, dlbb_ref, dg_ref,
             st_ref, dor_ref, dq2_ref, di2_ref):
        b = pl.program_id(1)
        tab_f, tab_b = lbf_ref[...], lbb_ref[...]
        lbf, lbb = _lower_bound(tab_f), _lower_bound(tab_b)

        dg_parts = []
        for hh in range(hb):
            cs = slice(hh * HEAD, (hh + 1) * HEAD)
            o, doa, hog, g = oraw_ref[:, cs], doa_ref[:, cs], hog_ref[:, cs], g_ref[:, cs]
            rs = lax.rsqrt(jnp.mean(o * o, axis=-1, keepdims=True) + LN_EPS)
            on = o * rs
            sg = _sigmoid(hog)
            gate = hog * sg
            dog_ref[:, cs] = (doa * on * g * sg * (1.0 + hog * (1.0 - sg))).astype(BF16)
            don = doa * g * gate
            dor_ref[:, cs] = rs * (don - on * jnp.mean(don * on, axis=-1, keepdims=True))
            dg_parts.append(jnp.sum(doa * on * gate, axis=0, keepdims=True))

        def fwd_step(j, carry):
            jb = n_blk - 1 - j
            rf = pl.ds(pl.multiple_of(j * rows, rows), rows)
            rb = pl.ds(pl.multiple_of(jb * rows, rows), rows)
            units = []
            for hh in range(hb):
                cs = slice(hh * HEAD, (hh + 1) * HEAD)
                st_ref[2 * hh, j] = carry[2 * hh]
                st_ref[2 * hh + 1, jb] = carry[2 * hh + 1]
                units.append((hq_ref[rf, cs], hff_ref[rf, cs], hi_ref[rf, cs], lbf[:, cs], carry[2 * hh]))
                units.append((hq_ref[rb, cs], hfb_ref[rb, cs], hi_ref[rb, cs], lbb[:, cs], carry[2 * hh + 1]))
            return tuple(r[1] for r in _hgrn_blocks(units, uppers))

        uppers = [False, True] * hb
        z = jnp.zeros((HEAD, HEAD), F32)
        lax.fori_loop(0, n_blk, fwd_step, (z,) * (2 * hb))

        def bwd_step(j, carry):
            gs, dls = carry
            jf = n_blk - 1 - j
            rf = pl.ds(pl.multiple_of(jf * rows, rows), rows)
            rb = pl.ds(pl.multiple_of(j * rows, rows), rows)
            flat, cots = [], []
            for hh in range(hb):
                cs = slice(hh * HEAD, (hh + 1) * HEAD)
                flat += [hq_ref[rf, cs], hff_ref[rf, cs], hi_ref[rf, cs], lbf[:, cs], st_ref[2 * hh, jf],
                         hq_ref[rb, cs], hfb_ref[rb, cs], hi_ref[rb, cs], lbb[:, cs], st_ref[2 * hh + 1, j]]
                cots += [(dor_ref[rf, cs], gs[2 * hh]), (dor_ref[rb, cs], gs[2 * hh + 1])]
            _, vjp = jax.vjp(lambda *a: _hgrn_blocks([a[5 * i:5 * i + 5] for i in range(2 * hb)], uppers), *flat)
            grads = vjp(cots)
            gs, dls = list(gs), list(dls)
            for hh in range(hb):
                cs = slice(hh * HEAD, (hh + 1) * HEAD)
                dq, df, di, dl, gs[2 * hh] = grads[10 * hh:10 * hh + 5]
                dq_ref[rf, cs] = dq.astype(BF16)
                dff_ref[rf, cs] = df.astype(BF16)
                di_ref[rf, cs] = di.astype(BF16)
                dls[2 * hh] = dls[2 * hh] + dl
                dq, df, di, dl, gs[2 * hh + 1] = grads[10 * hh + 5:10 * hh + 10]
                dq2_ref[rb, cs] = dq
                dfb_ref[rb, cs] = df.astype(BF16)
                di2_ref[rb, cs] = di
                dls[2 * hh + 1] = dls[2 * hh + 1] + dl
            return tuple(gs), tuple(dls)

        zl = jnp.zeros((1, HEAD), F32)
        _, dls = lax.fori_loop(0, n_blk, bwd_step, ((z,) * (2 * hb), (zl,) * (2 * hb)))
        dq_ref[...] = (dq_ref[...].astype(F32) + dq2_ref[...]).astype(BF16)
        di_ref[...] = (di_ref[...].astype(F32) + di2_ref[...]).astype(BF16)

        _, vjp_tf = jax.vjp(_lower_bound, tab_f)
        _, vjp_tb = jax.vjp(_lower_bound, tab_b)

        @pl.when(b == 0)
        def _():
            dlbf_ref[...] = jnp.zeros(dlbf_ref.shape, F32)
            dlbb_ref[...] = jnp.zeros(dlbb_ref.shape, F32)
            dg_ref[...] = jnp.zeros(dg_ref.shape, F32)

        dlbf_ref[...] += vjp_tf(jnp.concatenate(dls[0::2], axis=1))[0]
        dlbb_ref[...] += vjp_tb(jnp.concatenate(dls[1::2], axis=1))[0]
        dg_ref[...] += jnp.concatenate(dg_parts, axis=1)

    def col(k):
        return pl.BlockSpec((None, s_, wid), lambda h, b: (b, 0, k * (nh // hb) + h))

    tab = pl.BlockSpec((2, wid), lambda h, b: (0, h))
    vec = pl.BlockSpec((1, wid), lambda h, b: (0, h))
    oblk = pl.BlockSpec((None, s_, wid), lambda h, b: (b, 0, h))
    seq = _sds((b_, s_, d), BF16)
    return _run(
        body, "hgrn_bwd", (nh // hb, b_), [proj3, proj3, proj3, proj3, proj3, lb_f, lb_b, norm_g, o_raw, do_a],
        [col(0), col(1), col(2), col(3), col(4), tab, tab, vec, oblk, oblk],
        [seq, seq, seq, seq, seq, _sds((2, d), F32), _sds((2, d), F32), _sds((1, d), F32)],
        [oblk, oblk, oblk, oblk, oblk, tab, tab, vec],
        [pltpu.VMEM((2 * hb, n_blk, HEAD, HEAD), F32),
         pltpu.VMEM((s_, wid), F32), pltpu.VMEM((s_, wid), F32), pltpu.VMEM((s_, wid), F32)], 1, side)


def _rope_tables(s_):
    inv = ROPE_THETA ** (-jnp.arange(0, ROPE_DIM, 2, dtype=F32) / ROPE_DIM)
    ang = jnp.arange(s_, dtype=F32)[:, None] * inv
    cos, sin = jnp.cos(ang), jnp.sin(ang)
    rest = HEAD - ROPE_DIM
    ctab = jnp.concatenate([cos, cos, jnp.ones((s_, rest), F32)], axis=1)
    stab = jnp.concatenate([-sin, sin, jnp.zeros((s_, rest), F32)], axis=1)
    half = ROPE_DIM // 2
    perm = np.zeros((HEAD, HEAD), np.float32)
    for i in range(half):
        perm[i + half, i] = 1.0
        perm[i, i + half] = 1.0
    return ctab, stab, jnp.asarray(perm)


def _attn_tile(qr, kr, v, cq, sq, ck, sk, perm, qi0, kj0, half):
    hi = lax.Precision.HIGHEST
    q = qr * cq + _dot(qr, perm, NN, precision=hi) * sq
    k = kr * ck + _dot(kr, perm, NN, precision=hi) * sk
    s = _dot(q.astype(BF16), k.astype(BF16), NT) * (HEAD ** -0.5)
    qi = qi0 + lax.broadcasted_iota(jnp.int32, s.shape, 0)
    kj = kj0 + lax.broadcasted_iota(jnp.int32, s.shape, 1)
    s = jnp.where(jnp.abs(qi - kj) <= half, s, NEG_INF)
    m = lax.stop_gradient(jnp.max(s, axis=-1, keepdims=True))
    p = jnp.exp(s - m)
    den = jnp.sum(p, axis=-1, keepdims=True)
    o = _dot(p.astype(BF16), v.astype(BF16), NN) / den
    lse = m + jnp.log(den)
    return o, jnp.broadcast_to(lse, o.shape)


def _attn_tiling(seg):
    tq = seg if seg <= 256 else 256
    kw = seg if seg <= 512 else 512
    tiles = []
    for i in range(seg // tq):
        ws = min(max(i * tq - (kw - tq) // 2, 0), seg - kw)
        tiles.append((i * tq, ws))
    return tq, kw, tiles


def _attn_specs(b_, dil, seg):
    qkv = pl.BlockSpec((None, None, 3, None, seg, HEAD), lambda b, r, h: (b, r, 0, h, 0, 0))
    tab = pl.BlockSpec((None, seg, HEAD), lambda b, r, h: (r, 0, 0))
    perm = pl.BlockSpec((HEAD, HEAD), lambda b, r, h: (0, 0))
    oblk = pl.BlockSpec((None, None, None, seg, HEAD), lambda b, r, h: (b, r, h, 0, 0))
    return qkv, tab, perm, oblk


def _attn_fwd(qkv_g, ctab, stab, perm, window, dil):
    b_, _, _, nh, seg, _ = qkv_g.shape
    half = window // (2 * dil)
    tq, kw, tiles = _attn_tiling(seg)

    def body(qkv_ref, c_ref, s_ref, p_ref, o_ref, l_ref):
        pm = p_ref[...]
        for q0, k0 in tiles:
            rq, rk = pl.ds(q0, tq), pl.ds(k0, kw)
            o, l = _attn_tile(qkv_ref[0, rq, :], qkv_ref[1, rk, :], qkv_ref[2, rk, :], c_ref[rq, :], s_ref[rq, :],
                              c_ref[rk, :], s_ref[rk, :], pm, q0, k0, half)
            o_ref[rq, :] = o
            l_ref[rq, :] = l

    qkv, tab, pspec, oblk = _attn_specs(b_, dil, seg)
    shp = _sds((b_, dil, nh, seg, HEAD), F32)
    return pl.pallas_call(
        body, name=f"attn_fwd_d{dil}", grid=(b_, dil, nh), in_specs=[qkv, tab, tab, pspec],
        out_specs=[oblk, oblk], out_shape=[shp, shp], compiler_params=_params(3, 0))(qkv_g, ctab, stab, perm)


def _attn_bwd(qkv_g, ctab, stab, perm, do, dl, window, dil):
    b_, _, _, nh, seg, _ = qkv_g.shape
    half = window // (2 * dil)
    tq, kw, tiles = _attn_tiling(seg)

    def body(qkv_ref, c_ref, s_ref, p_ref, do_ref, dl_ref, d_ref):
        pm = p_ref[...]
        d_ref[...] = jnp.zeros(d_ref.shape, F32)
        for q0, k0 in tiles:
            rq, rk = pl.ds(q0, tq), pl.ds(k0, kw)
            cq, sq, ck, sk = c_ref[rq, :], s_ref[rq, :], c_ref[rk, :], s_ref[rk, :]
            _, vjp = jax.vjp(lambda a, b, c: _attn_tile(a, b, c, cq, sq, ck, sk, pm, q0, k0, half),
                             qkv_ref[0, rq, :], qkv_ref[1, rk, :], qkv_ref[2, rk, :])
            dq, dk, dv = vjp((do_ref[rq, :], dl_ref[rq, :]))
            d_ref[0, rq, :] = dq
            d_ref[1, rk, :] += dk
            d_ref[2, rk, :] += dv

    qkv, tab, pspec, oblk = _attn_specs(b_, dil, seg)
    return pl.pallas_call(
        body, name=f"attn_bwd_d{dil}", grid=(b_, dil, nh), in_specs=[qkv, tab, tab, pspec, oblk, oblk],
        out_specs=qkv, out_shape=_sds(qkv_g.shape, F32), compiler_params=_params(3, 0))(
            qkv_g, ctab, stab, perm, do, dl)


def _combine(os_, ls_):
    m = jnp.maximum(jnp.maximum(ls_[0], ls_[1]), ls_[2])
    es = [jnp.exp(l - m) for l in ls_]
    return (es[0] * os_[0] + es[1] * os_[1] + es[2] * os_[2]) / (es[0] + es[1] + es[2])


def _combine_fwd(os_, ls_):
    t, w = os_[0].shape
    tm = _tile(t, 512, 8)

    def body(*refs):
        refs[6][...] = _combine([r[...] for r in refs[:3]], [r[...] for r in refs[3:6]]).astype(BF16)

    row = pl.BlockSpec((tm, w), lambda i: (i, 0))
    return pl.pallas_call(body, name="combine_fwd", grid=(t // tm,), in_specs=[row] * 6, out_specs=row,
                          out_shape=_sds((t, w), BF16), compiler_params=_params(1, 0))(*os_, *ls_)


def _combine_bwd(os_, ls_, dob):
    t, w = os_[0].shape
    tm = _tile(t, 512, 8)

    def body(*refs):
        _, vjp = jax.vjp(lambda *a: _combine(a[:3], a[3:]), *[r[...] for r in refs[:6]])
        for r, g in zip(refs[7:], vjp(refs[6][...])):
            r[...] = g

    row = pl.BlockSpec((tm, w), lambda i: (i, 0))
    return pl.pallas_call(body, name="combine_bwd", grid=(t // tm,), in_specs=[row] * 7, out_specs=[row] * 6,
                          out_shape=[_sds((t, w), F32)] * 6, compiler_params=_params(1, 0))(*os_, *ls_, dob)


def _to_residues(cols, b_, s_, dil):
    seg = s_ // dil
    return cols.reshape(b_, seg, dil, 3, ATTN_HEADS, HEAD).transpose(0, 2, 3, 4, 1, 5)


def _from_residues(a, b_, s_):
    if a.ndim == 6:
        return a.transpose(0, 4, 1, 2, 3, 5).reshape(b_ * s_, 3 * ATTN_HEADS * HEAD)
    return a.transpose(0, 3, 1, 2, 4).reshape(b_ * s_, ATTN_HEADS * HEAD)


ROW_SHARDED = ("ffn1_out", "ffn2_out", "wa", "mix_out")


def _local_step(x, target, fulls, p, sc_arr):
    b_, s_, d = x.shape
    t = b_ * s_
    x2 = x.reshape(t, d)
    gate_col = 5 * d + QKV_W
    c_arr = sc_arr[1:2]
    w = {}

    def arrived(keys, arrays):
        for k, a in zip(keys, arrays):
            w[k] = a.reshape(a.shape[0] * a.shape[1], a.shape[2]) if k in ROW_SHARDED else a

    def reduce_begin(keys):
        g3 = [gw[k].reshape(fulls[k].shape) for k in keys]
        return [_sum_halves(a, b, c_arr) for a, b in zip(g3, _swap_halves(g3))]

    def reduce_end(keys, parts, got):
        for k, a, b in zip(keys, parts, got):
            reduced[k] = _sum_chips(a, b, sc_arr)

    arrived(["ffn1_in"], _gather_weights([fulls["ffn1_in"]]))
    xb = _cast_bf16(x2)
    mix_rows = fulls["mix_in"].shape[1]
    (u1, a1), got = _ffn_up(xb, w["ffn1_in"], _gather_exchange([fulls["ffn1_out"], fulls["mix_in"]],
                                                              [None, (0, mix_rows // 2)]))
    arrived(["ffn1_out"], got[:1])
    (h1, h1b, xh1, rs1), got = _down_ln(a1, w["ffn1_out"], x2, 0.5, p["ln1_g"], p["ln1_b"],
                                        _gather_exchange(got[1:], [(mix_rows // 2, mix_rows // 2)]))
    arrived(["mix_in"], got)

    keys = ["wa", "wb", "mix_out", "ffn2_in", "ffn2_out"]
    proj, got = _mm_w3(h1b, w["mix_in"], _gather_exchange([fulls[k] for k in keys]))
    arrived(keys, got)
    n_in = proj.shape[1]
    proj3 = proj.reshape(b_, s_, n_in)
    o_raw, oa = _hgrn_fwd(proj3, p["hgrn_lb_fwd"], p["hgrn_lb_bwd"], p["hgrn_norm_g"], d)
    oa2 = oa.reshape(t, d)

    ctab, stab, perm = _rope_tables(s_)
    qkv_gs, tabs, os_, ls_ = [], [], [], []
    for g, (window, dil) in enumerate(ATTN_GROUPS):
        c0 = 5 * d + g * 3 * ATTN_OUT
        qkv_g = _to_residues(proj[:, c0:c0 + 3 * ATTN_OUT], b_, s_, dil)
        seg = s_ // dil
        ct = ctab.reshape(seg, dil, HEAD).transpose(1, 0, 2)
        st = stab.reshape(seg, dil, HEAD).transpose(1, 0, 2)
        o_g, l_g = _attn_fwd(qkv_g, ct, st, perm, window, dil)
        qkv_gs.append(qkv_g)
        tabs.append((ct, st))
        os_.append(_from_residues(o_g, b_, s_))
        ls_.append(_from_residues(l_g, b_, s_))
    ob = _combine_fwd(os_, ls_)

    ya = _mm_w2(oa2, w["wa"], F32)
    yb, zb = _branch_gate(ob, w["wb"], ya, proj, gate_col)
    h2, h2b, xh2, rs2 = _down_ln(zb, w["mix_out"], h1, 1.0, p["ln2_g"], p["ln2_b"])

    u2, a2 = _ffn_up(h2b, w["ffn2_in"])
    h3, _, xh3, rs3 = _down_ln(a2, w["ffn2_out"], h2, 0.5, p["ln3_g"], p["ln3_b"])

    gw, gp, reduced = {}, {}, {}
    nf = w["ffn2_in"].shape[2]

    def du_map(j, m):
        return (j // 2, m, j % 2)

    dres3, dy3, gp["ln3_g"], gp["ln3_b"], loss = _ln_bwd([h3], xh3, rs3, p["ln3_g"], 0.5, target.reshape(t, d))
    du2 = _swiglu_bwd(dy3, w["ffn2_out"], u2)
    gw["ffn2_out"] = _tn_w2(a2, dy3)
    gw["ffn2_in"] = _tn_w3(h2b, du2, du_map, nf)
    parts_a = reduce_begin(["ffn2_out", "ffn2_in"])
    dh2, got = _ffn_dx(du2, w["ffn2_in"], dres3, _scatter_exchange(parts_a[:1]))
    reduce_end(["ffn2_out"], parts_a[:1], got)

    dres2, dmix, gp["ln2_g"], gp["ln2_b"], _ = _ln_bwd([dh2], xh2, rs2, p["ln2_g"], 1.0)
    dya, dyb, dpga, dpgb = _gate_bwd(dmix, w["mix_out"], proj, ya, yb, gate_col)
    gw["mix_out"] = _tn_w2(zb, dmix)
    do_a = _nt_w2(dya, w["wa"], F32)
    gw["wa"] = _tn_w2(oa2, dya)
    nb = w["wb"].shape[2]
    do_b = _nt_w3(dyb, w["wb"], None)
    gw["wb"] = _tn_w3(ob, dyb, lambda j, m: (m, j), nb)
    keys = ["ffn2_in", "mix_out", "wa", "wb"]
    parts_b = parts_a[1:] + reduce_begin(keys[1:])

    (dq, dff, dfb, di, dog, gp["hgrn_lb_fwd"], gp["hgrn_lb_bwd"], gp["hgrn_norm_g"]), got = _hgrn_bwd(
        proj3, p["hgrn_lb_fwd"], p["hgrn_lb_bwd"], p["hgrn_norm_g"], o_raw, do_a.reshape(b_, s_, d), d,
        _scatter_exchange(parts_b))
    reduce_end(keys, parts_b, got)

    douts = _combine_bwd(os_, ls_, do_b)
    dqkv = []
    for g, (window, dil) in enumerate(ATTN_GROUPS):
        seg = s_ // dil
        do_g = douts[g].reshape(b_, seg, dil, ATTN_HEADS, HEAD).transpose(0, 2, 3, 1, 4)
        dl_g = douts[3 + g].reshape(b_, seg, dil, ATTN_HEADS, HEAD).transpose(0, 2, 3, 1, 4)
        dqkv_g = _attn_bwd(qkv_gs[g], tabs[g][0], tabs[g][1], perm, do_g, dl_g, window, dil)
        dqkv.append(_from_residues(dqkv_g, b_, s_).astype(BF16))

    dproj = jnp.concatenate(
        [a.reshape(t, d) for a in (dq, dff, dfb, di, dog)] + dqkv + [dpga, dpgb], axis=1)
    nm = w["mix_in"].shape[2]
    gw["mix_in"] = _tn_w3(h1b, dproj, lambda j, m: (m, j), nm)
    parts_c = reduce_begin(["mix_in"])
    dh1, got = _nt_w3(dproj, w["mix_in"], dres2, _scatter_exchange(parts_c))
    reduce_end(["mix_in"], parts_c, got)

    dres1, dy1, gp["ln1_g"], gp["ln1_b"], _ = _ln_bwd([dh1], xh1, rs1, p["ln1_g"], 0.5)
    du1 = _swiglu_bwd(dy1, w["ffn1_out"], u1)
    gw["ffn1_out"] = _tn_w2(a1, dy1)
    parts_d = reduce_begin(["ffn1_out"])
    gw["ffn1_in"], got = _tn_w3(xb, du1, du_map, nf, _scatter_exchange(parts_d))
    reduce_end(["ffn1_out"], parts_d, got)
    parts_e = reduce_begin(["ffn1_in"])
    dx, got = _ffn_dx(du1, w["ffn1_in"], dres1, _scatter_exchange(parts_e))
    reduce_end(["ffn1_in"], parts_e, got)

    keys = list(reduced)
    grads = dict(zip(keys, _join_halves([reduced[k] for k in keys])))
    return loss, dx.reshape(b_, s_, d), grads, gp


MESH = pl.DeviceIdType.MESH
ANY = pl.BlockSpec(memory_space=pl.ANY)


def _place():
    x, y, c = lax.axis_index("x"), lax.axis_index("y"), lax.axis_index("c")
    chips = [(1 - x, y), (x, 1 - y), (1 - x, 1 - y)]
    return x, y, c, chips, (x, y, 1 - c)


def _half_rows(c, rows):
    hr = rows // 2
    return pl.ds(pl.multiple_of(c * hr, 16), hr)


def _remote(src, dst, send, recv, dev):
    return pltpu.make_async_remote_copy(src_ref=src, dst_ref=dst, send_sem=send, recv_sem=recv,
                                        device_id=dev, device_id_type=MESH)


def _gather_weights(fulls):
    n = len(fulls)

    def body(*refs):
        _gather_start(refs[n:2 * n], refs[2 * n:])
        _gather_finish(refs[n:2 * n], refs[2 * n:])

    return pl.pallas_call(
        body, name="gather_weights", in_specs=[ANY] * n, out_specs=[ANY] * n,
        out_shape=[_sds(a.shape, a.dtype) for a in fulls], input_output_aliases={i: i for i in range(n)},
        scratch_shapes=_gather_sems(n))(*fulls)


def _gather_sems(n):
    return [pltpu.SemaphoreType.DMA((n, 3)) for _ in range(4)]


def _span_half(c, span, rows):
    r0, cnt = (0, rows) if span is None else span
    return pl.ds(pl.multiple_of(r0 + c * (cnt // 2), 16), cnt // 2)


def _gather_start(bufs, sems, spans=None):
    isend, irecv = sems[0], sems[1]
    x, y, c, chips, sib = _place()
    for i, buf in enumerate(bufs):
        blk = buf.at[2 * x + y, _span_half(c, spans and spans[i], buf.shape[1])]
        for k, chip in enumerate(chips):
            _remote(blk, blk, isend.at[i, k], irecv.at[i, k], (*chip, c)).start()


def _gather_finish(bufs, sems, spans=None):
    isend, irecv, fsend, frecv = sems
    x, y, c, chips, sib = _place()
    for i, buf in enumerate(bufs):
        mine = _span_half(c, spans and spans[i], buf.shape[1])
        for k, chip in enumerate(chips):
            blk = buf.at[2 * chip[0] + chip[1], mine]
            _remote(blk, blk, isend.at[i, k], irecv.at[i, k], (*chip, c)).wait_recv()
            _remote(blk, blk, fsend.at[i, k], frecv.at[i, k], sib).start()
    for i, buf in enumerate(bufs):
        span = spans and spans[i]
        mine, other = _span_half(c, span, buf.shape[1]), _span_half(1 - c, span, buf.shape[1])
        own = buf.at[2 * x + y, mine]
        for k, chip in enumerate(chips):
            got = buf.at[2 * chip[0] + chip[1], other]
            _remote(got, got, fsend.at[i, k], frecv.at[i, k], sib).wait_recv()
            _remote(own, own, isend.at[i, k], irecv.at[i, k], (*chip, c)).wait_send()
            blk = buf.at[2 * chip[0] + chip[1], mine]
            _remote(blk, blk, fsend.at[i, k], frecv.at[i, k], sib).wait_send()


def _gather_exchange(fulls, spans=None):
    n = len(fulls)
    return _Exchange(fulls, [_sds(a.shape, a.dtype) for a in fulls], {i: i for i in range(n)}, _gather_sems(n),
                     lambda ins, outs, sems: _gather_start(outs, sems, spans),
                     lambda ins, outs, sems: _gather_finish(outs, sems, spans))


def _swap_halves(grads):
    n = len(grads)

    def body(*refs):
        ins, outs, send, recv = refs[:n], refs[n:2 * n], refs[2 * n], refs[2 * n + 1]
        x, y, c, chips, sib = _place()
        cps = []
        for i in range(n):
            other = _half_rows(1 - c, ins[i].shape[1])
            cp = _remote(ins[i].at[:, other], outs[i], send.at[i], recv.at[i], sib)
            cp.start()
            cps.append(cp)
        for cp in cps:
            cp.wait()

    dma = pltpu.SemaphoreType.DMA
    return pl.pallas_call(
        body, name="swap_halves", in_specs=[ANY] * n, out_specs=[ANY] * n,
        out_shape=[_sds((N_SHARD, a.shape[1] // 2, a.shape[2]), a.dtype) for a in grads],
        scratch_shapes=[dma((n,)), dma((n,))])(*grads)


def _scatter_partials(parts):
    side = _scatter_exchange(parts)
    n = len(parts)

    def body(*refs):
        side.start(refs[:n], refs[n:2 * n], refs[2 * n:])
        side.finish(refs[:n], refs[n:2 * n], refs[2 * n:])

    return pl.pallas_call(
        body, name="scatter_partials", in_specs=[ANY] * n, out_specs=[ANY] * n, out_shape=side.outs,
        scratch_shapes=side.sems)(*parts)


def _scatter_copies(ins, outs, sems):
    x, y, c, chips, sib = _place()
    return [_remote(a.at[2 * chip[0] + chip[1]], b.at[k], sems[0].at[i, k], sems[1].at[i, k], (*chip, c))
            for i, (a, b) in enumerate(zip(ins, outs)) for k, chip in enumerate(chips)]


def _scatter_exchange(parts):
    n = len(parts)

    def start(ins, outs, sems):
        for cp in _scatter_copies(ins, outs, sems):
            cp.start()

    def finish(ins, outs, sems):
        for cp in _scatter_copies(ins, outs, sems):
            cp.wait()

    return _Exchange(parts, [_sds((3,) + a.shape[1:], a.dtype) for a in parts], {},
                     [pltpu.SemaphoreType.DMA((n, 3)) for _ in range(2)], start, finish)


def _join_halves(grads):
    n = len(grads)

    def body(*refs):
        bufs, send, recv = refs[n:2 * n], refs[2 * n], refs[2 * n + 1]
        x, y, c, chips, sib = _place()
        cps = []
        for i in range(n):
            blk = bufs[i].at[_half_rows(c, bufs[i].shape[0])]
            other = bufs[i].at[_half_rows(1 - c, bufs[i].shape[0])]
            cp = _remote(blk, blk, send.at[i], recv.at[i], sib)
            cp.start()
            cps.append((cp, _remote(other, other, send.at[i], recv.at[i], sib)))
        for cp, got in cps:
            cp.wait_send()
            got.wait_recv()

    dma = pltpu.SemaphoreType.DMA
    return pl.pallas_call(
        body, name="join_halves", in_specs=[ANY] * n, out_specs=[ANY] * n,
        out_shape=[_sds(a.shape, a.dtype) for a in grads], input_output_aliases={i: i for i in range(n)},
        scratch_shapes=[dma((n,)), dma((n,))])(*grads)


def _gather_rows(block):
    m_per, n = block.shape

    def body(x_ref, out_ref, send_sems, recv_sems, local_sem):
        x, y, c, chips, sibling = _place()
        me = (x, y, c)

        def rows(px, py, pc):
            return out_ref.at[pl.ds((4 * px + 2 * py + pc) * m_per, m_per), :]

        def copy(k, blk, to, src=None):
            return _remote(rows(*blk) if src is None else src, rows(*blk), send_sems.at[k], recv_sems.at[k], to)

        mine = pltpu.make_async_copy(x_ref, rows(*me), local_sem)
        mine.start()
        first = [copy(0, me, sibling, src=x_ref)]
        first += [copy(1 + j, me, (*chip, c), src=x_ref) for j, chip in enumerate(chips)]
        for cp in first:
            cp.start()
        passed = [copy(4 + j, (*chip, c), sibling) for j, chip in enumerate(chips)]
        for j, chip in enumerate(chips):
            copy(1 + j, (*chip, c), me).wait_recv()
            passed[j].start()
        copy(0, sibling, me).wait_recv()
        for j, chip in enumerate(chips):
            copy(4 + j, (*chip, 1 - c), me).wait_recv()
        for cp in first + passed:
            cp.wait_send()
        mine.wait()

    vmem = pl.BlockSpec(memory_space=pltpu.VMEM)
    dma = pltpu.SemaphoreType.DMA
    return pl.pallas_call(
        body, name="gather_rows", in_specs=[vmem], out_specs=vmem, out_shape=_sds((8 * m_per, n), block.dtype),
        scratch_shapes=[dma((7,)), dma((7,)), dma(())])(block)


def _row_tile(rows, cols):
    return _tile(rows, max(16, (1 << 20) // cols), 16)


def _sum_halves(grad, got, c_arr):
    _, hr, cols = got.shape
    tr = _row_tile(hr, cols)
    nb = hr // tr

    def body(c_ref, a_ref, b_ref, o_ref):
        o_ref[...] = (a_ref[...].astype(F32) + b_ref[...].astype(F32)).astype(BF16)

    blk = pl.BlockSpec((None, tr, cols), lambda s, i, c_ref: (s, i, 0))
    return pl.pallas_call(
        body, name="sum_halves",
        grid_spec=pltpu.PrefetchScalarGridSpec(
            num_scalar_prefetch=1, grid=(N_SHARD, nb),
            in_specs=[pl.BlockSpec((None, tr, cols), lambda s, i, c_ref: (s, c_ref[0] * nb + i, 0)), blk],
            out_specs=blk),
        out_shape=_sds(got.shape, BF16), compiler_params=_params(2, 0))(c_arr, grad, got)


def _sum_chips(part, got, sc_arr):
    _, hr, cols = got.shape
    tr = _row_tile(hr, cols)
    nb = hr // tr

    def body(s_ref, a_ref, b_ref, o_ref):
        o_ref[...] = ((a_ref[...].astype(F32) + b_ref[0].astype(F32)) + b_ref[1].astype(F32)) + b_ref[2].astype(F32)

    return pl.pallas_call(
        body, name="sum_chips",
        grid_spec=pltpu.PrefetchScalarGridSpec(
            num_scalar_prefetch=1, grid=(nb,),
            in_specs=[pl.BlockSpec((None, tr, cols), lambda i, s_ref: (s_ref[0], i, 0)),
                      pl.BlockSpec((3, tr, cols), lambda i, s_ref: (0, i, 0))],
            out_specs=pl.BlockSpec((tr, cols), lambda i, s_ref: (s_ref[1] * nb + i, 0))),
        out_shape=_sds((2 * hr, cols), F32), compiler_params=_params(1, 0))(sc_arr, part, got)


def _cast_into_slot(x2d, sc_arr):
    rows, cols = x2d.shape
    tr = _row_tile(rows, cols)

    def body(s_ref, x_ref, o_ref):
        o_ref[...] = x_ref[...].astype(BF16)

    return pl.pallas_call(
        body, name="cast_into_slot",
        grid_spec=pltpu.PrefetchScalarGridSpec(
            num_scalar_prefetch=1, grid=(rows // tr,),
            in_specs=[pl.BlockSpec((tr, cols), lambda i, s_ref: (i, 0))],
            out_specs=pl.BlockSpec((None, tr, cols), lambda i, s_ref: (s_ref[0], i, 0))),
        out_shape=_sds((N_SHARD, rows, cols), BF16), compiler_params=_params(1, 0))(sc_arr, x2d)


def _adam_math(w, g, m, v):
    m = ADAM_B1 * m + (1.0 - ADAM_B1) * g
    v = ADAM_B2 * v + (1.0 - ADAM_B2) * (g * g)
    m_hat = m / (1.0 - ADAM_B1 ** ADAM_STEP)
    v_hat = v / (1.0 - ADAM_B2 ** ADAM_STEP)
    delta = -ADAM_LR * (m_hat / (jnp.sqrt(v_hat) + ADAM_EPS) + ADAM_WD * w)
    return delta, m, v


def _adamw(w, g, m, v):
    rows, cols = w.shape
    tr = _tile(rows, max(8, (1 << 19) // cols), 8)

    def body(w_ref, g_ref, m_ref, v_ref, go_ref, d_ref, mo_ref, vo_ref):
        g_ = g_ref[...]
        go_ref[...] = g_
        d_ref[...], mo_ref[...], vo_ref[...] = _adam_math(w_ref[...], g_, m_ref[...], v_ref[...])

    blk = pl.BlockSpec((tr, cols), lambda i: (i, 0))
    return pl.pallas_call(
        body, name="adamw", grid=(rows // tr,), in_specs=[blk] * 4, out_specs=[blk] * 4,
        out_shape=[_sds((rows, cols), F32)] * 4, compiler_params=_params(1, 0))(w, g, m, v)


def _adamw_small(gathered, w, m, v):
    rows, cols = w.shape

    def body(a_ref, w_ref, m_ref, v_ref, go_ref, d_ref, mo_ref, vo_ref):
        g_ = a_ref[pl.ds(0, rows), :]
        for k in range(1, 8):
            g_ = g_ + a_ref[pl.ds(k * rows, rows), :]
        go_ref[...] = g_
        d_ref[...], mo_ref[...], vo_ref[...] = _adam_math(w_ref[...], g_, m_ref[...], v_ref[...])

    vmem = pl.BlockSpec(memory_space=pltpu.VMEM)
    return pl.pallas_call(
        body, name="adamw_small", in_specs=[vmem] * 4, out_specs=[vmem] * 4,
        out_shape=[_sds((rows, cols), F32)] * 4)(gathered, w, m, v)


BIG = ("ffn1_w_in", "ffn1_w_out", "mix_w_in", "w_branch_a", "w_branch_b", "mix_w_out", "ffn2_w_in", "ffn2_w_out")
BIG_KEY = {"ffn1_w_in": "ffn1_in", "ffn1_w_out": "ffn1_out", "mix_w_in": "mix_in", "w_branch_a": "wa",
           "w_branch_b": "wb", "mix_w_out": "mix_out", "ffn2_w_in": "ffn2_in", "ffn2_w_out": "ffn2_out"}
SMALL = ("ln1_g", "ln1_b", "hgrn_lb_fwd", "hgrn_lb_bwd", "hgrn_norm_g", "ln2_g", "ln2_b", "ln3_g", "ln3_b")
ORDER = ("ffn1_w_in", "ffn1_w_out", "ln1_g", "ln1_b", "mix_w_in", "hgrn_lb_fwd", "hgrn_lb_bwd", "hgrn_norm_g",
         "w_branch_a", "w_branch_b", "mix_w_out", "ln2_g", "ln2_b", "ffn2_w_in", "ffn2_w_out", "ln3_g", "ln3_b")
SMALL_ROWS = 16


def _pack_small(d):
    rows = jnp.concatenate([d[k].reshape(-1, d[k].shape[-1]) for k in SMALL], axis=0)
    return jnp.pad(rows, ((0, SMALL_ROWS - rows.shape[0]), (0, 0)))


def _unpack_small(a, like):
    out, r = {}, 0
    for k in SMALL:
        n = like[k].shape[0]
        out[k] = a[r:r + n].reshape(like[k].shape)
        r += n
    return out


def kernel(x, ffn1_w_in, ffn1_w_out, ln1_g, ln1_b, mix_w_in, hgrn_lb_fwd, hgrn_lb_bwd, hgrn_norm_g, w_branch_a, w_branch_b, mix_w_out, ln2_g, ln2_b, ffn2_w_in, ffn2_w_out, ln3_g, ln3_b, loss_target, m_ffn1_w_in, m_ffn1_w_out, m_ln1_g, m_ln1_b, m_mix_w_in, m_hgrn_lb_fwd, m_hgrn_lb_bwd, m_hgrn_norm_g, m_w_branch_a, m_w_branch_b, m_mix_w_out, m_ln2_g, m_ln2_b, m_ffn2_w_in, m_ffn2_w_out, m_ln3_g, m_ln3_b, v_ffn1_w_in, v_ffn1_w_out, v_ln1_g, v_ln1_b, v_mix_w_in, v_hgrn_lb_fwd, v_hgrn_lb_bwd, v_hgrn_norm_g, v_w_branch_a, v_w_branch_b, v_mix_w_out, v_ln2_g, v_ln2_b, v_ffn2_w_in, v_ffn2_w_out, v_ln3_g, v_ln3_b):
    wts = dict(ffn1_w_in=ffn1_w_in, ffn1_w_out=ffn1_w_out, ln1_g=ln1_g, ln1_b=ln1_b, mix_w_in=mix_w_in,
               hgrn_lb_fwd=hgrn_lb_fwd, hgrn_lb_bwd=hgrn_lb_bwd, hgrn_norm_g=hgrn_norm_g, w_branch_a=w_branch_a,
               w_branch_b=w_branch_b, mix_w_out=mix_w_out, ln2_g=ln2_g, ln2_b=ln2_b, ffn2_w_in=ffn2_w_in,
               ffn2_w_out=ffn2_w_out, ln3_g=ln3_g, ln3_b=ln3_b)
    mom = dict(ffn1_w_in=m_ffn1_w_in, ffn1_w_out=m_ffn1_w_out, ln1_g=m_ln1_g, ln1_b=m_ln1_b, mix_w_in=m_mix_w_in,
               hgrn_lb_fwd=m_hgrn_lb_fwd, hgrn_lb_bwd=m_hgrn_lb_bwd, hgrn_norm_g=m_hgrn_norm_g,
               w_branch_a=m_w_branch_a, w_branch_b=m_w_branch_b, mix_w_out=m_mix_w_out, ln2_g=m_ln2_g, ln2_b=m_ln2_b,
               ffn2_w_in=m_ffn2_w_in, ffn2_w_out=m_ffn2_w_out, ln3_g=m_ln3_g, ln3_b=m_ln3_b)
    var = dict(ffn1_w_in=v_ffn1_w_in, ffn1_w_out=v_ffn1_w_out, ln1_g=v_ln1_g, ln1_b=v_ln1_b, mix_w_in=v_mix_w_in,
               hgrn_lb_fwd=v_hgrn_lb_fwd, hgrn_lb_bwd=v_hgrn_lb_bwd, hgrn_norm_g=v_hgrn_norm_g,
               w_branch_a=v_w_branch_a, w_branch_b=v_w_branch_b, mix_w_out=v_mix_w_out, ln2_g=v_ln2_g, ln2_b=v_ln2_b,
               ffn2_w_in=v_ffn2_w_in, ffn2_w_out=v_ffn2_w_out, ln3_g=v_ln3_g, ln3_b=v_ln3_b)
    shard = (2 * lax.axis_index("x") + lax.axis_index("y")).astype(jnp.int32)
    sc_arr = jnp.stack([shard, lax.axis_index("c").astype(jnp.int32)])

    shard2d = {k: wts[k].reshape(wts[k].shape[1:]) for k in BIG}
    fulls = {BIG_KEY[k]: _cast_into_slot(shard2d[k], sc_arr) for k in BIG}
    p = {k: wts[k] for k in SMALL}

    loss, grad_x, grads, gp = _local_step(x, loss_target, fulls, p, sc_arr)
    loss = lax.psum(loss[0, 0], ("x", "y", "c"))

    out_g, out_d, out_m, out_v = {}, {}, {}, {}
    for k in BIG:
        g = grads[BIG_KEY[k]]
        shp = wts[k].shape
        res = _adamw(shard2d[k], g, mom[k].reshape(shp[1:]), var[k].reshape(shp[1:]))
        out_g[k], out_d[k], out_m[k], out_v[k] = [a.reshape(shp) for a in res]

    gathered = _gather_rows(_pack_small(gp))
    res = _adamw_small(gathered, _pack_small(wts), _pack_small(mom), _pack_small(var))
    for dst, a in zip((out_g, out_d, out_m, out_v), res):
        dst.update(_unpack_small(a, wts))

    return (loss, grad_x, *[out_g[k] for k in ORDER], *[out_d[k] for k in ORDER],
            *[out_m[k] for k in ORDER], *[out_v[k] for k in ORDER])
```

```python
import functools

import numpy as np
import jax
import jax.numpy as jnp
from jax import lax
from jax.experimental import pallas as pl
from jax.experimental.pallas import tpu as pltpu

F32 = jnp.float32
BF16 = jnp.bfloat16

HEAD = 128
ATTN_GROUPS = ((128, 1), (512, 4), (2048, 16))
ATTN_HEADS = 4
N_GROUPS = len(ATTN_GROUPS)
QKV_W = N_GROUPS * 3 * ATTN_HEADS * HEAD
ATTN_OUT = ATTN_HEADS * HEAD
ROPE_THETA = 500000.0
ROPE_DIM = HEAD // 4
HGRN_CHUNK = 32
HGRN_FWD_HEADS = 2
HGRN_BWD_HEADS = 2
HGRN_SUB = 4
ALPHA = 2.0 ** 0.25
LN_EPS = 1e-5
NEG_INF = -1e30
ADAM_LR, ADAM_B1, ADAM_B2, ADAM_EPS, ADAM_WD, ADAM_STEP = 0.001, 0.9, 0.999, 1e-08, 0.01, 10

N_SHARD = 4
VMEM_LIMIT = 56 * 1024 * 1024

NN = ((1,), (0,))
NT = ((1,), (1,))
TN = ((0,), (0,))


def _dot(a, b, dims, precision=None):
    return lax.dot_general(a, b, (dims, ((), ())), preferred_element_type=F32, precision=precision)


def _tile(n, pref, mult=128):
    best = None
    for t in range(mult, min(n, pref) + 1, mult):
        if n % t == 0:
            best = t
    return n if best is None else best


def _params(n_parallel, n_arbitrary):
    return pltpu.CompilerParams(
        dimension_semantics=("parallel",) * n_parallel + ("arbitrary",) * n_arbitrary,
        vmem_limit_bytes=VMEM_LIMIT)


def _sigmoid(x):
    return 1.0 / (1.0 + jnp.exp(-x))


def _silu(x):
    return x * _sigmoid(x)


class _Exchange:
    def __init__(self, ins, outs, aliases, sems, start, finish):
        self.ins, self.outs, self.aliases, self.sems = list(ins), list(outs), dict(aliases), list(sems)
        self.start, self.finish = start, finish


def _run(body, name, grid, ins, in_specs, outs, out_specs, scratch, n_arbitrary, side=None):
    n_in, n_out, n_scr = len(ins), len(outs), len(scratch)
    if side is None:
        return pl.pallas_call(
            body, name=name, grid=grid, in_specs=in_specs, out_specs=out_specs, out_shape=outs,
            scratch_shapes=scratch, compiler_params=_params(len(grid) - n_arbitrary, n_arbitrary))(*ins)
    s_in, s_out = len(side.ins), len(side.outs)
    i1 = n_in + s_in
    o1 = i1 + n_out
    o2 = o1 + s_out
    c1 = o2 + n_scr

    def wrapped(*refs):
        s_refs = (refs[n_in:i1], refs[o1:o2], refs[c1:])
        ids = [pl.program_id(a) for a in range(len(grid))]
        first = functools.reduce(jnp.logical_and, [i == 0 for i in ids])
        last = functools.reduce(jnp.logical_and, [i == g - 1 for i, g in zip(ids, grid)])

        @pl.when(first)
        def _():
            side.start(*s_refs)

        body(*refs[:n_in], *refs[i1:o1], *refs[o2:c1])

        @pl.when(last)
        def _():
            side.finish(*s_refs)

    res = pl.pallas_call(
        wrapped, name=name, grid=grid, in_specs=list(in_specs) + [ANY] * s_in,
        out_specs=list(out_specs) + [ANY] * s_out, out_shape=list(outs) + side.outs,
        scratch_shapes=list(scratch) + side.sems,
        input_output_aliases={n_in + a: n_out + b for a, b in side.aliases.items()},
        compiler_params=_params(0, len(grid)))(*ins, *side.ins)
    return res[:n_out], res[n_out:]


def _gemm(name, grid, ins, in_specs, outs, out_specs, accs, dot_fn, epi_fn, side=None):
    n_in, n_out, n_k = len(ins), len(outs), grid[-1]

    def body(*refs):
        in_refs, out_refs, acc_refs = refs[:n_in], refs[n_in:n_in + n_out], refs[n_in + n_out:]
        k = pl.program_id(len(grid) - 1)

        @pl.when(k == 0)
        def _():
            for a in acc_refs:
                a[...] = jnp.zeros(a.shape, F32)

        dot_fn(in_refs, acc_refs)

        @pl.when(k == n_k - 1)
        def _():
            epi_fn(in_refs, acc_refs, out_refs)

    return _run(body, name, grid, ins, in_specs, outs, out_specs, [pltpu.VMEM(s, F32) for s in accs], 1, side)


def _sds(shape, dtype):
    return jax.ShapeDtypeStruct(shape, dtype)


def _ffn_up(xb, w3, side=None):
    t, d = xb.shape
    nf = w3.shape[2]
    f = 2 * nf
    tm = _tile(t, 512, 8)
    tn = nf // 2 if nf % 256 == 0 else nf
    ns = nf // tn

    def dot_fn(r, acc):
        acc[0][...] += _dot(r[0][...], r[1][...], NN)
        acc[1][...] += _dot(r[0][...], r[2][...], NN)

    def epi_fn(r, acc, out):
        g, u = acc[0][...], acc[1][...]
        out[0][0] = g.astype(BF16)
        out[0][1] = u.astype(BF16)
        out[1][...] = (_silu(g) * u).astype(BF16)

    return _gemm(
        "ffn_up", (t // tm, 2, ns, 1), [xb, w3, w3],
        [pl.BlockSpec((tm, d), lambda i, j, h, k: (i, 0)),
         pl.BlockSpec((None, d, tn), lambda i, j, h, k: (j, 0, h)),
         pl.BlockSpec((None, d, tn), lambda i, j, h, k: (j + 2, 0, h))],
        [_sds((2, t, f), BF16), _sds((t, f), BF16)],
        [pl.BlockSpec((2, tm, tn), lambda i, j, h, k: (0, i, j * ns + h)),
         pl.BlockSpec((tm, tn), lambda i, j, h, k: (i, j * ns + h))],
        [(tm, tn), (tm, tn)], dot_fn, epi_fn, side)


def _down_ln(a, w, resid, coef, g, b, side=None):
    t, kd = a.shape
    d = w.shape[1]
    tm, tk = _tile(t, 512, 8), _tile(kd, 704)

    def dot_fn(r, acc):
        acc[0][...] += _dot(r[0][...], r[1][...], NN)

    def epi_fn(r, acc, out):
        v = ALPHA * r[2][...] + coef * acc[0][...]
        mu = jnp.mean(v, axis=-1, keepdims=True)
        c = v - mu
        var = jnp.mean(c * c, axis=-1, keepdims=True)
        rstd = lax.rsqrt(var + LN_EPS)
        xhat = c * rstd
        h = xhat * r[3][...] + r[4][...]
        out[0][...] = h
        out[1][...] = h.astype(BF16)
        out[2][...] = xhat
        out[3][...] = rstd

    row = pl.BlockSpec((tm, d), lambda i, k: (i, 0))
    vec = pl.BlockSpec((1, d), lambda i, k: (0, 0))
    return _gemm(
        "down_ln", (t // tm, kd // tk), [a, w, resid, g, b],
        [pl.BlockSpec((tm, tk), lambda i, k: (i, k)), pl.BlockSpec((tk, d), lambda i, k: (k, 0)), row, vec, vec],
        [_sds((t, d), F32), _sds((t, d), BF16), _sds((t, d), F32), _sds((t, 1), F32)],
        [row, row, row, pl.BlockSpec((tm, 1), lambda i, k: (i, 0))],
        [(tm, d)], dot_fn, epi_fn, side)


def _only(res, side):
    return res[0] if side is None else (res[0][0], res[1])


def _mm_w3(a, w3, side=None):
    t, kd = a.shape
    n = w3.shape[2]
    tm, tk = _tile(t, 512, 8), _tile(kd, 1024)

    def body(a_ref, w_ref, o_ref):
        k = pl.program_id(2)
        p = _dot(a_ref[...], w_ref[...], NN)

        @pl.when(k == 0)
        def _():
            o_ref[...] = p

        @pl.when(k > 0)
        def _():
            o_ref[...] += p

    return _only(_run(
        body, "mm_w3", (t // tm, N_SHARD, kd // tk), [a, w3],
        [pl.BlockSpec((tm, tk), lambda i, j, k: (i, k)), pl.BlockSpec((None, tk, n), lambda i, j, k: (j, k, 0))],
        [_sds((t, N_SHARD * n), F32)], [pl.BlockSpec((tm, n), lambda i, j, k: (i, j))], [], 1, side), side)


def _mm_w2(a, w, out_dtype):
    t, kd = a.shape
    n = w.shape[1]
    tm, tn, tk = _tile(t, 512, 8), _tile(n, 1024), _tile(kd, 512)

    def dot_fn(r, acc):
        acc[0][...] += _dot(r[0][...], r[1][...], NN)

    def epi_fn(r, acc, out):
        out[0][...] = acc[0][...].astype(out_dtype)

    return _gemm(
        "mm_w2", (t // tm, n // tn, kd // tk), [a, w],
        [pl.BlockSpec((tm, tk), lambda i, j, k: (i, k)), pl.BlockSpec((tk, tn), lambda i, j, k: (k, j))],
        [_sds((t, n), out_dtype)], [pl.BlockSpec((tm, tn), lambda i, j, k: (i, j))],
        [(tm, tn)], dot_fn, epi_fn)[0]


def _branch_gate(ob, wb3, ya, proj, gate_col):
    t, kd = ob.shape
    n = wb3.shape[2]
    d = N_SHARD * n
    tm = _tile(t, 512, 8)
    ga0, gb0 = gate_col // n, (gate_col + d) // n

    def dot_fn(r, acc):
        acc[0][...] += _dot(r[0][...], r[1][...], NN)

    def epi_fn(r, acc, out):
        yb = acc[0][...]
        out[0][...] = yb
        out[1][...] = (_sigmoid(r[3][...]) * r[2][...] + _sigmoid(r[4][...]) * yb).astype(BF16)

    blk = pl.BlockSpec((tm, n), lambda i, j, k: (i, j))
    return _gemm(
        "branch_gate", (t // tm, N_SHARD, 1), [ob, wb3, ya, proj, proj],
        [pl.BlockSpec((tm, kd), lambda i, j, k: (i, 0)), pl.BlockSpec((None, kd, n), lambda i, j, k: (j, 0, 0)), blk,
         pl.BlockSpec((tm, n), lambda i, j, k: (i, ga0 + j)), pl.BlockSpec((tm, n), lambda i, j, k: (i, gb0 + j))],
        [_sds((t, d), F32), _sds((t, d), BF16)], [blk, blk], [(tm, n)], dot_fn, epi_fn)


def _swiglu_bwd(dyb, w, u3):
    t, d = dyb.shape
    f = w.shape[0]
    tm, tr = _tile(t, 1024, 8), _tile(f, 512)

    def dot_fn(r, acc):
        acc[0][...] += _dot(r[0][...], r[1][...], NT)

    def epi_fn(r, acc, out):
        da = acc[0][...]
        g, u = r[2][0].astype(F32), r[2][1].astype(F32)
        s = _sigmoid(g)
        out[0][0] = (da * u * s * (1.0 + g * (1.0 - s))).astype(BF16)
        out[0][1] = (da * g * s).astype(BF16)

    ublk = pl.BlockSpec((2, tm, tr), lambda i, j, k: (0, i, j))
    return _gemm(
        "swiglu_bwd", (t // tm, f // tr, 1), [dyb, w, u3],
        [pl.BlockSpec((tm, d), lambda i, j, k: (i, 0)), pl.BlockSpec((tr, d), lambda i, j, k: (j, 0)), ublk],
        [_sds((2, t, f), BF16)], [ublk], [(tm, tr)], dot_fn, epi_fn)[0]


def _ffn_dx(du3, w3, resid, side=None):
    t = du3.shape[1]
    d, nf = w3.shape[1], w3.shape[2]
    tm, tr = _tile(t, 512, 8), _tile(d, 1024)

    def dot_fn(r, acc):
        acc[0][...] += _dot(r[0][...], r[1][...], NT)

    def epi_fn(r, acc, out):
        out[0][...] = acc[0][...] + r[2][...]

    blk = pl.BlockSpec((tm, tr), lambda i, j, k: (i, j))
    return _only(_gemm(
        "ffn_dx", (t // tm, d // tr, N_SHARD), [du3, w3, resid],
        [pl.BlockSpec((None, tm, nf), lambda i, j, k: (k // 2, i, k % 2)),
         pl.BlockSpec((None, tr, nf), lambda i, j, k: (k, j, 0)), blk],
        [_sds((t, d), F32)], [blk], [(tm, tr)], dot_fn, epi_fn, side), side)


def _nt_w3(dy, w3, resid, side=None):
    t = dy.shape[0]
    kd, n = w3.shape[1], w3.shape[2]
    tm, tr = _tile(t, 512, 8), _tile(kd, 1024)
    has_res = resid is not None

    def dot_fn(r, acc):
        acc[0][...] += _dot(r[0][...], r[1][...], NT)

    def epi_fn(r, acc, out):
        v = acc[0][...]
        if has_res:
            v = v + r[2][...]
        out[0][...] = v

    blk = pl.BlockSpec((tm, tr), lambda i, j, k: (i, j))
    return _only(_gemm(
        "nt_w3", (t // tm, kd // tr, N_SHARD), [dy, w3] + ([resid] if has_res else []),
        [pl.BlockSpec((tm, n), lambda i, j, k: (i, k)), pl.BlockSpec((None, tr, n), lambda i, j, k: (k, j, 0))]
        + ([blk] if has_res else []),
        [_sds((t, kd), F32)], [blk], [(tm, tr)], dot_fn, epi_fn, side), side)


def _nt_w2(dy, w, out_dtype):
    t, n = dy.shape
    r_ = w.shape[0]
    tm, tr, tk = _tile(t, 512, 8), _tile(r_, 1024), _tile(n, 1024)

    def dot_fn(r, acc):
        acc[0][...] += _dot(r[0][...], r[1][...], NT)

    def epi_fn(r, acc, out):
        out[0][...] = acc[0][...].astype(out_dtype)

    return _gemm(
        "nt_w2", (t // tm, r_ // tr, n // tk), [dy, w],
        [pl.BlockSpec((tm, tk), lambda i, j, k: (i, k)), pl.BlockSpec((tr, tk), lambda i, j, k: (j, k))],
        [_sds((t, r_), out_dtype)], [pl.BlockSpec((tm, tr), lambda i, j, k: (i, j))],
        [(tm, tr)], dot_fn, epi_fn)[0]


def _gate_bwd(dmix, w, proj, ya, yb, gate_col):
    t, n = dmix.shape
    d = w.shape[0]
    tm, tr = _tile(t, 512, 8), _tile(d, 512)
    ga0, gb0 = gate_col // tr, (gate_col + d) // tr
    nb = d // tr

    def dot_fn(r, acc):
        acc[0][...] += _dot(r[0][...], r[1][...], NT)

    def epi_fn(r, acc, out):
        dz = acc[0][...]
        ga, gb = _sigmoid(r[2][...]), _sigmoid(r[3][...])
        out[0][...] = (dz * ga).astype(BF16)
        out[1][...] = (dz * gb).astype(BF16)
        out[2][...] = (dz * r[4][...] * ga * (1.0 - ga)).astype(BF16)
        out[3][...] = (dz * r[5][...] * gb * (1.0 - gb)).astype(BF16)

    blk = pl.BlockSpec((tm, tr), lambda i, j, k: (i, j))
    return _gemm(
        "gate_bwd", (t // tm, nb, 1), [dmix, w, proj, proj, ya, yb],
        [pl.BlockSpec((tm, n), lambda i, j, k: (i, 0)), pl.BlockSpec((tr, n), lambda i, j, k: (j, 0)),
         pl.BlockSpec((tm, tr), lambda i, j, k: (i, ga0 + j)), pl.BlockSpec((tm, tr), lambda i, j, k: (i, gb0 + j)),
         blk, blk],
        [_sds((t, d), BF16), _sds((t, d), BF16), _sds((t, d), BF16), _sds((t, d), BF16)],
        [blk, blk, blk, blk], [(tm, tr)], dot_fn, epi_fn)


def _tn_w3(a, dy, dy_map, n, side=None):
    t, kd = a.shape
    tm, tkk = _tile(t, (6 << 20) // n, 8), _tile(kd, 512)

    def dot_fn(r, acc):
        acc[0][...] += _dot(r[0][...], r[1][...], TN)

    def epi_fn(r, acc, out):
        out[0][...] = acc[0][...].astype(BF16)

    dy_block = (tm, n) if dy.ndim == 2 else (None, tm, n)
    return _only(_gemm(
        "tn_w3", (kd // tkk, N_SHARD, t // tm), [a, dy],
        [pl.BlockSpec((tm, tkk), lambda i, j, m: (m, i)), pl.BlockSpec(dy_block, lambda i, j, m: dy_map(j, m))],
        [_sds((N_SHARD, kd, n), BF16)], [pl.BlockSpec((None, tkk, n), lambda i, j, m: (j, i, 0))],
        [(tkk, n)], dot_fn, epi_fn, side), side)


def _tn_w2(a, dy):
    t, kd = a.shape
    n = dy.shape[1]
    tkk, tn = _tile(kd, 512), _tile(n, 2048)
    tm = _tile(t, (4 << 20) // tn, 8)

    def dot_fn(r, acc):
        acc[0][...] += _dot(r[0][...], r[1][...], TN)

    def epi_fn(r, acc, out):
        out[0][...] = acc[0][...].astype(BF16)

    return _gemm(
        "tn_w2", (kd // tkk, n // tn, t // tm), [a, dy],
        [pl.BlockSpec((tm, tkk), lambda i, j, m: (m, i)), pl.BlockSpec((tm, tn), lambda i, j, m: (m, j))],
        [_sds((kd, n), BF16)], [pl.BlockSpec((tkk, tn), lambda i, j, m: (i, j))],
        [(tkk, tn)], dot_fn, epi_fn)[0]


def _ln_bwd(dh_parts, xhat, rstd, g, coef, target=None):
    t, d = xhat.shape
    tm = _tile(t, 256, 8)
    n_parts = len(dh_parts)
    with_loss = target is not None
    ins = list(dh_parts) + [xhat, rstd, g] + ([target] if with_loss else [])

    def body(*refs):
        parts = refs[:n_parts]
        xh_ref, rstd_ref, g_ref = refs[n_parts:n_parts + 3]
        o = n_parts + 3 + (1 if with_loss else 0)
        dres_ref, dyb_ref, dg_ref, db_ref, loss_ref = refs[o:o + 5]
        i = pl.program_id(0)
        dh = parts[0][...]
        for p in parts[1:]:
            dh = dh + p[...]
        if with_loss:
            e = dh - refs[o - 1][...]
            part = 0.5 * jnp.sum(jnp.sum(e * e, axis=-1, keepdims=True) * (1.0 / d), axis=0, keepdims=True)
            dh = e * (1.0 / d)
        else:
            part = jnp.zeros((1, 1), F32)
        xh = xh_ref[...]
        dxh = dh * g_ref[...]
        m1 = jnp.mean(dxh, axis=-1, keepdims=True)
        m2 = jnp.mean(dxh * xh, axis=-1, keepdims=True)
        dv = rstd_ref[...] * (dxh - m1 - xh * m2)
        dres_ref[...] = ALPHA * dv
        dyb_ref[...] = (coef * dv).astype(BF16)

        @pl.when(i == 0)
        def _():
            dg_ref[...] = jnp.zeros(dg_ref.shape, F32)
            db_ref[...] = jnp.zeros(db_ref.shape, F32)
            loss_ref[...] = jnp.zeros(loss_ref.shape, F32)

        dg_ref[...] += jnp.sum(dh * xh, axis=0, keepdims=True)
        db_ref[...] += jnp.sum(dh, axis=0, keepdims=True)
        loss_ref[...] += jnp.broadcast_to(part, loss_ref.shape)

    row = pl.BlockSpec((tm, d), lambda i: (i, 0))
    vec = pl.BlockSpec((1, d), lambda i: (0, 0))
    return pl.pallas_call(
        body, name="ln_bwd", grid=(t // tm,),
        in_specs=[row] * n_parts + [row, pl.BlockSpec((tm, 1), lambda i: (i, 0)), vec] + ([row] if with_loss else []),
        out_specs=[row, row, vec, vec, pl.BlockSpec((1, 128), lambda i: (0, 0))],
        out_shape=[_sds((t, d), F32), _sds((t, d), BF16), _sds((1, d), F32), _sds((1, d), F32), _sds((1, 128), F32)],
        compiler_params=_params(0, 1))(*ins)


def _cast_bf16(x2d):
    t, d = x2d.shape
    tm = _tile(t, 512, 8)

    def body(x_ref, o_ref):
        o_ref[...] = x_ref[...].astype(BF16)

    row = pl.BlockSpec((tm, d), lambda i: (i, 0))
    return pl.pallas_call(body, name="cast_bf16", grid=(t // tm,), in_specs=[row], out_specs=row,
                          out_shape=_sds((t, d), BF16), compiler_params=_params(1, 0))(x2d)


def _lower_bound(table):
    t0, t1 = table[0:1, :], table[1:2, :]
    m = jnp.maximum(t0, t1)
    e0, e1 = jnp.exp(t0 - m), jnp.exp(t1 - m)
    return e0 / (e0 + e1)


def _chunk_tri(rows, upper):
    r = lax.broadcasted_iota(jnp.int32, (rows, rows), 0)
    s = lax.broadcasted_iota(jnp.int32, (rows, rows), 1)
    shift = HGRN_CHUNK.bit_length() - 1
    same = lax.shift_right_logical(r, shift) == lax.shift_right_logical(s, shift)
    return same & ((r <= s) if upper else (r >= s))


def _tri_apply(x, upper):
    tri = _chunk_tri(x.shape[0], upper).astype(F32).astype(BF16)
    hi = x.astype(BF16)
    r1 = x - hi.astype(F32)
    mid = r1.astype(BF16)
    lo = (r1 - mid.astype(F32)).astype(BF16)
    return _dot(tri, hi, NN) + _dot(tri, mid, NN) + _dot(tri, lo, NN)


@functools.partial(jax.custom_vjp, nondiff_argnums=(1,))
def _chunk_cumsum(x, upper):
    return _tri_apply(x, upper)


def _chunk_cumsum_fwd(x, upper):
    return _tri_apply(x, upper), None


def _chunk_cumsum_bwd(upper, _, g):
    return (_tri_apply(g, not upper),)


_chunk_cumsum.defvjp(_chunk_cumsum_fwd, _chunk_cumsum_bwd)


def _hgrn_blocks(units, uppers):
    c = HGRN_CHUNK
    rows = units[0][0].shape[0]
    n_sub = rows // c
    ids = range(len(units))
    chunk = lax.shift_right_logical(lax.broadcasted_iota(jnp.int32, (rows, HEAD), 0), c.bit_length() - 1)
    zero = jnp.zeros((rows, HEAD), BF16)

    def expand(a):
        return jnp.concatenate([jnp.where(chunk == n, a, zero) for n in range(n_sub)], axis=1)

    fs = [u[3] + (1.0 - u[3]) * _sigmoid(u[1]) for u in units]
    lgs = [jnp.log(f) for f in fs]
    cums = [_chunk_cumsum(lgs[i], uppers[i]) for i in ids]
    tots = [[jnp.sum(lg[n * c:(n + 1) * c], axis=0, keepdims=True) for n in range(n_sub)] for lg in lgs]
    qd, kd, ke, vb = [], [], [], []
    for i in ids:
        totb = jnp.concatenate([jnp.broadcast_to(t, (c, HEAD)) for t in tots[i]], axis=0)
        kk = 1.0 - fs[i]
        qd.append((_silu(units[i][0]) * jnp.exp(cums[i])).astype(BF16))
        kd.append((kk * jnp.exp(-cums[i])).astype(BF16))
        ke.append((kk * jnp.exp(totb - cums[i])).astype(BF16))
        vb.append(units[i][2].astype(BF16))
    scores = [_dot(qd[i], kd[i], NT) for i in ids]
    kvs = [_dot(vb[i], expand(ke[i]), TN) for i in ids]
    outs = []
    for i in ids:
        a = jnp.where(_chunk_tri(rows, uppers[i]), scores[i], 0.0).astype(BF16)
        st = units[i][4]
        entering = [None] * n_sub
        for n in (range(n_sub - 1, -1, -1) if uppers[i] else range(n_sub)):
            entering[n] = st.astype(BF16)
            st = st * jnp.exp(tots[i][n]) + kvs[i][:, n * HEAD:(n + 1) * HEAD]
        outs.append((a, jnp.concatenate(entering, axis=1), st))
    res = []
    for i in ids:
        a, entering, st = outs[i]
        res.append((_dot(a, vb[i], NN) + _dot(expand(qd[i]), entering, NT), st))
    return res


def _hgrn_fwd(proj3, lb_f, lb_b, norm_g, d):
    b_, s_, _ = proj3.shape
    nh = d // HEAD
    hb = HGRN_FWD_HEADS
    wid = hb * HEAD
    rows = HGRN_CHUNK * HGRN_SUB
    n_blk = s_ // rows

    def body(hq_ref, hff_ref, hfb_ref, hi_ref, hog_ref, lbf_ref, lbb_ref, g_ref, oraw_ref, out_ref, of_ref, ob_ref):
        lbf, lbb = _lower_bound(lbf_ref[...]), _lower_bound(lbb_ref[...])

        def step(j, sts):
            rf = pl.ds(pl.multiple_of(j * rows, rows), rows)
            rb = pl.ds(pl.multiple_of((n_blk - 1 - j) * rows, rows), rows)
            units = []
            for hh in range(hb):
                cs = slice(hh * HEAD, (hh + 1) * HEAD)
                units.append((hq_ref[rf, cs], hff_ref[rf, cs], hi_ref[rf, cs], lbf[:, cs], sts[2 * hh]))
                units.append((hq_ref[rb, cs], hfb_ref[rb, cs], hi_ref[rb, cs], lbb[:, cs], sts[2 * hh + 1]))
            res = _hgrn_blocks(units, [False, True] * hb)
            for hh in range(hb):
                cs = slice(hh * HEAD, (hh + 1) * HEAD)
                of_ref[rf, cs] = res[2 * hh][0]
                ob_ref[rb, cs] = res[2 * hh + 1][0]
            return tuple(r[1] for r in res)

        z = jnp.zeros((HEAD, HEAD), F32)
        lax.fori_loop(0, n_blk, step, (z,) * (2 * hb))
        for hh in range(hb):
            cs = slice(hh * HEAD, (hh + 1) * HEAD)
            o = of_ref[:, cs] + ob_ref[:, cs]
            oraw_ref[:, cs] = o
            on = o * lax.rsqrt(jnp.mean(o * o, axis=-1, keepdims=True) + LN_EPS)
            out_ref[:, cs] = (on * g_ref[:, cs] * _silu(hog_ref[:, cs])).astype(BF16)

    def col(k):
        return pl.BlockSpec((None, s_, wid), lambda h, b: (b, 0, k * (nh // hb) + h))

    tab = pl.BlockSpec((2, wid), lambda h, b: (0, h))
    oblk = pl.BlockSpec((None, s_, wid), lambda h, b: (b, 0, h))
    return pl.pallas_call(
        body, name="hgrn_fwd", grid=(nh // hb, b_),
        in_specs=[col(0), col(1), col(2), col(3), col(4), tab, tab, pl.BlockSpec((1, wid), lambda h, b: (0, h))],
        out_specs=[oblk, oblk], out_shape=[_sds((b_, s_, d), F32), _sds((b_, s_, d), BF16)],
        scratch_shapes=[pltpu.VMEM((s_, wid), F32), pltpu.VMEM((s_, wid), F32)],
        compiler_params=_params(2, 0))(proj3, proj3, proj3, proj3, proj3, lb_f, lb_b, norm_g)


def _hgrn_bwd(proj3, lb_f, lb_b, norm_g, o_raw, do_a, d, side=None):
    b_, s_, _ = proj3.shape
    nh = d // HEAD
    hb = HGRN_BWD_HEADS
    wid = hb * HEAD
    rows = HGRN_CHUNK * HGRN_SUB
    n_blk = s_ // rows

    def body(hq_ref, hff_ref, hfb_ref, hi_ref, hog_ref, lbf_ref, lbb_ref, g_ref, oraw_ref, doa_ref,
             dq_ref, dff_ref, dfb_ref, di_ref, dog_ref, dlbf_ref, dlbb_ref, dg_ref,
             st_ref, dor_ref, dq2_ref, di2_ref):
        b = pl.program_id(1)
        tab_f, tab_b = lbf_ref[...], lbb_ref[...]
        lbf, lbb = _lower_bound(tab_f), _lower_bound(tab_b)

        dg_parts = []
        for hh in range(hb):
            cs = slice(hh * HEAD, (hh + 1) * HEAD)
            o, doa, hog, g = oraw_ref[:, cs], doa_ref[:, cs], hog_ref[:, cs], g_ref[:, cs]
            rs = lax.rsqrt(jnp.mean(o * o, axis=-1, keepdims=True) + LN_EPS)
            on = o * rs
            sg = _sigmoid(hog)
            gate = hog * sg
            dog_ref[:, cs] = (doa * on * g * sg * (1.0 + hog * (1.0 - sg))).astype(BF16)
            don = doa * g * gate
            dor_ref[:, cs] = rs * (don - on * jnp.mean(don * on, axis=-1, keepdims=True))
            dg_parts.append(jnp.sum(doa * on * gate, axis=0, keepdims=True))

        def fwd_step(j, carry):
            jb = n_blk - 1 - j
            rf = pl.ds(pl.multiple_of(j * rows, rows), rows)
            rb = pl.ds(pl.multiple_of(jb * rows, rows), rows)
            units = []
            for hh in range(hb):
                cs = slice(hh * HEAD, (hh + 1) * HEAD)
                st_ref[2 * hh, j] = carry[2 * hh]
                st_ref[2 * hh + 1, jb] = carry[2 * hh + 1]
                units.append((hq_ref[rf, cs], hff_ref[rf, cs], hi_ref[rf, cs], lbf[:, cs], carry[2 * hh]))
                units.append((hq_ref[rb, cs], hfb_ref[rb, cs], hi_ref[rb, cs], lbb[:, cs], carry[2 * hh + 1]))
            return tuple(r[1] for r in _hgrn_blocks(units, uppers))

        uppers = [False, True] * hb
        z = jnp.zeros((HEAD, HEAD), F32)
        lax.fori_loop(0, n_blk, fwd_step, (z,) * (2 * hb))

        def bwd_step(j, carry):
            gs, dls = carry
            jf = n_blk - 1 - j
            rf = pl.ds(pl.multiple_of(jf * rows, rows), rows)
            rb = pl.ds(pl.multiple_of(j * rows, rows), rows)
            flat, cots = [], []
            for hh in range(hb):
                cs = slice(hh * HEAD, (hh + 1) * HEAD)
                flat += [hq_ref[rf, cs], hff_ref[rf, cs], hi_ref[rf, cs], lbf[:, cs], st_ref[2 * hh, jf],
                         hq_ref[rb, cs], hfb_ref[rb, cs], hi_ref[rb, cs], lbb[:, cs], st_ref[2 * hh + 1, j]]
                cots += [(dor_ref[rf, cs], gs[2 * hh]), (dor_ref[rb, cs], gs[2 * hh + 1])]
            _, vjp = jax.vjp(lambda *a: _hgrn_blocks([a[5 * i:5 * i + 5] for i in range(2 * hb)], uppers), *flat)
            grads = vjp(cots)
            gs, dls = list(gs), list(dls)
            for hh in range(hb):
                cs = slice(hh * HEAD, (hh + 1) * HEAD)
                dq, df, di, dl, gs[2 * hh] = grads[10 * hh:10 * hh + 5]
                dq_ref[rf, cs] = dq.astype(BF16)
                dff_ref[rf, cs] = df.astype(BF16)
                di_ref[rf, cs] = di.astype(BF16)
                dls[2 * hh] = dls[2 * hh] + dl
                dq, df, di, dl, gs[2 * hh + 1] = grads[10 * hh + 5:10 * hh + 10]
                dq2_ref[rb, cs] = dq
                dfb_ref[rb, cs] = df.astype(BF16)
                di2_ref[rb, cs] = di
                dls[2 * hh + 1] = dls[2 * hh + 1] + dl
            return tuple(gs), tuple(dls)

        zl = jnp.zeros((1, HEAD), F32)
        _, dls = lax.fori_loop(0, n_blk, bwd_step, ((z,) * (2 * hb), (zl,) * (2 * hb)))
        dq_ref[...] = (dq_ref[...].astype(F32) + dq2_ref[...]).astype(BF16)
        di_ref[...] = (di_ref[...].astype(F32) + di2_ref[...]).astype(BF16)

        _, vjp_tf = jax.vjp(_lower_bound, tab_f)
        _, vjp_tb = jax.vjp(_lower_bound, tab_b)

        @pl.when(b == 0)
        def _():
            dlbf_ref[...] = jnp.zeros(dlbf_ref.shape, F32)
            dlbb_ref[...] = jnp.zeros(dlbb_ref.shape, F32)
            dg_ref[...] = jnp.zeros(dg_ref.shape, F32)

        dlbf_ref[...] += vjp_tf(jnp.concatenate(dls[0::2], axis=1))[0]
        dlbb_ref[...] += vjp_tb(jnp.concatenate(dls[1::2], axis=1))[0]
        dg_ref[...] += jnp.concatenate(dg_parts, axis=1)

    def col(k):
        return pl.BlockSpec((None, s_, wid), lambda h, b: (b, 0, k * (nh // hb) + h))

    tab = pl.BlockSpec((2, wid), lambda h, b: (0, h))
    vec = pl.BlockSpec((1, wid), lambda h, b: (0, h))
    oblk = pl.BlockSpec((None, s_, wid), lambda h, b: (b, 0, h))
    seq = _sds((b_, s_, d), BF16)
    return _run(
        body, "hgrn_bwd", (nh // hb, b_), [proj3, proj3, proj3, proj3, proj3, lb_f, lb_b, norm_g, o_raw, do_a],
        [col(0), col(1), col(2), col(3), col(4), tab, tab, vec, oblk, oblk],
        [seq, seq, seq, seq, seq, _sds((2, d), F32), _sds((2, d), F32), _sds((1, d), F32)],
        [oblk, oblk, oblk, oblk, oblk, tab, tab, vec],
        [pltpu.VMEM((2 * hb, n_blk, HEAD, HEAD), F32),
         pltpu.VMEM((s_, wid), F32), pltpu.VMEM((s_, wid), F32), pltpu.VMEM((s_, wid), F32)], 1, side)


def _rope_tables(s_):
    inv = ROPE_THETA ** (-jnp.arange(0, ROPE_DIM, 2, dtype=F32) / ROPE_DIM)
    ang = jnp.arange(s_, dtype=F32)[:, None] * inv
    cos, sin = jnp.cos(ang), jnp.sin(ang)
    rest = HEAD - ROPE_DIM
    ctab = jnp.concatenate([cos, cos, jnp.ones((s_, rest), F32)], axis=1)
    stab = jnp.concatenate([-sin, sin, jnp.zeros((s_, rest), F32)], axis=1)
    half = ROPE_DIM // 2
    perm = np.zeros((HEAD, HEAD), np.float32)
    for i in range(half):
        perm[i + half, i] = 1.0
        perm[i, i + half] = 1.0
    return ctab, stab, jnp.asarray(perm)


def _attn_tile(qr, kr, v, cq, sq, ck, sk, perm, qi0, kj0, half):
    hi = lax.Precision.HIGHEST
    q = qr * cq + _dot(qr, perm, NN, precision=hi) * sq
    k = kr * ck + _dot(kr, perm, NN, precision=hi) * sk
    s = _dot(q.astype(BF16), k.astype(BF16), NT) * (HEAD ** -0.5)
    qi = qi0 + lax.broadcasted_iota(jnp.int32, s.shape, 0)
    kj = kj0 + lax.broadcasted_iota(jnp.int32, s.shape, 1)
    s = jnp.where(jnp.abs(qi - kj) <= half, s, NEG_INF)
    m = lax.stop_gradient(jnp.max(s, axis=-1, keepdims=True))
    p = jnp.exp(s - m)
    den = jnp.sum(p, axis=-1, keepdims=True)
    o = _dot(p.astype(BF16), v.astype(BF16), NN) / den
    lse = m + jnp.log(den)
    return o, jnp.broadcast_to(lse, o.shape)


def _attn_tiling(seg):
    tq = seg if seg <= 256 else 256
    kw = seg if seg <= 512 else 512
    tiles = []
    for i in range(seg // tq):
        ws = min(max(i * tq - (kw - tq) // 2, 0), seg - kw)
        tiles.append((i * tq, ws))
    return tq, kw, tiles


def _attn_specs(s_, d, g):
    c0 = (5 * d + g * 3 * ATTN_OUT) // HEAD

    def col(part):
        return pl.BlockSpec((None, s_, HEAD), lambda b, h: (b, 0, c0 + part * ATTN_HEADS + h))

    tab = pl.BlockSpec((s_, HEAD), lambda b, h: (0, 0))
    perm = pl.BlockSpec((HEAD, HEAD), lambda b, h: (0, 0))
    oblk = pl.BlockSpec((None, s_, HEAD), lambda b, h: (b, 0, h))
    return col, tab, perm, oblk


def _residue(r, seg, dil):
    return pl.ds(r, seg, stride=dil) if dil > 1 else pl.ds(0, seg)


def _attn_fwd(proj3, ctab, stab, perm, d, g):
    window, dil = ATTN_GROUPS[g]
    b_, s_, _ = proj3.shape
    seg = s_ // dil
    half = window // (2 * dil)
    tq, kw, tiles = _attn_tiling(seg)

    def body(q_ref, k_ref, v_ref, c_ref, s_ref, p_ref, o_ref, l_ref):
        pm = p_ref[...]
        for r in range(dil):
            rows = _residue(r, seg, dil)
            q, k, v, ct, st = q_ref[rows, :], k_ref[rows, :], v_ref[rows, :], c_ref[rows, :], s_ref[rows, :]
            res = [_attn_tile(q[q0:q0 + tq], k[k0:k0 + kw], v[k0:k0 + kw], ct[q0:q0 + tq], st[q0:q0 + tq],
                              ct[k0:k0 + kw], st[k0:k0 + kw], pm, q0, k0, half) for q0, k0 in tiles]
            o_ref[rows, :] = jnp.concatenate([o for o, _ in res], axis=0)
            l_ref[rows, :] = jnp.concatenate([l for _, l in res], axis=0)

    col, tab, pspec, oblk = _attn_specs(s_, d, g)
    shp = _sds((b_, s_, ATTN_OUT), F32)
    return pl.pallas_call(
        body, name=f"attn_fwd_d{dil}", grid=(b_, ATTN_HEADS), in_specs=[col(0), col(1), col(2), tab, tab, pspec],
        out_specs=[oblk, oblk], out_shape=[shp, shp], compiler_params=_params(2, 0))(
            proj3, proj3, proj3, ctab, stab, perm)


def _attn_bwd(proj3, ctab, stab, perm, do, dl, d, g):
    window, dil = ATTN_GROUPS[g]
    b_, s_, _ = proj3.shape
    seg = s_ // dil
    half = window // (2 * dil)
    tq, kw, tiles = _attn_tiling(seg)

    def body(q_ref, k_ref, v_ref, c_ref, s_ref, p_ref, do_ref, dl_ref, dq_ref, dk_ref, dv_ref,
             dq_s, dk_s, dv_s, kacc, vacc):
        pm = p_ref[...]
        for r in range(dil):
            rows = _residue(r, seg, dil)
            q, k, v, ct, st = q_ref[rows, :], k_ref[rows, :], v_ref[rows, :], c_ref[rows, :], s_ref[rows, :]
            do_r, dl_r = do_ref[rows, :], dl_ref[rows, :]
            kacc[...] = jnp.zeros(kacc.shape, F32)
            vacc[...] = jnp.zeros(vacc.shape, F32)
            dqs = []
            for q0, k0 in tiles:
                cq, sq, ck, sk = ct[q0:q0 + tq], st[q0:q0 + tq], ct[k0:k0 + kw], st[k0:k0 + kw]
                _, vjp = jax.vjp(lambda a, b, c: _attn_tile(a, b, c, cq, sq, ck, sk, pm, q0, k0, half),
                                 q[q0:q0 + tq], k[k0:k0 + kw], v[k0:k0 + kw])
                dq, dk, dv = vjp((do_r[q0:q0 + tq], dl_r[q0:q0 + tq]))
                dqs.append(dq)
                kacc[pl.ds(k0, kw), :] += dk
                vacc[pl.ds(k0, kw), :] += dv
            dq_s[rows, :] = jnp.concatenate(dqs, axis=0)
            dk_s[rows, :] = kacc[...]
            dv_s[rows, :] = vacc[...]
        dq_ref[...] = dq_s[...].astype(BF16)
        dk_ref[...] = dk_s[...].astype(BF16)
        dv_ref[...] = dv_s[...].astype(BF16)

    col, tab, pspec, oblk = _attn_specs(s_, d, g)
    shp = _sds((b_, s_, ATTN_OUT), BF16)
    full, res = pltpu.VMEM((s_, HEAD), F32), pltpu.VMEM((seg, HEAD), F32)
    return pl.pallas_call(
        body, name=f"attn_bwd_d{dil}", grid=(b_, ATTN_HEADS),
        in_specs=[col(0), col(1), col(2), tab, tab, pspec, oblk, oblk], out_specs=[oblk, oblk, oblk],
        out_shape=[shp, shp, shp], scratch_shapes=[full, full, full, res, res],
        compiler_params=_params(2, 0))(proj3, proj3, proj3, ctab, stab, perm, do, dl)


def _combine(os_, ls_):
    m = jnp.maximum(jnp.maximum(ls_[0], ls_[1]), ls_[2])
    es = [jnp.exp(l - m) for l in ls_]
    return (es[0] * os_[0] + es[1] * os_[1] + es[2] * os_[2]) / (es[0] + es[1] + es[2])


def _combine_fwd(os_, ls_):
    t, w = os_[0].shape
    tm = _tile(t, 512, 8)

    def body(*refs):
        refs[6][...] = _combine([r[...] for r in refs[:3]], [r[...] for r in refs[3:6]]).astype(BF16)

    row = pl.BlockSpec((tm, w), lambda i: (i, 0))
    return pl.pallas_call(body, name="combine_fwd", grid=(t // tm,), in_specs=[row] * 6, out_specs=row,
                          out_shape=_sds((t, w), BF16), compiler_params=_params(1, 0))(*os_, *ls_)


def _combine_bwd(os_, ls_, dob):
    t, w = os_[0].shape
    tm = _tile(t, 512, 8)

    def body(*refs):
        _, vjp = jax.vjp(lambda *a: _combine(a[:3], a[3:]), *[r[...] for r in refs[:6]])
        for r, g in zip(refs[7:], vjp(refs[6][...])):
            r[...] = g

    row = pl.BlockSpec((tm, w), lambda i: (i, 0))
    return pl.pallas_call(body, name="combine_bwd", grid=(t // tm,), in_specs=[row] * 7, out_specs=[row] * 6,
                          out_shape=[_sds((t, w), F32)] * 6, compiler_params=_params(1, 0))(*os_, *ls_, dob)


ROW_SHARDED = ("ffn1_out", "ffn2_out", "wa", "mix_out")


def _local_step(x, target, fulls, p, sc_arr):
    b_, s_, d = x.shape
    t = b_ * s_
    x2 = x.reshape(t, d)
    gate_col = 5 * d + QKV_W
    c_arr = sc_arr[1:2]
    w = {}

    def arrived(keys, arrays):
        for k, a in zip(keys, arrays):
            w[k] = a.reshape(a.shape[0] * a.shape[1], a.shape[2]) if k in ROW_SHARDED else a

    def reduce_begin(keys):
        g3 = [gw[k].reshape(fulls[k].shape) for k in keys]
        return [_sum_halves(a, b, c_arr) for a, b in zip(g3, _swap_halves(g3))]

    def reduce_end(keys, parts, got):
        for k, a, b in zip(keys, parts, got):
            reduced[k] = _sum_chips(a, b, sc_arr)

    arrived(["ffn1_in"], _gather_weights([fulls["ffn1_in"]]))
    xb = _cast_bf16(x2)
    mix_rows = fulls["mix_in"].shape[1]
    (u1, a1), got = _ffn_up(xb, w["ffn1_in"], _gather_exchange([fulls["ffn1_out"], fulls["mix_in"]],
                                                              [None, (0, mix_rows // 2)]))
    arrived(["ffn1_out"], got[:1])
    (h1, h1b, xh1, rs1), got = _down_ln(a1, w["ffn1_out"], x2, 0.5, p["ln1_g"], p["ln1_b"],
                                        _gather_exchange(got[1:], [(mix_rows // 2, mix_rows // 2)]))
    arrived(["mix_in"], got)

    keys = ["wa", "wb", "mix_out", "ffn2_in", "ffn2_out"]
    proj, got = _mm_w3(h1b, w["mix_in"], _gather_exchange([fulls[k] for k in keys]))
    arrived(keys, got)
    n_in = proj.shape[1]
    proj3 = proj.reshape(b_, s_, n_in)
    o_raw, oa = _hgrn_fwd(proj3, p["hgrn_lb_fwd"], p["hgrn_lb_bwd"], p["hgrn_norm_g"], d)
    oa2 = oa.reshape(t, d)

    ctab, stab, perm = _rope_tables(s_)
    os_, ls_ = [], []
    for g in range(N_GROUPS):
        o_g, l_g = _attn_fwd(proj3, ctab, stab, perm, d, g)
        os_.append(o_g.reshape(t, ATTN_OUT))
        ls_.append(l_g.reshape(t, ATTN_OUT))
    ob = _combine_fwd(os_, ls_)

    ya = _mm_w2(oa2, w["wa"], F32)
    yb, zb = _branch_gate(ob, w["wb"], ya, proj, gate_col)
    h2, h2b, xh2, rs2 = _down_ln(zb, w["mix_out"], h1, 1.0, p["ln2_g"], p["ln2_b"])

    u2, a2 = _ffn_up(h2b, w["ffn2_in"])
    h3, _, xh3, rs3 = _down_ln(a2, w["ffn2_out"], h2, 0.5, p["ln3_g"], p["ln3_b"])

    gw, gp, reduced = {}, {}, {}
    nf = w["ffn2_in"].shape[2]

    def du_map(j, m):
        return (j // 2, m, j % 2)

    dres3, dy3, gp["ln3_g"], gp["ln3_b"], loss = _ln_bwd([h3], xh3, rs3, p["ln3_g"], 0.5, target.reshape(t, d))
    du2 = _swiglu_bwd(dy3, w["ffn2_out"], u2)
    gw["ffn2_out"] = _tn_w2(a2, dy3)
    gw["ffn2_in"] = _tn_w3(h2b, du2, du_map, nf)
    parts_a = reduce_begin(["ffn2_out", "ffn2_in"])
    dh2, got = _ffn_dx(du2, w["ffn2_in"], dres3, _scatter_exchange(parts_a[:1]))
    reduce_end(["ffn2_out"], parts_a[:1], got)

    dres2, dmix, gp["ln2_g"], gp["ln2_b"], _ = _ln_bwd([dh2], xh2, rs2, p["ln2_g"], 1.0)
    dya, dyb, dpga, dpgb = _gate_bwd(dmix, w["mix_out"], proj, ya, yb, gate_col)
    gw["mix_out"] = _tn_w2(zb, dmix)
    do_a = _nt_w2(dya, w["wa"], F32)
    gw["wa"] = _tn_w2(oa2, dya)
    nb = w["wb"].shape[2]
    do_b = _nt_w3(dyb, w["wb"], None)
    gw["wb"] = _tn_w3(ob, dyb, lambda j, m: (m, j), nb)
    keys = ["ffn2_in", "mix_out", "wa", "wb"]
    parts_b = parts_a[1:] + reduce_begin(keys[1:])

    (dq, dff, dfb, di, dog, gp["hgrn_lb_fwd"], gp["hgrn_lb_bwd"], gp["hgrn_norm_g"]), got = _hgrn_bwd(
        proj3, p["hgrn_lb_fwd"], p["hgrn_lb_bwd"], p["hgrn_norm_g"], o_raw, do_a.reshape(b_, s_, d), d,
        _scatter_exchange(parts_b))
    reduce_end(keys, parts_b, got)

    douts = _combine_bwd(os_, ls_, do_b)
    dqkv = []
    for g in range(N_GROUPS):
        grads_g = _attn_bwd(proj3, ctab, stab, perm, douts[g].reshape(b_, s_, ATTN_OUT),
                            douts[3 + g].reshape(b_, s_, ATTN_OUT), d, g)
        dqkv += [a.reshape(t, ATTN_OUT) for a in grads_g]

    dproj = jnp.concatenate(
        [a.reshape(t, d) for a in (dq, dff, dfb, di, dog)] + dqkv + [dpga, dpgb], axis=1)
    nm = w["mix_in"].shape[2]
    gw["mix_in"] = _tn_w3(h1b, dproj, lambda j, m: (m, j), nm)
    parts_c = reduce_begin(["mix_in"])
    dh1, got = _nt_w3(dproj, w["mix_in"], dres2, _scatter_exchange(parts_c))
    reduce_end(["mix_in"], parts_c, got)

    dres1, dy1, gp["ln1_g"], gp["ln1_b"], _ = _ln_bwd([dh1], xh1, rs1, p["ln1_g"], 0.5)
    du1 = _swiglu_bwd(dy1, w["ffn1_out"], u1)
    gw["ffn1_out"] = _tn_w2(a1, dy1)
    parts_d = reduce_begin(["ffn1_out"])
    gw["ffn1_in"], got = _tn_w3(xb, du1, du_map, nf, _scatter_exchange(parts_d))
    reduce_end(["ffn1_out"], parts_d, got)
    parts_e = reduce_begin(["ffn1_in"])
    dx, got = _ffn_dx(du1, w["ffn1_in"], dres1, _scatter_exchange(parts_e))
    reduce_end(["ffn1_in"], parts_e, got)

    keys = list(reduced)
    grads = dict(zip(keys, _join_halves([reduced[k] for k in keys])))
    return loss, dx.reshape(b_, s_, d), grads, gp


MESH = pl.DeviceIdType.MESH
ANY = pl.BlockSpec(memory_space=pl.ANY)


def _place():
    x, y, c = lax.axis_index("x"), lax.axis_index("y"), lax.axis_index("c")
    chips = [(1 - x, y), (x, 1 - y), (1 - x, 1 - y)]
    return x, y, c, chips, (x, y, 1 - c)


def _half_rows(c, rows):
    hr = rows // 2
    return pl.ds(pl.multiple_of(c * hr, 16), hr)


def _remote(src, dst, send, recv, dev):
    return pltpu.make_async_remote_copy(src_ref=src, dst_ref=dst, send_sem=send, recv_sem=recv,
                                        device_id=dev, device_id_type=MESH)


def _gather_weights(fulls):
    n = len(fulls)

    def body(*refs):
        _gather_start(refs[n:2 * n], refs[2 * n:])
        _gather_finish(refs[n:2 * n], refs[2 * n:])

    return pl.pallas_call(
        body, name="gather_weights", in_specs=[ANY] * n, out_specs=[ANY] * n,
        out_shape=[_sds(a.shape, a.dtype) for a in fulls], input_output_aliases={i: i for i in range(n)},
        scratch_shapes=_gather_sems(n))(*fulls)


def _gather_sems(n):
    return [pltpu.SemaphoreType.DMA((n, 3)) for _ in range(4)]


def _span_half(c, span, rows):
    r0, cnt = (0, rows) if span is None else span
    return pl.ds(pl.multiple_of(r0 + c * (cnt // 2), 16), cnt // 2)


def _gather_start(bufs, sems, spans=None):
    isend, irecv = sems[0], sems[1]
    x, y, c, chips, sib = _place()
    for i, buf in enumerate(bufs):
        blk = buf.at[2 * x + y, _span_half(c, spans and spans[i], buf.shape[1])]
        for k, chip in enumerate(chips):
            _remote(blk, blk, isend.at[i, k], irecv.at[i, k], (*chip, c)).start()


def _gather_finish(bufs, sems, spans=None):
    isend, irecv, fsend, frecv = sems
    x, y, c, chips, sib = _place()
    for i, buf in enumerate(bufs):
        mine = _span_half(c, spans and spans[i], buf.shape[1])
        for k, chip in enumerate(chips):
            blk = buf.at[2 * chip[0] + chip[1], mine]
            _remote(blk, blk, isend.at[i, k], irecv.at[i, k], (*chip, c)).wait_recv()
            _remote(blk, blk, fsend.at[i, k], frecv.at[i, k], sib).start()
    for i, buf in enumerate(bufs):
        span = spans and spans[i]
        mine, other = _span_half(c, span, buf.shape[1]), _span_half(1 - c, span, buf.shape[1])
        own = buf.at[2 * x + y, mine]
        for k, chip in enumerate(chips):
            got = buf.at[2 * chip[0] + chip[1], other]
            _remote(got, got, fsend.at[i, k], frecv.at[i, k], sib).wait_recv()
            _remote(own, own, isend.at[i, k], irecv.at[i, k], (*chip, c)).wait_send()
            blk = buf.at[2 * chip[0] + chip[1], mine]
            _remote(blk, blk, fsend.at[i, k], frecv.at[i, k], sib).wait_send()


def _gather_exchange(fulls, spans=None):
    n = len(fulls)
    return _Exchange(fulls, [_sds(a.shape, a.dtype) for a in fulls], {i: i for i in range(n)}, _gather_sems(n),
                     lambda ins, outs, sems: _gather_start(outs, sems, spans),
                     lambda ins, outs, sems: _gather_finish(outs, sems, spans))


def _swap_halves(grads):
    n = len(grads)

    def body(*refs):
        ins, outs, send, recv = refs[:n], refs[n:2 * n], refs[2 * n], refs[2 * n + 1]
        x, y, c, chips, sib = _place()
        cps = []
        for i in range(n):
            other = _half_rows(1 - c, ins[i].shape[1])
            cp = _remote(ins[i].at[:, other], outs[i], send.at[i], recv.at[i], sib)
            cp.start()
            cps.append(cp)
        for cp in cps:
            cp.wait()

    dma = pltpu.SemaphoreType.DMA
    return pl.pallas_call(
        body, name="swap_halves", in_specs=[ANY] * n, out_specs=[ANY] * n,
        out_shape=[_sds((N_SHARD, a.shape[1] // 2, a.shape[2]), a.dtype) for a in grads],
        scratch_shapes=[dma((n,)), dma((n,))])(*grads)


def _scatter_partials(parts):
    side = _scatter_exchange(parts)
    n = len(parts)

    def body(*refs):
        side.start(refs[:n], refs[n:2 * n], refs[2 * n:])
        side.finish(refs[:n], refs[n:2 * n], refs[2 * n:])

    return pl.pallas_call(
        body, name="scatter_partials", in_specs=[ANY] * n, out_specs=[ANY] * n, out_shape=side.outs,
        scratch_shapes=side.sems)(*parts)


def _scatter_copies(ins, outs, sems):
    x, y, c, chips, sib = _place()
    return [_remote(a.at[2 * chip[0] + chip[1]], b.at[k], sems[0].at[i, k], sems[1].at[i, k], (*chip, c))
            for i, (a, b) in enumerate(zip(ins, outs)) for k, chip in enumerate(chips)]


def _scatter_exchange(parts):
    n = len(parts)

    def start(ins, outs, sems):
        for cp in _scatter_copies(ins, outs, sems):
            cp.start()

    def finish(ins, outs, sems):
        for cp in _scatter_copies(ins, outs, sems):
            cp.wait()

    return _Exchange(parts, [_sds((3,) + a.shape[1:], a.dtype) for a in parts], {},
                     [pltpu.SemaphoreType.DMA((n, 3)) for _ in range(2)], start, finish)


def _join_halves(grads):
    n = len(grads)

    def body(*refs):
        bufs, send, recv = refs[n:2 * n], refs[2 * n], refs[2 * n + 1]
        x, y, c, chips, sib = _place()
        cps = []
        for i in range(n):
            blk = bufs[i].at[_half_rows(c, bufs[i].shape[0])]
            other = bufs[i].at[_half_rows(1 - c, bufs[i].shape[0])]
            cp = _remote(blk, blk, send.at[i], recv.at[i], sib)
            cp.start()
            cps.append((cp, _remote(other, other, send.at[i], recv.at[i], sib)))
        for cp, got in cps:
            cp.wait_send()
            got.wait_recv()

    dma = pltpu.SemaphoreType.DMA
    return pl.pallas_call(
        body, name="join_halves", in_specs=[ANY] * n, out_specs=[ANY] * n,
        out_shape=[_sds(a.shape, a.dtype) for a in grads], input_output_aliases={i: i for i in range(n)},
        scratch_shapes=[dma((n,)), dma((n,))])(*grads)


def _gather_rows(block):
    m_per, n = block.shape

    def body(x_ref, out_ref, send_sems, recv_sems, local_sem):
        x, y, c, chips, sibling = _place()
        me = (x, y, c)

        def rows(px, py, pc):
            return out_ref.at[pl.ds((4 * px + 2 * py + pc) * m_per, m_per), :]

        def copy(k, blk, to, src=None):
            return _remote(rows(*blk) if src is None else src, rows(*blk), send_sems.at[k], recv_sems.at[k], to)

        mine = pltpu.make_async_copy(x_ref, rows(*me), local_sem)
        mine.start()
        first = [copy(0, me, sibling, src=x_ref)]
        first += [copy(1 + j, me, (*chip, c), src=x_ref) for j, chip in enumerate(chips)]
        for cp in first:
            cp.start()
        passed = [copy(4 + j, (*chip, c), sibling) for j, chip in enumerate(chips)]
        for j, chip in enumerate(chips):
            copy(1 + j, (*chip, c), me).wait_recv()
            passed[j].start()
        copy(0, sibling, me).wait_recv()
        for j, chip in enumerate(chips):
            copy(4 + j, (*chip, 1 - c), me).wait_recv()
        for cp in first + passed:
            cp.wait_send()
        mine.wait()

    vmem = pl.BlockSpec(memory_space=pltpu.VMEM)
    dma = pltpu.SemaphoreType.DMA
    return pl.pallas_call(
        body, name="gather_rows", in_specs=[vmem], out_specs=vmem, out_shape=_sds((8 * m_per, n), block.dtype),
        scratch_shapes=[dma((7,)), dma((7,)), dma(())])(block)


def _row_tile(rows, cols):
    return _tile(rows, max(16, (1 << 20) // cols), 16)


def _sum_halves(grad, got, c_arr):
    _, hr, cols = got.shape
    tr = _row_tile(hr, cols)
    nb = hr // tr

    def body(c_ref, a_ref, b_ref, o_ref):
        o_ref[...] = (a_ref[...].astype(F32) + b_ref[...].astype(F32)).astype(BF16)

    blk = pl.BlockSpec((None, tr, cols), lambda s, i, c_ref: (s, i, 0))
    return pl.pallas_call(
        body, name="sum_halves",
        grid_spec=pltpu.PrefetchScalarGridSpec(
            num_scalar_prefetch=1, grid=(N_SHARD, nb),
            in_specs=[pl.BlockSpec((None, tr, cols), lambda s, i, c_ref: (s, c_ref[0] * nb + i, 0)), blk],
            out_specs=blk),
        out_shape=_sds(got.shape, BF16), compiler_params=_params(2, 0))(c_arr, grad, got)


def _sum_chips(part, got, sc_arr):
    _, hr, cols = got.shape
    tr = _row_tile(hr, cols)
    nb = hr // tr

    def body(s_ref, a_ref, b_ref, o_ref):
        o_ref[...] = ((a_ref[...].astype(F32) + b_ref[0].astype(F32)) + b_ref[1].astype(F32)) + b_ref[2].astype(F32)

    return pl.pallas_call(
        body, name="sum_chips",
        grid_spec=pltpu.PrefetchScalarGridSpec(
            num_scalar_prefetch=1, grid=(nb,),
            in_specs=[pl.BlockSpec((None, tr, cols), lambda i, s_ref: (s_ref[0], i, 0)),
                      pl.BlockSpec((3, tr, cols), lambda i, s_ref: (0, i, 0))],
            out_specs=pl.BlockSpec((tr, cols), lambda i, s_ref: (s_ref[1] * nb + i, 0))),
        out_shape=_sds((2 * hr, cols), F32), compiler_params=_params(1, 0))(sc_arr, part, got)


def _cast_into_slot(x2d, sc_arr):
    rows, cols = x2d.shape
    tr = _row_tile(rows, cols)

    def body(s_ref, x_ref, o_ref):
        o_ref[...] = x_ref[...].astype(BF16)

    return pl.pallas_call(
        body, name="cast_into_slot",
        grid_spec=pltpu.PrefetchScalarGridSpec(
            num_scalar_prefetch=1, grid=(rows // tr,),
            in_specs=[pl.BlockSpec((tr, cols), lambda i, s_ref: (i, 0))],
            out_specs=pl.BlockSpec((None, tr, cols), lambda i, s_ref: (s_ref[0], i, 0))),
        out_shape=_sds((N_SHARD, rows, cols), BF16), compiler_params=_params(1, 0))(sc_arr, x2d)


def _adam_math(w, g, m, v):
    m = ADAM_B1 * m + (1.0 - ADAM_B1) * g
    v = ADAM_B2 * v + (1.0 - ADAM_B2) * (g * g)
    m_hat = m / (1.0 - ADAM_B1 ** ADAM_STEP)
    v_hat = v / (1.0 - ADAM_B2 ** ADAM_STEP)
    delta = -ADAM_LR * (m_hat / (jnp.sqrt(v_hat) + ADAM_EPS) + ADAM_WD * w)
    return delta, m, v


def _adamw(w, g, m, v):
    rows, cols = w.shape
    tr = _tile(rows, max(8, (1 << 19) // cols), 8)

    def body(w_ref, g_ref, m_ref, v_ref, go_ref, d_ref, mo_ref, vo_ref):
        g_ = g_ref[...]
        go_ref[...] = g_
        d_ref[...], mo_ref[...], vo_ref[...] = _adam_math(w_ref[...], g_, m_ref[...], v_ref[...])

    blk = pl.BlockSpec((tr, cols), lambda i: (i, 0))
    return pl.pallas_call(
        body, name="adamw", grid=(rows // tr,), in_specs=[blk] * 4, out_specs=[blk] * 4,
        out_shape=[_sds((rows, cols), F32)] * 4, compiler_params=_params(1, 0))(w, g, m, v)


def _adamw_small(gathered, w, m, v):
    rows, cols = w.shape

    def body(a_ref, w_ref, m_ref, v_ref, go_ref, d_ref, mo_ref, vo_ref):
        g_ = a_ref[pl.ds(0, rows), :]
        for k in range(1, 8):
            g_ = g_ + a_ref[pl.ds(k * rows, rows), :]
        go_ref[...] = g_
        d_ref[...], mo_ref[...], vo_ref[...] = _adam_math(w_ref[...], g_, m_ref[...], v_ref[...])

    vmem = pl.BlockSpec(memory_space=pltpu.VMEM)
    return pl.pallas_call(
        body, name="adamw_small", in_specs=[vmem] * 4, out_specs=[vmem] * 4,
        out_shape=[_sds((rows, cols), F32)] * 4)(gathered, w, m, v)


BIG = ("ffn1_w_in", "ffn1_w_out", "mix_w_in", "w_branch_a", "w_branch_b", "mix_w_out", "ffn2_w_in", "ffn2_w_out")
BIG_KEY = {"ffn1_w_in": "ffn1_in", "ffn1_w_out": "ffn1_out", "mix_w_in": "mix_in", "w_branch_a": "wa",
           "w_branch_b": "wb", "mix_w_out": "mix_out", "ffn2_w_in": "ffn2_in", "ffn2_w_out": "ffn2_out"}
SMALL = ("ln1_g", "ln1_b", "hgrn_lb_fwd", "hgrn_lb_bwd", "hgrn_norm_g", "ln2_g", "ln2_b", "ln3_g", "ln3_b")
ORDER = ("ffn1_w_in", "ffn1_w_out", "ln1_g", "ln1_b", "mix_w_in", "hgrn_lb_fwd", "hgrn_lb_bwd", "hgrn_norm_g",
         "w_branch_a", "w_branch_b", "mix_w_out", "ln2_g", "ln2_b", "ffn2_w_in", "ffn2_w_out", "ln3_g", "ln3_b")
SMALL_ROWS = 16


def _pack_small(d):
    rows = jnp.concatenate([d[k].reshape(-1, d[k].shape[-1]) for k in SMALL], axis=0)
    return jnp.pad(rows, ((0, SMALL_ROWS - rows.shape[0]), (0, 0)))


def _unpack_small(a, like):
    out, r = {}, 0
    for k in SMALL:
        n = like[k].shape[0]
        out[k] = a[r:r + n].reshape(like[k].shape)
        r += n
    return out


def kernel(x, ffn1_w_in, ffn1_w_out, ln1_g, ln1_b, mix_w_in, hgrn_lb_fwd, hgrn_lb_bwd, hgrn_norm_g, w_branch_a, w_branch_b, mix_w_out, ln2_g, ln2_b, ffn2_w_in, ffn2_w_out, ln3_g, ln3_b, loss_target, m_ffn1_w_in, m_ffn1_w_out, m_ln1_g, m_ln1_b, m_mix_w_in, m_hgrn_lb_fwd, m_hgrn_lb_bwd, m_hgrn_norm_g, m_w_branch_a, m_w_branch_b, m_mix_w_out, m_ln2_g, m_ln2_b, m_ffn2_w_in, m_ffn2_w_out, m_ln3_g, m_ln3_b, v_ffn1_w_in, v_ffn1_w_out, v_ln1_g, v_ln1_b, v_mix_w_in, v_hgrn_lb_fwd, v_hgrn_lb_bwd, v_hgrn_norm_g, v_w_branch_a, v_w_branch_b, v_mix_w_out, v_ln2_g, v_ln2_b, v_ffn2_w_in, v_ffn2_w_out, v_ln3_g, v_ln3_b):
    wts = dict(ffn1_w_in=ffn1_w_in, ffn1_w_out=ffn1_w_out, ln1_g=ln1_g, ln1_b=ln1_b, mix_w_in=mix_w_in,
               hgrn_lb_fwd=hgrn_lb_fwd, hgrn_lb_bwd=hgrn_lb_bwd, hgrn_norm_g=hgrn_norm_g, w_branch_a=w_branch_a,
               w_branch_b=w_branch_b, mix_w_out=mix_w_out, ln2_g=ln2_g, ln2_b=ln2_b, ffn2_w_in=ffn2_w_in,
               ffn2_w_out=ffn2_w_out, ln3_g=ln3_g, ln3_b=ln3_b)
    mom = dict(ffn1_w_in=m_ffn1_w_in, ffn1_w_out=m_ffn1_w_out, ln1_g=m_ln1_g, ln1_b=m_ln1_b, mix_w_in=m_mix_w_in,
               hgrn_lb_fwd=m_hgrn_lb_fwd, hgrn_lb_bwd=m_hgrn_lb_bwd, hgrn_norm_g=m_hgrn_norm_g,
               w_branch_a=m_w_branch_a, w_branch_b=m_w_branch_b, mix_w_out=m_mix_w_out, ln2_g=m_ln2_g, ln2_b=m_ln2_b,
               ffn2_w_in=m_ffn2_w_in, ffn2_w_out=m_ffn2_w_out, ln3_g=m_ln3_g, ln3_b=m_ln3_b)
    var = dict(ffn1_w_in=v_ffn1_w_in, ffn1_w_out=v_ffn1_w_out, ln1_g=v_ln1_g, ln1_b=v_ln1_b, mix_w_in=v_mix_w_in,
               hgrn_lb_fwd=v_hgrn_lb_fwd, hgrn_lb_bwd=v_hgrn_lb_bwd, hgrn_norm_g=v_hgrn_norm_g,
               w_branch_a=v_w_branch_a, w_branch_b=v_w_branch_b, mix_w_out=v_mix_w_out, ln2_g=v_ln2_g, ln2_b=v_ln2_b,
               ffn2_w_in=v_ffn2_w_in, ffn2_w_out=v_ffn2_w_out, ln3_g=v_ln3_g, ln3_b=v_ln3_b)
    shard = (2 * lax.axis_index("x") + lax.axis_index("y")).astype(jnp.int32)
    sc_arr = jnp.stack([shard, lax.axis_index("c").astype(jnp.int32)])

    shard2d = {k: wts[k].reshape(wts[k].shape[1:]) for k in BIG}
    fulls = {BIG_KEY[k]: _cast_into_slot(shard2d[k], sc_arr) for k in BIG}
    p = {k: wts[k] for k in SMALL}

    loss, grad_x, grads, gp = _local_step(x, loss_target, fulls, p, sc_arr)
    loss = lax.psum(loss[0, 0], ("x", "y", "c"))

    out_g, out_d, out_m, out_v = {}, {}, {}, {}
    for k in BIG:
        g = grads[BIG_KEY[k]]
        shp = wts[k].shape
        res = _adamw(shard2d[k], g, mom[k].reshape(shp[1:]), var[k].reshape(shp[1:]))
        out_g[k], out_d[k], out_m[k], out_v[k] = [a.reshape(shp) for a in res]

    gathered = _gather_rows(_pack_small(gp))
    res = _adamw_small(gathered, _pack_small(wts), _pack_small(mom), _pack_small(var))
    for dst, a in zip((out_g, out_d, out_m, out_v), res):
        dst.update(_unpack_small(a, wts))

    return (loss, grad_x, *[out_g[k] for k in ORDER], *[out_d[k] for k in ORDER],
            *[out_m[k] for k in ORDER], *[out_v[k] for k in ORDER])
```

```python
import functools

import numpy as np
import jax
import jax.numpy as jnp
from jax import lax
from jax.experimental import pallas as pl
from jax.experimental.pallas import tpu as pltpu

F32 = jnp.float32
BF16 = jnp.bfloat16

HEAD = 128
ATTN_GROUPS = ((128, 1), (512, 4), (2048, 16))
ATTN_HEADS = 4
N_GROUPS = len(ATTN_GROUPS)
QKV_W = N_GROUPS * 3 * ATTN_HEADS * HEAD
ATTN_OUT = ATTN_HEADS * HEAD
ROPE_THETA = 500000.0
ROPE_DIM = HEAD // 4
HGRN_CHUNK = 32
HGRN_FWD_HEADS = 2
HGRN_BWD_HEADS = 2
HGRN_SUB = 4
ALPHA = 2.0 ** 0.25
LN_EPS = 1e-5
NEG_INF = -1e30
ADAM_LR, ADAM_B1, ADAM_B2, ADAM_EPS, ADAM_WD, ADAM_STEP = 0.001, 0.9, 0.999, 1e-08, 0.01, 10

N_SHARD = 4
VMEM_LIMIT = 56 * 1024 * 1024

NN = ((1,), (0,))
NT = ((1,), (1,))
TN = ((0,), (0,))


def _dot(a, b, dims, precision=None):
    return lax.dot_general(a, b, (dims, ((), ())), preferred_element_type=F32, precision=precision)


def _tile(n, pref, mult=128):
    best = None
    for t in range(mult, min(n, pref) + 1, mult):
        if n % t == 0:
            best = t
    return n if best is None else best


def _params(n_parallel, n_arbitrary):
    return pltpu.CompilerParams(
        dimension_semantics=("parallel",) * n_parallel + ("arbitrary",) * n_arbitrary,
        vmem_limit_bytes=VMEM_LIMIT)


def _sigmoid(x):
    return 1.0 / (1.0 + jnp.exp(-x))


def _silu(x):
    return x * _sigmoid(x)


class _Exchange:
    def __init__(self, ins, outs, aliases, sems, start, finish):
        self.ins, self.outs, self.aliases, self.sems = list(ins), list(outs), dict(aliases), list(sems)
        self.start, self.finish = start, finish


def _run(body, name, grid, ins, in_specs, outs, out_specs, scratch, n_arbitrary, side=None):
    n_in, n_out, n_scr = len(ins), len(outs), len(scratch)
    if side is None:
        return pl.pallas_call(
            body, name=name, grid=grid, in_specs=in_specs, out_specs=out_specs, out_shape=outs,
            scratch_shapes=scratch, compiler_params=_params(len(grid) - n_arbitrary, n_arbitrary))(*ins)
    s_in, s_out = len(side.ins), len(side.outs)
    i1 = n_in + s_in
    o1 = i1 + n_out
    o2 = o1 + s_out
    c1 = o2 + n_scr

    def wrapped(*refs):
        s_refs = (refs[n_in:i1], refs[o1:o2], refs[c1:])
        ids = [pl.program_id(a) for a in range(len(grid))]
        first = functools.reduce(jnp.logical_and, [i == 0 for i in ids])
        last = functools.reduce(jnp.logical_and, [i == g - 1 for i, g in zip(ids, grid)])

        @pl.when(first)
        def _():
            side.start(*s_refs)

        body(*refs[:n_in], *refs[i1:o1], *refs[o2:c1])

        @pl.when(last)
        def _():
            side.finish(*s_refs)

    res = pl.pallas_call(
        wrapped, name=name, grid=grid, in_specs=list(in_specs) + [ANY] * s_in,
        out_specs=list(out_specs) + [ANY] * s_out, out_shape=list(outs) + side.outs,
        scratch_shapes=list(scratch) + side.sems,
        input_output_aliases={n_in + a: n_out + b for a, b in side.aliases.items()},
        compiler_params=_params(0, len(grid)))(*ins, *side.ins)
    return res[:n_out], res[n_out:]


def _gemm(name, grid, ins, in_specs, outs, out_specs, accs, dot_fn, epi_fn, side=None):
    n_in, n_out, n_k = len(ins), len(outs), grid[-1]

    def body(*refs):
        in_refs, out_refs, acc_refs = refs[:n_in], refs[n_in:n_in + n_out], refs[n_in + n_out:]
        k = pl.program_id(len(grid) - 1)

        @pl.when(k == 0)
        def _():
            for a in acc_refs:
                a[...] = jnp.zeros(a.shape, F32)

        dot_fn(in_refs, acc_refs)

        @pl.when(k == n_k - 1)
        def _():
            epi_fn(in_refs, acc_refs, out_refs)

    return _run(body, name, grid, ins, in_specs, outs, out_specs, [pltpu.VMEM(s, F32) for s in accs], 1, side)


def _sds(shape, dtype):
    return jax.ShapeDtypeStruct(shape, dtype)


def _gemm_rows(name, grid, ins, in_specs, outs, out_specs, tm, n_sub, fn, side=None):
    n_in, sub = len(ins), tm // n_sub

    def body(*refs):
        for s in range(n_sub):
            fn(pl.ds(s * sub, sub), refs[:n_in], refs[n_in:])

    return _run(body, name, grid, ins, in_specs, outs, out_specs, [], 0, side)


def _ffn_up(xb, w3, side=None):
    t, d = xb.shape
    nf = w3.shape[2]
    f = 2 * nf
    tm = _tile(t, 512, 8)
    tn = nf // 2 if nf % 256 == 0 else nf
    ns = nf // tn

    def fn(rows, r, out):
        x = r[0][rows, :]
        g, u = _dot(x, r[1][...], NN), _dot(x, r[2][...], NN)
        out[0][0, rows, :] = g.astype(BF16)
        out[0][1, rows, :] = u.astype(BF16)
        out[1][rows, :] = (_silu(g) * u).astype(BF16)

    return _gemm_rows(
        "ffn_up", (t // tm, 2, ns), [xb, w3, w3],
        [pl.BlockSpec((tm, d), lambda i, j, h: (i, 0)),
         pl.BlockSpec((None, d, tn), lambda i, j, h: (j, 0, h)),
         pl.BlockSpec((None, d, tn), lambda i, j, h: (j + 2, 0, h))],
        [_sds((2, t, f), BF16), _sds((t, f), BF16)],
        [pl.BlockSpec((2, tm, tn), lambda i, j, h: (0, i, j * ns + h)),
         pl.BlockSpec((tm, tn), lambda i, j, h: (i, j * ns + h))],
        tm, 2, fn, side)


def _down_ln(a, w, resid, coef, g, b, side=None):
    t, kd = a.shape
    d = w.shape[1]
    tm, tk = _tile(t, 512, 8), _tile(kd, 704)

    def dot_fn(r, acc):
        acc[0][...] += _dot(r[0][...], r[1][...], NN)

    def epi_fn(r, acc, out):
        v = ALPHA * r[2][...] + coef * acc[0][...]
        mu = jnp.mean(v, axis=-1, keepdims=True)
        c = v - mu
        var = jnp.mean(c * c, axis=-1, keepdims=True)
        rstd = lax.rsqrt(var + LN_EPS)
        xhat = c * rstd
        h = xhat * r[3][...] + r[4][...]
        out[0][...] = h
        out[1][...] = h.astype(BF16)
        out[2][...] = xhat
        out[3][...] = rstd

    row = pl.BlockSpec((tm, d), lambda i, k: (i, 0))
    vec = pl.BlockSpec((1, d), lambda i, k: (0, 0))
    return _gemm(
        "down_ln", (t // tm, kd // tk), [a, w, resid, g, b],
        [pl.BlockSpec((tm, tk), lambda i, k: (i, k)), pl.BlockSpec((tk, d), lambda i, k: (k, 0)), row, vec, vec],
        [_sds((t, d), F32), _sds((t, d), BF16), _sds((t, d), F32), _sds((t, 1), F32)],
        [row, row, row, pl.BlockSpec((tm, 1), lambda i, k: (i, 0))],
        [(tm, d)], dot_fn, epi_fn, side)


def _only(res, side):
    return res[0] if side is None else (res[0][0], res[1])


def _mm_w3(a, w3, side=None):
    t, kd = a.shape
    n = w3.shape[2]
    tm, tk = _tile(t, 512, 8), _tile(kd, 1024)

    def body(a_ref, w_ref, o_ref):
        k = pl.program_id(2)
        p = _dot(a_ref[...], w_ref[...], NN)

        @pl.when(k == 0)
        def _():
            o_ref[...] = p

        @pl.when(k > 0)
        def _():
            o_ref[...] += p

    return _only(_run(
        body, "mm_w3", (t // tm, N_SHARD, kd // tk), [a, w3],
        [pl.BlockSpec((tm, tk), lambda i, j, k: (i, k)), pl.BlockSpec((None, tk, n), lambda i, j, k: (j, k, 0))],
        [_sds((t, N_SHARD * n), F32)], [pl.BlockSpec((tm, n), lambda i, j, k: (i, j))], [], 1, side), side)


def _mm_w2(a, w, out_dtype):
    t, kd = a.shape
    n = w.shape[1]
    tm, tn = _tile(t, 512, 8), _tile(n, 1024)

    def fn(rows, r, out):
        out[0][rows, :] = _dot(r[0][rows, :], r[1][...], NN).astype(out_dtype)

    return _gemm_rows(
        "mm_w2", (t // tm, n // tn), [a, w],
        [pl.BlockSpec((tm, kd), lambda i, j: (i, 0)), pl.BlockSpec((kd, tn), lambda i, j: (0, j))],
        [_sds((t, n), out_dtype)], [pl.BlockSpec((tm, tn), lambda i, j: (i, j))], tm, 1, fn)[0]


def _branch_gate(ob, wb3, ya, proj, gate_col):
    t, kd = ob.shape
    n = wb3.shape[2]
    d = N_SHARD * n
    tm = _tile(t, 512, 8)
    ga0, gb0 = gate_col // n, (gate_col + d) // n

    def fn(rows, r, out):
        yb = _dot(r[0][rows, :], r[1][...], NN)
        out[0][rows, :] = yb
        out[1][rows, :] = (_sigmoid(r[3][rows, :]) * r[2][rows, :] + _sigmoid(r[4][rows, :]) * yb).astype(BF16)

    blk = pl.BlockSpec((tm, n), lambda i, j: (i, j))
    return _gemm_rows(
        "branch_gate", (t // tm, N_SHARD), [ob, wb3, ya, proj, proj],
        [pl.BlockSpec((tm, kd), lambda i, j: (i, 0)), pl.BlockSpec((None, kd, n), lambda i, j: (j, 0, 0)), blk,
         pl.BlockSpec((tm, n), lambda i, j: (i, ga0 + j)), pl.BlockSpec((tm, n), lambda i, j: (i, gb0 + j))],
        [_sds((t, d), F32), _sds((t, d), BF16)], [blk, blk], tm, 2, fn)


def _swiglu_bwd(dyb, w, u3, side=None):
    t, d = dyb.shape
    f = w.shape[0]
    tm, tr = _tile(t, 1024, 8), _tile(f, 512)

    def fn(rows, r, out):
        da = _dot(r[0][rows, :], r[1][...], NT)
        g, u = r[2][0, rows, :].astype(F32), r[2][1, rows, :].astype(F32)
        s = _sigmoid(g)
        out[0][0, rows, :] = (da * u * s * (1.0 + g * (1.0 - s))).astype(BF16)
        out[0][1, rows, :] = (da * g * s).astype(BF16)

    ublk = pl.BlockSpec((2, tm, tr), lambda i, j: (0, i, j))
    return _only(_gemm_rows(
        "swiglu_bwd", (t // tm, f // tr), [dyb, w, u3],
        [pl.BlockSpec((tm, d), lambda i, j: (i, 0)), pl.BlockSpec((tr, d), lambda i, j: (j, 0)), ublk],
        [_sds((2, t, f), BF16)], [ublk], tm, 4, fn, side), side)


def _ffn_dx(du3, w3, resid, side=None):
    t = du3.shape[1]
    d, nf = w3.shape[1], w3.shape[2]
    tm, tr = _tile(t, 512, 8), _tile(d, 1024)

    def dot_fn(r, acc):
        acc[0][...] += _dot(r[0][...], r[1][...], NT)

    def epi_fn(r, acc, out):
        out[0][...] = acc[0][...] + r[2][...]

    blk = pl.BlockSpec((tm, tr), lambda i, j, k: (i, j))
    return _only(_gemm(
        "ffn_dx", (t // tm, d // tr, N_SHARD), [du3, w3, resid],
        [pl.BlockSpec((None, tm, nf), lambda i, j, k: (k // 2, i, k % 2)),
         pl.BlockSpec((None, tr, nf), lambda i, j, k: (k, j, 0)), blk],
        [_sds((t, d), F32)], [blk], [(tm, tr)], dot_fn, epi_fn, side), side)


def _nt_w3(dy, w3, resid, side=None):
    t = dy.shape[0]
    kd, n = w3.shape[1], w3.shape[2]
    tm, tr = _tile(t, 512, 8), _tile(kd, 1024)
    has_res = resid is not None

    def dot_fn(r, acc):
        acc[0][...] += _dot(r[0][...], r[1][...], NT)

    def epi_fn(r, acc, out):
        v = acc[0][...]
        if has_res:
            v = v + r[2][...]
        out[0][...] = v

    blk = pl.BlockSpec((tm, tr), lambda i, j, k: (i, j))
    return _only(_gemm(
        "nt_w3", (t // tm, kd // tr, N_SHARD), [dy, w3] + ([resid] if has_res else []),
        [pl.BlockSpec((tm, n), lambda i, j, k: (i, k)), pl.BlockSpec((None, tr, n), lambda i, j, k: (k, j, 0))]
        + ([blk] if has_res else []),
        [_sds((t, kd), F32)], [blk], [(tm, tr)], dot_fn, epi_fn, side), side)


def _nt_w2(dy, w, out_dtype):
    t, n = dy.shape
    r_ = w.shape[0]
    tm, tr = _tile(t, 512, 8), _tile(r_, 1024)

    def fn(rows, r, out):
        out[0][rows, :] = _dot(r[0][rows, :], r[1][...], NT).astype(out_dtype)

    return _gemm_rows(
        "nt_w2", (t // tm, r_ // tr), [dy, w],
        [pl.BlockSpec((tm, n), lambda i, j: (i, 0)), pl.BlockSpec((tr, n), lambda i, j: (j, 0))],
        [_sds((t, r_), out_dtype)], [pl.BlockSpec((tm, tr), lambda i, j: (i, j))], tm, 1, fn)[0]


def _gate_bwd(dmix, w, proj, ya, yb, gate_col):
    t, n = dmix.shape
    d = w.shape[0]
    tm, tr = _tile(t, 512, 8), _tile(d, 512)
    ga0, gb0 = gate_col // tr, (gate_col + d) // tr
    nb = d // tr

    def fn(rows, r, out):
        dz = _dot(r[0][rows, :], r[1][...], NT)
        ga, gb = _sigmoid(r[2][rows, :]), _sigmoid(r[3][rows, :])
        out[0][rows, :] = (dz * ga).astype(BF16)
        out[1][rows, :] = (dz * gb).astype(BF16)
        out[2][rows, :] = (dz * r[4][rows, :] * ga * (1.0 - ga)).astype(BF16)
        out[3][rows, :] = (dz * r[5][rows, :] * gb * (1.0 - gb)).astype(BF16)

    blk = pl.BlockSpec((tm, tr), lambda i, j: (i, j))
    return _gemm_rows(
        "gate_bwd", (t // tm, nb), [dmix, w, proj, proj, ya, yb],
        [pl.BlockSpec((tm, n), lambda i, j: (i, 0)), pl.BlockSpec((tr, n), lambda i, j: (j, 0)),
         pl.BlockSpec((tm, tr), lambda i, j: (i, ga0 + j)), pl.BlockSpec((tm, tr), lambda i, j: (i, gb0 + j)),
         blk, blk],
        [_sds((t, d), BF16), _sds((t, d), BF16), _sds((t, d), BF16), _sds((t, d), BF16)],
        [blk, blk, blk, blk], tm, 2, fn)


def _tn_w3(a, dy, dy_map, n, side=None):
    t, kd = a.shape
    tm, tkk = _tile(t, (6 << 20) // n, 8), _tile(kd, 512)

    def dot_fn(r, acc):
        acc[0][...] += _dot(r[0][...], r[1][...], TN)

    def epi_fn(r, acc, out):
        out[0][...] = acc[0][...].astype(BF16)

    dy_block = (tm, n) if dy.ndim == 2 else (None, tm, n)
    return _only(_gemm(
        "tn_w3", (kd // tkk, N_SHARD, t // tm), [a, dy],
        [pl.BlockSpec((tm, tkk), lambda i, j, m: (m, i)), pl.BlockSpec(dy_block, lambda i, j, m: dy_map(j, m))],
        [_sds((N_SHARD, kd, n), BF16)], [pl.BlockSpec((None, tkk, n), lambda i, j, m: (j, i, 0))],
        [(tkk, n)], dot_fn, epi_fn, side), side)


def _tn_w2(a, dy):
    t, kd = a.shape
    n = dy.shape[1]
    tkk, tn = _tile(kd, 512), _tile(n, 2048)
    tm = _tile(t, (4 << 20) // tn, 8)

    def dot_fn(r, acc):
        acc[0][...] += _dot(r[0][...], r[1][...], TN)

    def epi_fn(r, acc, out):
        out[0][...] = acc[0][...].astype(BF16)

    return _gemm(
        "tn_w2", (kd // tkk, n // tn, t // tm), [a, dy],
        [pl.BlockSpec((tm, tkk), lambda i, j, m: (m, i)), pl.BlockSpec((tm, tn), lambda i, j, m: (m, j))],
        [_sds((kd, n), BF16)], [pl.BlockSpec((tkk, tn), lambda i, j, m: (i, j))],
        [(tkk, tn)], dot_fn, epi_fn)[0]


def _ln_bwd(dh_parts, xhat, rstd, g, coef, target=None):
    t, d = xhat.shape
    tm = _tile(t, 256, 8)
    n_parts = len(dh_parts)
    with_loss = target is not None
    ins = list(dh_parts) + [xhat, rstd, g] + ([target] if with_loss else [])

    def body(*refs):
        parts = refs[:n_parts]
        xh_ref, rstd_ref, g_ref = refs[n_parts:n_parts + 3]
        o = n_parts + 3 + (1 if with_loss else 0)
        dres_ref, dyb_ref, dg_ref, db_ref, loss_ref = refs[o:o + 5]
        i = pl.program_id(0)
        dh = parts[0][...]
        for p in parts[1:]:
            dh = dh + p[...]
        if with_loss:
            e = dh - refs[o - 1][...]
            part = 0.5 * jnp.sum(jnp.sum(e * e, axis=-1, keepdims=True) * (1.0 / d), axis=0, keepdims=True)
            dh = e * (1.0 / d)
        else:
            part = jnp.zeros((1, 1), F32)
        xh = xh_ref[...]
        dxh = dh * g_ref[...]
        m1 = jnp.mean(dxh, axis=-1, keepdims=True)
        m2 = jnp.mean(dxh * xh, axis=-1, keepdims=True)
        dv = rstd_ref[...] * (dxh - m1 - xh * m2)
        dres_ref[...] = ALPHA * dv
        dyb_ref[...] = (coef * dv).astype(BF16)

        @pl.when(i == 0)
        def _():
            dg_ref[...] = jnp.zeros(dg_ref.shape, F32)
            db_ref[...] = jnp.zeros(db_ref.shape, F32)
            loss_ref[...] = jnp.zeros(loss_ref.shape, F32)

        dg_ref[...] += jnp.sum(dh * xh, axis=0, keepdims=True)
        db_ref[...] += jnp.sum(dh, axis=0, keepdims=True)
        loss_ref[...] += jnp.broadcast_to(part, loss_ref.shape)

    row = pl.BlockSpec((tm, d), lambda i: (i, 0))
    vec = pl.BlockSpec((1, d), lambda i: (0, 0))
    return pl.pallas_call(
        body, name="ln_bwd", grid=(t // tm,),
        in_specs=[row] * n_parts + [row, pl.BlockSpec((tm, 1), lambda i: (i, 0)), vec] + ([row] if with_loss else []),
        out_specs=[row, row, vec, vec, pl.BlockSpec((1, 128), lambda i: (0, 0))],
        out_shape=[_sds((t, d), F32), _sds((t, d), BF16), _sds((1, d), F32), _sds((1, d), F32), _sds((1, 128), F32)],
        compiler_params=_params(0, 1))(*ins)


def _cast_bf16(x2d):
    t, d = x2d.shape
    tm = _tile(t, 512, 8)

    def body(x_ref, o_ref):
        o_ref[...] = x_ref[...].astype(BF16)

    row = pl.BlockSpec((tm, d), lambda i: (i, 0))
    return pl.pallas_call(body, name="cast_bf16", grid=(t // tm,), in_specs=[row], out_specs=row,
                          out_shape=_sds((t, d), BF16), compiler_params=_params(1, 0))(x2d)


def _lower_bound(table):
    t0, t1 = table[0:1, :], table[1:2, :]
    m = jnp.maximum(t0, t1)
    e0, e1 = jnp.exp(t0 - m), jnp.exp(t1 - m)
    return e0 / (e0 + e1)


def _chunk_tri(rows, upper):
    r = lax.broadcasted_iota(jnp.int32, (rows, rows), 0)
    s = lax.broadcasted_iota(jnp.int32, (rows, rows), 1)
    shift = HGRN_CHUNK.bit_length() - 1
    same = lax.shift_right_logical(r, shift) == lax.shift_right_logical(s, shift)
    return same & ((r <= s) if upper else (r >= s))


def _tri_apply(x, upper):
    tri = _chunk_tri(x.shape[0], upper).astype(F32).astype(BF16)
    hi = x.astype(BF16)
    r1 = x - hi.astype(F32)
    mid = r1.astype(BF16)
    lo = (r1 - mid.astype(F32)).astype(BF16)
    return _dot(tri, hi, NN) + _dot(tri, mid, NN) + _dot(tri, lo, NN)


@functools.partial(jax.custom_vjp, nondiff_argnums=(1,))
def _chunk_cumsum(x, upper):
    return _tri_apply(x, upper)


def _chunk_cumsum_fwd(x, upper):
    return _tri_apply(x, upper), None


def _chunk_cumsum_bwd(upper, _, g):
    return (_tri_apply(g, not upper),)


_chunk_cumsum.defvjp(_chunk_cumsum_fwd, _chunk_cumsum_bwd)


def _hgrn_blocks(units, uppers):
    c = HGRN_CHUNK
    rows = units[0][0].shape[0]
    n_sub = rows // c
    ids = range(len(units))
    chunk = lax.shift_right_logical(lax.broadcasted_iota(jnp.int32, (rows, HEAD), 0), c.bit_length() - 1)
    zero = jnp.zeros((rows, HEAD), BF16)

    def expand(a):
        return jnp.concatenate([jnp.where(chunk == n, a, zero) for n in range(n_sub)], axis=1)

    fs = [u[3] + (1.0 - u[3]) * _sigmoid(u[1]) for u in units]
    lgs = [jnp.log(f) for f in fs]
    cums = [_chunk_cumsum(lgs[i], uppers[i]) for i in ids]
    tots = [[jnp.sum(lg[n * c:(n + 1) * c], axis=0, keepdims=True) for n in range(n_sub)] for lg in lgs]
    qd, kd, ke, vb = [], [], [], []
    for i in ids:
        totb = jnp.concatenate([jnp.broadcast_to(t, (c, HEAD)) for t in tots[i]], axis=0)
        kk = 1.0 - fs[i]
        qd.append((_silu(units[i][0]) * jnp.exp(cums[i])).astype(BF16))
        kd.append((kk * jnp.exp(-cums[i])).astype(BF16))
        ke.append((kk * jnp.exp(totb - cums[i])).astype(BF16))
        vb.append(units[i][2].astype(BF16))
    scores = [_dot(qd[i], kd[i], NT) for i in ids]
    kvs = [_dot(vb[i], expand(ke[i]), TN) for i in ids]
    outs = []
    for i in ids:
        a = jnp.where(_chunk_tri(rows, uppers[i]), scores[i], 0.0).astype(BF16)
        st = units[i][4]
        entering = [None] * n_sub
        for n in (range(n_sub - 1, -1, -1) if uppers[i] else range(n_sub)):
            entering[n] = st.astype(BF16)
            st = st * jnp.exp(tots[i][n]) + kvs[i][:, n * HEAD:(n + 1) * HEAD]
        outs.append((a, jnp.concatenate(entering, axis=1), st))
    res = []
    for i in ids:
        a, entering, st = outs[i]
        res.append((_dot(a, vb[i], NN) + _dot(expand(qd[i]), entering, NT), st))
    return res


def _hgrn_fwd(proj3, lb_f, lb_b, norm_g, d, side=None):
    b_, s_, _ = proj3.shape
    nh = d // HEAD
    hb = HGRN_FWD_HEADS
    wid = hb * HEAD
    rows = HGRN_CHUNK * HGRN_SUB
    n_blk = s_ // rows

    def body(hq_ref, hff_ref, hfb_ref, hi_ref, hog_ref, lbf_ref, lbb_ref, g_ref, oraw_ref, out_ref, of_ref, ob_ref):
        lbf, lbb = _lower_bound(lbf_ref[...]), _lower_bound(lbb_ref[...])

        def step(j, sts):
            rf = pl.ds(pl.multiple_of(j * rows, rows), rows)
            rb = pl.ds(pl.multiple_of((n_blk - 1 - j) * rows, rows), rows)
            units = []
            for hh in range(hb):
                cs = slice(hh * HEAD, (hh + 1) * HEAD)
                units.append((hq_ref[rf, cs], hff_ref[rf, cs], hi_ref[rf, cs], lbf[:, cs], sts[2 * hh]))
                units.append((hq_ref[rb, cs], hfb_ref[rb, cs], hi_ref[rb, cs], lbb[:, cs], sts[2 * hh + 1]))
            res = _hgrn_blocks(units, [False, True] * hb)
            for hh in range(hb):
                cs = slice(hh * HEAD, (hh + 1) * HEAD)
                of_ref[rf, cs] = res[2 * hh][0]
                ob_ref[rb, cs] = res[2 * hh + 1][0]
            return tuple(r[1] for r in res)

        z = jnp.zeros((HEAD, HEAD), F32)
        lax.fori_loop(0, n_blk, step, (z,) * (2 * hb))
        for hh in range(hb):
            cs = slice(hh * HEAD, (hh + 1) * HEAD)
            o = of_ref[:, cs] + ob_ref[:, cs]
            oraw_ref[:, cs] = o
            on = o * lax.rsqrt(jnp.mean(o * o, axis=-1, keepdims=True) + LN_EPS)
            out_ref[:, cs] = (on * g_ref[:, cs] * _silu(hog_ref[:, cs])).astype(BF16)

    def col(k):
        return pl.BlockSpec((None, s_, wid), lambda h, b: (b, 0, k * (nh // hb) + h))

    tab = pl.BlockSpec((2, wid), lambda h, b: (0, h))
    oblk = pl.BlockSpec((None, s_, wid), lambda h, b: (b, 0, h))
    return _run(
        body, "hgrn_fwd", (nh // hb, b_), [proj3, proj3, proj3, proj3, proj3, lb_f, lb_b, norm_g],
        [col(0), col(1), col(2), col(3), col(4), tab, tab, pl.BlockSpec((1, wid), lambda h, b: (0, h))],
        [_sds((b_, s_, d), F32), _sds((b_, s_, d), BF16)], [oblk, oblk],
        [pltpu.VMEM((s_, wid), F32), pltpu.VMEM((s_, wid), F32)], 0, side)


def _hgrn_bwd(proj3, lb_f, lb_b, norm_g, o_raw, do_a, d, side=None):
    b_, s_, _ = proj3.shape
    nh = d // HEAD
    hb = HGRN_BWD_HEADS
    wid = hb * HEAD
    rows = HGRN_CHUNK * HGRN_SUB
    n_blk = s_ // rows

    def body(hq_ref, hff_ref, hfb_ref, hi_ref, hog_ref, lbf_ref, lbb_ref, g_ref, oraw_ref, doa_ref,
             dq_ref, dff_ref, dfb_ref, di_ref, dog_ref, dlbf_ref, dlbb_ref, dg_ref,
             st_ref, dor_ref, dq2_ref, di2_ref):
        b = pl.program_id(1)
        tab_f, tab_b = lbf_ref[...], lbb_ref[...]
        lbf, lbb = _lower_bound(tab_f), _lower_bound(tab_b)

        dg_parts = []
        for hh in range(hb):
            cs = slice(hh * HEAD, (hh + 1) * HEAD)
            o, doa, hog, g = oraw_ref[:, cs], doa_ref[:, cs], hog_ref[:, cs], g_ref[:, cs]
            rs = lax.rsqrt(jnp.mean(o * o, axis=-1, keepdims=True) + LN_EPS)
            on = o * rs
            sg = _sigmoid(hog)
            gate = hog * sg
            dog_ref[:, cs] = (doa * on * g * sg * (1.0 + hog * (1.0 - sg))).astype(BF16)
            don = doa * g * gate
            dor_ref[:, cs] = rs * (don - on * jnp.mean(don * on, axis=-1, keepdims=True))
            dg_parts.append(jnp.sum(doa * on * gate, axis=0, keepdims=True))

        def fwd_step(j, carry):
            jb = n_blk - 1 - j
            rf = pl.ds(pl.multiple_of(j * rows, rows), rows)
            rb = pl.ds(pl.multiple_of(jb * rows, rows), rows)
            units = []
            for hh in range(hb):
                cs = slice(hh * HEAD, (hh + 1) * HEAD)
                st_ref[2 * hh, j] = carry[2 * hh]
                st_ref[2 * hh + 1, jb] = carry[2 * hh + 1]
                units.append((hq_ref[rf, cs], hff_ref[rf, cs], hi_ref[rf, cs], lbf[:, cs], carry[2 * hh]))
                units.append((hq_ref[rb, cs], hfb_ref[rb, cs], hi_ref[rb, cs], lbb[:, cs], carry[2 * hh + 1]))
            return tuple(r[1] for r in _hgrn_blocks(units, uppers))

        uppers = [False, True] * hb
        z = jnp.zeros((HEAD, HEAD), F32)
        lax.fori_loop(0, n_blk, fwd_step, (z,) * (2 * hb))

        def bwd_step(j, carry):
            gs, dls = carry
            jf = n_blk - 1 - j
            rf = pl.ds(pl.multiple_of(jf * rows, rows), rows)
            rb = pl.ds(pl.multiple_of(j * rows, rows), rows)
            flat, cots = [], []
            for hh in range(hb):
                cs = slice(hh * HEAD, (hh + 1) * HEAD)
                flat += [hq_ref[rf, cs], hff_ref[rf, cs], hi_ref[rf, cs], lbf[:, cs], st_ref[2 * hh, jf],
                         hq_ref[rb, cs], hfb_ref[rb, cs], hi_ref[rb, cs], lbb[:, cs], st_ref[2 * hh + 1, j]]
                cots += [(dor_ref[rf, cs], gs[2 * hh]), (dor_ref[rb, cs], gs[2 * hh + 1])]
            _, vjp = jax.vjp(lambda *a: _hgrn_blocks([a[5 * i:5 * i + 5] for i in range(2 * hb)], uppers), *flat)
            grads = vjp(cots)
            gs, dls = list(gs), list(dls)
            for hh in range(hb):
                cs = slice(hh * HEAD, (hh + 1) * HEAD)
                dq, df, di, dl, gs[2 * hh] = grads[10 * hh:10 * hh + 5]
                dq_ref[rf, cs] = dq.astype(BF16)
                dff_ref[rf, cs] = df.astype(BF16)
                di_ref[rf, cs] = di.astype(BF16)
                dls[2 * hh] = dls[2 * hh] + dl
                dq, df, di, dl, gs[2 * hh + 1] = grads[10 * hh + 5:10 * hh + 10]
                dq2_ref[rb, cs] = dq
                dfb_ref[rb, cs] = df.astype(BF16)
                di2_ref[rb, cs] = di
                dls[2 * hh + 1] = dls[2 * hh + 1] + dl
            return tuple(gs), tuple(dls)

        zl = jnp.zeros((1, HEAD), F32)
        _, dls = lax.fori_loop(0, n_blk, bwd_step, ((z,) * (2 * hb), (zl,) * (2 * hb)))
        dq_ref[...] = (dq_ref[...].astype(F32) + dq2_ref[...]).astype(BF16)
        di_ref[...] = (di_ref[...].astype(F32) + di2_ref[...]).astype(BF16)

        _, vjp_tf = jax.vjp(_lower_bound, tab_f)
        _, vjp_tb = jax.vjp(_lower_bound, tab_b)

        @pl.when(b == 0)
        def _():
            dlbf_ref[...] = jnp.zeros(dlbf_ref.shape, F32)
            dlbb_ref[...] = jnp.zeros(dlbb_ref.shape, F32)
            dg_ref[...] = jnp.zeros(dg_ref.shape, F32)

        dlbf_ref[...] += vjp_tf(jnp.concatenate(dls[0::2], axis=1))[0]
        dlbb_ref[...] += vjp_tb(jnp.concatenate(dls[1::2], axis=1))[0]
        dg_ref[...] += jnp.concatenate(dg_parts, axis=1)

    def col(k):
        return pl.BlockSpec((None, s_, wid), lambda h, b: (b, 0, k * (nh // hb) + h))

    tab = pl.BlockSpec((2, wid), lambda h, b: (0, h))
    vec = pl.BlockSpec((1, wid), lambda h, b: (0, h))
    oblk = pl.BlockSpec((None, s_, wid), lambda h, b: (b, 0, h))
    seq = _sds((b_, s_, d), BF16)
    return _run(
        body, "hgrn_bwd", (nh // hb, b_), [proj3, proj3, proj3, proj3, proj3, lb_f, lb_b, norm_g, o_raw, do_a],
        [col(0), col(1), col(2), col(3), col(4), tab, tab, vec, oblk, oblk],
        [seq, seq, seq, seq, seq, _sds((2, d), F32), _sds((2, d), F32), _sds((1, d), F32)],
        [oblk, oblk, oblk, oblk, oblk, tab, tab, vec],
        [pltpu.VMEM((2 * hb, n_blk, HEAD, HEAD), F32),
         pltpu.VMEM((s_, wid), F32), pltpu.VMEM((s_, wid), F32), pltpu.VMEM((s_, wid), F32)], 1, side)


def _rope_tables(s_):
    inv = ROPE_THETA ** (-jnp.arange(0, ROPE_DIM, 2, dtype=F32) / ROPE_DIM)
    ang = jnp.arange(s_, dtype=F32)[:, None] * inv
    cos, sin = jnp.cos(ang), jnp.sin(ang)
    rest = HEAD - ROPE_DIM
    ctab = jnp.concatenate([cos, cos, jnp.ones((s_, rest), F32)], axis=1)
    stab = jnp.concatenate([-sin, sin, jnp.zeros((s_, rest), F32)], axis=1)
    half = ROPE_DIM // 2
    perm = np.zeros((HEAD, HEAD), np.float32)
    for i in range(half):
        perm[i + half, i] = 1.0
        perm[i, i + half] = 1.0
    return ctab, stab, jnp.asarray(perm)


def _attn_tile(qr, kr, v, cq, sq, ck, sk, perm, qi0, kj0, half):
    hi = lax.Precision.HIGHEST
    q = qr * cq + _dot(qr, perm, NN, precision=hi) * sq
    k = kr * ck + _dot(kr, perm, NN, precision=hi) * sk
    s = _dot(q.astype(BF16), k.astype(BF16), NT) * (HEAD ** -0.5)
    qi = qi0 + lax.broadcasted_iota(jnp.int32, s.shape, 0)
    kj = kj0 + lax.broadcasted_iota(jnp.int32, s.shape, 1)
    s = jnp.where(jnp.abs(qi - kj) <= half, s, NEG_INF)
    m = lax.stop_gradient(jnp.max(s, axis=-1, keepdims=True))
    p = jnp.exp(s - m)
    den = jnp.sum(p, axis=-1, keepdims=True)
    o = _dot(p.astype(BF16), v.astype(BF16), NN) / den
    lse = m + jnp.log(den)
    return o, jnp.broadcast_to(lse, o.shape)


def _attn_tiling(seg):
    tq = seg if seg <= 256 else 256
    kw = seg if seg <= 512 else 512
    tiles = []
    for i in range(seg // tq):
        ws = min(max(i * tq - (kw - tq) // 2, 0), seg - kw)
        tiles.append((i * tq, ws))
    return tq, kw, tiles


def _attn_specs(s_, d, g):
    c0 = (5 * d + g * 3 * ATTN_OUT) // HEAD

    def col(part):
        return pl.BlockSpec((None, s_, HEAD), lambda b, h: (b, 0, c0 + part * ATTN_HEADS + h))

    tab = pl.BlockSpec((s_, HEAD), lambda b, h: (0, 0))
    perm = pl.BlockSpec((HEAD, HEAD), lambda b, h: (0, 0))
    oblk = pl.BlockSpec((None, s_, HEAD), lambda b, h: (b, 0, h))
    return col, tab, perm, oblk


def _residue(r, seg, dil):
    return pl.ds(r, seg, stride=dil) if dil > 1 else pl.ds(0, seg)


def _attn_fwd(proj3, ctab, stab, perm, d, g):
    window, dil = ATTN_GROUPS[g]
    b_, s_, _ = proj3.shape
    seg = s_ // dil
    half = window // (2 * dil)
    tq, kw, tiles = _attn_tiling(seg)

    def body(q_ref, k_ref, v_ref, c_ref, s_ref, p_ref, o_ref, l_ref):
        pm = p_ref[...]
        for r in range(dil):
            rows = _residue(r, seg, dil)
            q, k, v, ct, st = q_ref[rows, :], k_ref[rows, :], v_ref[rows, :], c_ref[rows, :], s_ref[rows, :]
            res = [_attn_tile(q[q0:q0 + tq], k[k0:k0 + kw], v[k0:k0 + kw], ct[q0:q0 + tq], st[q0:q0 + tq],
                              ct[k0:k0 + kw], st[k0:k0 + kw], pm, q0, k0, half) for q0, k0 in tiles]
            o_ref[rows, :] = jnp.concatenate([o for o, _ in res], axis=0)
            l_ref[rows, :] = jnp.concatenate([l for _, l in res], axis=0)

    col, tab, pspec, oblk = _attn_specs(s_, d, g)
    shp = _sds((b_, s_, ATTN_OUT), F32)
    return pl.pallas_call(
        body, name=f"attn_fwd_d{dil}", grid=(b_, ATTN_HEADS), in_specs=[col(0), col(1), col(2), tab, tab, pspec],
        out_specs=[oblk, oblk], out_shape=[shp, shp], compiler_params=_params(2, 0))(
            proj3, proj3, proj3, ctab, stab, perm)


def _attn_bwd(proj3, ctab, stab, perm, do, dl, d, g):
    window, dil = ATTN_GROUPS[g]
    b_, s_, _ = proj3.shape
    seg = s_ // dil
    half = window // (2 * dil)
    tq, kw, tiles = _attn_tiling(seg)

    def body(q_ref, k_ref, v_ref, c_ref, s_ref, p_ref, do_ref, dl_ref, dq_ref, dk_ref, dv_ref,
             dq_s, dk_s, dv_s, kacc, vacc):
        pm = p_ref[...]
        for r in range(dil):
            rows = _residue(r, seg, dil)
            q, k, v, ct, st = q_ref[rows, :], k_ref[rows, :], v_ref[rows, :], c_ref[rows, :], s_ref[rows, :]
            do_r, dl_r = do_ref[rows, :], dl_ref[rows, :]
            kacc[...] = jnp.zeros(kacc.shape, F32)
            vacc[...] = jnp.zeros(vacc.shape, F32)
            dqs = []
            for q0, k0 in tiles:
                cq, sq, ck, sk = ct[q0:q0 + tq], st[q0:q0 + tq], ct[k0:k0 + kw], st[k0:k0 + kw]
                _, vjp = jax.vjp(lambda a, b, c: _attn_tile(a, b, c, cq, sq, ck, sk, pm, q0, k0, half),
                                 q[q0:q0 + tq], k[k0:k0 + kw], v[k0:k0 + kw])
                dq, dk, dv = vjp((do_r[q0:q0 + tq], dl_r[q0:q0 + tq]))
                dqs.append(dq)
                kacc[pl.ds(k0, kw), :] += dk
                vacc[pl.ds(k0, kw), :] += dv
            dq_s[rows, :] = jnp.concatenate(dqs, axis=0)
            dk_s[rows, :] = kacc[...]
            dv_s[rows, :] = vacc[...]
        dq_ref[...] = dq_s[...].astype(BF16)
        dk_ref[...] = dk_s[...].astype(BF16)
        dv_ref[...] = dv_s[...].astype(BF16)

    col, tab, pspec, oblk = _attn_specs(s_, d, g)
    shp = _sds((b_, s_, ATTN_OUT), BF16)
    full, res = pltpu.VMEM((s_, HEAD), F32), pltpu.VMEM((seg, HEAD), F32)
    return pl.pallas_call(
        body, name=f"attn_bwd_d{dil}", grid=(b_, ATTN_HEADS),
        in_specs=[col(0), col(1), col(2), tab, tab, pspec, oblk, oblk], out_specs=[oblk, oblk, oblk],
        out_shape=[shp, shp, shp], scratch_shapes=[full, full, full, res, res],
        compiler_params=_params(2, 0))(proj3, proj3, proj3, ctab, stab, perm, do, dl)


def _combine(os_, ls_):
    m = jnp.maximum(jnp.maximum(ls_[0], ls_[1]), ls_[2])
    es = [jnp.exp(l - m) for l in ls_]
    return (es[0] * os_[0] + es[1] * os_[1] + es[2] * os_[2]) / (es[0] + es[1] + es[2])


def _combine_fwd(os_, ls_):
    t, w = os_[0].shape
    tm = _tile(t, 512, 8)

    def body(*refs):
        refs[6][...] = _combine([r[...] for r in refs[:3]], [r[...] for r in refs[3:6]]).astype(BF16)

    row = pl.BlockSpec((tm, w), lambda i: (i, 0))
    return pl.pallas_call(body, name="combine_fwd", grid=(t // tm,), in_specs=[row] * 6, out_specs=row,
                          out_shape=_sds((t, w), BF16), compiler_params=_params(1, 0))(*os_, *ls_)


def _combine_bwd(os_, ls_, dob):
    t, w = os_[0].shape
    tm = _tile(t, 512, 8)

    def body(*refs):
        _, vjp = jax.vjp(lambda *a: _combine(a[:3], a[3:]), *[r[...] for r in refs[:6]])
        for r, g in zip(refs[7:], vjp(refs[6][...])):
            r[...] = g

    row = pl.BlockSpec((tm, w), lambda i: (i, 0))
    return pl.pallas_call(body, name="combine_bwd", grid=(t // tm,), in_specs=[row] * 7, out_specs=[row] * 6,
                          out_shape=[_sds((t, w), F32)] * 6, compiler_params=_params(1, 0))(*os_, *ls_, dob)


ROW_SHARDED = ("ffn1_out", "ffn2_out", "wa", "mix_out")


def _local_step(x, target, fulls, p, sc_arr):
    b_, s_, d = x.shape
    t = b_ * s_
    x2 = x.reshape(t, d)
    gate_col = 5 * d + QKV_W
    c_arr = sc_arr[1:2]
    w = {}

    def arrived(keys, arrays):
        for k, a in zip(keys, arrays):
            w[k] = a.reshape(a.shape[0] * a.shape[1], a.shape[2]) if k in ROW_SHARDED else a

    def reduce_begin(keys):
        g3 = [gw[k].reshape(fulls[k].shape) for k in keys]
        return [_sum_halves(a, b, c_arr) for a, b in zip(g3, _swap_halves(g3))]

    def reduce_end(keys, parts, got):
        for k, a, b in zip(keys, parts, got):
            reduced[k] = _sum_chips(a, b, sc_arr)

    arrived(["ffn1_in"], _gather_weights([fulls["ffn1_in"]]))
    xb = _cast_bf16(x2)
    mix_rows = fulls["mix_in"].shape[1]
    (u1, a1), got = _ffn_up(xb, w["ffn1_in"], _gather_exchange([fulls["ffn1_out"], fulls["mix_in"]],
                                                              [None, (0, mix_rows // 2)]))
    arrived(["ffn1_out"], got[:1])
    (h1, h1b, xh1, rs1), got = _down_ln(a1, w["ffn1_out"], x2, 0.5, p["ln1_g"], p["ln1_b"],
                                        _gather_exchange(got[1:], [(mix_rows // 2, mix_rows // 2)]))
    arrived(["mix_in"], got)

    keys = ["wa", "wb", "mix_out", "ffn2_in"]
    proj, got = _mm_w3(h1b, w["mix_in"], _gather_exchange([fulls[k] for k in keys]))
    arrived(keys, got)
    n_in = proj.shape[1]
    proj3 = proj.reshape(b_, s_, n_in)
    (o_raw, oa), got = _hgrn_fwd(proj3, p["hgrn_lb_fwd"], p["hgrn_lb_bwd"], p["hgrn_norm_g"], d,
                                 _gather_exchange([fulls["ffn2_out"]]))
    arrived(["ffn2_out"], got)
    oa2 = oa.reshape(t, d)

    ctab, stab, perm = _rope_tables(s_)
    os_, ls_ = [], []
    for g in range(N_GROUPS):
        o_g, l_g = _attn_fwd(proj3, ctab, stab, perm, d, g)
        os_.append(o_g.reshape(t, ATTN_OUT))
        ls_.append(l_g.reshape(t, ATTN_OUT))
    ob = _combine_fwd(os_, ls_)

    ya = _mm_w2(oa2, w["wa"], F32)
    yb, zb = _branch_gate(ob, w["wb"], ya, proj, gate_col)
    h2, h2b, xh2, rs2 = _down_ln(zb, w["mix_out"], h1, 1.0, p["ln2_g"], p["ln2_b"])

    u2, a2 = _ffn_up(h2b, w["ffn2_in"])
    h3, _, xh3, rs3 = _down_ln(a2, w["ffn2_out"], h2, 0.5, p["ln3_g"], p["ln3_b"])

    gw, gp, reduced = {}, {}, {}
    nf = w["ffn2_in"].shape[2]

    def du_map(j, m):
        return (j // 2, m, j % 2)

    dres3, dy3, gp["ln3_g"], gp["ln3_b"], loss = _ln_bwd([h3], xh3, rs3, p["ln3_g"], 0.5, target.reshape(t, d))
    du2 = _swiglu_bwd(dy3, w["ffn2_out"], u2)
    g_out = _tn_w2(a2, dy3).reshape(fulls["ffn2_out"].shape)
    g_in, got = _tn_w3(h2b, du2, du_map, nf, _swap_exchange([g_out]))
    parts_a = [_sum_halves(g_out, got[0], c_arr)]
    dh2, got = _ffn_dx(du2, w["ffn2_in"], dres3, _both(_scatter_exchange(parts_a), _swap_exchange([g_in])))
    reduce_end(["ffn2_out"], parts_a, got[:1])
    parts_a.append(_sum_halves(g_in, got[1], c_arr))

    dres2, dmix, gp["ln2_g"], gp["ln2_b"], _ = _ln_bwd([dh2], xh2, rs2, p["ln2_g"], 1.0)
    dya, dyb, dpga, dpgb = _gate_bwd(dmix, w["mix_out"], proj, ya, yb, gate_col)
    gw["mix_out"] = _tn_w2(zb, dmix)
    do_a = _nt_w2(dya, w["wa"], F32)
    gw["wa"] = _tn_w2(oa2, dya)
    nb = w["wb"].shape[2]
    do_b = _nt_w3(dyb, w["wb"], None)
    gw["wb"] = _tn_w3(ob, dyb, lambda j, m: (m, j), nb)
    keys = ["ffn2_in", "mix_out", "wa", "wb"]
    parts_b = parts_a[1:] + reduce_begin(keys[1:])

    (dq, dff, dfb, di, dog, gp["hgrn_lb_fwd"], gp["hgrn_lb_bwd"], gp["hgrn_norm_g"]), got = _hgrn_bwd(
        proj3, p["hgrn_lb_fwd"], p["hgrn_lb_bwd"], p["hgrn_norm_g"], o_raw, do_a.reshape(b_, s_, d), d,
        _scatter_exchange(parts_b))
    reduce_end(keys, parts_b, got)

    douts = _combine_bwd(os_, ls_, do_b)
    dqkv = []
    for g in range(N_GROUPS):
        grads_g = _attn_bwd(proj3, ctab, stab, perm, douts[g].reshape(b_, s_, ATTN_OUT),
                            douts[3 + g].reshape(b_, s_, ATTN_OUT), d, g)
        dqkv += [a.reshape(t, ATTN_OUT) for a in grads_g]

    dproj = jnp.concatenate(
        [a.reshape(t, d) for a in (dq, dff, dfb, di, dog)] + dqkv + [dpga, dpgb], axis=1)
    nm = w["mix_in"].shape[2]
    gw["mix_in"] = _tn_w3(h1b, dproj, lambda j, m: (m, j), nm)
    parts_c = reduce_begin(["mix_in"])
    dh1, got = _nt_w3(dproj, w["mix_in"], dres2, _scatter_exchange(parts_c))
    reduce_end(["mix_in"], parts_c, got)

    dres1, dy1, gp["ln1_g"], gp["ln1_b"], _ = _ln_bwd([dh1], xh1, rs1, p["ln1_g"], 0.5)
    g_out = _tn_w2(a1, dy1).reshape(fulls["ffn1_out"].shape)
    du1, got = _swiglu_bwd(dy1, w["ffn1_out"], u1, _swap_exchange([g_out]))
    parts_d = [_sum_halves(g_out, got[0], c_arr)]
    gw["ffn1_in"], got = _tn_w3(xb, du1, du_map, nf, _scatter_exchange(parts_d))
    reduce_end(["ffn1_out"], parts_d, got)
    parts_e = reduce_begin(["ffn1_in"])
    dx, got = _ffn_dx(du1, w["ffn1_in"], dres1, _scatter_exchange(parts_e))
    reduce_end(["ffn1_in"], parts_e, got)

    keys = list(reduced)
    grads = dict(zip(keys, _join_halves([reduced[k] for k in keys])))
    return loss, dx.reshape(b_, s_, d), grads, gp


MESH = pl.DeviceIdType.MESH
ANY = pl.BlockSpec(memory_space=pl.ANY)


def _place():
    x, y, c = lax.axis_index("x"), lax.axis_index("y"), lax.axis_index("c")
    chips = [(1 - x, y), (x, 1 - y), (1 - x, 1 - y)]
    return x, y, c, chips, (x, y, 1 - c)


def _half_rows(c, rows):
    hr = rows // 2
    return pl.ds(pl.multiple_of(c * hr, 16), hr)


def _remote(src, dst, send, recv, dev):
    return pltpu.make_async_remote_copy(src_ref=src, dst_ref=dst, send_sem=send, recv_sem=recv,
                                        device_id=dev, device_id_type=MESH)


def _gather_weights(fulls):
    n = len(fulls)

    def body(*refs):
        _gather_start(refs[n:2 * n], refs[2 * n:])
        _gather_finish(refs[n:2 * n], refs[2 * n:])

    return pl.pallas_call(
        body, name="gather_weights", in_specs=[ANY] * n, out_specs=[ANY] * n,
        out_shape=[_sds(a.shape, a.dtype) for a in fulls], input_output_aliases={i: i for i in range(n)},
        scratch_shapes=_gather_sems(n))(*fulls)


def _gather_sems(n):
    return [pltpu.SemaphoreType.DMA((n, 3)) for _ in range(4)]


def _span_half(c, span, rows):
    r0, cnt = (0, rows) if span is None else span
    return pl.ds(pl.multiple_of(r0 + c * (cnt // 2), 16), cnt // 2)


def _gather_start(bufs, sems, spans=None):
    isend, irecv = sems[0], sems[1]
    x, y, c, chips, sib = _place()
    for i, buf in enumerate(bufs):
        blk = buf.at[2 * x + y, _span_half(c, spans and spans[i], buf.shape[1])]
        for k, chip in enumerate(chips):
            _remote(blk, blk, isend.at[i, k], irecv.at[i, k], (*chip, c)).start()


def _gather_finish(bufs, sems, spans=None):
    isend, irecv, fsend, frecv = sems
    x, y, c, chips, sib = _place()
    for i, buf in enumerate(bufs):
        mine = _span_half(c, spans and spans[i], buf.shape[1])
        for k, chip in enumerate(chips):
            blk = buf.at[2 * chip[0] + chip[1], mine]
            _remote(blk, blk, isend.at[i, k], irecv.at[i, k], (*chip, c)).wait_recv()
            _remote(blk, blk, fsend.at[i, k], frecv.at[i, k], sib).start()
    for i, buf in enumerate(bufs):
        span = spans and spans[i]
        mine, other = _span_half(c, span, buf.shape[1]), _span_half(1 - c, span, buf.shape[1])
        own = buf.at[2 * x + y, mine]
        for k, chip in enumerate(chips):
            got = buf.at[2 * chip[0] + chip[1], other]
            _remote(got, got, fsend.at[i, k], frecv.at[i, k], sib).wait_recv()
            _remote(own, own, isend.at[i, k], irecv.at[i, k], (*chip, c)).wait_send()
            blk = buf.at[2 * chip[0] + chip[1], mine]
            _remote(blk, blk, fsend.at[i, k], frecv.at[i, k], sib).wait_send()


def _gather_exchange(fulls, spans=None):
    n = len(fulls)
    return _Exchange(fulls, [_sds(a.shape, a.dtype) for a in fulls], {i: i for i in range(n)}, _gather_sems(n),
                     lambda ins, outs, sems: _gather_start(outs, sems, spans),
                     lambda ins, outs, sems: _gather_finish(outs, sems, spans))


def _swap_halves(grads):
    side = _swap_exchange(grads)
    n = len(grads)

    def body(*refs):
        side.start(refs[:n], refs[n:2 * n], refs[2 * n:])
        side.finish(refs[:n], refs[n:2 * n], refs[2 * n:])

    return pl.pallas_call(
        body, name="swap_halves", in_specs=[ANY] * n, out_specs=[ANY] * n, out_shape=side.outs,
        scratch_shapes=side.sems)(*grads)


def _swap_exchange(grads):
    n = len(grads)

    def copies(ins, outs, sems):
        x, y, c, chips, sib = _place()
        return [_remote(a.at[:, _half_rows(1 - c, a.shape[1])], b, sems[0].at[i], sems[1].at[i], sib)
                for i, (a, b) in enumerate(zip(ins, outs))]

    def start(ins, outs, sems):
        for cp in copies(ins, outs, sems):
            cp.start()

    def finish(ins, outs, sems):
        for cp in copies(ins, outs, sems):
            cp.wait()

    return _Exchange(grads, [_sds((N_SHARD, a.shape[1] // 2, a.shape[2]), a.dtype) for a in grads], {},
                     [pltpu.SemaphoreType.DMA((n,)) for _ in range(2)], start, finish)


def _both(a, b):
    i, o, s = len(a.ins), len(a.outs), len(a.sems)

    def start(ins, outs, sems):
        a.start(ins[:i], outs[:o], sems[:s])
        b.start(ins[i:], outs[o:], sems[s:])

    def finish(ins, outs, sems):
        a.finish(ins[:i], outs[:o], sems[:s])
        b.finish(ins[i:], outs[o:], sems[s:])

    aliases = dict(a.aliases)
    aliases.update({i + k: o + v for k, v in b.aliases.items()})
    return _Exchange(a.ins + b.ins, a.outs + b.outs, aliases, a.sems + b.sems, start, finish)


def _scatter_partials(parts):
    side = _scatter_exchange(parts)
    n = len(parts)

    def body(*refs):
        side.start(refs[:n], refs[n:2 * n], refs[2 * n:])
        side.finish(refs[:n], refs[n:2 * n], refs[2 * n:])

    return pl.pallas_call(
        body, name="scatter_partials", in_specs=[ANY] * n, out_specs=[ANY] * n, out_shape=side.outs,
        scratch_shapes=side.sems)(*parts)


def _scatter_copies(ins, outs, sems):
    x, y, c, chips, sib = _place()
    return [_remote(a.at[2 * chip[0] + chip[1]], b.at[k], sems[0].at[i, k], sems[1].at[i, k], (*chip, c))
            for i, (a, b) in enumerate(zip(ins, outs)) for k, chip in enumerate(chips)]


def _scatter_exchange(parts):
    n = len(parts)

    def start(ins, outs, sems):
        for cp in _scatter_copies(ins, outs, sems):
            cp.start()

    def finish(ins, outs, sems):
        for cp in _scatter_copies(ins, outs, sems):
            cp.wait()

    return _Exchange(parts, [_sds((3,) + a.shape[1:], a.dtype) for a in parts], {},
                     [pltpu.SemaphoreType.DMA((n, 3)) for _ in range(2)], start, finish)


def _join_halves(grads):
    n = len(grads)

    def body(*refs):
        bufs, send, recv = refs[n:2 * n], refs[2 * n], refs[2 * n + 1]
        x, y, c, chips, sib = _place()
        cps = []
        for i in range(n):
            blk = bufs[i].at[_half_rows(c, bufs[i].shape[0])]
            other = bufs[i].at[_half_rows(1 - c, bufs[i].shape[0])]
            cp = _remote(blk, blk, send.at[i], recv.at[i], sib)
            cp.start()
            cps.append((cp, _remote(other, other, send.at[i], recv.at[i], sib)))
        for cp, got in cps:
            cp.wait_send()
            got.wait_recv()

    dma = pltpu.SemaphoreType.DMA
    return pl.pallas_call(
        body, name="join_halves", in_specs=[ANY] * n, out_specs=[ANY] * n,
        out_shape=[_sds(a.shape, a.dtype) for a in grads], input_output_aliases={i: i for i in range(n)},
        scratch_shapes=[dma((n,)), dma((n,))])(*grads)


def _gather_rows(block):
    m_per, n = block.shape

    def body(x_ref, out_ref, send_sems, recv_sems, local_sem):
        x, y, c, chips, sibling = _place()
        me = (x, y, c)

        def rows(px, py, pc):
            return out_ref.at[pl.ds((4 * px + 2 * py + pc) * m_per, m_per), :]

        def copy(k, blk, to, src=None):
            return _remote(rows(*blk) if src is None else src, rows(*blk), send_sems.at[k], recv_sems.at[k], to)

        mine = pltpu.make_async_copy(x_ref, rows(*me), local_sem)
        mine.start()
        first = [copy(0, me, sibling, src=x_ref)]
        first += [copy(1 + j, me, (*chip, c), src=x_ref) for j, chip in enumerate(chips)]
        for cp in first:
            cp.start()
        passed = [copy(4 + j, (*chip, c), sibling) for j, chip in enumerate(chips)]
        for j, chip in enumerate(chips):
            copy(1 + j, (*chip, c), me).wait_recv()
            passed[j].start()
        copy(0, sibling, me).wait_recv()
        for j, chip in enumerate(chips):
            copy(4 + j, (*chip, 1 - c), me).wait_recv()
        for cp in first + passed:
            cp.wait_send()
        mine.wait()

    vmem = pl.BlockSpec(memory_space=pltpu.VMEM)
    dma = pltpu.SemaphoreType.DMA
    return pl.pallas_call(
        body, name="gather_rows", in_specs=[vmem], out_specs=vmem, out_shape=_sds((8 * m_per, n), block.dtype),
        scratch_shapes=[dma((7,)), dma((7,)), dma(())])(block)


def _row_tile(rows, cols):
    return _tile(rows, max(16, (1 << 20) // cols), 16)


def _sum_halves(grad, got, c_arr):
    _, hr, cols = got.shape
    tr = _row_tile(hr, cols)
    nb = hr // tr

    def body(c_ref, a_ref, b_ref, o_ref):
        o_ref[...] = (a_ref[...].astype(F32) + b_ref[...].astype(F32)).astype(BF16)

    blk = pl.BlockSpec((None, tr, cols), lambda s, i, c_ref: (s, i, 0))
    return pl.pallas_call(
        body, name="sum_halves",
        grid_spec=pltpu.PrefetchScalarGridSpec(
            num_scalar_prefetch=1, grid=(N_SHARD, nb),
            in_specs=[pl.BlockSpec((None, tr, cols), lambda s, i, c_ref: (s, c_ref[0] * nb + i, 0)), blk],
            out_specs=blk),
        out_shape=_sds(got.shape, BF16), compiler_params=_params(2, 0))(c_arr, grad, got)


def _sum_chips(part, got, sc_arr):
    _, hr, cols = got.shape
    tr = _row_tile(hr, cols)
    nb = hr // tr

    def body(s_ref, a_ref, b_ref, o_ref):
        o_ref[...] = ((a_ref[...].astype(F32) + b_ref[0].astype(F32)) + b_ref[1].astype(F32)) + b_ref[2].astype(F32)

    return pl.pallas_call(
        body, name="sum_chips",
        grid_spec=pltpu.PrefetchScalarGridSpec(
            num_scalar_prefetch=1, grid=(nb,),
            in_specs=[pl.BlockSpec((None, tr, cols), lambda i, s_ref: (s_ref[0], i, 0)),
                      pl.BlockSpec((3, tr, cols), lambda i, s_ref: (0, i, 0))],
            out_specs=pl.BlockSpec((tr, cols), lambda i, s_ref: (s_ref[1] * nb + i, 0))),
        out_shape=_sds((2 * hr, cols), F32), compiler_params=_params(1, 0))(sc_arr, part, got)


def _cast_into_slot(x2d, sc_arr):
    rows, cols = x2d.shape
    tr = _row_tile(rows, cols)

    def body(s_ref, x_ref, o_ref):
        o_ref[...] = x_ref[...].astype(BF16)

    return pl.pallas_call(
        body, name="cast_into_slot",
        grid_spec=pltpu.PrefetchScalarGridSpec(
            num_scalar_prefetch=1, grid=(rows // tr,),
            in_specs=[pl.BlockSpec((tr, cols), lambda i, s_ref: (i, 0))],
            out_specs=pl.BlockSpec((None, tr, cols), lambda i, s_ref: (s_ref[0], i, 0))),
        out_shape=_sds((N_SHARD, rows, cols), BF16), compiler_params=_params(1, 0))(sc_arr, x2d)


def _adam_math(w, g, m, v):
    m = ADAM_B1 * m + (1.0 - ADAM_B1) * g
    v = ADAM_B2 * v + (1.0 - ADAM_B2) * (g * g)
    m_hat = m / (1.0 - ADAM_B1 ** ADAM_STEP)
    v_hat = v / (1.0 - ADAM_B2 ** ADAM_STEP)
    delta = -ADAM_LR * (m_hat / (jnp.sqrt(v_hat) + ADAM_EPS) + ADAM_WD * w)
    return delta, m, v


def _adamw(w, g, m, v):
    rows, cols = w.shape
    tr = _tile(rows, max(8, (1 << 19) // cols), 8)

    def body(w_ref, g_ref, m_ref, v_ref, go_ref, d_ref, mo_ref, vo_ref):
        g_ = g_ref[...]
        go_ref[...] = g_
        d_ref[...], mo_ref[...], vo_ref[...] = _adam_math(w_ref[...], g_, m_ref[...], v_ref[...])

    blk = pl.BlockSpec((tr, cols), lambda i: (i, 0))
    return pl.pallas_call(
        body, name="adamw", grid=(rows // tr,), in_specs=[blk] * 4, out_specs=[blk] * 4,
        out_shape=[_sds((rows, cols), F32)] * 4, compiler_params=_params(1, 0))(w, g, m, v)


def _adamw_small(gathered, w, m, v):
    rows, cols = w.shape

    def body(a_ref, w_ref, m_ref, v_ref, go_ref, d_ref, mo_ref, vo_ref):
        g_ = a_ref[pl.ds(0, rows), :]
        for k in range(1, 8):
            g_ = g_ + a_ref[pl.ds(k * rows, rows), :]
        go_ref[...] = g_
        d_ref[...], mo_ref[...], vo_ref[...] = _adam_math(w_ref[...], g_, m_ref[...], v_ref[...])

    vmem = pl.BlockSpec(memory_space=pltpu.VMEM)
    return pl.pallas_call(
        body, name="adamw_small", in_specs=[vmem] * 4, out_specs=[vmem] * 4,
        out_shape=[_sds((rows, cols), F32)] * 4)(gathered, w, m, v)


BIG = ("ffn1_w_in", "ffn1_w_out", "mix_w_in", "w_branch_a", "w_branch_b", "mix_w_out", "ffn2_w_in", "ffn2_w_out")
BIG_KEY = {"ffn1_w_in": "ffn1_in", "ffn1_w_out": "ffn1_out", "mix_w_in": "mix_in", "w_branch_a": "wa",
           "w_branch_b": "wb", "mix_w_out": "mix_out", "ffn2_w_in": "ffn2_in", "ffn2_w_out": "ffn2_out"}
SMALL = ("ln1_g", "ln1_b", "hgrn_lb_fwd", "hgrn_lb_bwd", "hgrn_norm_g", "ln2_g", "ln2_b", "ln3_g", "ln3_b")
ORDER = ("ffn1_w_in", "ffn1_w_out", "ln1_g", "ln1_b", "mix_w_in", "hgrn_lb_fwd", "hgrn_lb_bwd", "hgrn_norm_g",
         "w_branch_a", "w_branch_b", "mix_w_out", "ln2_g", "ln2_b", "ffn2_w_in", "ffn2_w_out", "ln3_g", "ln3_b")
SMALL_ROWS = 16


def _pack_small(d):
    rows = jnp.concatenate([d[k].reshape(-1, d[k].shape[-1]) for k in SMALL], axis=0)
    return jnp.pad(rows, ((0, SMALL_ROWS - rows.shape[0]), (0, 0)))


def _unpack_small(a, like):
    out, r = {}, 0
    for k in SMALL:
        n = like[k].shape[0]
        out[k] = a[r:r + n].reshape(like[k].shape)
        r += n
    return out


def kernel(x, ffn1_w_in, ffn1_w_out, ln1_g, ln1_b, mix_w_in, hgrn_lb_fwd, hgrn_lb_bwd, hgrn_norm_g, w_branch_a, w_branch_b, mix_w_out, ln2_g, ln2_b, ffn2_w_in, ffn2_w_out, ln3_g, ln3_b, loss_target, m_ffn1_w_in, m_ffn1_w_out, m_ln1_g, m_ln1_b, m_mix_w_in, m_hgrn_lb_fwd, m_hgrn_lb_bwd, m_hgrn_norm_g, m_w_branch_a, m_w_branch_b, m_mix_w_out, m_ln2_g, m_ln2_b, m_ffn2_w_in, m_ffn2_w_out, m_ln3_g, m_ln3_b, v_ffn1_w_in, v_ffn1_w_out, v_ln1_g, v_ln1_b, v_mix_w_in, v_hgrn_lb_fwd, v_hgrn_lb_bwd, v_hgrn_norm_g, v_w_branch_a, v_w_branch_b, v_mix_w_out, v_ln2_g, v_ln2_b, v_ffn2_w_in, v_ffn2_w_out, v_ln3_g, v_ln3_b):
    wts = dict(ffn1_w_in=ffn1_w_in, ffn1_w_out=ffn1_w_out, ln1_g=ln1_g, ln1_b=ln1_b, mix_w_in=mix_w_in,
               hgrn_lb_fwd=hgrn_lb_fwd, hgrn_lb_bwd=hgrn_lb_bwd, hgrn_norm_g=hgrn_norm_g, w_branch_a=w_branch_a,
               w_branch_b=w_branch_b, mix_w_out=mix_w_out, ln2_g=ln2_g, ln2_b=ln2_b, ffn2_w_in=ffn2_w_in,
               ffn2_w_out=ffn2_w_out, ln3_g=ln3_g, ln3_b=ln3_b)
    mom = dict(ffn1_w_in=m_ffn1_w_in, ffn1_w_out=m_ffn1_w_out, ln1_g=m_ln1_g, ln1_b=m_ln1_b, mix_w_in=m_mix_w_in,
               hgrn_lb_fwd=m_hgrn_lb_fwd, hgrn_lb_bwd=m_hgrn_lb_bwd, hgrn_norm_g=m_hgrn_norm_g,
               w_branch_a=m_w_branch_a, w_branch_b=m_w_branch_b, mix_w_out=m_mix_w_out, ln2_g=m_ln2_g, ln2_b=m_ln2_b,
               ffn2_w_in=m_ffn2_w_in, ffn2_w_out=m_ffn2_w_out, ln3_g=m_ln3_g, ln3_b=m_ln3_b)
    var = dict(ffn1_w_in=v_ffn1_w_in, ffn1_w_out=v_ffn1_w_out, ln1_g=v_ln1_g, ln1_b=v_ln1_b, mix_w_in=v_mix_w_in,
               hgrn_lb_fwd=v_hgrn_lb_fwd, hgrn_lb_bwd=v_hgrn_lb_bwd, hgrn_norm_g=v_hgrn_norm_g,
               w_branch_a=v_w_branch_a, w_branch_b=v_w_branch_b, mix_w_out=v_mix_w_out, ln2_g=v_ln2_g, ln2_b=v_ln2_b,
               ffn2_w_in=v_ffn2_w_in, ffn2_w_out=v_ffn2_w_out, ln3_g=v_ln3_g, ln3_b=v_ln3_b)
    shard = (2 * lax.axis_index("x") + lax.axis_index("y")).astype(jnp.int32)
    sc_arr = jnp.stack([shard, lax.axis_index("c").astype(jnp.int32)])

    shard2d = {k: wts[k].reshape(wts[k].shape[1:]) for k in BIG}
    fulls = {BIG_KEY[k]: _cast_into_slot(shard2d[k], sc_arr) for k in BIG}
    p = {k: wts[k] for k in SMALL}

    loss, grad_x, grads, gp = _local_step(x, loss_target, fulls, p, sc_arr)
    loss = lax.psum(loss[0, 0], ("x", "y", "c"))

    out_g, out_d, out_m, out_v = {}, {}, {}, {}
    for k in BIG:
        g = grads[BIG_KEY[k]]
        shp = wts[k].shape
        res = _adamw(shard2d[k], g, mom[k].reshape(shp[1:]), var[k].reshape(shp[1:]))
        out_g[k], out_d[k], out_m[k], out_v[k] = [a.reshape(shp) for a in res]

    gathered = _gather_rows(_pack_small(gp))
    res = _adamw_small(gathered, _pack_small(wts), _pack_small(mom), _pack_small(var))
    for dst, a in zip((out_g, out_d, out_m, out_v), res):
        dst.update(_unpack_small(a, wts))

    return (loss, grad_x, *[out_g[k] for k in ORDER], *[out_d[k] for k in ORDER],
            *[out_m[k] for k in ORDER], *[out_v[k] for k in ORDER])
```

```python
import functools

import numpy as np
import jax
import jax.numpy as jnp
from jax import lax
from jax.experimental import pallas as pl
from jax.experimental.pallas import tpu as pltpu

F32 = jnp.float32
BF16 = jnp.bfloat16

HEAD = 128
ATTN_GROUPS = ((128, 1), (512, 4), (2048, 16))
ATTN_HEADS = 4
N_GROUPS = len(ATTN_GROUPS)
QKV_W = N_GROUPS * 3 * ATTN_HEADS * HEAD
ATTN_OUT = ATTN_HEADS * HEAD
ROPE_THETA = 500000.0
ROPE_DIM = HEAD // 4
HGRN_CHUNK = 32
HGRN_FWD_HEADS = 2
HGRN_BWD_HEADS = 2
HGRN_SUB = 4
ALPHA = 2.0 ** 0.25
LN_EPS = 1e-5
NEG_INF = -1e30
ADAM_LR, ADAM_B1, ADAM_B2, ADAM_EPS, ADAM_WD, ADAM_STEP = 0.001, 0.9, 0.999, 1e-08, 0.01, 10

N_SHARD = 4
VMEM_LIMIT = 56 * 1024 * 1024

NN = ((1,), (0,))
NT = ((1,), (1,))
TN = ((0,), (0,))


def _dot(a, b, dims, precision=None):
    return lax.dot_general(a, b, (dims, ((), ())), preferred_element_type=F32, precision=precision)


def _tile(n, pref, mult=128):
    best = None
    for t in range(mult, min(n, pref) + 1, mult):
        if n % t == 0:
            best = t
    return n if best is None else best


def _params(n_parallel, n_arbitrary):
    return pltpu.CompilerParams(
        dimension_semantics=("parallel",) * n_parallel + ("arbitrary",) * n_arbitrary,
        vmem_limit_bytes=VMEM_LIMIT)


def _sigmoid(x):
    return 1.0 / (1.0 + jnp.exp(-x))


def _silu(x):
    return x * _sigmoid(x)


class _Exchange:
    def __init__(self, ins, outs, aliases, sems, start, finish):
        self.ins, self.outs, self.aliases, self.sems = list(ins), list(outs), dict(aliases), list(sems)
        self.start, self.finish = start, finish


def _run(body, name, grid, ins, in_specs, outs, out_specs, scratch, n_arbitrary, side=None):
    n_in, n_out, n_scr = len(ins), len(outs), len(scratch)
    if side is None:
        return pl.pallas_call(
            body, name=name, grid=grid, in_specs=in_specs, out_specs=out_specs, out_shape=outs,
            scratch_shapes=scratch, compiler_params=_params(len(grid) - n_arbitrary, n_arbitrary))(*ins)
    s_in, s_out = len(side.ins), len(side.outs)
    i1 = n_in + s_in
    o1 = i1 + n_out
    o2 = o1 + s_out
    c1 = o2 + n_scr

    def wrapped(*refs):
        s_refs = (refs[n_in:i1], refs[o1:o2], refs[c1:])
        ids = [pl.program_id(a) for a in range(len(grid))]
        first = functools.reduce(jnp.logical_and, [i == 0 for i in ids])
        last = functools.reduce(jnp.logical_and, [i == g - 1 for i, g in zip(ids, grid)])

        @pl.when(first)
        def _():
            side.start(*s_refs)

        body(*refs[:n_in], *refs[i1:o1], *refs[o2:c1])

        @pl.when(last)
        def _():
            side.finish(*s_refs)

    res = pl.pallas_call(
        wrapped, name=name, grid=grid, in_specs=list(in_specs) + [ANY] * s_in,
        out_specs=list(out_specs) + [ANY] * s_out, out_shape=list(outs) + side.outs,
        scratch_shapes=list(scratch) + side.sems,
        input_output_aliases={n_in + a: n_out + b for a, b in side.aliases.items()},
        compiler_params=_params(0, len(grid)))(*ins, *side.ins)
    return res[:n_out], res[n_out:]


def _gemm(name, grid, ins, in_specs, outs, out_specs, accs, dot_fn, epi_fn, side=None):
    n_in, n_out, n_k = len(ins), len(outs), grid[-1]

    def body(*refs):
        in_refs, out_refs, acc_refs = refs[:n_in], refs[n_in:n_in + n_out], refs[n_in + n_out:]
        k = pl.program_id(len(grid) - 1)

        @pl.when(k == 0)
        def _():
            for a in acc_refs:
                a[...] = jnp.zeros(a.shape, F32)

        dot_fn(in_refs, acc_refs)

        @pl.when(k == n_k - 1)
        def _():
            epi_fn(in_refs, acc_refs, out_refs)

    return _run(body, name, grid, ins, in_specs, outs, out_specs, [pltpu.VMEM(s, F32) for s in accs], 1, side)


def _sds(shape, dtype):
    return jax.ShapeDtypeStruct(shape, dtype)


def _gemm_rows(name, grid, ins, in_specs, outs, out_specs, tm, n_sub, fn, side=None):
    n_in, sub = len(ins), tm // n_sub

    def body(*refs):
        for s in range(n_sub):
            fn(pl.ds(s * sub, sub), refs[:n_in], refs[n_in:])

    return _run(body, name, grid, ins, in_specs, outs, out_specs, [], 0, side)


def _ffn_up(xb, w3, side=None):
    t, d = xb.shape
    nf = w3.shape[2]
    f = 2 * nf
    tm = _tile(t, 512, 8)
    tn = nf // 2 if nf % 256 == 0 else nf
    ns = nf // tn

    def fn(rows, r, out):
        x = r[0][rows, :]
        g, u = _dot(x, r[1][...], NN), _dot(x, r[2][...], NN)
        out[0][0, rows, :] = g.astype(BF16)
        out[0][1, rows, :] = u.astype(BF16)
        out[1][rows, :] = (_silu(g) * u).astype(BF16)

    return _gemm_rows(
        "ffn_up", (t // tm, 2, ns), [xb, w3, w3],
        [pl.BlockSpec((tm, d), lambda i, j, h: (i, 0)),
         pl.BlockSpec((None, d, tn), lambda i, j, h: (j, 0, h)),
         pl.BlockSpec((None, d, tn), lambda i, j, h: (j + 2, 0, h))],
        [_sds((2, t, f), BF16), _sds((t, f), BF16)],
        [pl.BlockSpec((2, tm, tn), lambda i, j, h: (0, i, j * ns + h)),
         pl.BlockSpec((tm, tn), lambda i, j, h: (i, j * ns + h))],
        tm, 2, fn, side)


def _down_ln(a, w, resid, coef, g, b, side=None):
    t, kd = a.shape
    d = w.shape[1]
    tm, tk = _tile(t, 512, 8), _tile(kd, 704)

    def dot_fn(r, acc):
        acc[0][...] += _dot(r[0][...], r[1][...], NN)

    def epi_fn(r, acc, out):
        v = ALPHA * r[2][...] + coef * acc[0][...]
        mu = jnp.mean(v, axis=-1, keepdims=True)
        c = v - mu
        var = jnp.mean(c * c, axis=-1, keepdims=True)
        rstd = lax.rsqrt(var + LN_EPS)
        xhat = c * rstd
        h = xhat * r[3][...] + r[4][...]
        out[0][...] = h
        out[1][...] = h.astype(BF16)
        out[2][...] = xhat
        out[3][...] = rstd

    row = pl.BlockSpec((tm, d), lambda i, k: (i, 0))
    vec = pl.BlockSpec((1, d), lambda i, k: (0, 0))
    return _gemm(
        "down_ln", (t // tm, kd // tk), [a, w, resid, g, b],
        [pl.BlockSpec((tm, tk), lambda i, k: (i, k)), pl.BlockSpec((tk, d), lambda i, k: (k, 0)), row, vec, vec],
        [_sds((t, d), F32), _sds((t, d), BF16), _sds((t, d), F32), _sds((t, 1), F32)],
        [row, row, row, pl.BlockSpec((tm, 1), lambda i, k: (i, 0))],
        [(tm, d)], dot_fn, epi_fn, side)


def _only(res, side):
    return res[0] if side is None else (res[0][0], res[1])


def _mm_w3(a, w3, side=None):
    t, kd = a.shape
    n = w3.shape[2]
    tm, tk = _tile(t, 512, 8), _tile(kd, 1024)

    def body(a_ref, w_ref, o_ref):
        k = pl.program_id(2)
        p = _dot(a_ref[...], w_ref[...], NN)

        @pl.when(k == 0)
        def _():
            o_ref[...] = p

        @pl.when(k > 0)
        def _():
            o_ref[...] += p

    return _only(_run(
        body, "mm_w3", (t // tm, N_SHARD, kd // tk), [a, w3],
        [pl.BlockSpec((tm, tk), lambda i, j, k: (i, k)), pl.BlockSpec((None, tk, n), lambda i, j, k: (j, k, 0))],
        [_sds((t, N_SHARD * n), F32)], [pl.BlockSpec((tm, n), lambda i, j, k: (i, j))], [], 1, side), side)


def _mm_w2(a, w, out_dtype):
    t, kd = a.shape
    n = w.shape[1]
    tm, tn = _tile(t, 512, 8), _tile(n, 1024)

    def fn(rows, r, out):
        out[0][rows, :] = _dot(r[0][rows, :], r[1][...], NN).astype(out_dtype)

    return _gemm_rows(
        "mm_w2", (t // tm, n // tn), [a, w],
        [pl.BlockSpec((tm, kd), lambda i, j: (i, 0)), pl.BlockSpec((kd, tn), lambda i, j: (0, j))],
        [_sds((t, n), out_dtype)], [pl.BlockSpec((tm, tn), lambda i, j: (i, j))], tm, 1, fn)[0]


def _branch_gate(ob, wb3, ya, proj, gate_col):
    t, kd = ob.shape
    n = wb3.shape[2]
    d = N_SHARD * n
    tm = _tile(t, 512, 8)
    ga0, gb0 = gate_col // n, (gate_col + d) // n

    def fn(rows, r, out):
        yb = _dot(r[0][rows, :], r[1][...], NN)
        out[0][rows, :] = yb
        out[1][rows, :] = (_sigmoid(r[3][rows, :]) * r[2][rows, :] + _sigmoid(r[4][rows, :]) * yb).astype(BF16)

    blk = pl.BlockSpec((tm, n), lambda i, j: (i, j))
    return _gemm_rows(
        "branch_gate", (t // tm, N_SHARD), [ob, wb3, ya, proj, proj],
        [pl.BlockSpec((tm, kd), lambda i, j: (i, 0)), pl.BlockSpec((None, kd, n), lambda i, j: (j, 0, 0)), blk,
         pl.BlockSpec((tm, n), lambda i, j: (i, ga0 + j)), pl.BlockSpec((tm, n), lambda i, j: (i, gb0 + j))],
        [_sds((t, d), F32), _sds((t, d), BF16)], [blk, blk], tm, 2, fn)


def _swiglu_bwd(dyb, w, u3, side=None):
    t, d = dyb.shape
    f = w.shape[0]
    tm, tr = _tile(t, 1024, 8), _tile(f, 512)

    def fn(rows, r, out):
        da = _dot(r[0][rows, :], r[1][...], NT)
        g, u = r[2][0, rows, :].astype(F32), r[2][1, rows, :].astype(F32)
        s = _sigmoid(g)
        out[0][0, rows, :] = (da * u * s * (1.0 + g * (1.0 - s))).astype(BF16)
        out[0][1, rows, :] = (da * g * s).astype(BF16)

    ublk = pl.BlockSpec((2, tm, tr), lambda i, j: (0, i, j))
    return _only(_gemm_rows(
        "swiglu_bwd", (t // tm, f // tr), [dyb, w, u3],
        [pl.BlockSpec((tm, d), lambda i, j: (i, 0)), pl.BlockSpec((tr, d), lambda i, j: (j, 0)), ublk],
        [_sds((2, t, f), BF16)], [ublk], tm, 4, fn, side), side)


def _ffn_dx(du3, w3, resid, side=None):
    t = du3.shape[1]
    d, nf = w3.shape[1], w3.shape[2]
    tm, tr = _tile(t, 512, 8), _tile(d, 1024)

    def dot_fn(r, acc):
        acc[0][...] += _dot(r[0][...], r[1][...], NT)

    def epi_fn(r, acc, out):
        out[0][...] = acc[0][...] + r[2][...]

    blk = pl.BlockSpec((tm, tr), lambda i, j, k: (i, j))
    return _only(_gemm(
        "ffn_dx", (t // tm, d // tr, N_SHARD), [du3, w3, resid],
        [pl.BlockSpec((None, tm, nf), lambda i, j, k: (k // 2, i, k % 2)),
         pl.BlockSpec((None, tr, nf), lambda i, j, k: (k, j, 0)), blk],
        [_sds((t, d), F32)], [blk], [(tm, tr)], dot_fn, epi_fn, side), side)


def _nt_w3(dy, w3, resid, side=None):
    t = dy.shape[0]
    kd, n = w3.shape[1], w3.shape[2]
    tm, tr = _tile(t, 512, 8), _tile(kd, 1024)
    has_res = resid is not None

    def dot_fn(r, acc):
        acc[0][...] += _dot(r[0][...], r[1][...], NT)

    def epi_fn(r, acc, out):
        v = acc[0][...]
        if has_res:
            v = v + r[2][...]
        out[0][...] = v

    blk = pl.BlockSpec((tm, tr), lambda i, j, k: (i, j))
    return _only(_gemm(
        "nt_w3", (t // tm, kd // tr, N_SHARD), [dy, w3] + ([resid] if has_res else []),
        [pl.BlockSpec((tm, n), lambda i, j, k: (i, k)), pl.BlockSpec((None, tr, n), lambda i, j, k: (k, j, 0))]
        + ([blk] if has_res else []),
        [_sds((t, kd), F32)], [blk], [(tm, tr)], dot_fn, epi_fn, side), side)


def _nt_w2(dy, w, out_dtype):
    t, n = dy.shape
    r_ = w.shape[0]
    tm, tr = _tile(t, 512, 8), _tile(r_, 1024)

    def fn(rows, r, out):
        out[0][rows, :] = _dot(r[0][rows, :], r[1][...], NT).astype(out_dtype)

    return _gemm_rows(
        "nt_w2", (t // tm, r_ // tr), [dy, w],
        [pl.BlockSpec((tm, n), lambda i, j: (i, 0)), pl.BlockSpec((tr, n), lambda i, j: (j, 0))],
        [_sds((t, r_), out_dtype)], [pl.BlockSpec((tm, tr), lambda i, j: (i, j))], tm, 1, fn)[0]


def _gate_bwd(dmix, w, proj, ya, yb, gate_col):
    t, n = dmix.shape
    d = w.shape[0]
    tm, tr = _tile(t, 512, 8), _tile(d, 512)
    ga0, gb0 = gate_col // tr, (gate_col + d) // tr
    nb = d // tr

    def fn(rows, r, out):
        dz = _dot(r[0][rows, :], r[1][...], NT)
        ga, gb = _sigmoid(r[2][rows, :]), _sigmoid(r[3][rows, :])
        out[0][rows, :] = (dz * ga).astype(BF16)
        out[1][rows, :] = (dz * gb).astype(BF16)
        out[2][rows, :] = (dz * r[4][rows, :] * ga * (1.0 - ga)).astype(BF16)
        out[3][rows, :] = (dz * r[5][rows, :] * gb * (1.0 - gb)).astype(BF16)

    blk = pl.BlockSpec((tm, tr), lambda i, j: (i, j))
    return _gemm_rows(
        "gate_bwd", (t // tm, nb), [dmix, w, proj, proj, ya, yb],
        [pl.BlockSpec((tm, n), lambda i, j: (i, 0)), pl.BlockSpec((tr, n), lambda i, j: (j, 0)),
         pl.BlockSpec((tm, tr), lambda i, j: (i, ga0 + j)), pl.BlockSpec((tm, tr), lambda i, j: (i, gb0 + j)),
         blk, blk],
        [_sds((t, d), BF16), _sds((t, d), BF16), _sds((t, d), BF16), _sds((t, d), BF16)],
        [blk, blk, blk, blk], tm, 2, fn)


def _tn_w3(a, dy, dy_map, n, side=None):
    t, kd = a.shape
    tm, tkk = _tile(t, (6 << 20) // n, 8), _tile(kd, 512)

    def dot_fn(r, acc):
        acc[0][...] += _dot(r[0][...], r[1][...], TN)

    def epi_fn(r, acc, out):
        out[0][...] = acc[0][...].astype(BF16)

    dy_block = (tm, n) if dy.ndim == 2 else (None, tm, n)
    return _only(_gemm(
        "tn_w3", (kd // tkk, N_SHARD, t // tm), [a, dy],
        [pl.BlockSpec((tm, tkk), lambda i, j, m: (m, i)), pl.BlockSpec(dy_block, lambda i, j, m: dy_map(j, m))],
        [_sds((N_SHARD, kd, n), BF16)], [pl.BlockSpec((None, tkk, n), lambda i, j, m: (j, i, 0))],
        [(tkk, n)], dot_fn, epi_fn, side), side)


def _tn_w2(a, dy):
    t, kd = a.shape
    n = dy.shape[1]
    tkk, tn = _tile(kd, 512), _tile(n, 2048)
    tm = _tile(t, (4 << 20) // tn, 8)

    def dot_fn(r, acc):
        acc[0][...] += _dot(r[0][...], r[1][...], TN)

    def epi_fn(r, acc, out):
        out[0][...] = acc[0][...].astype(BF16)

    return _gemm(
        "tn_w2", (kd // tkk, n // tn, t // tm), [a, dy],
        [pl.BlockSpec((tm, tkk), lambda i, j, m: (m, i)), pl.BlockSpec((tm, tn), lambda i, j, m: (m, j))],
        [_sds((kd, n), BF16)], [pl.BlockSpec((tkk, tn), lambda i, j, m: (i, j))],
        [(tkk, tn)], dot_fn, epi_fn)[0]


def _ln_bwd(dh_parts, xhat, rstd, g, coef, target=None):
    t, d = xhat.shape
    tm = _tile(t, 256, 8)
    n_parts = len(dh_parts)
    with_loss = target is not None
    ins = list(dh_parts) + [xhat, rstd, g] + ([target] if with_loss else [])

    def body(*refs):
        parts = refs[:n_parts]
        xh_ref, rstd_ref, g_ref = refs[n_parts:n_parts + 3]
        o = n_parts + 3 + (1 if with_loss else 0)
        dres_ref, dyb_ref, dg_ref, db_ref, loss_ref = refs[o:o + 5]
        i = pl.program_id(0)
        dh = parts[0][...]
        for p in parts[1:]:
            dh = dh + p[...]
        if with_loss:
            e = dh - refs[o - 1][...]
            part = 0.5 * jnp.sum(jnp.sum(e * e, axis=-1, keepdims=True) * (1.0 / d), axis=0, keepdims=True)
            dh = e * (1.0 / d)
        else:
            part = jnp.zeros((1, 1), F32)
        xh = xh_ref[...]
        dxh = dh * g_ref[...]
        m1 = jnp.mean(dxh, axis=-1, keepdims=True)
        m2 = jnp.mean(dxh * xh, axis=-1, keepdims=True)
        dv = rstd_ref[...] * (dxh - m1 - xh * m2)
        dres_ref[...] = ALPHA * dv
        dyb_ref[...] = (coef * dv).astype(BF16)

        @pl.when(i == 0)
        def _():
            dg_ref[...] = jnp.zeros(dg_ref.shape, F32)
            db_ref[...] = jnp.zeros(db_ref.shape, F32)
            loss_ref[...] = jnp.zeros(loss_ref.shape, F32)

        dg_ref[...] += jnp.sum(dh * xh, axis=0, keepdims=True)
        db_ref[...] += jnp.sum(dh, axis=0, keepdims=True)
        loss_ref[...] += jnp.broadcast_to(part, loss_ref.shape)

    row = pl.BlockSpec((tm, d), lambda i: (i, 0))
    vec = pl.BlockSpec((1, d), lambda i: (0, 0))
    return pl.pallas_call(
        body, name="ln_bwd", grid=(t // tm,),
        in_specs=[row] * n_parts + [row, pl.BlockSpec((tm, 1), lambda i: (i, 0)), vec] + ([row] if with_loss else []),
        out_specs=[row, row, vec, vec, pl.BlockSpec((1, 128), lambda i: (0, 0))],
        out_shape=[_sds((t, d), F32), _sds((t, d), BF16), _sds((1, d), F32), _sds((1, d), F32), _sds((1, 128), F32)],
        compiler_params=_params(0, 1))(*ins)


def _cast_bf16(x2d):
    t, d = x2d.shape
    tm = _tile(t, 512, 8)

    def body(x_ref, o_ref):
        o_ref[...] = x_ref[...].astype(BF16)

    row = pl.BlockSpec((tm, d), lambda i: (i, 0))
    return pl.pallas_call(body, name="cast_bf16", grid=(t // tm,), in_specs=[row], out_specs=row,
                          out_shape=_sds((t, d), BF16), compiler_params=_params(1, 0))(x2d)


def _lower_bound(table):
    t0, t1 = table[0:1, :], table[1:2, :]
    m = jnp.maximum(t0, t1)
    e0, e1 = jnp.exp(t0 - m), jnp.exp(t1 - m)
    return e0 / (e0 + e1)


def _chunk_tri(rows, upper):
    r = lax.broadcasted_iota(jnp.int32, (rows, rows), 0)
    s = lax.broadcasted_iota(jnp.int32, (rows, rows), 1)
    shift = HGRN_CHUNK.bit_length() - 1
    same = lax.shift_right_logical(r, shift) == lax.shift_right_logical(s, shift)
    return same & ((r <= s) if upper else (r >= s))


def _tri_apply(x, upper):
    tri = _chunk_tri(x.shape[0], upper).astype(F32).astype(BF16)
    hi = x.astype(BF16)
    r1 = x - hi.astype(F32)
    mid = r1.astype(BF16)
    lo = (r1 - mid.astype(F32)).astype(BF16)
    return _dot(tri, hi, NN) + _dot(tri, mid, NN) + _dot(tri, lo, NN)


@functools.partial(jax.custom_vjp, nondiff_argnums=(1,))
def _chunk_cumsum(x, upper):
    return _tri_apply(x, upper)


def _chunk_cumsum_fwd(x, upper):
    return _tri_apply(x, upper), None


def _chunk_cumsum_bwd(upper, _, g):
    return (_tri_apply(g, not upper),)


_chunk_cumsum.defvjp(_chunk_cumsum_fwd, _chunk_cumsum_bwd)


def _hgrn_blocks(units, uppers):
    c = HGRN_CHUNK
    rows = units[0][0].shape[0]
    n_sub = rows // c
    ids = range(len(units))
    chunk = lax.shift_right_logical(lax.broadcasted_iota(jnp.int32, (rows, HEAD), 0), c.bit_length() - 1)
    zero = jnp.zeros((rows, HEAD), BF16)

    def expand(a):
        return jnp.concatenate([jnp.where(chunk == n, a, zero) for n in range(n_sub)], axis=1)

    fs = [u[3] + (1.0 - u[3]) * _sigmoid(u[1]) for u in units]
    lgs = [jnp.log(f) for f in fs]
    cums = [_chunk_cumsum(lgs[i], uppers[i]) for i in ids]
    tots = [[jnp.sum(lg[n * c:(n + 1) * c], axis=0, keepdims=True) for n in range(n_sub)] for lg in lgs]
    qd, kd, ke, vb = [], [], [], []
    for i in ids:
        totb = jnp.concatenate([jnp.broadcast_to(t, (c, HEAD)) for t in tots[i]], axis=0)
        kk = 1.0 - fs[i]
        qd.append((_silu(units[i][0]) * jnp.exp(cums[i])).astype(BF16))
        kd.append((kk * jnp.exp(-cums[i])).astype(BF16))
        ke.append((kk * jnp.exp(totb - cums[i])).astype(BF16))
        vb.append(units[i][2].astype(BF16))
    scores = [_dot(qd[i], kd[i], NT) for i in ids]
    kvs = [_dot(vb[i], expand(ke[i]), TN) for i in ids]
    outs = []
    for i in ids:
        a = jnp.where(_chunk_tri(rows, uppers[i]), scores[i], 0.0).astype(BF16)
        st = units[i][4]
        entering = [None] * n_sub
        for n in (range(n_sub - 1, -1, -1) if uppers[i] else range(n_sub)):
            entering[n] = st.astype(BF16)
            st = st * jnp.exp(tots[i][n]) + kvs[i][:, n * HEAD:(n + 1) * HEAD]
        outs.append((a, jnp.concatenate(entering, axis=1), st))
    res = []
    for i in ids:
        a, entering, st = outs[i]
        res.append((_dot(a, vb[i], NN) + _dot(expand(qd[i]), entering, NT), st))
    return res


def _hgrn_fwd(proj3, lb_f, lb_b, norm_g, d, side=None):
    b_, s_, _ = proj3.shape
    nh = d // HEAD
    hb = HGRN_FWD_HEADS
    wid = hb * HEAD
    rows = HGRN_CHUNK * HGRN_SUB
    n_blk = s_ // rows

    def body(hq_ref, hff_ref, hfb_ref, hi_ref, hog_ref, lbf_ref, lbb_ref, g_ref, oraw_ref, out_ref, of_ref, ob_ref):
        lbf, lbb = _lower_bound(lbf_ref[...]), _lower_bound(lbb_ref[...])

        def step(j, sts):
            rf = pl.ds(pl.multiple_of(j * rows, rows), rows)
            rb = pl.ds(pl.multiple_of((n_blk - 1 - j) * rows, rows), rows)
            units = []
            for hh in range(hb):
                cs = slice(hh * HEAD, (hh + 1) * HEAD)
                units.append((hq_ref[rf, cs], hff_ref[rf, cs], hi_ref[rf, cs], lbf[:, cs], sts[2 * hh]))
                units.append((hq_ref[rb, cs], hfb_ref[rb, cs], hi_ref[rb, cs], lbb[:, cs], sts[2 * hh + 1]))
            res = _hgrn_blocks(units, [False, True] * hb)
            for hh in range(hb):
                cs = slice(hh * HEAD, (hh + 1) * HEAD)
                of_ref[rf, cs] = res[2 * hh][0]
                ob_ref[rb, cs] = res[2 * hh + 1][0]
            return tuple(r[1] for r in res)

        z = jnp.zeros((HEAD, HEAD), F32)
        lax.fori_loop(0, n_blk, step, (z,) * (2 * hb))
        for hh in range(hb):
            cs = slice(hh * HEAD, (hh + 1) * HEAD)
            o = of_ref[:, cs] + ob_ref[:, cs]
            oraw_ref[:, cs] = o
            on = o * lax.rsqrt(jnp.mean(o * o, axis=-1, keepdims=True) + LN_EPS)
            out_ref[:, cs] = (on * g_ref[:, cs] * _silu(hog_ref[:, cs])).astype(BF16)

    def col(k):
        return pl.BlockSpec((None, s_, wid), lambda h, b: (b, 0, k * (nh // hb) + h))

    tab = pl.BlockSpec((2, wid), lambda h, b: (0, h))
    oblk = pl.BlockSpec((None, s_, wid), lambda h, b: (b, 0, h))
    return _run(
        body, "hgrn_fwd", (nh // hb, b_), [proj3, proj3, proj3, proj3, proj3, lb_f, lb_b, norm_g],
        [col(0), col(1), col(2), col(3), col(4), tab, tab, pl.BlockSpec((1, wid), lambda h, b: (0, h))],
        [_sds((b_, s_, d), F32), _sds((b_, s_, d), BF16)], [oblk, oblk],
        [pltpu.VMEM((s_, wid), F32), pltpu.VMEM((s_, wid), F32)], 0, side)


def _hgrn_bwd(proj3, lb_f, lb_b, norm_g, o_raw, do_a, d, side=None):
    b_, s_, _ = proj3.shape
    nh = d // HEAD
    hb = HGRN_BWD_HEADS
    wid = hb * HEAD
    rows = HGRN_CHUNK * HGRN_SUB
    n_blk = s_ // rows

    def body(hq_ref, hff_ref, hfb_ref, hi_ref, hog_ref, lbf_ref, lbb_ref, g_ref, oraw_ref, doa_ref,
             dq_ref, dff_ref, dfb_ref, di_ref, dog_ref, dlbf_ref, dlbb_ref, dg_ref,
             st_ref, dor_ref, dq2_ref, di2_ref):
        b = pl.program_id(1)
        tab_f, tab_b = lbf_ref[...], lbb_ref[...]
        lbf, lbb = _lower_bound(tab_f), _lower_bound(tab_b)

        dg_parts = []
        for hh in range(hb):
            cs = slice(hh * HEAD, (hh + 1) * HEAD)
            o, doa, hog, g = oraw_ref[:, cs], doa_ref[:, cs], hog_ref[:, cs], g_ref[:, cs]
            rs = lax.rsqrt(jnp.mean(o * o, axis=-1, keepdims=True) + LN_EPS)
            on = o * rs
            sg = _sigmoid(hog)
            gate = hog * sg
            dog_ref[:, cs] = (doa * on * g * sg * (1.0 + hog * (1.0 - sg))).astype(BF16)
            don = doa * g * gate
            dor_ref[:, cs] = rs * (don - on * jnp.mean(don * on, axis=-1, keepdims=True))
            dg_parts.append(jnp.sum(doa * on * gate, axis=0, keepdims=True))

        def fwd_step(j, carry):
            jb = n_blk - 1 - j
            rf = pl.ds(pl.multiple_of(j * rows, rows), rows)
            rb = pl.ds(pl.multiple_of(jb * rows, rows), rows)
            units = []
            for hh in range(hb):
                cs = slice(hh * HEAD, (hh + 1) * HEAD)
                st_ref[2 * hh, j] = carry[2 * hh]
                st_ref[2 * hh + 1, jb] = carry[2 * hh + 1]
                units.append((hq_ref[rf, cs], hff_ref[rf, cs], hi_ref[rf, cs], lbf[:, cs], carry[2 * hh]))
                units.append((hq_ref[rb, cs], hfb_ref[rb, cs], hi_ref[rb, cs], lbb[:, cs], carry[2 * hh + 1]))
            return tuple(r[1] for r in _hgrn_blocks(units, uppers))

        uppers = [False, True] * hb
        z = jnp.zeros((HEAD, HEAD), F32)
        lax.fori_loop(0, n_blk, fwd_step, (z,) * (2 * hb))

        def bwd_step(j, carry):
            gs, dls = carry
            jf = n_blk - 1 - j
            rf = pl.ds(pl.multiple_of(jf * rows, rows), rows)
            rb = pl.ds(pl.multiple_of(j * rows, rows), rows)
            flat, cots = [], []
            for hh in range(hb):
                cs = slice(hh * HEAD, (hh + 1) * HEAD)
                flat += [hq_ref[rf, cs], hff_ref[rf, cs], hi_ref[rf, cs], lbf[:, cs], st_ref[2 * hh, jf],
                         hq_ref[rb, cs], hfb_ref[rb, cs], hi_ref[rb, cs], lbb[:, cs], st_ref[2 * hh + 1, j]]
                cots += [(dor_ref[rf, cs], gs[2 * hh]), (dor_ref[rb, cs], gs[2 * hh + 1])]
            _, vjp = jax.vjp(lambda *a: _hgrn_blocks([a[5 * i:5 * i + 5] for i in range(2 * hb)], uppers), *flat)
            grads = vjp(cots)
            gs, dls = list(gs), list(dls)
            for hh in range(hb):
                cs = slice(hh * HEAD, (hh + 1) * HEAD)
                dq, df, di, dl, gs[2 * hh] = grads[10 * hh:10 * hh + 5]
                dq_ref[rf, cs] = dq.astype(BF16)
                dff_ref[rf, cs] = df.astype(BF16)
                di_ref[rf, cs] = di.astype(BF16)
                dls[2 * hh] = dls[2 * hh] + dl
                dq, df, di, dl, gs[2 * hh + 1] = grads[10 * hh + 5:10 * hh + 10]
                dq2_ref[rb, cs] = dq
                dfb_ref[rb, cs] = df.astype(BF16)
                di2_ref[rb, cs] = di
                dls[2 * hh + 1] = dls[2 * hh + 1] + dl
            return tuple(gs), tuple(dls)

        zl = jnp.zeros((1, HEAD), F32)
        _, dls = lax.fori_loop(0, n_blk, bwd_step, ((z,) * (2 * hb), (zl,) * (2 * hb)))
        dq_ref[...] = (dq_ref[...].astype(F32) + dq2_ref[...]).astype(BF16)
        di_ref[...] = (di_ref[...].astype(F32) + di2_ref[...]).astype(BF16)

        _, vjp_tf = jax.vjp(_lower_bound, tab_f)
        _, vjp_tb = jax.vjp(_lower_bound, tab_b)

        @pl.when(b == 0)
        def _():
            dlbf_ref[...] = jnp.zeros(dlbf_ref.shape, F32)
            dlbb_ref[...] = jnp.zeros(dlbb_ref.shape, F32)
            dg_ref[...] = jnp.zeros(dg_ref.shape, F32)

        dlbf_ref[...] += vjp_tf(jnp.concatenate(dls[0::2], axis=1))[0]
        dlbb_ref[...] += vjp_tb(jnp.concatenate(dls[1::2], axis=1))[0]
        dg_ref[...] += jnp.concatenate(dg_parts, axis=1)

    def col(k):
        return pl.BlockSpec((None, s_, wid), lambda h, b: (b, 0, k * (nh // hb) + h))

    tab = pl.BlockSpec((2, wid), lambda h, b: (0, h))
    vec = pl.BlockSpec((1, wid), lambda h, b: (0, h))
    oblk = pl.BlockSpec((None, s_, wid), lambda h, b: (b, 0, h))
    seq = _sds((b_, s_, d), BF16)
    return _run(
        body, "hgrn_bwd", (nh // hb, b_), [proj3, proj3, proj3, proj3, proj3, lb_f, lb_b, norm_g, o_raw, do_a],
        [col(0), col(1), col(2), col(3), col(4), tab, tab, vec, oblk, oblk],
        [seq, seq, seq, seq, seq, _sds((2, d), F32), _sds((2, d), F32), _sds((1, d), F32)],
        [oblk, oblk, oblk, oblk, oblk, tab, tab, vec],
        [pltpu.VMEM((2 * hb, n_blk, HEAD, HEAD), F32),
         pltpu.VMEM((s_, wid), F32), pltpu.VMEM((s_, wid), F32), pltpu.VMEM((s_, wid), F32)], 1, side)


def _rope_tables(s_):
    inv = ROPE_THETA ** (-jnp.arange(0, ROPE_DIM, 2, dtype=F32) / ROPE_DIM)
    ang = jnp.arange(s_, dtype=F32)[:, None] * inv
    cos, sin = jnp.cos(ang), jnp.sin(ang)
    rest = HEAD - ROPE_DIM
    ctab = jnp.concatenate([cos, cos, jnp.ones((s_, rest), F32)], axis=1)
    stab = jnp.concatenate([-sin, sin, jnp.zeros((s_, rest), F32)], axis=1)
    half = ROPE_DIM // 2
    perm = np.zeros((HEAD, HEAD), np.float32)
    for i in range(half):
        perm[i + half, i] = 1.0
        perm[i, i + half] = 1.0
    return ctab, stab, jnp.asarray(perm)


ATTN_GROUP = 8


def _attn_tiles(qkv, tabs, offs, perm, half):
    hi = lax.Precision.HIGHEST
    ids = range(len(qkv))
    qrot = [_dot(qkv[i][0], perm, NN, precision=hi) for i in ids]
    krot = [_dot(qkv[i][1], perm, NN, precision=hi) for i in ids]
    qs = [(qkv[i][0] * tabs[i][0] + qrot[i] * tabs[i][1]).astype(BF16) for i in ids]
    ks = [(qkv[i][1] * tabs[i][2] + krot[i] * tabs[i][3]).astype(BF16) for i in ids]
    ss = [_dot(qs[i], ks[i], NT) * (HEAD ** -0.5) for i in ids]
    ps, dens, lses = [], [], []
    for i in ids:
        qi = offs[i][0] + lax.broadcasted_iota(jnp.int32, ss[i].shape, 0)
        kj = offs[i][1] + lax.broadcasted_iota(jnp.int32, ss[i].shape, 1)
        s = jnp.where(jnp.abs(qi - kj) <= half, ss[i], NEG_INF)
        m = lax.stop_gradient(jnp.max(s, axis=-1, keepdims=True))
        p = jnp.exp(s - m)
        den = jnp.sum(p, axis=-1, keepdims=True)
        ps.append(p.astype(BF16))
        dens.append(den)
        lses.append(m + jnp.log(den))
    res = []
    for i in ids:
        o = _dot(ps[i], qkv[i][2].astype(BF16), NN) / dens[i]
        res.append((o, jnp.broadcast_to(lses[i], o.shape)))
    return res


def _attn_tiling(seg):
    tq = seg if seg <= 256 else 256
    kw = seg if seg <= 512 else 512
    tiles = []
    for i in range(seg // tq):
        ws = min(max(i * tq - (kw - tq) // 2, 0), seg - kw)
        tiles.append((i * tq, ws))
    return tq, kw, tiles


def _attn_specs(s_, d, g):
    c0 = (5 * d + g * 3 * ATTN_OUT) // HEAD

    def col(part):
        return pl.BlockSpec((None, s_, HEAD), lambda b, h: (b, 0, c0 + part * ATTN_HEADS + h))

    tab = pl.BlockSpec((s_, HEAD), lambda b, h: (0, 0))
    perm = pl.BlockSpec((HEAD, HEAD), lambda b, h: (0, 0))
    oblk = pl.BlockSpec((None, s_, HEAD), lambda b, h: (b, 0, h))
    return col, tab, perm, oblk


def _rows(r, first, count, dil):
    return pl.ds(r + dil * first, count, stride=dil) if dil > 1 else pl.ds(first, count)


def _attn_operands(group, dil, tq, kw, q_ref, k_ref, v_ref, c_ref, s_ref):
    qkv, tabs = [], []
    for r, q0, k0 in group:
        rq, rk = _rows(r, q0, tq, dil), _rows(r, k0, kw, dil)
        qkv.append((q_ref[rq, :], k_ref[rk, :], v_ref[rk, :]))
        tabs.append((c_ref[rq, :], s_ref[rq, :], c_ref[rk, :], s_ref[rk, :]))
    return qkv, tabs


def _attn_fwd(proj3, ctab, stab, perm, d, g):
    window, dil = ATTN_GROUPS[g]
    b_, s_, _ = proj3.shape
    seg = s_ // dil
    half = window // (2 * dil)
    tq, kw, tiles = _attn_tiling(seg)

    work = [(r, q0, k0) for r in range(dil) for q0, k0 in tiles]

    def body(q_ref, k_ref, v_ref, c_ref, s_ref, p_ref, o_ref, l_ref):
        pm = p_ref[...]
        for g0 in range(0, len(work), ATTN_GROUP):
            group = work[g0:g0 + ATTN_GROUP]
            qkv, tabs = _attn_operands(group, dil, tq, kw, q_ref, k_ref, v_ref, c_ref, s_ref)
            res = _attn_tiles(qkv, tabs, [(q0, k0) for _, q0, k0 in group], pm, half)
            for (r, q0, _), (o, l) in zip(group, res):
                o_ref[_rows(r, q0, tq, dil), :] = o
                l_ref[_rows(r, q0, tq, dil), :] = l

    col, tab, pspec, oblk = _attn_specs(s_, d, g)
    shp = _sds((b_, s_, ATTN_OUT), F32)
    return pl.pallas_call(
        body, name=f"attn_fwd_d{dil}", grid=(b_, ATTN_HEADS), in_specs=[col(0), col(1), col(2), tab, tab, pspec],
        out_specs=[oblk, oblk], out_shape=[shp, shp], compiler_params=_params(2, 0))(
            proj3, proj3, proj3, ctab, stab, perm)


def _attn_bwd(proj3, ctab, stab, perm, do, dl, d, g):
    window, dil = ATTN_GROUPS[g]
    b_, s_, _ = proj3.shape
    seg = s_ // dil
    half = window // (2 * dil)
    tq, kw, tiles = _attn_tiling(seg)

    work = [(r, q0, k0) for r in range(dil) for q0, k0 in tiles]

    def body(q_ref, k_ref, v_ref, c_ref, s_ref, p_ref, do_ref, dl_ref, dq_ref, dk_ref, dv_ref, dq_s, dk_s, dv_s):
        pm = p_ref[...]
        dk_s[...] = jnp.zeros(dk_s.shape, F32)
        dv_s[...] = jnp.zeros(dv_s.shape, F32)
        for g0 in range(0, len(work), ATTN_GROUP):
            group = work[g0:g0 + ATTN_GROUP]
            qkv, tabs = _attn_operands(group, dil, tq, kw, q_ref, k_ref, v_ref, c_ref, s_ref)
            offs = [(q0, k0) for _, q0, k0 in group]
            _, vjp = jax.vjp(
                lambda *a: _attn_tiles([a[3 * i:3 * i + 3] for i in range(len(group))], tabs, offs, pm, half),
                *[x for tile in qkv for x in tile])
            grads = vjp([(do_ref[_rows(r, q0, tq, dil), :], dl_ref[_rows(r, q0, tq, dil), :]) for r, q0, _ in group])
            for i, (r, q0, k0) in enumerate(group):
                rk = _rows(r, k0, kw, dil)
                dq_s[_rows(r, q0, tq, dil), :] = grads[3 * i]
                dk_s[rk, :] += grads[3 * i + 1]
                dv_s[rk, :] += grads[3 * i + 2]
        dq_ref[...] = dq_s[...].astype(BF16)
        dk_ref[...] = dk_s[...].astype(BF16)
        dv_ref[...] = dv_s[...].astype(BF16)

    col, tab, pspec, oblk = _attn_specs(s_, d, g)
    shp = _sds((b_, s_, ATTN_OUT), BF16)
    full = pltpu.VMEM((s_, HEAD), F32)
    return pl.pallas_call(
        body, name=f"attn_bwd_d{dil}", grid=(b_, ATTN_HEADS),
        in_specs=[col(0), col(1), col(2), tab, tab, pspec, oblk, oblk], out_specs=[oblk, oblk, oblk],
        out_shape=[shp, shp, shp], scratch_shapes=[full, full, full],
        compiler_params=_params(2, 0))(proj3, proj3, proj3, ctab, stab, perm, do, dl)


def _combine(os_, ls_):
    m = jnp.maximum(jnp.maximum(ls_[0], ls_[1]), ls_[2])
    es = [jnp.exp(l - m) for l in ls_]
    return (es[0] * os_[0] + es[1] * os_[1] + es[2] * os_[2]) / (es[0] + es[1] + es[2])


def _combine_fwd(os_, ls_):
    t, w = os_[0].shape
    tm = _tile(t, 512, 8)

    def body(*refs):
        refs[6][...] = _combine([r[...] for r in refs[:3]], [r[...] for r in refs[3:6]]).astype(BF16)

    row = pl.BlockSpec((tm, w), lambda i: (i, 0))
    return pl.pallas_call(body, name="combine_fwd", grid=(t // tm,), in_specs=[row] * 6, out_specs=row,
                          out_shape=_sds((t, w), BF16), compiler_params=_params(1, 0))(*os_, *ls_)


def _combine_bwd(os_, ls_, dob):
    t, w = os_[0].shape
    tm = _tile(t, 512, 8)

    def body(*refs):
        _, vjp = jax.vjp(lambda *a: _combine(a[:3], a[3:]), *[r[...] for r in refs[:6]])
        for r, g in zip(refs[7:], vjp(refs[6][...])):
            r[...] = g

    row = pl.BlockSpec((tm, w), lambda i: (i, 0))
    return pl.pallas_call(body, name="combine_bwd", grid=(t // tm,), in_specs=[row] * 7, out_specs=[row] * 6,
                          out_shape=[_sds((t, w), F32)] * 6, compiler_params=_params(1, 0))(*os_, *ls_, dob)


ROW_SHARDED = ("ffn1_out", "ffn2_out", "wa", "mix_out")


def _local_step(x, target, fulls, p, sc_arr):
    b_, s_, d = x.shape
    t = b_ * s_
    x2 = x.reshape(t, d)
    gate_col = 5 * d + QKV_W
    c_arr = sc_arr[1:2]
    w = {}

    def arrived(keys, arrays):
        for k, a in zip(keys, arrays):
            w[k] = a.reshape(a.shape[0] * a.shape[1], a.shape[2]) if k in ROW_SHARDED else a

    def reduce_begin(keys):
        g3 = [gw[k].reshape(fulls[k].shape) for k in keys]
        return [_sum_halves(a, b, c_arr) for a, b in zip(g3, _swap_halves(g3))]

    def reduce_end(keys, parts, got):
        for k, a, b in zip(keys, parts, got):
            reduced[k] = _sum_chips(a, b, sc_arr)

    arrived(["ffn1_in"], _gather_weights([fulls["ffn1_in"]]))
    xb = _cast_bf16(x2)
    mix_rows = fulls["mix_in"].shape[1]
    (u1, a1), got = _ffn_up(xb, w["ffn1_in"], _gather_exchange([fulls["ffn1_out"], fulls["mix_in"]],
                                                              [None, (0, mix_rows // 2)]))
    arrived(["ffn1_out"], got[:1])
    (h1, h1b, xh1, rs1), got = _down_ln(a1, w["ffn1_out"], x2, 0.5, p["ln1_g"], p["ln1_b"],
                                        _gather_exchange(got[1:], [(mix_rows // 2, mix_rows // 2)]))
    arrived(["mix_in"], got)

    keys = ["wa", "wb", "mix_out", "ffn2_in"]
    proj, got = _mm_w3(h1b, w["mix_in"], _gather_exchange([fulls[k] for k in keys]))
    arrived(keys, got)
    n_in = proj.shape[1]
    proj3 = proj.reshape(b_, s_, n_in)
    (o_raw, oa), got = _hgrn_fwd(proj3, p["hgrn_lb_fwd"], p["hgrn_lb_bwd"], p["hgrn_norm_g"], d,
                                 _gather_exchange([fulls["ffn2_out"]]))
    arrived(["ffn2_out"], got)
    oa2 = oa.reshape(t, d)

    ctab, stab, perm = _rope_tables(s_)
    os_, ls_ = [], []
    for g in range(N_GROUPS):
        o_g, l_g = _attn_fwd(proj3, ctab, stab, perm, d, g)
        os_.append(o_g.reshape(t, ATTN_OUT))
        ls_.append(l_g.reshape(t, ATTN_OUT))
    ob = _combine_fwd(os_, ls_)

    ya = _mm_w2(oa2, w["wa"], F32)
    yb, zb = _branch_gate(ob, w["wb"], ya, proj, gate_col)
    h2, h2b, xh2, rs2 = _down_ln(zb, w["mix_out"], h1, 1.0, p["ln2_g"], p["ln2_b"])

    u2, a2 = _ffn_up(h2b, w["ffn2_in"])
    h3, _, xh3, rs3 = _down_ln(a2, w["ffn2_out"], h2, 0.5, p["ln3_g"], p["ln3_b"])

    gw, gp, reduced = {}, {}, {}
    nf = w["ffn2_in"].shape[2]

    def du_map(j, m):
        return (j // 2, m, j % 2)

    dres3, dy3, gp["ln3_g"], gp["ln3_b"], loss = _ln_bwd([h3], xh3, rs3, p["ln3_g"], 0.5, target.reshape(t, d))
    du2 = _swiglu_bwd(dy3, w["ffn2_out"], u2)
    g_out = _tn_w2(a2, dy3).reshape(fulls["ffn2_out"].shape)
    g_in, got = _tn_w3(h2b, du2, du_map, nf, _swap_exchange([g_out]))
    parts_a = [_sum_halves(g_out, got[0], c_arr)]
    dh2, got = _ffn_dx(du2, w["ffn2_in"], dres3, _both(_scatter_exchange(parts_a), _swap_exchange([g_in])))
    reduce_end(["ffn2_out"], parts_a, got[:1])
    parts_a.append(_sum_halves(g_in, got[1], c_arr))

    dres2, dmix, gp["ln2_g"], gp["ln2_b"], _ = _ln_bwd([dh2], xh2, rs2, p["ln2_g"], 1.0)
    dya, dyb, dpga, dpgb = _gate_bwd(dmix, w["mix_out"], proj, ya, yb, gate_col)
    gw["mix_out"] = _tn_w2(zb, dmix)
    do_a = _nt_w2(dya, w["wa"], F32)
    gw["wa"] = _tn_w2(oa2, dya)
    nb = w["wb"].shape[2]
    do_b = _nt_w3(dyb, w["wb"], None)
    gw["wb"] = _tn_w3(ob, dyb, lambda j, m: (m, j), nb)
    keys = ["ffn2_in", "mix_out", "wa", "wb"]
    parts_b = parts_a[1:] + reduce_begin(keys[1:])

    (dq, dff, dfb, di, dog, gp["hgrn_lb_fwd"], gp["hgrn_lb_bwd"], gp["hgrn_norm_g"]), got = _hgrn_bwd(
        proj3, p["hgrn_lb_fwd"], p["hgrn_lb_bwd"], p["hgrn_norm_g"], o_raw, do_a.reshape(b_, s_, d), d,
        _scatter_exchange(parts_b))
    reduce_end(keys, parts_b, got)

    douts = _combine_bwd(os_, ls_, do_b)
    dqkv = []
    for g in range(N_GROUPS):
        grads_g = _attn_bwd(proj3, ctab, stab, perm, douts[g].reshape(b_, s_, ATTN_OUT),
                            douts[3 + g].reshape(b_, s_, ATTN_OUT), d, g)
        dqkv += [a.reshape(t, ATTN_OUT) for a in grads_g]

    dproj = jnp.concatenate(
        [a.reshape(t, d) for a in (dq, dff, dfb, di, dog)] + dqkv + [dpga, dpgb], axis=1)
    nm = w["mix_in"].shape[2]
    gw["mix_in"] = _tn_w3(h1b, dproj, lambda j, m: (m, j), nm)
    parts_c = reduce_begin(["mix_in"])
    dh1, got = _nt_w3(dproj, w["mix_in"], dres2, _scatter_exchange(parts_c))
    reduce_end(["mix_in"], parts_c, got)

    dres1, dy1, gp["ln1_g"], gp["ln1_b"], _ = _ln_bwd([dh1], xh1, rs1, p["ln1_g"], 0.5)
    g_out = _tn_w2(a1, dy1).reshape(fulls["ffn1_out"].shape)
    du1, got = _swiglu_bwd(dy1, w["ffn1_out"], u1, _swap_exchange([g_out]))
    parts_d = [_sum_halves(g_out, got[0], c_arr)]
    gw["ffn1_in"], got = _tn_w3(xb, du1, du_map, nf, _scatter_exchange(parts_d))
    reduce_end(["ffn1_out"], parts_d, got)
    parts_e = reduce_begin(["ffn1_in"])
    dx, got = _ffn_dx(du1, w["ffn1_in"], dres1, _scatter_exchange(parts_e))
    reduce_end(["ffn1_in"], parts_e, got)

    keys = list(reduced)
    grads = dict(zip(keys, _join_halves([reduced[k] for k in keys])))
    return loss, dx.reshape(b_, s_, d), grads, gp


MESH = pl.DeviceIdType.MESH
ANY = pl.BlockSpec(memory_space=pl.ANY)


def _place():
    x, y, c = lax.axis_index("x"), lax.axis_index("y"), lax.axis_index("c")
    chips = [(1 - x, y), (x, 1 - y), (1 - x, 1 - y)]
    return x, y, c, chips, (x, y, 1 - c)


def _half_rows(c, rows):
    hr = rows // 2
    return pl.ds(pl.multiple_of(c * hr, 16), hr)


def _remote(src, dst, send, recv, dev):
    return pltpu.make_async_remote_copy(src_ref=src, dst_ref=dst, send_sem=send, recv_sem=recv,
                                        device_id=dev, device_id_type=MESH)


def _gather_weights(fulls):
    n = len(fulls)

    def body(*refs):
        _gather_start(refs[n:2 * n], refs[2 * n:])
        _gather_finish(refs[n:2 * n], refs[2 * n:])

    return pl.pallas_call(
        body, name="gather_weights", in_specs=[ANY] * n, out_specs=[ANY] * n,
        out_shape=[_sds(a.shape, a.dtype) for a in fulls], input_output_aliases={i: i for i in range(n)},
        scratch_shapes=_gather_sems(n))(*fulls)


def _gather_sems(n):
    return [pltpu.SemaphoreType.DMA((n, 3)) for _ in range(4)]


def _span_half(c, span, rows):
    r0, cnt = (0, rows) if span is None else span
    return pl.ds(pl.multiple_of(r0 + c * (cnt // 2), 16), cnt // 2)


def _gather_start(bufs, sems, spans=None):
    isend, irecv = sems[0], sems[1]
    x, y, c, chips, sib = _place()
    for i, buf in enumerate(bufs):
        blk = buf.at[2 * x + y, _span_half(c, spans and spans[i], buf.shape[1])]
        for k, chip in enumerate(chips):
            _remote(blk, blk, isend.at[i, k], irecv.at[i, k], (*chip, c)).start()


def _gather_finish(bufs, sems, spans=None):
    isend, irecv, fsend, frecv = sems
    x, y, c, chips, sib = _place()
    for i, buf in enumerate(bufs):
        mine = _span_half(c, spans and spans[i], buf.shape[1])
        for k, chip in enumerate(chips):
            blk = buf.at[2 * chip[0] + chip[1], mine]
            _remote(blk, blk, isend.at[i, k], irecv.at[i, k], (*chip, c)).wait_recv()
            _remote(blk, blk, fsend.at[i, k], frecv.at[i, k], sib).start()
    for i, buf in enumerate(bufs):
        span = spans and spans[i]
        mine, other = _span_half(c, span, buf.shape[1]), _span_half(1 - c, span, buf.shape[1])
        own = buf.at[2 * x + y, mine]
        for k, chip in enumerate(chips):
            got = buf.at[2 * chip[0] + chip[1], other]
            _remote(got, got, fsend.at[i, k], frecv.at[i, k], sib).wait_recv()
            _remote(own, own, isend.at[i, k], irecv.at[i, k], (*chip, c)).wait_send()
            blk = buf.at[2 * chip[0] + chip[1], mine]
            _remote(blk, blk, fsend.at[i, k], frecv.at[i, k], sib).wait_send()


def _gather_exchange(fulls, spans=None):
    n = len(fulls)
    return _Exchange(fulls, [_sds(a.shape, a.dtype) for a in fulls], {i: i for i in range(n)}, _gather_sems(n),
                     lambda ins, outs, sems: _gather_start(outs, sems, spans),
                     lambda ins, outs, sems: _gather_finish(outs, sems, spans))


def _swap_halves(grads):
    side = _swap_exchange(grads)
    n = len(grads)

    def body(*refs):
        side.start(refs[:n], refs[n:2 * n], refs[2 * n:])
        side.finish(refs[:n], refs[n:2 * n], refs[2 * n:])

    return pl.pallas_call(
        body, name="swap_halves", in_specs=[ANY] * n, out_specs=[ANY] * n, out_shape=side.outs,
        scratch_shapes=side.sems)(*grads)


def _swap_exchange(grads):
    n = len(grads)

    def copies(ins, outs, sems):
        x, y, c, chips, sib = _place()
        return [_remote(a.at[:, _half_rows(1 - c, a.shape[1])], b, sems[0].at[i], sems[1].at[i], sib)
                for i, (a, b) in enumerate(zip(ins, outs))]

    def start(ins, outs, sems):
        for cp in copies(ins, outs, sems):
            cp.start()

    def finish(ins, outs, sems):
        for cp in copies(ins, outs, sems):
            cp.wait()

    return _Exchange(grads, [_sds((N_SHARD, a.shape[1] // 2, a.shape[2]), a.dtype) for a in grads], {},
                     [pltpu.SemaphoreType.DMA((n,)) for _ in range(2)], start, finish)


def _both(a, b):
    i, o, s = len(a.ins), len(a.outs), len(a.sems)

    def start(ins, outs, sems):
        a.start(ins[:i], outs[:o], sems[:s])
        b.start(ins[i:], outs[o:], sems[s:])

    def finish(ins, outs, sems):
        a.finish(ins[:i], outs[:o], sems[:s])
        b.finish(ins[i:], outs[o:], sems[s:])

    aliases = dict(a.aliases)
    aliases.update({i + k: o + v for k, v in b.aliases.items()})
    return _Exchange(a.ins + b.ins, a.outs + b.outs, aliases, a.sems + b.sems, start, finish)


def _scatter_partials(parts):
    side = _scatter_exchange(parts)
    n = len(parts)

    def body(*refs):
        side.start(refs[:n], refs[n:2 * n], refs[2 * n:])
        side.finish(refs[:n], refs[n:2 * n], refs[2 * n:])

    return pl.pallas_call(
        body, name="scatter_partials", in_specs=[ANY] * n, out_specs=[ANY] * n, out_shape=side.outs,
        scratch_shapes=side.sems)(*parts)


def _scatter_copies(ins, outs, sems):
    x, y, c, chips, sib = _place()
    return [_remote(a.at[2 * chip[0] + chip[1]], b.at[k], sems[0].at[i, k], sems[1].at[i, k], (*chip, c))
            for i, (a, b) in enumerate(zip(ins, outs)) for k, chip in enumerate(chips)]


def _scatter_exchange(parts):
    n = len(parts)

    def start(ins, outs, sems):
        for cp in _scatter_copies(ins, outs, sems):
            cp.start()

    def finish(ins, outs, sems):
        for cp in _scatter_copies(ins, outs, sems):
            cp.wait()

    return _Exchange(parts, [_sds((3,) + a.shape[1:], a.dtype) for a in parts], {},
                     [pltpu.SemaphoreType.DMA((n, 3)) for _ in range(2)], start, finish)


def _join_halves(grads):
    n = len(grads)

    def body(*refs):
        bufs, send, recv = refs[n:2 * n], refs[2 * n], refs[2 * n + 1]
        x, y, c, chips, sib = _place()
        cps = []
        for i in range(n):
            blk = bufs[i].at[_half_rows(c, bufs[i].shape[0])]
            other = bufs[i].at[_half_rows(1 - c, bufs[i].shape[0])]
            cp = _remote(blk, blk, send.at[i], recv.at[i], sib)
            cp.start()
            cps.append((cp, _remote(other, other, send.at[i], recv.at[i], sib)))
        for cp, got in cps:
            cp.wait_send()
            got.wait_recv()

    dma = pltpu.SemaphoreType.DMA
    return pl.pallas_call(
        body, name="join_halves", in_specs=[ANY] * n, out_specs=[ANY] * n,
        out_shape=[_sds(a.shape, a.dtype) for a in grads], input_output_aliases={i: i for i in range(n)},
        scratch_shapes=[dma((n,)), dma((n,))])(*grads)


def _gather_rows(block):
    m_per, n = block.shape

    def body(x_ref, out_ref, send_sems, recv_sems, local_sem):
        x, y, c, chips, sibling = _place()
        me = (x, y, c)

        def rows(px, py, pc):
            return out_ref.at[pl.ds((4 * px + 2 * py + pc) * m_per, m_per), :]

        def copy(k, blk, to, src=None):
            return _remote(rows(*blk) if src is None else src, rows(*blk), send_sems.at[k], recv_sems.at[k], to)

        mine = pltpu.make_async_copy(x_ref, rows(*me), local_sem)
        mine.start()
        first = [copy(0, me, sibling, src=x_ref)]
        first += [copy(1 + j, me, (*chip, c), src=x_ref) for j, chip in enumerate(chips)]
        for cp in first:
            cp.start()
        passed = [copy(4 + j, (*chip, c), sibling) for j, chip in enumerate(chips)]
        for j, chip in enumerate(chips):
            copy(1 + j, (*chip, c), me).wait_recv()
            passed[j].start()
        copy(0, sibling, me).wait_recv()
        for j, chip in enumerate(chips):
            copy(4 + j, (*chip, 1 - c), me).wait_recv()
        for cp in first + passed:
            cp.wait_send()
        mine.wait()

    vmem = pl.BlockSpec(memory_space=pltpu.VMEM)
    dma = pltpu.SemaphoreType.DMA
    return pl.pallas_call(
        body, name="gather_rows", in_specs=[vmem], out_specs=vmem, out_shape=_sds((8 * m_per, n), block.dtype),
        scratch_shapes=[dma((7,)), dma((7,)), dma(())])(block)


def _row_tile(rows, cols):
    return _tile(rows, max(16, (1 << 20) // cols), 16)


def _sum_halves(grad, got, c_arr):
    _, hr, cols = got.shape
    tr = _row_tile(hr, cols)
    nb = hr // tr

    def body(c_ref, a_ref, b_ref, o_ref):
        o_ref[...] = (a_ref[...].astype(F32) + b_ref[...].astype(F32)).astype(BF16)

    blk = pl.BlockSpec((None, tr, cols), lambda s, i, c_ref: (s, i, 0))
    return pl.pallas_call(
        body, name="sum_halves",
        grid_spec=pltpu.PrefetchScalarGridSpec(
            num_scalar_prefetch=1, grid=(N_SHARD, nb),
            in_specs=[pl.BlockSpec((None, tr, cols), lambda s, i, c_ref: (s, c_ref[0] * nb + i, 0)), blk],
            out_specs=blk),
        out_shape=_sds(got.shape, BF16), compiler_params=_params(2, 0))(c_arr, grad, got)


def _sum_chips(part, got, sc_arr):
    _, hr, cols = got.shape
    tr = _row_tile(hr, cols)
    nb = hr // tr

    def body(s_ref, a_ref, b_ref, o_ref):
        o_ref[...] = ((a_ref[...].astype(F32) + b_ref[0].astype(F32)) + b_ref[1].astype(F32)) + b_ref[2].astype(F32)

    return pl.pallas_call(
        body, name="sum_chips",
        grid_spec=pltpu.PrefetchScalarGridSpec(
            num_scalar_prefetch=1, grid=(nb,),
            in_specs=[pl.BlockSpec((None, tr, cols), lambda i, s_ref: (s_ref[0], i, 0)),
                      pl.BlockSpec((3, tr, cols), lambda i, s_ref: (0, i, 0))],
            out_specs=pl.BlockSpec((tr, cols), lambda i, s_ref: (s_ref[1] * nb + i, 0))),
        out_shape=_sds((2 * hr, cols), F32), compiler_params=_params(1, 0))(sc_arr, part, got)


def _cast_into_slot(x2d, sc_arr):
    rows, cols = x2d.shape
    tr = _row_tile(rows, cols)

    def body(s_ref, x_ref, o_ref):
        o_ref[...] = x_ref[...].astype(BF16)

    return pl.pallas_call(
        body, name="cast_into_slot",
        grid_spec=pltpu.PrefetchScalarGridSpec(
            num_scalar_prefetch=1, grid=(rows // tr,),
            in_specs=[pl.BlockSpec((tr, cols), lambda i, s_ref: (i, 0))],
            out_specs=pl.BlockSpec((None, tr, cols), lambda i, s_ref: (s_ref[0], i, 0))),
        out_shape=_sds((N_SHARD, rows, cols), BF16), compiler_params=_params(1, 0))(sc_arr, x2d)


def _adam_math(w, g, m, v):
    m = ADAM_B1 * m + (1.0 - ADAM_B1) * g
    v = ADAM_B2 * v + (1.0 - ADAM_B2) * (g * g)
    m_hat = m / (1.0 - ADAM_B1 ** ADAM_STEP)
    v_hat = v / (1.0 - ADAM_B2 ** ADAM_STEP)
    delta = -ADAM_LR * (m_hat / (jnp.sqrt(v_hat) + ADAM_EPS) + ADAM_WD * w)
    return delta, m, v


def _adamw(w, g, m, v):
    rows, cols = w.shape
    tr = _tile(rows, max(8, (1 << 19) // cols), 8)

    def body(w_ref, g_ref, m_ref, v_ref, go_ref, d_ref, mo_ref, vo_ref):
        g_ = g_ref[...]
        go_ref[...] = g_
        d_ref[...], mo_ref[...], vo_ref[...] = _adam_math(w_ref[...], g_, m_ref[...], v_ref[...])

    blk = pl.BlockSpec((tr, cols), lambda i: (i, 0))
    return pl.pallas_call(
        body, name="adamw", grid=(rows // tr,), in_specs=[blk] * 4, out_specs=[blk] * 4,
        out_shape=[_sds((rows, cols), F32)] * 4, compiler_params=_params(1, 0))(w, g, m, v)


def _adamw_small(gathered, w, m, v):
    rows, cols = w.shape

    def body(a_ref, w_ref, m_ref, v_ref, go_ref, d_ref, mo_ref, vo_ref):
        g_ = a_ref[pl.ds(0, rows), :]
        for k in range(1, 8):
            g_ = g_ + a_ref[pl.ds(k * rows, rows), :]
        go_ref[...] = g_
        d_ref[...], mo_ref[...], vo_ref[...] = _adam_math(w_ref[...], g_, m_ref[...], v_ref[...])

    vmem = pl.BlockSpec(memory_space=pltpu.VMEM)
    return pl.pallas_call(
        body, name="adamw_small", in_specs=[vmem] * 4, out_specs=[vmem] * 4,
        out_shape=[_sds((rows, cols), F32)] * 4)(gathered, w, m, v)


BIG = ("ffn1_w_in", "ffn1_w_out", "mix_w_in", "w_branch_a", "w_branch_b", "mix_w_out", "ffn2_w_in", "ffn2_w_out")
BIG_KEY = {"ffn1_w_in": "ffn1_in", "ffn1_w_out": "ffn1_out", "mix_w_in": "mix_in", "w_branch_a": "wa",
           "w_branch_b": "wb", "mix_w_out": "mix_out", "ffn2_w_in": "ffn2_in", "ffn2_w_out": "ffn2_out"}
SMALL = ("ln1_g", "ln1_b", "hgrn_lb_fwd", "hgrn_lb_bwd", "hgrn_norm_g", "ln2_g", "ln2_b", "ln3_g", "ln3_b")
ORDER = ("ffn1_w_in", "ffn1_w_out", "ln1_g", "ln1_b", "mix_w_in", "hgrn_lb_fwd", "hgrn_lb_bwd", "hgrn_norm_g",
         "w_branch_a", "w_branch_b", "mix_w_out", "ln2_g", "ln2_b", "ffn2_w_in", "ffn2_w_out", "ln3_g", "ln3_b")
SMALL_ROWS = 16


def _pack_small(d):
    rows = jnp.concatenate([d[k].reshape(-1, d[k].shape[-1]) for k in SMALL], axis=0)
    return jnp.pad(rows, ((0, SMALL_ROWS - rows.shape[0]), (0, 0)))


def _unpack_small(a, like):
    out, r = {}, 0
    for k in SMALL:
        n = like[k].shape[0]
        out[k] = a[r:r + n].reshape(like[k].shape)
        r += n
    return out


def kernel(x, ffn1_w_in, ffn1_w_out, ln1_g, ln1_b, mix_w_in, hgrn_lb_fwd, hgrn_lb_bwd, hgrn_norm_g, w_branch_a, w_branch_b, mix_w_out, ln2_g, ln2_b, ffn2_w_in, ffn2_w_out, ln3_g, ln3_b, loss_target, m_ffn1_w_in, m_ffn1_w_out, m_ln1_g, m_ln1_b, m_mix_w_in, m_hgrn_lb_fwd, m_hgrn_lb_bwd, m_hgrn_norm_g, m_w_branch_a, m_w_branch_b, m_mix_w_out, m_ln2_g, m_ln2_b, m_ffn2_w_in, m_ffn2_w_out, m_ln3_g, m_ln3_b, v_ffn1_w_in, v_ffn1_w_out, v_ln1_g, v_ln1_b, v_mix_w_in, v_hgrn_lb_fwd, v_hgrn_lb_bwd, v_hgrn_norm_g, v_w_branch_a, v_w_branch_b, v_mix_w_out, v_ln2_g, v_ln2_b, v_ffn2_w_in, v_ffn2_w_out, v_ln3_g, v_ln3_b):
    wts = dict(ffn1_w_in=ffn1_w_in, ffn1_w_out=ffn1_w_out, ln1_g=ln1_g, ln1_b=ln1_b, mix_w_in=mix_w_in,
               hgrn_lb_fwd=hgrn_lb_fwd, hgrn_lb_bwd=hgrn_lb_bwd, hgrn_norm_g=hgrn_norm_g, w_branch_a=w_branch_a,
               w_branch_b=w_branch_b, mix_w_out=mix_w_out, ln2_g=ln2_g, ln2_b=ln2_b, ffn2_w_in=ffn2_w_in,
               ffn2_w_out=ffn2_w_out, ln3_g=ln3_g, ln3_b=ln3_b)
    mom = dict(ffn1_w_in=m_ffn1_w_in, ffn1_w_out=m_ffn1_w_out, ln1_g=m_ln1_g, ln1_b=m_ln1_b, mix_w_in=m_mix_w_in,
               hgrn_lb_fwd=m_hgrn_lb_fwd, hgrn_lb_bwd=m_hgrn_lb_bwd, hgrn_norm_g=m_hgrn_norm_g,
               w_branch_a=m_w_branch_a, w_branch_b=m_w_branch_b, mix_w_out=m_mix_w_out, ln2_g=m_ln2_g, ln2_b=m_ln2_b,
               ffn2_w_in=m_ffn2_w_in, ffn2_w_out=m_ffn2_w_out, ln3_g=m_ln3_g, ln3_b=m_ln3_b)
    var = dict(ffn1_w_in=v_ffn1_w_in, ffn1_w_out=v_ffn1_w_out, ln1_g=v_ln1_g, ln1_b=v_ln1_b, mix_w_in=v_mix_w_in,
               hgrn_lb_fwd=v_hgrn_lb_fwd, hgrn_lb_bwd=v_hgrn_lb_bwd, hgrn_norm_g=v_hgrn_norm_g,
               w_branch_a=v_w_branch_a, w_branch_b=v_w_branch_b, mix_w_out=v_mix_w_out, ln2_g=v_ln2_g, ln2_b=v_ln2_b,
               ffn2_w_in=v_ffn2_w_in, ffn2_w_out=v_ffn2_w_out, ln3_g=v_ln3_g, ln3_b=v_ln3_b)
    shard = (2 * lax.axis_index("x") + lax.axis_index("y")).astype(jnp.int32)
    sc_arr = jnp.stack([shard, lax.axis_index("c").astype(jnp.int32)])

    shard2d = {k: wts[k].reshape(wts[k].shape[1:]) for k in BIG}
    fulls = {BIG_KEY[k]: _cast_into_slot(shard2d[k], sc_arr) for k in BIG}
    p = {k: wts[k] for k in SMALL}

    loss, grad_x, grads, gp = _local_step(x, loss_target, fulls, p, sc_arr)
    loss = lax.psum(loss[0, 0], ("x", "y", "c"))

    out_g, out_d, out_m, out_v = {}, {}, {}, {}
    for k in BIG:
        g = grads[BIG_KEY[k]]
        shp = wts[k].shape
        res = _adamw(shard2d[k], g, mom[k].reshape(shp[1:]), var[k].reshape(shp[1:]))
        out_g[k], out_d[k], out_m[k], out_v[k] = [a.reshape(shp) for a in res]

    gathered = _gather_rows(_pack_small(gp))
    res = _adamw_small(gathered, _pack_small(wts), _pack_small(mom), _pack_small(var))
    for dst, a in zip((out_g, out_d, out_m, out_v), res):
        dst.update(_unpack_small(a, wts))

    return (loss, grad_x, *[out_g[k] for k in ORDER], *[out_d[k] for k in ORDER],
            *[out_m[k] for k in ORDER], *[out_v[k] for k in ORDER])
```

```python
import functools

import numpy as np
import jax
import jax.numpy as jnp
from jax import lax
from jax.experimental import pallas as pl
from jax.experimental.pallas import tpu as pltpu

F32 = jnp.float32
BF16 = jnp.bfloat16

HEAD = 128
ATTN_GROUPS = ((128, 1), (512, 4), (2048, 16))
ATTN_HEADS = 4
N_GROUPS = len(ATTN_GROUPS)
QKV_W = N_GROUPS * 3 * ATTN_HEADS * HEAD
ATTN_OUT = ATTN_HEADS * HEAD
ROPE_THETA = 500000.0
ROPE_DIM = HEAD // 4
HGRN_CHUNK = 32
HGRN_FWD_HEADS = 2
HGRN_BWD_HEADS = 2
HGRN_SUB = 4
ALPHA = 2.0 ** 0.25
LN_EPS = 1e-5
NEG_INF = -1e30
ADAM_LR, ADAM_B1, ADAM_B2, ADAM_EPS, ADAM_WD, ADAM_STEP = 0.001, 0.9, 0.999, 1e-08, 0.01, 10

N_SHARD = 4
VMEM_LIMIT = 56 * 1024 * 1024

NN = ((1,), (0,))
NT = ((1,), (1,))
TN = ((0,), (0,))


def _dot(a, b, dims, precision=None):
    return lax.dot_general(a, b, (dims, ((), ())), preferred_element_type=F32, precision=precision)


def _tile(n, pref, mult=128):
    best = None
    for t in range(mult, min(n, pref) + 1, mult):
        if n % t == 0:
            best = t
    return n if best is None else best


def _params(n_parallel, n_arbitrary):
    return pltpu.CompilerParams(
        dimension_semantics=("parallel",) * n_parallel + ("arbitrary",) * n_arbitrary,
        vmem_limit_bytes=VMEM_LIMIT)


def _sigmoid(x):
    return 1.0 / (1.0 + jnp.exp(-x))


def _silu(x):
    return x * _sigmoid(x)


class _Exchange:
    def __init__(self, ins, outs, aliases, sems, start, finish):
        self.ins, self.outs, self.aliases, self.sems = list(ins), list(outs), dict(aliases), list(sems)
        self.start, self.finish = start, finish


def _run(body, name, grid, ins, in_specs, outs, out_specs, scratch, n_arbitrary, side=None):
    n_in, n_out, n_scr = len(ins), len(outs), len(scratch)
    if side is None:
        return pl.pallas_call(
            body, name=name, grid=grid, in_specs=in_specs, out_specs=out_specs, out_shape=outs,
            scratch_shapes=scratch, compiler_params=_params(len(grid) - n_arbitrary, n_arbitrary))(*ins)
    s_in, s_out = len(side.ins), len(side.outs)
    i1 = n_in + s_in
    o1 = i1 + n_out
    o2 = o1 + s_out
    c1 = o2 + n_scr

    def wrapped(*refs):
        s_refs = (refs[n_in:i1], refs[o1:o2], refs[c1:])
        ids = [pl.program_id(a) for a in range(len(grid))]
        first = functools.reduce(jnp.logical_and, [i == 0 for i in ids])
        last = functools.reduce(jnp.logical_and, [i == g - 1 for i, g in zip(ids, grid)])

        @pl.when(first)
        def _():
            side.start(*s_refs)

        body(*refs[:n_in], *refs[i1:o1], *refs[o2:c1])

        @pl.when(last)
        def _():
            side.finish(*s_refs)

    res = pl.pallas_call(
        wrapped, name=name, grid=grid, in_specs=list(in_specs) + [ANY] * s_in,
        out_specs=list(out_specs) + [ANY] * s_out, out_shape=list(outs) + side.outs,
        scratch_shapes=list(scratch) + side.sems,
        input_output_aliases={n_in + a: n_out + b for a, b in side.aliases.items()},
        compiler_params=_params(0, len(grid)))(*ins, *side.ins)
    return res[:n_out], res[n_out:]


def _gemm(name, grid, ins, in_specs, outs, out_specs, accs, dot_fn, epi_fn, side=None):
    n_in, n_out, n_k = len(ins), len(outs), grid[-1]

    def body(*refs):
        in_refs, out_refs, acc_refs = refs[:n_in], refs[n_in:n_in + n_out], refs[n_in + n_out:]
        k = pl.program_id(len(grid) - 1)

        @pl.when(k == 0)
        def _():
            for a in acc_refs:
                a[...] = jnp.zeros(a.shape, F32)

        dot_fn(in_refs, acc_refs)

        @pl.when(k == n_k - 1)
        def _():
            epi_fn(in_refs, acc_refs, out_refs)

    return _run(body, name, grid, ins, in_specs, outs, out_specs, [pltpu.VMEM(s, F32) for s in accs], 1, side)


def _sds(shape, dtype):
    return jax.ShapeDtypeStruct(shape, dtype)


def _gemm_rows(name, grid, ins, in_specs, outs, out_specs, tm, n_sub, fn, side=None):
    n_in, sub = len(ins), tm // n_sub

    def body(*refs):
        for s in range(n_sub):
            fn(pl.ds(s * sub, sub), refs[:n_in], refs[n_in:])

    return _run(body, name, grid, ins, in_specs, outs, out_specs, [], 0, side)


def _ffn_up(xb, w3, side=None):
    t, d = xb.shape
    nf = w3.shape[2]
    f = 2 * nf
    tm = _tile(t, 512, 8)
    tn = nf // 2 if nf % 256 == 0 else nf
    ns = nf // tn

    def fn(rows, r, out):
        x = r[0][rows, :]
        g, u = _dot(x, r[1][...], NN), _dot(x, r[2][...], NN)
        out[0][0, rows, :] = g.astype(BF16)
        out[0][1, rows, :] = u.astype(BF16)
        out[1][rows, :] = (_silu(g) * u).astype(BF16)

    return _gemm_rows(
        "ffn_up", (t // tm, 2, ns), [xb, w3, w3],
        [pl.BlockSpec((tm, d), lambda i, j, h: (i, 0)),
         pl.BlockSpec((None, d, tn), lambda i, j, h: (j, 0, h)),
         pl.BlockSpec((None, d, tn), lambda i, j, h: (j + 2, 0, h))],
        [_sds((2, t, f), BF16), _sds((t, f), BF16)],
        [pl.BlockSpec((2, tm, tn), lambda i, j, h: (0, i, j * ns + h)),
         pl.BlockSpec((tm, tn), lambda i, j, h: (i, j * ns + h))],
        tm, 2, fn, side)


def _down_ln(a, w, resid, coef, g, b, side=None):
    t, kd = a.shape
    d = w.shape[1]
    tm = _tile(t, 256, 8)

    def fn(rows, r, out):
        v = ALPHA * r[2][rows, :] + coef * _dot(r[0][rows, :], r[1][...], NN)
        mu = jnp.mean(v, axis=-1, keepdims=True)
        c = v - mu
        var = jnp.mean(c * c, axis=-1, keepdims=True)
        rstd = lax.rsqrt(var + LN_EPS)
        xhat = c * rstd
        h = xhat * r[3][...] + r[4][...]
        out[0][rows, :] = h
        out[1][rows, :] = h.astype(BF16)
        out[2][rows, :] = xhat
        out[3][rows, :] = rstd

    row = pl.BlockSpec((tm, d), lambda i: (i, 0))
    vec = pl.BlockSpec((1, d), lambda i: (0, 0))
    return _gemm_rows(
        "down_ln", (t // tm,), [a, w, resid, g, b],
        [pl.BlockSpec((tm, kd), lambda i: (i, 0)),
         pl.BlockSpec((kd, d), lambda i: (0, 0), pipeline_mode=pl.Buffered(1)), row, vec, vec],
        [_sds((t, d), F32), _sds((t, d), BF16), _sds((t, d), F32), _sds((t, 1), F32)],
        [row, row, row, pl.BlockSpec((tm, 1), lambda i: (i, 0))], tm, 2, fn, side)


def _only(res, side):
    return res[0] if side is None else (res[0][0], res[1])


def _mm_w3(a, w3, side=None):
    t, kd = a.shape
    n = w3.shape[2]
    tm = _tile(t, 512, 8)

    def fn(rows, r, out):
        out[0][rows, :] = _dot(r[0][rows, :], r[1][...], NN)

    return _only(_gemm_rows(
        "mm_w3", (N_SHARD, t // tm), [a, w3],
        [pl.BlockSpec((tm, kd), lambda j, i: (i, 0)),
         pl.BlockSpec((None, kd, n), lambda j, i: (j, 0, 0), pipeline_mode=pl.Buffered(1))],
        [_sds((t, N_SHARD * n), F32)], [pl.BlockSpec((tm, n), lambda j, i: (i, j))], tm, 1, fn, side), side)


def _mm_w2(a, w, out_dtype):
    t, kd = a.shape
    n = w.shape[1]
    tm, tn = _tile(t, 512, 8), _tile(n, 1024)

    def fn(rows, r, out):
        out[0][rows, :] = _dot(r[0][rows, :], r[1][...], NN).astype(out_dtype)

    return _gemm_rows(
        "mm_w2", (t // tm, n // tn), [a, w],
        [pl.BlockSpec((tm, kd), lambda i, j: (i, 0)), pl.BlockSpec((kd, tn), lambda i, j: (0, j))],
        [_sds((t, n), out_dtype)], [pl.BlockSpec((tm, tn), lambda i, j: (i, j))], tm, 1, fn)[0]


def _branch_gate(ob, wb3, ya, proj, gate_col):
    t, kd = ob.shape
    n = wb3.shape[2]
    d = N_SHARD * n
    tm = _tile(t, 512, 8)
    ga0, gb0 = gate_col // n, (gate_col + d) // n

    def fn(rows, r, out):
        yb = _dot(r[0][rows, :], r[1][...], NN)
        out[0][rows, :] = yb
        out[1][rows, :] = (_sigmoid(r[3][rows, :]) * r[2][rows, :] + _sigmoid(r[4][rows, :]) * yb).astype(BF16)

    blk = pl.BlockSpec((tm, n), lambda i, j: (i, j))
    return _gemm_rows(
        "branch_gate", (t // tm, N_SHARD), [ob, wb3, ya, proj, proj],
        [pl.BlockSpec((tm, kd), lambda i, j: (i, 0)), pl.BlockSpec((None, kd, n), lambda i, j: (j, 0, 0)), blk,
         pl.BlockSpec((tm, n), lambda i, j: (i, ga0 + j)), pl.BlockSpec((tm, n), lambda i, j: (i, gb0 + j))],
        [_sds((t, d), F32), _sds((t, d), BF16)], [blk, blk], tm, 2, fn)


def _swiglu_bwd(dyb, w, u3, side=None):
    t, d = dyb.shape
    f = w.shape[0]
    tm, tr = _tile(t, 1024, 8), _tile(f, 512)

    def fn(rows, r, out):
        da = _dot(r[0][rows, :], r[1][...], NT)
        g, u = r[2][0, rows, :].astype(F32), r[2][1, rows, :].astype(F32)
        s = _sigmoid(g)
        out[0][0, rows, :] = (da * u * s * (1.0 + g * (1.0 - s))).astype(BF16)
        out[0][1, rows, :] = (da * g * s).astype(BF16)

    ublk = pl.BlockSpec((2, tm, tr), lambda i, j: (0, i, j))
    return _only(_gemm_rows(
        "swiglu_bwd", (t // tm, f // tr), [dyb, w, u3],
        [pl.BlockSpec((tm, d), lambda i, j: (i, 0)), pl.BlockSpec((tr, d), lambda i, j: (j, 0)), ublk],
        [_sds((2, t, f), BF16)], [ublk], tm, 4, fn, side), side)


def _ffn_dx(du3, w3, resid, side=None):
    t = du3.shape[1]
    d, nf = w3.shape[1], w3.shape[2]
    tm, tr = _tile(t, 512, 8), _tile(d, 1024)

    def dot_fn(r, acc):
        acc[0][...] += _dot(r[0][...], r[1][...], NT)

    def epi_fn(r, acc, out):
        out[0][...] = acc[0][...] + r[2][...]

    blk = pl.BlockSpec((tm, tr), lambda i, j, k: (i, j))
    return _only(_gemm(
        "ffn_dx", (t // tm, d // tr, N_SHARD), [du3, w3, resid],
        [pl.BlockSpec((None, tm, nf), lambda i, j, k: (k // 2, i, k % 2)),
         pl.BlockSpec((None, tr, nf), lambda i, j, k: (k, j, 0)), blk],
        [_sds((t, d), F32)], [blk], [(tm, tr)], dot_fn, epi_fn, side), side)


def _nt_w3(dy, w3, resid, side=None):
    t = dy.shape[0]
    kd, n = w3.shape[1], w3.shape[2]
    tm, tr = _tile(t, 512, 8), _tile(kd, 1024)
    has_res = resid is not None

    def dot_fn(r, acc):
        acc[0][...] += _dot(r[0][...], r[1][...], NT)

    def epi_fn(r, acc, out):
        v = acc[0][...]
        if has_res:
            v = v + r[2][...]
        out[0][...] = v

    blk = pl.BlockSpec((tm, tr), lambda i, j, k: (i, j))
    return _only(_gemm(
        "nt_w3", (t // tm, kd // tr, N_SHARD), [dy, w3] + ([resid] if has_res else []),
        [pl.BlockSpec((tm, n), lambda i, j, k: (i, k)), pl.BlockSpec((None, tr, n), lambda i, j, k: (k, j, 0))]
        + ([blk] if has_res else []),
        [_sds((t, kd), F32)], [blk], [(tm, tr)], dot_fn, epi_fn, side), side)


def _nt_w2(dy, w, out_dtype):
    t, n = dy.shape
    r_ = w.shape[0]
    tm, tr = _tile(t, 512, 8), _tile(r_, 1024)

    def fn(rows, r, out):
        out[0][rows, :] = _dot(r[0][rows, :], r[1][...], NT).astype(out_dtype)

    return _gemm_rows(
        "nt_w2", (t // tm, r_ // tr), [dy, w],
        [pl.BlockSpec((tm, n), lambda i, j: (i, 0)), pl.BlockSpec((tr, n), lambda i, j: (j, 0))],
        [_sds((t, r_), out_dtype)], [pl.BlockSpec((tm, tr), lambda i, j: (i, j))], tm, 1, fn)[0]


def _gate_bwd(dmix, w, proj, ya, yb, gate_col):
    t, n = dmix.shape
    d = w.shape[0]
    tm, tr = _tile(t, 512, 8), _tile(d, 512)
    ga0, gb0 = gate_col // tr, (gate_col + d) // tr
    nb = d // tr

    def fn(rows, r, out):
        dz = _dot(r[0][rows, :], r[1][...], NT)
        ga, gb = _sigmoid(r[2][rows, :]), _sigmoid(r[3][rows, :])
        out[0][rows, :] = (dz * ga).astype(BF16)
        out[1][rows, :] = (dz * gb).astype(BF16)
        out[2][rows, :] = (dz * r[4][rows, :] * ga * (1.0 - ga)).astype(BF16)
        out[3][rows, :] = (dz * r[5][rows, :] * gb * (1.0 - gb)).astype(BF16)

    blk = pl.BlockSpec((tm, tr), lambda i, j: (i, j))
    return _gemm_rows(
        "gate_bwd", (t // tm, nb), [dmix, w, proj, proj, ya, yb],
        [pl.BlockSpec((tm, n), lambda i, j: (i, 0)), pl.BlockSpec((tr, n), lambda i, j: (j, 0)),
         pl.BlockSpec((tm, tr), lambda i, j: (i, ga0 + j)), pl.BlockSpec((tm, tr), lambda i, j: (i, gb0 + j)),
         blk, blk],
        [_sds((t, d), BF16), _sds((t, d), BF16), _sds((t, d), BF16), _sds((t, d), BF16)],
        [blk, blk, blk, blk], tm, 2, fn)


def _tn_w3(a, dy, dy_map, n, side=None):
    t, kd = a.shape
    tm, tkk = _tile(t, (6 << 20) // n, 8), _tile(kd, 512)

    def dot_fn(r, acc):
        acc[0][...] += _dot(r[0][...], r[1][...], TN)

    def epi_fn(r, acc, out):
        out[0][...] = acc[0][...].astype(BF16)

    dy_block = (tm, n) if dy.ndim == 2 else (None, tm, n)
    return _only(_gemm(
        "tn_w3", (kd // tkk, N_SHARD, t // tm), [a, dy],
        [pl.BlockSpec((tm, tkk), lambda i, j, m: (m, i)), pl.BlockSpec(dy_block, lambda i, j, m: dy_map(j, m))],
        [_sds((N_SHARD, kd, n), BF16)], [pl.BlockSpec((None, tkk, n), lambda i, j, m: (j, i, 0))],
        [(tkk, n)], dot_fn, epi_fn, side), side)


def _tn_w2(a, dy):
    t, kd = a.shape
    n = dy.shape[1]
    tkk, tn = _tile(kd, 512), _tile(n, 2048)
    tm = _tile(t, (4 << 20) // tn, 8)

    def dot_fn(r, acc):
        acc[0][...] += _dot(r[0][...], r[1][...], TN)

    def epi_fn(r, acc, out):
        out[0][...] = acc[0][...].astype(BF16)

    return _gemm(
        "tn_w2", (kd // tkk, n // tn, t // tm), [a, dy],
        [pl.BlockSpec((tm, tkk), lambda i, j, m: (m, i)), pl.BlockSpec((tm, tn), lambda i, j, m: (m, j))],
        [_sds((kd, n), BF16)], [pl.BlockSpec((tkk, tn), lambda i, j, m: (i, j))],
        [(tkk, tn)], dot_fn, epi_fn)[0]


def _ln_bwd(dh_parts, xhat, rstd, g, coef, target=None):
    t, d = xhat.shape
    tm = _tile(t, 256, 8)
    n_parts = len(dh_parts)
    with_loss = target is not None
    ins = list(dh_parts) + [xhat, rstd, g] + ([target] if with_loss else [])

    def body(*refs):
        parts = refs[:n_parts]
        xh_ref, rstd_ref, g_ref = refs[n_parts:n_parts + 3]
        o = n_parts + 3 + (1 if with_loss else 0)
        dres_ref, dyb_ref, dg_ref, db_ref, loss_ref = refs[o:o + 5]
        i = pl.program_id(0)
        dh = parts[0][...]
        for p in parts[1:]:
            dh = dh + p[...]
        if with_loss:
            e = dh - refs[o - 1][...]
            part = 0.5 * jnp.sum(jnp.sum(e * e, axis=-1, keepdims=True) * (1.0 / d), axis=0, keepdims=True)
            dh = e * (1.0 / d)
        else:
            part = jnp.zeros((1, 1), F32)
        xh = xh_ref[...]
        dxh = dh * g_ref[...]
        m1 = jnp.mean(dxh, axis=-1, keepdims=True)
        m2 = jnp.mean(dxh * xh, axis=-1, keepdims=True)
        dv = rstd_ref[...] * (dxh - m1 - xh * m2)
        dres_ref[...] = ALPHA * dv
        dyb_ref[...] = (coef * dv).astype(BF16)

        @pl.when(i == 0)
        def _():
            dg_ref[...] = jnp.zeros(dg_ref.shape, F32)
            db_ref[...] = jnp.zeros(db_ref.shape, F32)
            loss_ref[...] = jnp.zeros(loss_ref.shape, F32)

        dg_ref[...] += jnp.sum(dh * xh, axis=0, keepdims=True)
        db_ref[...] += jnp.sum(dh, axis=0, keepdims=True)
        loss_ref[...] += jnp.broadcast_to(part, loss_ref.shape)

    row = pl.BlockSpec((tm, d), lambda i: (i, 0))
    vec = pl.BlockSpec((1, d), lambda i: (0, 0))
    return pl.pallas_call(
        body, name="ln_bwd", grid=(t // tm,),
        in_specs=[row] * n_parts + [row, pl.BlockSpec((tm, 1), lambda i: (i, 0)), vec] + ([row] if with_loss else []),
        out_specs=[row, row, vec, vec, pl.BlockSpec((1, 128), lambda i: (0, 0))],
        out_shape=[_sds((t, d), F32), _sds((t, d), BF16), _sds((1, d), F32), _sds((1, d), F32), _sds((1, 128), F32)],
        compiler_params=_params(0, 1))(*ins)


def _cast_bf16(x2d):
    t, d = x2d.shape
    tm = _tile(t, 512, 8)

    def body(x_ref, o_ref):
        o_ref[...] = x_ref[...].astype(BF16)

    row = pl.BlockSpec((tm, d), lambda i: (i, 0))
    return pl.pallas_call(body, name="cast_bf16", grid=(t // tm,), in_specs=[row], out_specs=row,
                          out_shape=_sds((t, d), BF16), compiler_params=_params(1, 0))(x2d)


def _lower_bound(table):
    t0, t1 = table[0:1, :], table[1:2, :]
    m = jnp.maximum(t0, t1)
    e0, e1 = jnp.exp(t0 - m), jnp.exp(t1 - m)
    return e0 / (e0 + e1)


def _chunk_tri(rows, upper):
    r = lax.broadcasted_iota(jnp.int32, (rows, rows), 0)
    s = lax.broadcasted_iota(jnp.int32, (rows, rows), 1)
    shift = HGRN_CHUNK.bit_length() - 1
    same = lax.shift_right_logical(r, shift) == lax.shift_right_logical(s, shift)
    return same & ((r <= s) if upper else (r >= s))


def _tri_apply(x, upper):
    tri = _chunk_tri(x.shape[0], upper).astype(F32).astype(BF16)
    hi = x.astype(BF16)
    r1 = x - hi.astype(F32)
    mid = r1.astype(BF16)
    lo = (r1 - mid.astype(F32)).astype(BF16)
    return _dot(tri, hi, NN) + _dot(tri, mid, NN) + _dot(tri, lo, NN)


@functools.partial(jax.custom_vjp, nondiff_argnums=(1,))
def _chunk_cumsum(x, upper):
    return _tri_apply(x, upper)


def _chunk_cumsum_fwd(x, upper):
    return _tri_apply(x, upper), None


def _chunk_cumsum_bwd(upper, _, g):
    return (_tri_apply(g, not upper),)


_chunk_cumsum.defvjp(_chunk_cumsum_fwd, _chunk_cumsum_bwd)


def _hgrn_blocks(units, uppers):
    c = HGRN_CHUNK
    rows = units[0][0].shape[0]
    n_sub = rows // c
    ids = range(len(units))
    chunk = lax.shift_right_logical(lax.broadcasted_iota(jnp.int32, (rows, HEAD), 0), c.bit_length() - 1)
    zero = jnp.zeros((rows, HEAD), BF16)

    def expand(a):
        return jnp.concatenate([jnp.where(chunk == n, a, zero) for n in range(n_sub)], axis=1)

    fs = [u[3] + (1.0 - u[3]) * _sigmoid(u[1]) for u in units]
    lgs = [jnp.log(f) for f in fs]
    cums = [_chunk_cumsum(lgs[i], uppers[i]) for i in ids]
    tots = [[jnp.sum(lg[n * c:(n + 1) * c], axis=0, keepdims=True) for n in range(n_sub)] for lg in lgs]
    qd, kd, ke, vb = [], [], [], []
    for i in ids:
        totb = jnp.concatenate([jnp.broadcast_to(t, (c, HEAD)) for t in tots[i]], axis=0)
        kk = 1.0 - fs[i]
        qd.append((_silu(units[i][0]) * jnp.exp(cums[i])).astype(BF16))
        kd.append((kk * jnp.exp(-cums[i])).astype(BF16))
        ke.append((kk * jnp.exp(totb - cums[i])).astype(BF16))
        vb.append(units[i][2].astype(BF16))
    scores = [_dot(qd[i], kd[i], NT) for i in ids]
    kvs = [_dot(vb[i], expand(ke[i]), TN) for i in ids]
    outs = []
    for i in ids:
        a = jnp.where(_chunk_tri(rows, uppers[i]), scores[i], 0.0).astype(BF16)
        st = units[i][4]
        entering = [None] * n_sub
        for n in (range(n_sub - 1, -1, -1) if uppers[i] else range(n_sub)):
            entering[n] = st.astype(BF16)
            st = st * jnp.exp(tots[i][n]) + kvs[i][:, n * HEAD:(n + 1) * HEAD]
        outs.append((a, jnp.concatenate(entering, axis=1), st))
    res = []
    for i in ids:
        a, entering, st = outs[i]
        res.append((_dot(a, vb[i], NN) + _dot(expand(qd[i]), entering, NT), st))
    return res


def _hgrn_fwd(proj3, lb_f, lb_b, norm_g, d, side=None):
    b_, s_, _ = proj3.shape
    nh = d // HEAD
    hb = HGRN_FWD_HEADS
    wid = hb * HEAD
    rows = HGRN_CHUNK * HGRN_SUB
    n_blk = s_ // rows

    def body(hq_ref, hff_ref, hfb_ref, hi_ref, hog_ref, lbf_ref, lbb_ref, g_ref, oraw_ref, out_ref, of_ref, ob_ref):
        lbf, lbb = _lower_bound(lbf_ref[...]), _lower_bound(lbb_ref[...])

        def step(j, sts):
            rf = pl.ds(pl.multiple_of(j * rows, rows), rows)
            rb = pl.ds(pl.multiple_of((n_blk - 1 - j) * rows, rows), rows)
            units = []
            for hh in range(hb):
                cs = slice(hh * HEAD, (hh + 1) * HEAD)
                units.append((hq_ref[rf, cs], hff_ref[rf, cs], hi_ref[rf, cs], lbf[:, cs], sts[2 * hh]))
                units.append((hq_ref[rb, cs], hfb_ref[rb, cs], hi_ref[rb, cs], lbb[:, cs], sts[2 * hh + 1]))
            res = _hgrn_blocks(units, [False, True] * hb)
            for hh in range(hb):
                cs = slice(hh * HEAD, (hh + 1) * HEAD)
                of_ref[rf, cs] = res[2 * hh][0]
                ob_ref[rb, cs] = res[2 * hh + 1][0]
            return tuple(r[1] for r in res)

        z = jnp.zeros((HEAD, HEAD), F32)
        lax.fori_loop(0, n_blk, step, (z,) * (2 * hb))
        for hh in range(hb):
            cs = slice(hh * HEAD, (hh + 1) * HEAD)
            o = of_ref[:, cs] + ob_ref[:, cs]
            oraw_ref[:, cs] = o
            on = o * lax.rsqrt(jnp.mean(o * o, axis=-1, keepdims=True) + LN_EPS)
            out_ref[:, cs] = (on * g_ref[:, cs] * _silu(hog_ref[:, cs])).astype(BF16)

    def col(k):
        return pl.BlockSpec((None, s_, wid), lambda h, b: (b, 0, k * (nh // hb) + h))

    tab = pl.BlockSpec((2, wid), lambda h, b: (0, h))
    oblk = pl.BlockSpec((None, s_, wid), lambda h, b: (b, 0, h))
    return _run(
        body, "hgrn_fwd", (nh // hb, b_), [proj3, proj3, proj3, proj3, proj3, lb_f, lb_b, norm_g],
        [col(0), col(1), col(2), col(3), col(4), tab, tab, pl.BlockSpec((1, wid), lambda h, b: (0, h))],
        [_sds((b_, s_, d), F32), _sds((b_, s_, d), BF16)], [oblk, oblk],
        [pltpu.VMEM((s_, wid), F32), pltpu.VMEM((s_, wid), F32)], 0, side)


def _hgrn_bwd(proj3, lb_f, lb_b, norm_g, o_raw, do_a, d, side=None):
    b_, s_, _ = proj3.shape
    nh = d // HEAD
    hb = HGRN_BWD_HEADS
    wid = hb * HEAD
    rows = HGRN_CHUNK * HGRN_SUB
    n_blk = s_ // rows

    def body(hq_ref, hff_ref, hfb_ref, hi_ref, hog_ref, lbf_ref, lbb_ref, g_ref, oraw_ref, doa_ref,
             dq_ref, dff_ref, dfb_ref, di_ref, dog_ref, dlbf_ref, dlbb_ref, dg_ref,
             st_ref, dor_ref, dq2_ref, di2_ref):
        b = pl.program_id(1)
        tab_f, tab_b = lbf_ref[...], lbb_ref[...]
        lbf, lbb = _lower_bound(tab_f), _lower_bound(tab_b)

        dg_parts = []
        for hh in range(hb):
            cs = slice(hh * HEAD, (hh + 1) * HEAD)
            o, doa, hog, g = oraw_ref[:, cs], doa_ref[:, cs], hog_ref[:, cs], g_ref[:, cs]
            rs = lax.rsqrt(jnp.mean(o * o, axis=-1, keepdims=True) + LN_EPS)
            on = o * rs
            sg = _sigmoid(hog)
            gate = hog * sg
            dog_ref[:, cs] = (doa * on * g * sg * (1.0 + hog * (1.0 - sg))).astype(BF16)
            don = doa * g * gate
            dor_ref[:, cs] = rs * (don - on * jnp.mean(don * on, axis=-1, keepdims=True))
            dg_parts.append(jnp.sum(doa * on * gate, axis=0, keepdims=True))

        def fwd_step(j, carry):
            jb = n_blk - 1 - j
            rf = pl.ds(pl.multiple_of(j * rows, rows), rows)
            rb = pl.ds(pl.multiple_of(jb * rows, rows), rows)
            units = []
            for hh in range(hb):
                cs = slice(hh * HEAD, (hh + 1) * HEAD)
                st_ref[2 * hh, j] = carry[2 * hh]
                st_ref[2 * hh + 1, jb] = carry[2 * hh + 1]
                units.append((hq_ref[rf, cs], hff_ref[rf, cs], hi_ref[rf, cs], lbf[:, cs], carry[2 * hh]))
                units.append((hq_ref[rb, cs], hfb_ref[rb, cs], hi_ref[rb, cs], lbb[:, cs], carry[2 * hh + 1]))
            return tuple(r[1] for r in _hgrn_blocks(units, uppers))

        uppers = [False, True] * hb
        z = jnp.zeros((HEAD, HEAD), F32)
        lax.fori_loop(0, n_blk, fwd_step, (z,) * (2 * hb))

        def bwd_step(j, carry):
            gs, dls = carry
            jf = n_blk - 1 - j
            rf = pl.ds(pl.multiple_of(jf * rows, rows), rows)
            rb = pl.ds(pl.multiple_of(j * rows, rows), rows)
            flat, cots = [], []
            for hh in range(hb):
                cs = slice(hh * HEAD, (hh + 1) * HEAD)
                flat += [hq_ref[rf, cs], hff_ref[rf, cs], hi_ref[rf, cs], lbf[:, cs], st_ref[2 * hh, jf],
                         hq_ref[rb, cs], hfb_ref[rb, cs], hi_ref[rb, cs], lbb[:, cs], st_ref[2 * hh + 1, j]]
                cots += [(dor_ref[rf, cs], gs[2 * hh]), (dor_ref[rb, cs], gs[2 * hh + 1])]
            _, vjp = jax.vjp(lambda *a: _hgrn_blocks([a[5 * i:5 * i + 5] for i in range(2 * hb)], uppers), *flat)
            grads = vjp(cots)
            gs, dls = list(gs), list(dls)
            for hh in range(hb):
                cs = slice(hh * HEAD, (hh + 1) * HEAD)
                dq, df, di, dl, gs[2 * hh] = grads[10 * hh:10 * hh + 5]
                dq_ref[rf, cs] = dq.astype(BF16)
                dff_ref[rf, cs] = df.astype(BF16)
                di_ref[rf, cs] = di.astype(BF16)
                dls[2 * hh] = dls[2 * hh] + dl
                dq, df, di, dl, gs[2 * hh + 1] = grads[10 * hh + 5:10 * hh + 10]
                dq2_ref[rb, cs] = dq
                dfb_ref[rb, cs] = df.astype(BF16)
                di2_ref[rb, cs] = di
                dls[2 * hh + 1] = dls[2 * hh + 1] + dl
            return tuple(gs), tuple(dls)

        zl = jnp.zeros((1, HEAD), F32)
        _, dls = lax.fori_loop(0, n_blk, bwd_step, ((z,) * (2 * hb), (zl,) * (2 * hb)))
        dq_ref[...] = (dq_ref[...].astype(F32) + dq2_ref[...]).astype(BF16)
        di_ref[...] = (di_ref[...].astype(F32) + di2_ref[...]).astype(BF16)

        _, vjp_tf = jax.vjp(_lower_bound, tab_f)
        _, vjp_tb = jax.vjp(_lower_bound, tab_b)

        @pl.when(b == 0)
        def _():
            dlbf_ref[...] = jnp.zeros(dlbf_ref.shape, F32)
            dlbb_ref[...] = jnp.zeros(dlbb_ref.shape, F32)
            dg_ref[...] = jnp.zeros(dg_ref.shape, F32)

        dlbf_ref[...] += vjp_tf(jnp.concatenate(dls[0::2], axis=1))[0]
        dlbb_ref[...] += vjp_tb(jnp.concatenate(dls[1::2], axis=1))[0]
        dg_ref[...] += jnp.concatenate(dg_parts, axis=1)

    def col(k):
        return pl.BlockSpec((None, s_, wid), lambda h, b: (b, 0, k * (nh // hb) + h))

    tab = pl.BlockSpec((2, wid), lambda h, b: (0, h))
    vec = pl.BlockSpec((1, wid), lambda h, b: (0, h))
    oblk = pl.BlockSpec((None, s_, wid), lambda h, b: (b, 0, h))
    seq = _sds((b_, s_, d), BF16)
    return _run(
        body, "hgrn_bwd", (nh // hb, b_), [proj3, proj3, proj3, proj3, proj3, lb_f, lb_b, norm_g, o_raw, do_a],
        [col(0), col(1), col(2), col(3), col(4), tab, tab, vec, oblk, oblk],
        [seq, seq, seq, seq, seq, _sds((2, d), F32), _sds((2, d), F32), _sds((1, d), F32)],
        [oblk, oblk, oblk, oblk, oblk, tab, tab, vec],
        [pltpu.VMEM((2 * hb, n_blk, HEAD, HEAD), F32),
         pltpu.VMEM((s_, wid), F32), pltpu.VMEM((s_, wid), F32), pltpu.VMEM((s_, wid), F32)], 1, side)


def _rope_tables(s_):
    inv = ROPE_THETA ** (-jnp.arange(0, ROPE_DIM, 2, dtype=F32) / ROPE_DIM)
    ang = jnp.arange(s_, dtype=F32)[:, None] * inv
    cos, sin = jnp.cos(ang), jnp.sin(ang)
    rest = HEAD - ROPE_DIM
    ctab = jnp.concatenate([cos, cos, jnp.ones((s_, rest), F32)], axis=1)
    stab = jnp.concatenate([-sin, sin, jnp.zeros((s_, rest), F32)], axis=1)
    half = ROPE_DIM // 2
    perm = np.zeros((HEAD, HEAD), np.float32)
    for i in range(half):
        perm[i + half, i] = 1.0
        perm[i, i + half] = 1.0
    return ctab, stab, jnp.asarray(perm)


ATTN_GROUP = 8


def _attn_tiles(qkv, tabs, offs, perm, half):
    hi = lax.Precision.HIGHEST
    ids = range(len(qkv))
    qrot = [_dot(qkv[i][0], perm, NN, precision=hi) for i in ids]
    krot = [_dot(qkv[i][1], perm, NN, precision=hi) for i in ids]
    qs = [(qkv[i][0] * tabs[i][0] + qrot[i] * tabs[i][1]).astype(BF16) for i in ids]
    ks = [(qkv[i][1] * tabs[i][2] + krot[i] * tabs[i][3]).astype(BF16) for i in ids]
    ss = [_dot(qs[i], ks[i], NT) * (HEAD ** -0.5) for i in ids]
    ps, dens, lses = [], [], []
    for i in ids:
        qi = offs[i][0] + lax.broadcasted_iota(jnp.int32, ss[i].shape, 0)
        kj = offs[i][1] + lax.broadcasted_iota(jnp.int32, ss[i].shape, 1)
        s = jnp.where(jnp.abs(qi - kj) <= half, ss[i], NEG_INF)
        m = lax.stop_gradient(jnp.max(s, axis=-1, keepdims=True))
        p = jnp.exp(s - m)
        den = jnp.sum(p, axis=-1, keepdims=True)
        ps.append(p.astype(BF16))
        dens.append(den)
        lses.append(m + jnp.log(den))
    res = []
    for i in ids:
        o = _dot(ps[i], qkv[i][2].astype(BF16), NN) / dens[i]
        res.append((o, jnp.broadcast_to(lses[i], o.shape)))
    return res


def _attn_tiling(seg):
    tq = seg if seg <= 256 else 256
    kw = seg if seg <= 512 else 512
    tiles = []
    for i in range(seg // tq):
        ws = min(max(i * tq - (kw - tq) // 2, 0), seg - kw)
        tiles.append((i * tq, ws))
    return tq, kw, tiles


def _attn_specs(s_, d, g):
    c0 = (5 * d + g * 3 * ATTN_OUT) // HEAD

    def col(part):
        return pl.BlockSpec((None, s_, HEAD), lambda b, h: (b, 0, c0 + part * ATTN_HEADS + h))

    tab = pl.BlockSpec((s_, HEAD), lambda b, h: (0, 0))
    perm = pl.BlockSpec((HEAD, HEAD), lambda b, h: (0, 0))
    oblk = pl.BlockSpec((None, s_, HEAD), lambda b, h: (b, 0, h))
    return col, tab, perm, oblk


def _rows(r, first, count, dil):
    return pl.ds(r + dil * first, count, stride=dil) if dil > 1 else pl.ds(first, count)


def _attn_operands(group, dil, tq, kw, q_ref, k_ref, v_ref, c_ref, s_ref):
    qkv, tabs = [], []
    for r, q0, k0 in group:
        rq, rk = _rows(r, q0, tq, dil), _rows(r, k0, kw, dil)
        qkv.append((q_ref[rq, :], k_ref[rk, :], v_ref[rk, :]))
        tabs.append((c_ref[rq, :], s_ref[rq, :], c_ref[rk, :], s_ref[rk, :]))
    return qkv, tabs


def _attn_fwd(proj3, ctab, stab, perm, d, g):
    window, dil = ATTN_GROUPS[g]
    b_, s_, _ = proj3.shape
    seg = s_ // dil
    half = window // (2 * dil)
    tq, kw, tiles = _attn_tiling(seg)

    work = [(r, q0, k0) for r in range(dil) for q0, k0 in tiles]

    def body(q_ref, k_ref, v_ref, c_ref, s_ref, p_ref, o_ref, l_ref):
        pm = p_ref[...]
        for g0 in range(0, len(work), ATTN_GROUP):
            group = work[g0:g0 + ATTN_GROUP]
            qkv, tabs = _attn_operands(group, dil, tq, kw, q_ref, k_ref, v_ref, c_ref, s_ref)
            res = _attn_tiles(qkv, tabs, [(q0, k0) for _, q0, k0 in group], pm, half)
            for (r, q0, _), (o, l) in zip(group, res):
                o_ref[_rows(r, q0, tq, dil), :] = o
                l_ref[_rows(r, q0, tq, dil), :] = l

    col, tab, pspec, oblk = _attn_specs(s_, d, g)
    shp = _sds((b_, s_, ATTN_OUT), F32)
    return pl.pallas_call(
        body, name=f"attn_fwd_d{dil}", grid=(b_, ATTN_HEADS), in_specs=[col(0), col(1), col(2), tab, tab, pspec],
        out_specs=[oblk, oblk], out_shape=[shp, shp], compiler_params=_params(2, 0))(
            proj3, proj3, proj3, ctab, stab, perm)


def _attn_bwd(proj3, ctab, stab, perm, do, dl, d, g):
    window, dil = ATTN_GROUPS[g]
    b_, s_, _ = proj3.shape
    seg = s_ // dil
    half = window // (2 * dil)
    tq, kw, tiles = _attn_tiling(seg)

    work = [(r, q0, k0) for r in range(dil) for q0, k0 in tiles]

    def body(q_ref, k_ref, v_ref, c_ref, s_ref, p_ref, do_ref, dl_ref, dq_ref, dk_ref, dv_ref, dq_s, dk_s, dv_s):
        pm = p_ref[...]
        dk_s[...] = jnp.zeros(dk_s.shape, F32)
        dv_s[...] = jnp.zeros(dv_s.shape, F32)
        for g0 in range(0, len(work), ATTN_GROUP):
            group = work[g0:g0 + ATTN_GROUP]
            qkv, tabs = _attn_operands(group, dil, tq, kw, q_ref, k_ref, v_ref, c_ref, s_ref)
            offs = [(q0, k0) for _, q0, k0 in group]
            _, vjp = jax.vjp(
                lambda *a: _attn_tiles([a[3 * i:3 * i + 3] for i in range(len(group))], tabs, offs, pm, half),
                *[x for tile in qkv for x in tile])
            grads = vjp([(do_ref[_rows(r, q0, tq, dil), :], dl_ref[_rows(r, q0, tq, dil), :]) for r, q0, _ in group])
            for i, (r, q0, k0) in enumerate(group):
                rk = _rows(r, k0, kw, dil)
                dq_s[_rows(r, q0, tq, dil), :] = grads[3 * i]
                dk_s[rk, :] += grads[3 * i + 1]
                dv_s[rk, :] += grads[3 * i + 2]
        dq_ref[...] = dq_s[...].astype(BF16)
        dk_ref[...] = dk_s[...].astype(BF16)
        dv_ref[...] = dv_s[...].astype(BF16)

    col, tab, pspec, oblk = _attn_specs(s_, d, g)
    shp = _sds((b_, s_, ATTN_OUT), BF16)
    full = pltpu.VMEM((s_, HEAD), F32)
    return pl.pallas_call(
        body, name=f"attn_bwd_d{dil}", grid=(b_, ATTN_HEADS),
        in_specs=[col(0), col(1), col(2), tab, tab, pspec, oblk, oblk], out_specs=[oblk, oblk, oblk],
        out_shape=[shp, shp, shp], scratch_shapes=[full, full, full],
        compiler_params=_params(2, 0))(proj3, proj3, proj3, ctab, stab, perm, do, dl)


def _combine(os_, ls_):
    m = jnp.maximum(jnp.maximum(ls_[0], ls_[1]), ls_[2])
    es = [jnp.exp(l - m) for l in ls_]
    return (es[0] * os_[0] + es[1] * os_[1] + es[2] * os_[2]) / (es[0] + es[1] + es[2])


def _combine_fwd(os_, ls_):
    t, w = os_[0].shape
    tm = _tile(t, 512, 8)

    def body(*refs):
        refs[6][...] = _combine([r[...] for r in refs[:3]], [r[...] for r in refs[3:6]]).astype(BF16)

    row = pl.BlockSpec((tm, w), lambda i: (i, 0))
    return pl.pallas_call(body, name="combine_fwd", grid=(t // tm,), in_specs=[row] * 6, out_specs=row,
                          out_shape=_sds((t, w), BF16), compiler_params=_params(1, 0))(*os_, *ls_)


def _combine_bwd(os_, ls_, dob):
    t, w = os_[0].shape
    tm = _tile(t, 512, 8)

    def body(*refs):
        _, vjp = jax.vjp(lambda *a: _combine(a[:3], a[3:]), *[r[...] for r in refs[:6]])
        for r, g in zip(refs[7:], vjp(refs[6][...])):
            r[...] = g

    row = pl.BlockSpec((tm, w), lambda i: (i, 0))
    return pl.pallas_call(body, name="combine_bwd", grid=(t // tm,), in_specs=[row] * 7, out_specs=[row] * 6,
                          out_shape=[_sds((t, w), F32)] * 6, compiler_params=_params(1, 0))(*os_, *ls_, dob)


ROW_SHARDED = ("ffn1_out", "ffn2_out", "wa", "mix_out")


def _local_step(x, target, fulls, p, sc_arr):
    b_, s_, d = x.shape
    t = b_ * s_
    x2 = x.reshape(t, d)
    gate_col = 5 * d + QKV_W
    c_arr = sc_arr[1:2]
    w = {}

    def arrived(keys, arrays):
        for k, a in zip(keys, arrays):
            w[k] = a.reshape(a.shape[0] * a.shape[1], a.shape[2]) if k in ROW_SHARDED else a

    def reduce_begin(keys):
        g3 = [gw[k].reshape(fulls[k].shape) for k in keys]
        return [_sum_halves(a, b, c_arr) for a, b in zip(g3, _swap_halves(g3))]

    def reduce_end(keys, parts, got):
        for k, a, b in zip(keys, parts, got):
            reduced[k] = _sum_chips(a, b, sc_arr)

    arrived(["ffn1_in"], _gather_weights([fulls["ffn1_in"]]))
    xb = _cast_bf16(x2)
    mix_rows = fulls["mix_in"].shape[1]
    (u1, a1), got = _ffn_up(xb, w["ffn1_in"], _gather_exchange([fulls["ffn1_out"], fulls["mix_in"]],
                                                              [None, (0, mix_rows // 2)]))
    arrived(["ffn1_out"], got[:1])
    (h1, h1b, xh1, rs1), got = _down_ln(a1, w["ffn1_out"], x2, 0.5, p["ln1_g"], p["ln1_b"],
                                        _gather_exchange(got[1:], [(mix_rows // 2, mix_rows // 2)]))
    arrived(["mix_in"], got)

    keys = ["wa", "wb", "mix_out", "ffn2_in"]
    proj, got = _mm_w3(h1b, w["mix_in"], _gather_exchange([fulls[k] for k in keys]))
    arrived(keys, got)
    n_in = proj.shape[1]
    proj3 = proj.reshape(b_, s_, n_in)
    (o_raw, oa), got = _hgrn_fwd(proj3, p["hgrn_lb_fwd"], p["hgrn_lb_bwd"], p["hgrn_norm_g"], d,
                                 _gather_exchange([fulls["ffn2_out"]]))
    arrived(["ffn2_out"], got)
    oa2 = oa.reshape(t, d)

    ctab, stab, perm = _rope_tables(s_)
    os_, ls_ = [], []
    for g in range(N_GROUPS):
        o_g, l_g = _attn_fwd(proj3, ctab, stab, perm, d, g)
        os_.append(o_g.reshape(t, ATTN_OUT))
        ls_.append(l_g.reshape(t, ATTN_OUT))
    ob = _combine_fwd(os_, ls_)

    ya = _mm_w2(oa2, w["wa"], F32)
    yb, zb = _branch_gate(ob, w["wb"], ya, proj, gate_col)
    h2, h2b, xh2, rs2 = _down_ln(zb, w["mix_out"], h1, 1.0, p["ln2_g"], p["ln2_b"])

    u2, a2 = _ffn_up(h2b, w["ffn2_in"])
    h3, _, xh3, rs3 = _down_ln(a2, w["ffn2_out"], h2, 0.5, p["ln3_g"], p["ln3_b"])

    gw, gp, reduced = {}, {}, {}
    nf = w["ffn2_in"].shape[2]

    def du_map(j, m):
        return (j // 2, m, j % 2)

    dres3, dy3, gp["ln3_g"], gp["ln3_b"], loss = _ln_bwd([h3], xh3, rs3, p["ln3_g"], 0.5, target.reshape(t, d))
    du2 = _swiglu_bwd(dy3, w["ffn2_out"], u2)
    g_out = _tn_w2(a2, dy3).reshape(fulls["ffn2_out"].shape)
    g_in, got = _tn_w3(h2b, du2, du_map, nf, _swap_exchange([g_out]))
    parts_a = [_sum_halves(g_out, got[0], c_arr)]
    dh2, got = _ffn_dx(du2, w["ffn2_in"], dres3, _both(_scatter_exchange(parts_a), _swap_exchange([g_in])))
    reduce_end(["ffn2_out"], parts_a, got[:1])
    parts_a.append(_sum_halves(g_in, got[1], c_arr))

    dres2, dmix, gp["ln2_g"], gp["ln2_b"], _ = _ln_bwd([dh2], xh2, rs2, p["ln2_g"], 1.0)
    dya, dyb, dpga, dpgb = _gate_bwd(dmix, w["mix_out"], proj, ya, yb, gate_col)
    gw["mix_out"] = _tn_w2(zb, dmix)
    do_a = _nt_w2(dya, w["wa"], F32)
    gw["wa"] = _tn_w2(oa2, dya)
    nb = w["wb"].shape[2]
    do_b = _nt_w3(dyb, w["wb"], None)
    gw["wb"] = _tn_w3(ob, dyb, lambda j, m: (m, j), nb)
    keys = ["ffn2_in", "mix_out", "wa", "wb"]
    parts_b = parts_a[1:] + reduce_begin(keys[1:])

    (dq, dff, dfb, di, dog, gp["hgrn_lb_fwd"], gp["hgrn_lb_bwd"], gp["hgrn_norm_g"]), got = _hgrn_bwd(
        proj3, p["hgrn_lb_fwd"], p["hgrn_lb_bwd"], p["hgrn_norm_g"], o_raw, do_a.reshape(b_, s_, d), d,
        _scatter_exchange(parts_b))
    reduce_end(keys, parts_b, got)

    douts = _combine_bwd(os_, ls_, do_b)
    dqkv = []
    for g in range(N_GROUPS):
        grads_g = _attn_bwd(proj3, ctab, stab, perm, douts[g].reshape(b_, s_, ATTN_OUT),
                            douts[3 + g].reshape(b_, s_, ATTN_OUT), d, g)
        dqkv += [a.reshape(t, ATTN_OUT) for a in grads_g]

    dproj = jnp.concatenate(
        [a.reshape(t, d) for a in (dq, dff, dfb, di, dog)] + dqkv + [dpga, dpgb], axis=1)
    nm = w["mix_in"].shape[2]
    gw["mix_in"] = _tn_w3(h1b, dproj, lambda j, m: (m, j), nm)
    parts_c = reduce_begin(["mix_in"])
    cut = 3 * parts_c[0].shape[1] // 4
    dh1, got_c = _nt_w3(dproj, w["mix_in"], dres2, _scatter_exchange(parts_c, [(0, cut)]))

    dres1, dy1, gp["ln1_g"], gp["ln1_b"], _ = _ln_bwd([dh1], xh1, rs1, p["ln1_g"], 0.5)
    g_out = _tn_w2(a1, dy1).reshape(fulls["ffn1_out"].shape)
    du1, got = _swiglu_bwd(dy1, w["ffn1_out"], u1, _both(
        _swap_exchange([g_out]), _scatter_exchange(parts_c, [(cut, parts_c[0].shape[1] - cut)], got_c)))
    reduce_end(["mix_in"], parts_c, got[1:])
    parts_d = [_sum_halves(g_out, got[0], c_arr)]
    gw["ffn1_in"], got = _tn_w3(xb, du1, du_map, nf, _scatter_exchange(parts_d))
    reduce_end(["ffn1_out"], parts_d, got)
    parts_e = reduce_begin(["ffn1_in"])
    dx, got = _ffn_dx(du1, w["ffn1_in"], dres1, _scatter_exchange(parts_e))
    reduce_end(["ffn1_in"], parts_e, got)

    keys = list(reduced)
    grads = dict(zip(keys, _join_halves([reduced[k] for k in keys])))
    return loss, dx.reshape(b_, s_, d), grads, gp


MESH = pl.DeviceIdType.MESH
ANY = pl.BlockSpec(memory_space=pl.ANY)


def _place():
    x, y, c = lax.axis_index("x"), lax.axis_index("y"), lax.axis_index("c")
    chips = [(1 - x, y), (x, 1 - y), (1 - x, 1 - y)]
    return x, y, c, chips, (x, y, 1 - c)


def _half_rows(c, rows):
    hr = rows // 2
    return pl.ds(pl.multiple_of(c * hr, 16), hr)


def _remote(src, dst, send, recv, dev):
    return pltpu.make_async_remote_copy(src_ref=src, dst_ref=dst, send_sem=send, recv_sem=recv,
                                        device_id=dev, device_id_type=MESH)


def _gather_weights(fulls):
    n = len(fulls)

    def body(*refs):
        _gather_start(refs[n:2 * n], refs[2 * n:])
        _gather_finish(refs[n:2 * n], refs[2 * n:])

    return pl.pallas_call(
        body, name="gather_weights", in_specs=[ANY] * n, out_specs=[ANY] * n,
        out_shape=[_sds(a.shape, a.dtype) for a in fulls], input_output_aliases={i: i for i in range(n)},
        scratch_shapes=_gather_sems(n))(*fulls)


def _gather_sems(n):
    return [pltpu.SemaphoreType.DMA((n, 3)) for _ in range(4)]


def _span_half(c, span, rows):
    r0, cnt = (0, rows) if span is None else span
    return pl.ds(pl.multiple_of(r0 + c * (cnt // 2), 16), cnt // 2)


def _gather_start(bufs, sems, spans=None):
    isend, irecv = sems[0], sems[1]
    x, y, c, chips, sib = _place()
    for i, buf in enumerate(bufs):
        blk = buf.at[2 * x + y, _span_half(c, spans and spans[i], buf.shape[1])]
        for k, chip in enumerate(chips):
            _remote(blk, blk, isend.at[i, k], irecv.at[i, k], (*chip, c)).start()


def _gather_finish(bufs, sems, spans=None):
    isend, irecv, fsend, frecv = sems
    x, y, c, chips, sib = _place()
    for i, buf in enumerate(bufs):
        mine = _span_half(c, spans and spans[i], buf.shape[1])
        for k, chip in enumerate(chips):
            blk = buf.at[2 * chip[0] + chip[1], mine]
            _remote(blk, blk, isend.at[i, k], irecv.at[i, k], (*chip, c)).wait_recv()
            _remote(blk, blk, fsend.at[i, k], frecv.at[i, k], sib).start()
    for i, buf in enumerate(bufs):
        span = spans and spans[i]
        mine, other = _span_half(c, span, buf.shape[1]), _span_half(1 - c, span, buf.shape[1])
        own = buf.at[2 * x + y, mine]
        for k, chip in enumerate(chips):
            got = buf.at[2 * chip[0] + chip[1], other]
            _remote(got, got, fsend.at[i, k], frecv.at[i, k], sib).wait_recv()
            _remote(own, own, isend.at[i, k], irecv.at[i, k], (*chip, c)).wait_send()
            blk = buf.at[2 * chip[0] + chip[1], mine]
            _remote(blk, blk, fsend.at[i, k], frecv.at[i, k], sib).wait_send()


def _gather_exchange(fulls, spans=None):
    n = len(fulls)
    return _Exchange(fulls, [_sds(a.shape, a.dtype) for a in fulls], {i: i for i in range(n)}, _gather_sems(n),
                     lambda ins, outs, sems: _gather_start(outs, sems, spans),
                     lambda ins, outs, sems: _gather_finish(outs, sems, spans))


def _swap_halves(grads):
    side = _swap_exchange(grads)
    n = len(grads)

    def body(*refs):
        side.start(refs[:n], refs[n:2 * n], refs[2 * n:])
        side.finish(refs[:n], refs[n:2 * n], refs[2 * n:])

    return pl.pallas_call(
        body, name="swap_halves", in_specs=[ANY] * n, out_specs=[ANY] * n, out_shape=side.outs,
        scratch_shapes=side.sems)(*grads)


def _swap_exchange(grads):
    n = len(grads)

    def copies(ins, outs, sems):
        x, y, c, chips, sib = _place()
        return [_remote(a.at[:, _half_rows(1 - c, a.shape[1])], b, sems[0].at[i], sems[1].at[i], sib)
                for i, (a, b) in enumerate(zip(ins, outs))]

    def start(ins, outs, sems):
        for cp in copies(ins, outs, sems):
            cp.start()

    def finish(ins, outs, sems):
        for cp in copies(ins, outs, sems):
            cp.wait()

    return _Exchange(grads, [_sds((N_SHARD, a.shape[1] // 2, a.shape[2]), a.dtype) for a in grads], {},
                     [pltpu.SemaphoreType.DMA((n,)) for _ in range(2)], start, finish)


def _both(a, b):
    i, o, s = len(a.ins), len(a.outs), len(a.sems)

    def start(ins, outs, sems):
        a.start(ins[:i], outs[:o], sems[:s])
        b.start(ins[i:], outs[o:], sems[s:])

    def finish(ins, outs, sems):
        a.finish(ins[:i], outs[:o], sems[:s])
        b.finish(ins[i:], outs[o:], sems[s:])

    aliases = dict(a.aliases)
    aliases.update({i + k: o + v for k, v in b.aliases.items()})
    return _Exchange(a.ins + b.ins, a.outs + b.outs, aliases, a.sems + b.sems, start, finish)


def _scatter_partials(parts):
    side = _scatter_exchange(parts)
    n = len(parts)

    def body(*refs):
        side.start(refs[:n], refs[n:2 * n], refs[2 * n:])
        side.finish(refs[:n], refs[n:2 * n], refs[2 * n:])

    return pl.pallas_call(
        body, name="scatter_partials", in_specs=[ANY] * n, out_specs=[ANY] * n, out_shape=side.outs,
        scratch_shapes=side.sems)(*parts)


def _scatter_exchange(parts, spans=None, into=None):
    n = len(parts)

    def copies(ins, outs, sems):
        x, y, c, chips, sib = _place()
        cps = []
        for i in range(n):
            rows = pl.ds(*spans[i]) if spans and spans[i] else slice(None)
            for k, chip in enumerate(chips):
                cps.append(_remote(ins[i].at[2 * chip[0] + chip[1], rows], outs[i].at[k, rows],
                                   sems[0].at[i, k], sems[1].at[i, k], (*chip, c)))
        return cps

    def start(ins, outs, sems):
        for cp in copies(ins, outs, sems):
            cp.start()

    def finish(ins, outs, sems):
        for cp in copies(ins, outs, sems):
            cp.wait()

    return _Exchange(list(parts) + list(into or []), [_sds((3,) + a.shape[1:], a.dtype) for a in parts],
                     {n + i: i for i in range(n)} if into else {},
                     [pltpu.SemaphoreType.DMA((n, 3)) for _ in range(2)], start, finish)


def _join_halves(grads):
    n = len(grads)

    def body(*refs):
        bufs, send, recv = refs[n:2 * n], refs[2 * n], refs[2 * n + 1]
        x, y, c, chips, sib = _place()
        cps = []
        for i in range(n):
            blk = bufs[i].at[_half_rows(c, bufs[i].shape[0])]
            other = bufs[i].at[_half_rows(1 - c, bufs[i].shape[0])]
            cp = _remote(blk, blk, send.at[i], recv.at[i], sib)
            cp.start()
            cps.append((cp, _remote(other, other, send.at[i], recv.at[i], sib)))
        for cp, got in cps:
            cp.wait_send()
            got.wait_recv()

    dma = pltpu.SemaphoreType.DMA
    return pl.pallas_call(
        body, name="join_halves", in_specs=[ANY] * n, out_specs=[ANY] * n,
        out_shape=[_sds(a.shape, a.dtype) for a in grads], input_output_aliases={i: i for i in range(n)},
        scratch_shapes=[dma((n,)), dma((n,))])(*grads)


def _gather_rows(block):
    m_per, n = block.shape

    def body(x_ref, out_ref, send_sems, recv_sems, local_sem):
        x, y, c, chips, sibling = _place()
        me = (x, y, c)

        def rows(px, py, pc):
            return out_ref.at[pl.ds((4 * px + 2 * py + pc) * m_per, m_per), :]

        def copy(k, blk, to, src=None):
            return _remote(rows(*blk) if src is None else src, rows(*blk), send_sems.at[k], recv_sems.at[k], to)

        mine = pltpu.make_async_copy(x_ref, rows(*me), local_sem)
        mine.start()
        first = [copy(0, me, sibling, src=x_ref)]
        first += [copy(1 + j, me, (*chip, c), src=x_ref) for j, chip in enumerate(chips)]
        for cp in first:
            cp.start()
        passed = [copy(4 + j, (*chip, c), sibling) for j, chip in enumerate(chips)]
        for j, chip in enumerate(chips):
            copy(1 + j, (*chip, c), me).wait_recv()
            passed[j].start()
        copy(0, sibling, me).wait_recv()
        for j, chip in enumerate(chips):
            copy(4 + j, (*chip, 1 - c), me).wait_recv()
        for cp in first + passed:
            cp.wait_send()
        mine.wait()

    vmem = pl.BlockSpec(memory_space=pltpu.VMEM)
    dma = pltpu.SemaphoreType.DMA
    return pl.pallas_call(
        body, name="gather_rows", in_specs=[vmem], out_specs=vmem, out_shape=_sds((8 * m_per, n), block.dtype),
        scratch_shapes=[dma((7,)), dma((7,)), dma(())])(block)


def _row_tile(rows, cols):
    return _tile(rows, max(16, (1 << 20) // cols), 16)


def _sum_halves(grad, got, c_arr):
    _, hr, cols = got.shape
    tr = _row_tile(hr, cols)
    nb = hr // tr

    def body(c_ref, a_ref, b_ref, o_ref):
        o_ref[...] = (a_ref[...].astype(F32) + b_ref[...].astype(F32)).astype(BF16)

    blk = pl.BlockSpec((None, tr, cols), lambda s, i, c_ref: (s, i, 0))
    return pl.pallas_call(
        body, name="sum_halves",
        grid_spec=pltpu.PrefetchScalarGridSpec(
            num_scalar_prefetch=1, grid=(N_SHARD, nb),
            in_specs=[pl.BlockSpec((None, tr, cols), lambda s, i, c_ref: (s, c_ref[0] * nb + i, 0)), blk],
            out_specs=blk),
        out_shape=_sds(got.shape, BF16), compiler_params=_params(2, 0))(c_arr, grad, got)


def _sum_chips(part, got, sc_arr):
    _, hr, cols = got.shape
    tr = _row_tile(hr, cols)
    nb = hr // tr

    def body(s_ref, a_ref, b_ref, o_ref):
        o_ref[...] = ((a_ref[...].astype(F32) + b_ref[0].astype(F32)) + b_ref[1].astype(F32)) + b_ref[2].astype(F32)

    return pl.pallas_call(
        body, name="sum_chips",
        grid_spec=pltpu.PrefetchScalarGridSpec(
            num_scalar_prefetch=1, grid=(nb,),
            in_specs=[pl.BlockSpec((None, tr, cols), lambda i, s_ref: (s_ref[0], i, 0)),
                      pl.BlockSpec((3, tr, cols), lambda i, s_ref: (0, i, 0))],
            out_specs=pl.BlockSpec((tr, cols), lambda i, s_ref: (s_ref[1] * nb + i, 0))),
        out_shape=_sds((2 * hr, cols), F32), compiler_params=_params(1, 0))(sc_arr, part, got)


def _cast_into_slot(x2d, sc_arr):
    rows, cols = x2d.shape
    tr = _row_tile(rows, cols)

    def body(s_ref, x_ref, o_ref):
        o_ref[...] = x_ref[...].astype(BF16)

    return pl.pallas_call(
        body, name="cast_into_slot",
        grid_spec=pltpu.PrefetchScalarGridSpec(
            num_scalar_prefetch=1, grid=(rows // tr,),
            in_specs=[pl.BlockSpec((tr, cols), lambda i, s_ref: (i, 0))],
            out_specs=pl.BlockSpec((None, tr, cols), lambda i, s_ref: (s_ref[0], i, 0))),
        out_shape=_sds((N_SHARD, rows, cols), BF16), compiler_params=_params(1, 0))(sc_arr, x2d)


def _adam_math(w, g, m, v):
    m = ADAM_B1 * m + (1.0 - ADAM_B1) * g
    v = ADAM_B2 * v + (1.0 - ADAM_B2) * (g * g)
    m_hat = m / (1.0 - ADAM_B1 ** ADAM_STEP)
    v_hat = v / (1.0 - ADAM_B2 ** ADAM_STEP)
    delta = -ADAM_LR * (m_hat / (jnp.sqrt(v_hat) + ADAM_EPS) + ADAM_WD * w)
    return delta, m, v


def _adamw(w, g, m, v):
    rows, cols = w.shape
    tr = _tile(rows, max(8, (1 << 19) // cols), 8)

    def body(w_ref, g_ref, m_ref, v_ref, go_ref, d_ref, mo_ref, vo_ref):
        g_ = g_ref[...]
        go_ref[...] = g_
        d_ref[...], mo_ref[...], vo_ref[...] = _adam_math(w_ref[...], g_, m_ref[...], v_ref[...])

    blk = pl.BlockSpec((tr, cols), lambda i: (i, 0))
    return pl.pallas_call(
        body, name="adamw", grid=(rows // tr,), in_specs=[blk] * 4, out_specs=[blk] * 4,
        out_shape=[_sds((rows, cols), F32)] * 4, compiler_params=_params(1, 0))(w, g, m, v)


def _adamw_small(gathered, w, m, v):
    rows, cols = w.shape

    def body(a_ref, w_ref, m_ref, v_ref, go_ref, d_ref, mo_ref, vo_ref):
        g_ = a_ref[pl.ds(0, rows), :]
        for k in range(1, 8):
            g_ = g_ + a_ref[pl.ds(k * rows, rows), :]
        go_ref[...] = g_
        d_ref[...], mo_ref[...], vo_ref[...] = _adam_math(w_ref[...], g_, m_ref[...], v_ref[...])

    vmem = pl.BlockSpec(memory_space=pltpu.VMEM)
    return pl.pallas_call(
        body, name="adamw_small", in_specs=[vmem] * 4, out_specs=[vmem] * 4,
        out_shape=[_sds((rows, cols), F32)] * 4)(gathered, w, m, v)


BIG = ("ffn1_w_in", "ffn1_w_out", "mix_w_in", "w_branch_a", "w_branch_b", "mix_w_out", "ffn2_w_in", "ffn2_w_out")
BIG_KEY = {"ffn1_w_in": "ffn1_in", "ffn1_w_out": "ffn1_out", "mix_w_in": "mix_in", "w_branch_a": "wa",
           "w_branch_b": "wb", "mix_w_out": "mix_out", "ffn2_w_in": "ffn2_in", "ffn2_w_out": "ffn2_out"}
SMALL = ("ln1_g", "ln1_b", "hgrn_lb_fwd", "hgrn_lb_bwd", "hgrn_norm_g", "ln2_g", "ln2_b", "ln3_g", "ln3_b")
ORDER = ("ffn1_w_in", "ffn1_w_out", "ln1_g", "ln1_b", "mix_w_in", "hgrn_lb_fwd", "hgrn_lb_bwd", "hgrn_norm_g",
         "w_branch_a", "w_branch_b", "mix_w_out", "ln2_g", "ln2_b", "ffn2_w_in", "ffn2_w_out", "ln3_g", "ln3_b")
SMALL_ROWS = 16


def _pack_small(d):
    rows = jnp.concatenate([d[k].reshape(-1, d[k].shape[-1]) for k in SMALL], axis=0)
    return jnp.pad(rows, ((0, SMALL_ROWS - rows.shape[0]), (0, 0)))


def _unpack_small(a, like):
    out, r = {}, 0
    for k in SMALL:
        n = like[k].shape[0]
        out[k] = a[r:r + n].reshape(like[k].shape)
        r += n
    return out


def kernel(x, ffn1_w_in, ffn1_w_out, ln1_g, ln1_b, mix_w_in, hgrn_lb_fwd, hgrn_lb_bwd, hgrn_norm_g, w_branch_a, w_branch_b, mix_w_out, ln2_g, ln2_b, ffn2_w_in, ffn2_w_out, ln3_g, ln3_b, loss_target, m_ffn1_w_in, m_ffn1_w_out, m_ln1_g, m_ln1_b, m_mix_w_in, m_hgrn_lb_fwd, m_hgrn_lb_bwd, m_hgrn_norm_g, m_w_branch_a, m_w_branch_b, m_mix_w_out, m_ln2_g, m_ln2_b, m_ffn2_w_in, m_ffn2_w_out, m_ln3_g, m_ln3_b, v_ffn1_w_in, v_ffn1_w_out, v_ln1_g, v_ln1_b, v_mix_w_in, v_hgrn_lb_fwd, v_hgrn_lb_bwd, v_hgrn_norm_g, v_w_branch_a, v_w_branch_b, v_mix_w_out, v_ln2_g, v_ln2_b, v_ffn2_w_in, v_ffn2_w_out, v_ln3_g, v_ln3_b):
    wts = dict(ffn1_w_in=ffn1_w_in, ffn1_w_out=ffn1_w_out, ln1_g=ln1_g, ln1_b=ln1_b, mix_w_in=mix_w_in,
               hgrn_lb_fwd=hgrn_lb_fwd, hgrn_lb_bwd=hgrn_lb_bwd, hgrn_norm_g=hgrn_norm_g, w_branch_a=w_branch_a,
               w_branch_b=w_branch_b, mix_w_out=mix_w_out, ln2_g=ln2_g, ln2_b=ln2_b, ffn2_w_in=ffn2_w_in,
               ffn2_w_out=ffn2_w_out, ln3_g=ln3_g, ln3_b=ln3_b)
    mom = dict(ffn1_w_in=m_ffn1_w_in, ffn1_w_out=m_ffn1_w_out, ln1_g=m_ln1_g, ln1_b=m_ln1_b, mix_w_in=m_mix_w_in,
               hgrn_lb_fwd=m_hgrn_lb_fwd, hgrn_lb_bwd=m_hgrn_lb_bwd, hgrn_norm_g=m_hgrn_norm_g,
               w_branch_a=m_w_branch_a, w_branch_b=m_w_branch_b, mix_w_out=m_mix_w_out, ln2_g=m_ln2_g, ln2_b=m_ln2_b,
               ffn2_w_in=m_ffn2_w_in, ffn2_w_out=m_ffn2_w_out, ln3_g=m_ln3_g, ln3_b=m_ln3_b)
    var = dict(ffn1_w_in=v_ffn1_w_in, ffn1_w_out=v_ffn1_w_out, ln1_g=v_ln1_g, ln1_b=v_ln1_b, mix_w_in=v_mix_w_in,
               hgrn_lb_fwd=v_hgrn_lb_fwd, hgrn_lb_bwd=v_hgrn_lb_bwd, hgrn_norm_g=v_hgrn_norm_g,
               w_branch_a=v_w_branch_a, w_branch_b=v_w_branch_b, mix_w_out=v_mix_w_out, ln2_g=v_ln2_g, ln2_b=v_ln2_b,
               ffn2_w_in=v_ffn2_w_in, ffn2_w_out=v_ffn2_w_out, ln3_g=v_ln3_g, ln3_b=v_ln3_b)
    shard = (2 * lax.axis_index("x") + lax.axis_index("y")).astype(jnp.int32)
    sc_arr = jnp.stack([shard, lax.axis_index("c").astype(jnp.int32)])

    shard2d = {k: wts[k].reshape(wts[k].shape[1:]) for k in BIG}
    fulls = {BIG_KEY[k]: _cast_into_slot(shard2d[k], sc_arr) for k in BIG}
    p = {k: wts[k] for k in SMALL}

    loss, grad_x, grads, gp = _local_step(x, loss_target, fulls, p, sc_arr)
    loss = lax.psum(loss[0, 0], ("x", "y", "c"))

    out_g, out_d, out_m, out_v = {}, {}, {}, {}
    for k in BIG:
        g = grads[BIG_KEY[k]]
        shp = wts[k].shape
        res = _adamw(shard2d[k], g, mom[k].reshape(shp[1:]), var[k].reshape(shp[1:]))
        out_g[k], out_d[k], out_m[k], out_v[k] = [a.reshape(shp) for a in res]

    gathered = _gather_rows(_pack_small(gp))
    res = _adamw_small(gathered, _pack_small(wts), _pack_small(mom), _pack_small(var))
    for dst, a in zip((out_g, out_d, out_m, out_v), res):
        dst.update(_unpack_small(a, wts))

    return (loss, grad_x, *[out_g[k] for k in ORDER], *[out_d[k] for k in ORDER],
            *[out_m[k] for k in ORDER], *[out_v[k] for k in ORDER])
```

```python
import functools

import numpy as np
import jax
import jax.numpy as jnp
from jax import lax
from jax.experimental import pallas as pl
from jax.experimental.pallas import tpu as pltpu

F32 = jnp.float32
BF16 = jnp.bfloat16

HEAD = 128
ATTN_GROUPS = ((128, 1), (512, 4), (2048, 16))
ATTN_HEADS = 4
N_GROUPS = len(ATTN_GROUPS)
QKV_W = N_GROUPS * 3 * ATTN_HEADS * HEAD
ATTN_OUT = ATTN_HEADS * HEAD
ROPE_THETA = 500000.0
ROPE_DIM = HEAD // 4
HGRN_CHUNK = 32
HGRN_FWD_HEADS = 2
HGRN_BWD_HEADS = 2
HGRN_SUB = 4
ALPHA = 2.0 ** 0.25
LN_EPS = 1e-5
NEG_INF = -1e30
ADAM_LR, ADAM_B1, ADAM_B2, ADAM_EPS, ADAM_WD, ADAM_STEP = 0.001, 0.9, 0.999, 1e-08, 0.01, 10

N_SHARD = 4
VMEM_LIMIT = 56 * 1024 * 1024

NN = ((1,), (0,))
NT = ((1,), (1,))
TN = ((0,), (0,))


def _dot(a, b, dims, precision=None):
    return lax.dot_general(a, b, (dims, ((), ())), preferred_element_type=F32, precision=precision)


def _tile(n, pref, mult=128):
    best = None
    for t in range(mult, min(n, pref) + 1, mult):
        if n % t == 0:
            best = t
    return n if best is None else best


def _params(n_parallel, n_arbitrary):
    return pltpu.CompilerParams(
        dimension_semantics=("parallel",) * n_parallel + ("arbitrary",) * n_arbitrary,
        vmem_limit_bytes=VMEM_LIMIT)


def _sigmoid(x):
    return 1.0 / (1.0 + jnp.exp(-x))


def _silu(x):
    return x * _sigmoid(x)


class _Exchange:
    def __init__(self, ins, outs, aliases, sems, start, finish):
        self.ins, self.outs, self.aliases, self.sems = list(ins), list(outs), dict(aliases), list(sems)
        self.start, self.finish = start, finish


def _run(body, name, grid, ins, in_specs, outs, out_specs, scratch, n_arbitrary, side=None):
    n_in, n_out, n_scr = len(ins), len(outs), len(scratch)
    if side is None:
        return pl.pallas_call(
            body, name=name, grid=grid, in_specs=in_specs, out_specs=out_specs, out_shape=outs,
            scratch_shapes=scratch, compiler_params=_params(len(grid) - n_arbitrary, n_arbitrary))(*ins)
    s_in, s_out = len(side.ins), len(side.outs)
    i1 = n_in + s_in
    o1 = i1 + n_out
    o2 = o1 + s_out
    c1 = o2 + n_scr

    def wrapped(*refs):
        s_refs = (refs[n_in:i1], refs[o1:o2], refs[c1:])
        ids = [pl.program_id(a) for a in range(len(grid))]
        first = functools.reduce(jnp.logical_and, [i == 0 for i in ids])
        last = functools.reduce(jnp.logical_and, [i == g - 1 for i, g in zip(ids, grid)])

        @pl.when(first)
        def _():
            side.start(*s_refs)

        body(*refs[:n_in], *refs[i1:o1], *refs[o2:c1])

        @pl.when(last)
        def _():
            side.finish(*s_refs)

    res = pl.pallas_call(
        wrapped, name=name, grid=grid, in_specs=list(in_specs) + [ANY] * s_in,
        out_specs=list(out_specs) + [ANY] * s_out, out_shape=list(outs) + side.outs,
        scratch_shapes=list(scratch) + side.sems,
        input_output_aliases={n_in + a: n_out + b for a, b in side.aliases.items()},
        compiler_params=_params(0, len(grid)))(*ins, *side.ins)
    return res[:n_out], res[n_out:]


def _gemm(name, grid, ins, in_specs, outs, out_specs, accs, dot_fn, epi_fn, side=None):
    n_in, n_out, n_k = len(ins), len(outs), grid[-1]

    def body(*refs):
        in_refs, out_refs, acc_refs = refs[:n_in], refs[n_in:n_in + n_out], refs[n_in + n_out:]
        k = pl.program_id(len(grid) - 1)

        @pl.when(k == 0)
        def _():
            for a in acc_refs:
                a[...] = jnp.zeros(a.shape, F32)

        dot_fn(in_refs, acc_refs)

        @pl.when(k == n_k - 1)
        def _():
            epi_fn(in_refs, acc_refs, out_refs)

    return _run(body, name, grid, ins, in_specs, outs, out_specs, [pltpu.VMEM(s, F32) for s in accs], 1, side)


def _sds(shape, dtype):
    return jax.ShapeDtypeStruct(shape, dtype)


def _gemm_rows(name, grid, ins, in_specs, outs, out_specs, tm, n_sub, fn, side=None):
    n_in, sub = len(ins), tm // n_sub

    def body(*refs):
        for s in range(n_sub):
            fn(pl.ds(s * sub, sub), refs[:n_in], refs[n_in:])

    return _run(body, name, grid, ins, in_specs, outs, out_specs, [], 0, side)


def _ffn_up(xb, w3, side=None):
    t, d = xb.shape
    nf = w3.shape[2]
    f = 2 * nf
    tm = _tile(t, 512, 8)
    tn = nf // 2 if nf % 256 == 0 else nf
    ns = nf // tn

    def fn(rows, r, out):
        x = r[0][rows, :]
        g, u = _dot(x, r[1][...], NN), _dot(x, r[2][...], NN)
        out[0][0, rows, :] = g.astype(BF16)
        out[0][1, rows, :] = u.astype(BF16)
        out[1][rows, :] = (_silu(g) * u).astype(BF16)

    return _gemm_rows(
        "ffn_up", (t // tm, 2, ns), [xb, w3, w3],
        [pl.BlockSpec((tm, d), lambda i, j, h: (i, 0)),
         pl.BlockSpec((None, d, tn), lambda i, j, h: (j, 0, h)),
         pl.BlockSpec((None, d, tn), lambda i, j, h: (j + 2, 0, h))],
        [_sds((2, t, f), BF16), _sds((t, f), BF16)],
        [pl.BlockSpec((2, tm, tn), lambda i, j, h: (0, i, j * ns + h)),
         pl.BlockSpec((tm, tn), lambda i, j, h: (i, j * ns + h))],
        tm, 2, fn, side)


def _down_ln(a, w, resid, coef, g, b, side=None):
    t, kd = a.shape
    d = w.shape[1]
    tm = _tile(t, 256, 8)

    def fn(rows, r, out):
        v = ALPHA * r[2][rows, :] + coef * _dot(r[0][rows, :], r[1][...], NN)
        mu = jnp.mean(v, axis=-1, keepdims=True)
        c = v - mu
        var = jnp.mean(c * c, axis=-1, keepdims=True)
        rstd = lax.rsqrt(var + LN_EPS)
        xhat = c * rstd
        h = xhat * r[3][...] + r[4][...]
        out[0][rows, :] = h
        out[1][rows, :] = h.astype(BF16)
        out[2][rows, :] = xhat
        out[3][rows, :] = rstd

    row = pl.BlockSpec((tm, d), lambda i: (i, 0))
    vec = pl.BlockSpec((1, d), lambda i: (0, 0))
    return _gemm_rows(
        "down_ln", (t // tm,), [a, w, resid, g, b],
        [pl.BlockSpec((tm, kd), lambda i: (i, 0)),
         pl.BlockSpec((kd, d), lambda i: (0, 0), pipeline_mode=pl.Buffered(1)), row, vec, vec],
        [_sds((t, d), F32), _sds((t, d), BF16), _sds((t, d), F32), _sds((t, 1), F32)],
        [row, row, row, pl.BlockSpec((tm, 1), lambda i: (i, 0))], tm, 2, fn, side)


def _down_ln_loss(a, w, resid, coef, g, b, target):
    t, kd = a.shape
    d = w.shape[1]
    tm = _tile(t, 256, 8)
    n_sub = 2
    sub = tm // n_sub

    def body(a_ref, w_ref, r_ref, g_ref, b_ref, t_ref, dres_ref, dyb_ref, dg_ref, db_ref, loss_ref):
        @pl.when(pl.program_id(0) == 0)
        def _():
            dg_ref[...] = jnp.zeros(dg_ref.shape, F32)
            db_ref[...] = jnp.zeros(db_ref.shape, F32)
            loss_ref[...] = jnp.zeros(loss_ref.shape, F32)

        for s in range(n_sub):
            rows = pl.ds(s * sub, sub)
            v = ALPHA * r_ref[rows, :] + coef * _dot(a_ref[rows, :], w_ref[...], NN)
            mu = jnp.mean(v, axis=-1, keepdims=True)
            c = v - mu
            rstd = lax.rsqrt(jnp.mean(c * c, axis=-1, keepdims=True) + LN_EPS)
            xh = c * rstd
            e = xh * g_ref[...] + b_ref[...] - t_ref[rows, :]
            dh = e * (1.0 / d)
            dxh = dh * g_ref[...]
            m1 = jnp.mean(dxh, axis=-1, keepdims=True)
            m2 = jnp.mean(dxh * xh, axis=-1, keepdims=True)
            dv = rstd * (dxh - m1 - xh * m2)
            dres_ref[rows, :] = ALPHA * dv
            dyb_ref[rows, :] = (coef * dv).astype(BF16)
            dg_ref[...] += jnp.sum(dh * xh, axis=0, keepdims=True)
            db_ref[...] += jnp.sum(dh, axis=0, keepdims=True)
            part = 0.5 * jnp.sum(jnp.sum(e * e, axis=-1, keepdims=True) * (1.0 / d), axis=0, keepdims=True)
            loss_ref[...] += jnp.broadcast_to(part, loss_ref.shape)

    row = pl.BlockSpec((tm, d), lambda i: (i, 0))
    vec = pl.BlockSpec((1, d), lambda i: (0, 0))
    return _run(
        body, "down_ln_loss", (t // tm,), [a, w, resid, g, b, target],
        [pl.BlockSpec((tm, kd), lambda i: (i, 0)),
         pl.BlockSpec((kd, d), lambda i: (0, 0), pipeline_mode=pl.Buffered(1)), row, vec, vec, row],
        [_sds((t, d), F32), _sds((t, d), BF16), _sds((1, d), F32), _sds((1, d), F32), _sds((1, 128), F32)],
        [row, row, vec, vec, pl.BlockSpec((1, 128), lambda i: (0, 0))], [], 1)


def _only(res, side):
    return res[0] if side is None else (res[0][0], res[1])


def _mm_w3(a, w3, side=None):
    t, kd = a.shape
    n = w3.shape[2]
    tm = _tile(t, 512, 8)

    def fn(rows, r, out):
        out[0][rows, :] = _dot(r[0][rows, :], r[1][...], NN)

    return _only(_gemm_rows(
        "mm_w3", (N_SHARD, t // tm), [a, w3],
        [pl.BlockSpec((tm, kd), lambda j, i: (i, 0)),
         pl.BlockSpec((None, kd, n), lambda j, i: (j, 0, 0), pipeline_mode=pl.Buffered(1))],
        [_sds((t, N_SHARD * n), F32)], [pl.BlockSpec((tm, n), lambda j, i: (i, j))], tm, 1, fn, side), side)


def _mm_w2(a, w, out_dtype):
    t, kd = a.shape
    n = w.shape[1]
    tm, tn = _tile(t, 512, 8), _tile(n, 1024)

    def fn(rows, r, out):
        out[0][rows, :] = _dot(r[0][rows, :], r[1][...], NN).astype(out_dtype)

    return _gemm_rows(
        "mm_w2", (t // tm, n // tn), [a, w],
        [pl.BlockSpec((tm, kd), lambda i, j: (i, 0)), pl.BlockSpec((kd, tn), lambda i, j: (0, j))],
        [_sds((t, n), out_dtype)], [pl.BlockSpec((tm, tn), lambda i, j: (i, j))], tm, 1, fn)[0]


def _branch_gate(ob, wb3, ya, proj, gate_col):
    t, kd = ob.shape
    n = wb3.shape[2]
    d = N_SHARD * n
    tm = _tile(t, 512, 8)
    ga0, gb0 = gate_col // n, (gate_col + d) // n

    def fn(rows, r, out):
        yb = _dot(r[0][rows, :], r[1][...], NN)
        out[0][rows, :] = yb
        out[1][rows, :] = (_sigmoid(r[3][rows, :]) * r[2][rows, :] + _sigmoid(r[4][rows, :]) * yb).astype(BF16)

    blk = pl.BlockSpec((tm, n), lambda i, j: (i, j))
    return _gemm_rows(
        "branch_gate", (t // tm, N_SHARD), [ob, wb3, ya, proj, proj],
        [pl.BlockSpec((tm, kd), lambda i, j: (i, 0)), pl.BlockSpec((None, kd, n), lambda i, j: (j, 0, 0)), blk,
         pl.BlockSpec((tm, n), lambda i, j: (i, ga0 + j)), pl.BlockSpec((tm, n), lambda i, j: (i, gb0 + j))],
        [_sds((t, d), F32), _sds((t, d), BF16)], [blk, blk], tm, 2, fn)


def _swiglu_bwd(dyb, w, u3, side=None):
    t, d = dyb.shape
    f = w.shape[0]
    tm, tr = _tile(t, 1024, 8), _tile(f, 512)

    def fn(rows, r, out):
        da = _dot(r[0][rows, :], r[1][...], NT)
        g, u = r[2][0, rows, :].astype(F32), r[2][1, rows, :].astype(F32)
        s = _sigmoid(g)
        out[0][0, rows, :] = (da * u * s * (1.0 + g * (1.0 - s))).astype(BF16)
        out[0][1, rows, :] = (da * g * s).astype(BF16)

    ublk = pl.BlockSpec((2, tm, tr), lambda i, j: (0, i, j))
    return _only(_gemm_rows(
        "swiglu_bwd", (t // tm, f // tr), [dyb, w, u3],
        [pl.BlockSpec((tm, d), lambda i, j: (i, 0)), pl.BlockSpec((tr, d), lambda i, j: (j, 0)), ublk],
        [_sds((2, t, f), BF16)], [ublk], tm, 4, fn, side), side)


def _ffn_dx(du3, w3, resid, side=None):
    t = du3.shape[1]
    d, nf = w3.shape[1], w3.shape[2]
    tm, tr = _tile(t, 512, 8), _tile(d, 1024)

    def dot_fn(r, acc):
        acc[0][...] += _dot(r[0][...], r[1][...], NT)

    def epi_fn(r, acc, out):
        out[0][...] = acc[0][...] + r[2][...]

    blk = pl.BlockSpec((tm, tr), lambda i, j, k: (i, j))
    return _only(_gemm(
        "ffn_dx", (t // tm, d // tr, N_SHARD), [du3, w3, resid],
        [pl.BlockSpec((None, tm, nf), lambda i, j, k: (k // 2, i, k % 2)),
         pl.BlockSpec((None, tr, nf), lambda i, j, k: (k, j, 0)), blk],
        [_sds((t, d), F32)], [blk], [(tm, tr)], dot_fn, epi_fn, side), side)


def _nt_w3(dy, w3, resid, side=None):
    t = dy.shape[0]
    kd, n = w3.shape[1], w3.shape[2]
    tm, tr = _tile(t, 512, 8), _tile(kd, 1024)
    has_res = resid is not None

    def dot_fn(r, acc):
        acc[0][...] += _dot(r[0][...], r[1][...], NT)

    def epi_fn(r, acc, out):
        v = acc[0][...]
        if has_res:
            v = v + r[2][...]
        out[0][...] = v

    blk = pl.BlockSpec((tm, tr), lambda i, j, k: (i, j))
    return _only(_gemm(
        "nt_w3", (t // tm, kd // tr, N_SHARD), [dy, w3] + ([resid] if has_res else []),
        [pl.BlockSpec((tm, n), lambda i, j, k: (i, k)), pl.BlockSpec((None, tr, n), lambda i, j, k: (k, j, 0))]
        + ([blk] if has_res else []),
        [_sds((t, kd), F32)], [blk], [(tm, tr)], dot_fn, epi_fn, side), side)


def _nt_w2(dy, w, out_dtype):
    t, n = dy.shape
    r_ = w.shape[0]
    tm, tr = _tile(t, 512, 8), _tile(r_, 1024)

    def fn(rows, r, out):
        out[0][rows, :] = _dot(r[0][rows, :], r[1][...], NT).astype(out_dtype)

    return _gemm_rows(
        "nt_w2", (t // tm, r_ // tr), [dy, w],
        [pl.BlockSpec((tm, n), lambda i, j: (i, 0)), pl.BlockSpec((tr, n), lambda i, j: (j, 0))],
        [_sds((t, r_), out_dtype)], [pl.BlockSpec((tm, tr), lambda i, j: (i, j))], tm, 1, fn)[0]


def _gate_bwd(dmix, w, proj, ya, yb, gate_col):
    t, n = dmix.shape
    d = w.shape[0]
    tm, tr = _tile(t, 512, 8), _tile(d, 512)
    ga0, gb0 = gate_col // tr, (gate_col + d) // tr
    nb = d // tr

    def fn(rows, r, out):
        dz = _dot(r[0][rows, :], r[1][...], NT)
        ga, gb = _sigmoid(r[2][rows, :]), _sigmoid(r[3][rows, :])
        out[0][rows, :] = (dz * ga).astype(BF16)
        out[1][rows, :] = (dz * gb).astype(BF16)
        out[2][rows, :] = (dz * r[4][rows, :] * ga * (1.0 - ga)).astype(BF16)
        out[3][rows, :] = (dz * r[5][rows, :] * gb * (1.0 - gb)).astype(BF16)

    blk = pl.BlockSpec((tm, tr), lambda i, j: (i, j))
    return _gemm_rows(
        "gate_bwd", (t // tm, nb), [dmix, w, proj, proj, ya, yb],
        [pl.BlockSpec((tm, n), lambda i, j: (i, 0)), pl.BlockSpec((tr, n), lambda i, j: (j, 0)),
         pl.BlockSpec((tm, tr), lambda i, j: (i, ga0 + j)), pl.BlockSpec((tm, tr), lambda i, j: (i, gb0 + j)),
         blk, blk],
        [_sds((t, d), BF16), _sds((t, d), BF16), _sds((t, d), BF16), _sds((t, d), BF16)],
        [blk, blk, blk, blk], tm, 2, fn)


def _tn_w3(a, dy, dy_map, n, side=None):
    t, kd = a.shape
    tm, tkk = _tile(t, (6 << 20) // n, 8), _tile(kd, 512)

    def dot_fn(r, acc):
        acc[0][...] += _dot(r[0][...], r[1][...], TN)

    def epi_fn(r, acc, out):
        out[0][...] = acc[0][...].astype(BF16)

    dy_block = (tm, n) if dy.ndim == 2 else (None, tm, n)
    return _only(_gemm(
        "tn_w3", (kd // tkk, N_SHARD, t // tm), [a, dy],
        [pl.BlockSpec((tm, tkk), lambda i, j, m: (m, i)), pl.BlockSpec(dy_block, lambda i, j, m: dy_map(j, m))],
        [_sds((N_SHARD, kd, n), BF16)], [pl.BlockSpec((None, tkk, n), lambda i, j, m: (j, i, 0))],
        [(tkk, n)], dot_fn, epi_fn, side), side)


def _tn_w2(a, dy):
    t, kd = a.shape
    n = dy.shape[1]
    tkk, tn = _tile(kd, 512), _tile(n, 2048)
    tm = _tile(t, (4 << 20) // tn, 8)

    def dot_fn(r, acc):
        acc[0][...] += _dot(r[0][...], r[1][...], TN)

    def epi_fn(r, acc, out):
        out[0][...] = acc[0][...].astype(BF16)

    return _gemm(
        "tn_w2", (kd // tkk, n // tn, t // tm), [a, dy],
        [pl.BlockSpec((tm, tkk), lambda i, j, m: (m, i)), pl.BlockSpec((tm, tn), lambda i, j, m: (m, j))],
        [_sds((kd, n), BF16)], [pl.BlockSpec((tkk, tn), lambda i, j, m: (i, j))],
        [(tkk, tn)], dot_fn, epi_fn)[0]


def _ln_bwd(dh, xhat, rstd, g, coef):
    t, d = xhat.shape
    tm = _tile(t, 256, 8)

    def body(dh_ref, xh_ref, rstd_ref, g_ref, dres_ref, dyb_ref, dg_ref, db_ref):
        dh, xh = dh_ref[...], xh_ref[...]
        dxh = dh * g_ref[...]
        m1 = jnp.mean(dxh, axis=-1, keepdims=True)
        m2 = jnp.mean(dxh * xh, axis=-1, keepdims=True)
        dv = rstd_ref[...] * (dxh - m1 - xh * m2)
        dres_ref[...] = ALPHA * dv
        dyb_ref[...] = (coef * dv).astype(BF16)

        @pl.when(pl.program_id(0) == 0)
        def _():
            dg_ref[...] = jnp.zeros(dg_ref.shape, F32)
            db_ref[...] = jnp.zeros(db_ref.shape, F32)

        dg_ref[...] += jnp.sum(dh * xh, axis=0, keepdims=True)
        db_ref[...] += jnp.sum(dh, axis=0, keepdims=True)

    row = pl.BlockSpec((tm, d), lambda i: (i, 0))
    vec = pl.BlockSpec((1, d), lambda i: (0, 0))
    return pl.pallas_call(
        body, name="ln_bwd", grid=(t // tm,),
        in_specs=[row, row, pl.BlockSpec((tm, 1), lambda i: (i, 0)), vec], out_specs=[row, row, vec, vec],
        out_shape=[_sds((t, d), F32), _sds((t, d), BF16), _sds((1, d), F32), _sds((1, d), F32)],
        compiler_params=_params(0, 1))(dh, xhat, rstd, g)


def _cast_bf16(x2d):
    t, d = x2d.shape
    tm = _tile(t, 512, 8)

    def body(x_ref, o_ref):
        o_ref[...] = x_ref[...].astype(BF16)

    row = pl.BlockSpec((tm, d), lambda i: (i, 0))
    return pl.pallas_call(body, name="cast_bf16", grid=(t // tm,), in_specs=[row], out_specs=row,
                          out_shape=_sds((t, d), BF16), compiler_params=_params(1, 0))(x2d)


def _lower_bound(table):
    t0, t1 = table[0:1, :], table[1:2, :]
    m = jnp.maximum(t0, t1)
    e0, e1 = jnp.exp(t0 - m), jnp.exp(t1 - m)
    return e0 / (e0 + e1)


def _chunk_tri(rows, upper):
    r = lax.broadcasted_iota(jnp.int32, (rows, rows), 0)
    s = lax.broadcasted_iota(jnp.int32, (rows, rows), 1)
    shift = HGRN_CHUNK.bit_length() - 1
    same = lax.shift_right_logical(r, shift) == lax.shift_right_logical(s, shift)
    return same & ((r <= s) if upper else (r >= s))


def _tri_apply(x, upper):
    tri = _chunk_tri(x.shape[0], upper).astype(F32).astype(BF16)
    hi = x.astype(BF16)
    r1 = x - hi.astype(F32)
    mid = r1.astype(BF16)
    lo = (r1 - mid.astype(F32)).astype(BF16)
    return _dot(tri, hi, NN) + _dot(tri, mid, NN) + _dot(tri, lo, NN)


@functools.partial(jax.custom_vjp, nondiff_argnums=(1,))
def _chunk_cumsum(x, upper):
    return _tri_apply(x, upper)


def _chunk_cumsum_fwd(x, upper):
    return _tri_apply(x, upper), None


def _chunk_cumsum_bwd(upper, _, g):
    return (_tri_apply(g, not upper),)


_chunk_cumsum.defvjp(_chunk_cumsum_fwd, _chunk_cumsum_bwd)


def _hgrn_blocks(units, uppers):
    c = HGRN_CHUNK
    rows = units[0][0].shape[0]
    n_sub = rows // c
    ids = range(len(units))
    chunk = lax.shift_right_logical(lax.broadcasted_iota(jnp.int32, (rows, HEAD), 0), c.bit_length() - 1)
    zero = jnp.zeros((rows, HEAD), BF16)

    def expand(a):
        return jnp.concatenate([jnp.where(chunk == n, a, zero) for n in range(n_sub)], axis=1)

    fs = [u[3] + (1.0 - u[3]) * _sigmoid(u[1]) for u in units]
    lgs = [jnp.log(f) for f in fs]
    cums = [_chunk_cumsum(lgs[i], uppers[i]) for i in ids]
    tots = [[jnp.sum(lg[n * c:(n + 1) * c], axis=0, keepdims=True) for n in range(n_sub)] for lg in lgs]
    qd, kd, ke, vb = [], [], [], []
    for i in ids:
        totb = jnp.concatenate([jnp.broadcast_to(t, (c, HEAD)) for t in tots[i]], axis=0)
        kk = 1.0 - fs[i]
        qd.append((_silu(units[i][0]) * jnp.exp(cums[i])).astype(BF16))
        kd.append((kk * jnp.exp(-cums[i])).astype(BF16))
        ke.append((kk * jnp.exp(totb - cums[i])).astype(BF16))
        vb.append(units[i][2].astype(BF16))
    scores = [_dot(qd[i], kd[i], NT) for i in ids]
    kvs = [_dot(vb[i], expand(ke[i]), TN) for i in ids]
    outs = []
    for i in ids:
        a = jnp.where(_chunk_tri(rows, uppers[i]), scores[i], 0.0).astype(BF16)
        st = units[i][4]
        entering = [None] * n_sub
        for n in (range(n_sub - 1, -1, -1) if uppers[i] else range(n_sub)):
            entering[n] = st.astype(BF16)
            st = st * jnp.exp(tots[i][n]) + kvs[i][:, n * HEAD:(n + 1) * HEAD]
        outs.append((a, jnp.concatenate(entering, axis=1), st))
    res = []
    for i in ids:
        a, entering, st = outs[i]
        res.append((_dot(a, vb[i], NN) + _dot(expand(qd[i]), entering, NT), st))
    return res


def _hgrn_fwd(proj3, lb_f, lb_b, norm_g, d, side=None):
    b_, s_, _ = proj3.shape
    nh = d // HEAD
    hb = HGRN_FWD_HEADS
    wid = hb * HEAD
    rows = HGRN_CHUNK * HGRN_SUB
    n_blk = s_ // rows

    def body(hq_ref, hff_ref, hfb_ref, hi_ref, hog_ref, lbf_ref, lbb_ref, g_ref, oraw_ref, out_ref, of_ref, ob_ref):
        lbf, lbb = _lower_bound(lbf_ref[...]), _lower_bound(lbb_ref[...])

        def step(j, sts):
            rf = pl.ds(pl.multiple_of(j * rows, rows), rows)
            rb = pl.ds(pl.multiple_of((n_blk - 1 - j) * rows, rows), rows)
            units = []
            for hh in range(hb):
                cs = slice(hh * HEAD, (hh + 1) * HEAD)
                units.append((hq_ref[rf, cs], hff_ref[rf, cs], hi_ref[rf, cs], lbf[:, cs], sts[2 * hh]))
                units.append((hq_ref[rb, cs], hfb_ref[rb, cs], hi_ref[rb, cs], lbb[:, cs], sts[2 * hh + 1]))
            res = _hgrn_blocks(units, [False, True] * hb)
            for hh in range(hb):
                cs = slice(hh * HEAD, (hh + 1) * HEAD)
                of_ref[rf, cs] = res[2 * hh][0]
                ob_ref[rb, cs] = res[2 * hh + 1][0]
            return tuple(r[1] for r in res)

        z = jnp.zeros((HEAD, HEAD), F32)
        lax.fori_loop(0, n_blk, step, (z,) * (2 * hb))
        for hh in range(hb):
            cs = slice(hh * HEAD, (hh + 1) * HEAD)
            o = of_ref[:, cs] + ob_ref[:, cs]
            oraw_ref[:, cs] = o
            on = o * lax.rsqrt(jnp.mean(o * o, axis=-1, keepdims=True) + LN_EPS)
            out_ref[:, cs] = (on * g_ref[:, cs] * _silu(hog_ref[:, cs])).astype(BF16)

    def col(k):
        return pl.BlockSpec((None, s_, wid), lambda h, b: (b, 0, k * (nh // hb) + h))

    tab = pl.BlockSpec((2, wid), lambda h, b: (0, h))
    oblk = pl.BlockSpec((None, s_, wid), lambda h, b: (b, 0, h))
    return _run(
        body, "hgrn_fwd", (nh // hb, b_), [proj3, proj3, proj3, proj3, proj3, lb_f, lb_b, norm_g],
        [col(0), col(1), col(2), col(3), col(4), tab, tab, pl.BlockSpec((1, wid), lambda h, b: (0, h))],
        [_sds((b_, s_, d), F32), _sds((b_, s_, d), BF16)], [oblk, oblk],
        [pltpu.VMEM((s_, wid), F32), pltpu.VMEM((s_, wid), F32)], 0, side)


def _hgrn_bwd(proj3, lb_f, lb_b, norm_g, o_raw, do_a, d, side=None):
    b_, s_, _ = proj3.shape
    nh = d // HEAD
    hb = HGRN_BWD_HEADS
    wid = hb * HEAD
    rows = HGRN_CHUNK * HGRN_SUB
    n_blk = s_ // rows

    def body(hq_ref, hff_ref, hfb_ref, hi_ref, hog_ref, lbf_ref, lbb_ref, g_ref, oraw_ref, doa_ref,
             dq_ref, dff_ref, dfb_ref, di_ref, dog_ref, dlbf_ref, dlbb_ref, dg_ref,
             st_ref, dor_ref, dq2_ref, di2_ref):
        b = pl.program_id(1)
        tab_f, tab_b = lbf_ref[...], lbb_ref[...]
        lbf, lbb = _lower_bound(tab_f), _lower_bound(tab_b)

        dg_parts = []
        for hh in range(hb):
            cs = slice(hh * HEAD, (hh + 1) * HEAD)
            o, doa, hog, g = oraw_ref[:, cs], doa_ref[:, cs], hog_ref[:, cs], g_ref[:, cs]
            rs = lax.rsqrt(jnp.mean(o * o, axis=-1, keepdims=True) + LN_EPS)
            on = o * rs
            sg = _sigmoid(hog)
            gate = hog * sg
            dog_ref[:, cs] = (doa * on * g * sg * (1.0 + hog * (1.0 - sg))).astype(BF16)
            don = doa * g * gate
            dor_ref[:, cs] = rs * (don - on * jnp.mean(don * on, axis=-1, keepdims=True))
            dg_parts.append(jnp.sum(doa * on * gate, axis=0, keepdims=True))

        def fwd_step(j, carry):
            jb = n_blk - 1 - j
            rf = pl.ds(pl.multiple_of(j * rows, rows), rows)
            rb = pl.ds(pl.multiple_of(jb * rows, rows), rows)
            units = []
            for hh in range(hb):
                cs = slice(hh * HEAD, (hh + 1) * HEAD)
                st_ref[2 * hh, j] = carry[2 * hh]
                st_ref[2 * hh + 1, jb] = carry[2 * hh + 1]
                units.append((hq_ref[rf, cs], hff_ref[rf, cs], hi_ref[rf, cs], lbf[:, cs], carry[2 * hh]))
                units.append((hq_ref[rb, cs], hfb_ref[rb, cs], hi_ref[rb, cs], lbb[:, cs], carry[2 * hh + 1]))
            return tuple(r[1] for r in _hgrn_blocks(units, uppers))

        uppers = [False, True] * hb
        z = jnp.zeros((HEAD, HEAD), F32)
        lax.fori_loop(0, n_blk, fwd_step, (z,) * (2 * hb))

        def bwd_step(j, carry):
            gs, dls = carry
            jf = n_blk - 1 - j
            rf = pl.ds(pl.multiple_of(jf * rows, rows), rows)
            rb = pl.ds(pl.multiple_of(j * rows, rows), rows)
            flat, cots = [], []
            for hh in range(hb):
                cs = slice(hh * HEAD, (hh + 1) * HEAD)
                flat += [hq_ref[rf, cs], hff_ref[rf, cs], hi_ref[rf, cs], lbf[:, cs], st_ref[2 * hh, jf],
                         hq_ref[rb, cs], hfb_ref[rb, cs], hi_ref[rb, cs], lbb[:, cs], st_ref[2 * hh + 1, j]]
                cots += [(dor_ref[rf, cs], gs[2 * hh]), (dor_ref[rb, cs], gs[2 * hh + 1])]
            _, vjp = jax.vjp(lambda *a: _hgrn_blocks([a[5 * i:5 * i + 5] for i in range(2 * hb)], uppers), *flat)
            grads = vjp(cots)
            gs, dls = list(gs), list(dls)
            for hh in range(hb):
                cs = slice(hh * HEAD, (hh + 1) * HEAD)
                dq, df, di, dl, gs[2 * hh] = grads[10 * hh:10 * hh + 5]
                dq_ref[rf, cs] = dq.astype(BF16)
                dff_ref[rf, cs] = df.astype(BF16)
                di_ref[rf, cs] = di.astype(BF16)
                dls[2 * hh] = dls[2 * hh] + dl
                dq, df, di, dl, gs[2 * hh + 1] = grads[10 * hh + 5:10 * hh + 10]
                dq2_ref[rb, cs] = dq
                dfb_ref[rb, cs] = df.astype(BF16)
                di2_ref[rb, cs] = di
                dls[2 * hh + 1] = dls[2 * hh + 1] + dl
            return tuple(gs), tuple(dls)

        zl = jnp.zeros((1, HEAD), F32)
        _, dls = lax.fori_loop(0, n_blk, bwd_step, ((z,) * (2 * hb), (zl,) * (2 * hb)))
        dq_ref[...] = (dq_ref[...].astype(F32) + dq2_ref[...]).astype(BF16)
        di_ref[...] = (di_ref[...].astype(F32) + di2_ref[...]).astype(BF16)

        _, vjp_tf = jax.vjp(_lower_bound, tab_f)
        _, vjp_tb = jax.vjp(_lower_bound, tab_b)

        @pl.when(b == 0)
        def _():
            dlbf_ref[...] = jnp.zeros(dlbf_ref.shape, F32)
            dlbb_ref[...] = jnp.zeros(dlbb_ref.shape, F32)
            dg_ref[...] = jnp.zeros(dg_ref.shape, F32)

        dlbf_ref[...] += vjp_tf(jnp.concatenate(dls[0::2], axis=1))[0]
        dlbb_ref[...] += vjp_tb(jnp.concatenate(dls[1::2], axis=1))[0]
        dg_ref[...] += jnp.concatenate(dg_parts, axis=1)

    def col(k):
        return pl.BlockSpec((None, s_, wid), lambda h, b: (b, 0, k * (nh // hb) + h))

    tab = pl.BlockSpec((2, wid), lambda h, b: (0, h))
    vec = pl.BlockSpec((1, wid), lambda h, b: (0, h))
    oblk = pl.BlockSpec((None, s_, wid), lambda h, b: (b, 0, h))
    seq = _sds((b_, s_, d), BF16)
    return _run(
        body, "hgrn_bwd", (nh // hb, b_), [proj3, proj3, proj3, proj3, proj3, lb_f, lb_b, norm_g, o_raw, do_a],
        [col(0), col(1), col(2), col(3), col(4), tab, tab, vec, oblk, oblk],
        [seq, seq, seq, seq, seq, _sds((2, d), F32), _sds((2, d), F32), _sds((1, d), F32)],
        [oblk, oblk, oblk, oblk, oblk, tab, tab, vec],
        [pltpu.VMEM((2 * hb, n_blk, HEAD, HEAD), F32),
         pltpu.VMEM((s_, wid), F32), pltpu.VMEM((s_, wid), F32), pltpu.VMEM((s_, wid), F32)], 1, side)


def _rope_tables(s_):
    inv = ROPE_THETA ** (-jnp.arange(0, ROPE_DIM, 2, dtype=F32) / ROPE_DIM)
    ang = jnp.arange(s_, dtype=F32)[:, None] * inv
    cos, sin = jnp.cos(ang), jnp.sin(ang)
    rest = HEAD - ROPE_DIM
    ctab = jnp.concatenate([cos, cos, jnp.ones((s_, rest), F32)], axis=1)
    stab = jnp.concatenate([-sin, sin, jnp.zeros((s_, rest), F32)], axis=1)
    half = ROPE_DIM // 2
    perm = np.zeros((HEAD, HEAD), np.float32)
    for i in range(half):
        perm[i + half, i] = 1.0
        perm[i, i + half] = 1.0
    return ctab, stab, jnp.asarray(perm)


ATTN_GROUP = 8


def _attn_tiles(qkv, tabs, offs, perm, half):
    hi = lax.Precision.HIGHEST
    ids = range(len(qkv))
    qrot = [_dot(qkv[i][0], perm, NN, precision=hi) for i in ids]
    krot = [_dot(qkv[i][1], perm, NN, precision=hi) for i in ids]
    qs = [(qkv[i][0] * tabs[i][0] + qrot[i] * tabs[i][1]).astype(BF16) for i in ids]
    ks = [(qkv[i][1] * tabs[i][2] + krot[i] * tabs[i][3]).astype(BF16) for i in ids]
    ss = [_dot(qs[i], ks[i], NT) * (HEAD ** -0.5) for i in ids]
    ps, dens, lses = [], [], []
    for i in ids:
        qi = offs[i][0] + lax.broadcasted_iota(jnp.int32, ss[i].shape, 0)
        kj = offs[i][1] + lax.broadcasted_iota(jnp.int32, ss[i].shape, 1)
        s = jnp.where(jnp.abs(qi - kj) <= half, ss[i], NEG_INF)
        m = lax.stop_gradient(jnp.max(s, axis=-1, keepdims=True))
        p = jnp.exp(s - m)
        den = jnp.sum(p, axis=-1, keepdims=True)
        ps.append(p.astype(BF16))
        dens.append(den)
        lses.append(m + jnp.log(den))
    res = []
    for i in ids:
        o = _dot(ps[i], qkv[i][2].astype(BF16), NN) / dens[i]
        res.append((o, jnp.broadcast_to(lses[i], o.shape)))
    return res


def _attn_tiling(seg):
    tq = seg if seg <= 256 else 256
    kw = seg if seg <= 512 else 512
    tiles = []
    for i in range(seg // tq):
        ws = min(max(i * tq - (kw - tq) // 2, 0), seg - kw)
        tiles.append((i * tq, ws))
    return tq, kw, tiles


def _attn_specs(s_, d, g):
    c0 = (5 * d + g * 3 * ATTN_OUT) // HEAD

    def col(part):
        return pl.BlockSpec((None, s_, HEAD), lambda b, h: (b, 0, c0 + part * ATTN_HEADS + h))

    tab = pl.BlockSpec((s_, HEAD), lambda b, h: (0, 0))
    perm = pl.BlockSpec((HEAD, HEAD), lambda b, h: (0, 0))
    oblk = pl.BlockSpec((None, s_, HEAD), lambda b, h: (b, 0, h))
    return col, tab, perm, oblk


def _rows(r, first, count, dil):
    return pl.ds(r + dil * first, count, stride=dil) if dil > 1 else pl.ds(first, count)


def _attn_operands(group, dil, tq, kw, q_ref, k_ref, v_ref, c_ref, s_ref):
    qkv, tabs = [], []
    for r, q0, k0 in group:
        rq, rk = _rows(r, q0, tq, dil), _rows(r, k0, kw, dil)
        qkv.append((q_ref[rq, :], k_ref[rk, :], v_ref[rk, :]))
        tabs.append((c_ref[rq, :], s_ref[rq, :], c_ref[rk, :], s_ref[rk, :]))
    return qkv, tabs


def _attn_fwd(proj3, ctab, stab, perm, d, g):
    window, dil = ATTN_GROUPS[g]
    b_, s_, _ = proj3.shape
    seg = s_ // dil
    half = window // (2 * dil)
    tq, kw, tiles = _attn_tiling(seg)

    work = [(r, q0, k0) for r in range(dil) for q0, k0 in tiles]

    def body(q_ref, k_ref, v_ref, c_ref, s_ref, p_ref, o_ref, l_ref):
        pm = p_ref[...]
        for g0 in range(0, len(work), ATTN_GROUP):
            group = work[g0:g0 + ATTN_GROUP]
            qkv, tabs = _attn_operands(group, dil, tq, kw, q_ref, k_ref, v_ref, c_ref, s_ref)
            res = _attn_tiles(qkv, tabs, [(q0, k0) for _, q0, k0 in group], pm, half)
            for (r, q0, _), (o, l) in zip(group, res):
                o_ref[_rows(r, q0, tq, dil), :] = o
                l_ref[_rows(r, q0, tq, dil), :] = l

    col, tab, pspec, oblk = _attn_specs(s_, d, g)
    shp = _sds((b_, s_, ATTN_OUT), F32)
    return pl.pallas_call(
        body, name=f"attn_fwd_d{dil}", grid=(b_, ATTN_HEADS), in_specs=[col(0), col(1), col(2), tab, tab, pspec],
        out_specs=[oblk, oblk], out_shape=[shp, shp], compiler_params=_params(2, 0))(
            proj3, proj3, proj3, ctab, stab, perm)


def _attn_bwd(proj3, ctab, stab, perm, do, dl, d, g):
    window, dil = ATTN_GROUPS[g]
    b_, s_, _ = proj3.shape
    seg = s_ // dil
    half = window // (2 * dil)
    tq, kw, tiles = _attn_tiling(seg)

    work = [(r, q0, k0) for r in range(dil) for q0, k0 in tiles]

    def body(q_ref, k_ref, v_ref, c_ref, s_ref, p_ref, do_ref, dl_ref, dq_ref, dk_ref, dv_ref, dq_s, dk_s, dv_s):
        pm = p_ref[...]
        dk_s[...] = jnp.zeros(dk_s.shape, F32)
        dv_s[...] = jnp.zeros(dv_s.shape, F32)
        for g0 in range(0, len(work), ATTN_GROUP):
            group = work[g0:g0 + ATTN_GROUP]
            qkv, tabs = _attn_operands(group, dil, tq, kw, q_ref, k_ref, v_ref, c_ref, s_ref)
            offs = [(q0, k0) for _, q0, k0 in group]
            _, vjp = jax.vjp(
                lambda *a: _attn_tiles([a[3 * i:3 * i + 3] for i in range(len(group))], tabs, offs, pm, half),
                *[x for tile in qkv for x in tile])
            grads = vjp([(do_ref[_rows(r, q0, tq, dil), :], dl_ref[_rows(r, q0, tq, dil), :]) for r, q0, _ in group])
            for i, (r, q0, k0) in enumerate(group):
                rk = _rows(r, k0, kw, dil)
                dq_s[_rows(r, q0, tq, dil), :] = grads[3 * i]
                dk_s[rk, :] += grads[3 * i + 1]
                dv_s[rk, :] += grads[3 * i + 2]
        dq_ref[...] = dq_s[...].astype(BF16)
        dk_ref[...] = dk_s[...].astype(BF16)
        dv_ref[...] = dv_s[...].astype(BF16)

    col, tab, pspec, oblk = _attn_specs(s_, d, g)
    shp = _sds((b_, s_, ATTN_OUT), BF16)
    full = pltpu.VMEM((s_, HEAD), F32)
    return pl.pallas_call(
        body, name=f"attn_bwd_d{dil}", grid=(b_, ATTN_HEADS),
        in_specs=[col(0), col(1), col(2), tab, tab, pspec, oblk, oblk], out_specs=[oblk, oblk, oblk],
        out_shape=[shp, shp, shp], scratch_shapes=[full, full, full],
        compiler_params=_params(2, 0))(proj3, proj3, proj3, ctab, stab, perm, do, dl)


def _combine(os_, ls_):
    m = jnp.maximum(jnp.maximum(ls_[0], ls_[1]), ls_[2])
    es = [jnp.exp(l - m) for l in ls_]
    return (es[0] * os_[0] + es[1] * os_[1] + es[2] * os_[2]) / (es[0] + es[1] + es[2])


def _combine_fwd(os_, ls_):
    t, w = os_[0].shape
    tm = _tile(t, 512, 8)

    def body(*refs):
        refs[6][...] = _combine([r[...] for r in refs[:3]], [r[...] for r in refs[3:6]]).astype(BF16)

    row = pl.BlockSpec((tm, w), lambda i: (i, 0))
    return pl.pallas_call(body, name="combine_fwd", grid=(t // tm,), in_specs=[row] * 6, out_specs=row,
                          out_shape=_sds((t, w), BF16), compiler_params=_params(1, 0))(*os_, *ls_)


def _combine_bwd(os_, ls_, dob):
    t, w = os_[0].shape
    tm = _tile(t, 512, 8)

    def body(*refs):
        _, vjp = jax.vjp(lambda *a: _combine(a[:3], a[3:]), *[r[...] for r in refs[:6]])
        for r, g in zip(refs[7:], vjp(refs[6][...])):
            r[...] = g

    row = pl.BlockSpec((tm, w), lambda i: (i, 0))
    return pl.pallas_call(body, name="combine_bwd", grid=(t // tm,), in_specs=[row] * 7, out_specs=[row] * 6,
                          out_shape=[_sds((t, w), F32)] * 6, compiler_params=_params(1, 0))(*os_, *ls_, dob)


ROW_SHARDED = ("ffn1_out", "ffn2_out", "wa", "mix_out")


def _local_step(x, target, fulls, p, sc_arr):
    b_, s_, d = x.shape
    t = b_ * s_
    x2 = x.reshape(t, d)
    gate_col = 5 * d + QKV_W
    c_arr = sc_arr[1:2]
    w = {}

    def arrived(keys, arrays):
        for k, a in zip(keys, arrays):
            w[k] = a.reshape(a.shape[0] * a.shape[1], a.shape[2]) if k in ROW_SHARDED else a

    def reduce_begin(keys):
        g3 = [gw[k].reshape(fulls[k].shape) for k in keys]
        return [_sum_halves(a, b, c_arr) for a, b in zip(g3, _swap_halves(g3))]

    def reduce_end(keys, parts, got):
        for k, a, b in zip(keys, parts, got):
            reduced[k] = _sum_chips(a, b, sc_arr)

    arrived(["ffn1_in"], _gather_weights([fulls["ffn1_in"]]))
    xb = _cast_bf16(x2)
    mix_rows = fulls["mix_in"].shape[1]
    (u1, a1), got = _ffn_up(xb, w["ffn1_in"], _gather_exchange([fulls["ffn1_out"], fulls["mix_in"]],
                                                              [None, (0, mix_rows // 2)]))
    arrived(["ffn1_out"], got[:1])
    (h1, h1b, xh1, rs1), got = _down_ln(a1, w["ffn1_out"], x2, 0.5, p["ln1_g"], p["ln1_b"],
                                        _gather_exchange(got[1:], [(mix_rows // 2, mix_rows // 2)]))
    arrived(["mix_in"], got)

    keys = ["wa", "wb", "mix_out", "ffn2_in"]
    proj, got = _mm_w3(h1b, w["mix_in"], _gather_exchange([fulls[k] for k in keys]))
    arrived(keys, got)
    n_in = proj.shape[1]
    proj3 = proj.reshape(b_, s_, n_in)
    (o_raw, oa), got = _hgrn_fwd(proj3, p["hgrn_lb_fwd"], p["hgrn_lb_bwd"], p["hgrn_norm_g"], d,
                                 _gather_exchange([fulls["ffn2_out"]]))
    arrived(["ffn2_out"], got)
    oa2 = oa.reshape(t, d)

    ctab, stab, perm = _rope_tables(s_)
    os_, ls_ = [], []
    for g in range(N_GROUPS):
        o_g, l_g = _attn_fwd(proj3, ctab, stab, perm, d, g)
        os_.append(o_g.reshape(t, ATTN_OUT))
        ls_.append(l_g.reshape(t, ATTN_OUT))
    ob = _combine_fwd(os_, ls_)

    ya = _mm_w2(oa2, w["wa"], F32)
    yb, zb = _branch_gate(ob, w["wb"], ya, proj, gate_col)
    h2, h2b, xh2, rs2 = _down_ln(zb, w["mix_out"], h1, 1.0, p["ln2_g"], p["ln2_b"])

    u2, a2 = _ffn_up(h2b, w["ffn2_in"])

    gw, gp, reduced = {}, {}, {}
    nf = w["ffn2_in"].shape[2]

    def du_map(j, m):
        return (j // 2, m, j % 2)

    dres3, dy3, gp["ln3_g"], gp["ln3_b"], loss = _down_ln_loss(
        a2, w["ffn2_out"], h2, 0.5, p["ln3_g"], p["ln3_b"], target.reshape(t, d))
    du2 = _swiglu_bwd(dy3, w["ffn2_out"], u2)
    g_out = _tn_w2(a2, dy3).reshape(fulls["ffn2_out"].shape)
    g_in, got = _tn_w3(h2b, du2, du_map, nf, _swap_exchange([g_out]))
    parts_a = [_sum_halves(g_out, got[0], c_arr)]
    dh2, got = _ffn_dx(du2, w["ffn2_in"], dres3, _both(_scatter_exchange(parts_a), _swap_exchange([g_in])))
    reduce_end(["ffn2_out"], parts_a, got[:1])
    parts_a.append(_sum_halves(g_in, got[1], c_arr))

    dres2, dmix, gp["ln2_g"], gp["ln2_b"] = _ln_bwd(dh2, xh2, rs2, p["ln2_g"], 1.0)
    dya, dyb, dpga, dpgb = _gate_bwd(dmix, w["mix_out"], proj, ya, yb, gate_col)
    gw["mix_out"] = _tn_w2(zb, dmix)
    do_a = _nt_w2(dya, w["wa"], F32)
    gw["wa"] = _tn_w2(oa2, dya)
    nb = w["wb"].shape[2]
    do_b = _nt_w3(dyb, w["wb"], None)
    gw["wb"] = _tn_w3(ob, dyb, lambda j, m: (m, j), nb)
    keys = ["ffn2_in", "mix_out", "wa", "wb"]
    parts_b = parts_a[1:] + reduce_begin(keys[1:])

    (dq, dff, dfb, di, dog, gp["hgrn_lb_fwd"], gp["hgrn_lb_bwd"], gp["hgrn_norm_g"]), got = _hgrn_bwd(
        proj3, p["hgrn_lb_fwd"], p["hgrn_lb_bwd"], p["hgrn_norm_g"], o_raw, do_a.reshape(b_, s_, d), d,
        _scatter_exchange(parts_b))
    reduce_end(keys, parts_b, got)

    douts = _combine_bwd(os_, ls_, do_b)
    dqkv = []
    for g in range(N_GROUPS):
        grads_g = _attn_bwd(proj3, ctab, stab, perm, douts[g].reshape(b_, s_, ATTN_OUT),
                            douts[3 + g].reshape(b_, s_, ATTN_OUT), d, g)
        dqkv += [a.reshape(t, ATTN_OUT) for a in grads_g]

    dproj = jnp.concatenate(
        [a.reshape(t, d) for a in (dq, dff, dfb, di, dog)] + dqkv + [dpga, dpgb], axis=1)
    nm = w["mix_in"].shape[2]
    gw["mix_in"] = _tn_w3(h1b, dproj, lambda j, m: (m, j), nm)
    parts_c = reduce_begin(["mix_in"])
    cut = 3 * parts_c[0].shape[1] // 4
    dh1, got_c = _nt_w3(dproj, w["mix_in"], dres2, _scatter_exchange(parts_c, [(0, cut)]))

    dres1, dy1, gp["ln1_g"], gp["ln1_b"] = _ln_bwd(dh1, xh1, rs1, p["ln1_g"], 0.5)
    g_out = _tn_w2(a1, dy1).reshape(fulls["ffn1_out"].shape)
    du1, got = _swiglu_bwd(dy1, w["ffn1_out"], u1, _both(
        _swap_exchange([g_out]), _scatter_exchange(parts_c, [(cut, parts_c[0].shape[1] - cut)], got_c)))
    reduce_end(["mix_in"], parts_c, got[1:])
    parts_d = [_sum_halves(g_out, got[0], c_arr)]
    gw["ffn1_in"], got = _tn_w3(xb, du1, du_map, nf, _scatter_exchange(parts_d))
    reduce_end(["ffn1_out"], parts_d, got)
    parts_e = reduce_begin(["ffn1_in"])
    dx, got = _ffn_dx(du1, w["ffn1_in"], dres1, _scatter_exchange(parts_e))
    reduce_end(["ffn1_in"], parts_e, got)

    keys = list(reduced)
    grads = dict(zip(keys, _join_halves([reduced[k] for k in keys])))
    return loss, dx.reshape(b_, s_, d), grads, gp


MESH = pl.DeviceIdType.MESH
ANY = pl.BlockSpec(memory_space=pl.ANY)


def _place():
    x, y, c = lax.axis_index("x"), lax.axis_index("y"), lax.axis_index("c")
    chips = [(1 - x, y), (x, 1 - y), (1 - x, 1 - y)]
    return x, y, c, chips, (x, y, 1 - c)


def _half_rows(c, rows):
    hr = rows // 2
    return pl.ds(pl.multiple_of(c * hr, 16), hr)


def _remote(src, dst, send, recv, dev):
    return pltpu.make_async_remote_copy(src_ref=src, dst_ref=dst, send_sem=send, recv_sem=recv,
                                        device_id=dev, device_id_type=MESH)


def _gather_weights(fulls):
    n = len(fulls)

    def body(*refs):
        _gather_start(refs[n:2 * n], refs[2 * n:])
        _gather_finish(refs[n:2 * n], refs[2 * n:])

    return pl.pallas_call(
        body, name="gather_weights", in_specs=[ANY] * n, out_specs=[ANY] * n,
        out_shape=[_sds(a.shape, a.dtype) for a in fulls], input_output_aliases={i: i for i in range(n)},
        scratch_shapes=_gather_sems(n))(*fulls)


def _gather_sems(n):
    return [pltpu.SemaphoreType.DMA((n, 3)) for _ in range(4)]


def _span_half(c, span, rows):
    r0, cnt = (0, rows) if span is None else span
    return pl.ds(pl.multiple_of(r0 + c * (cnt // 2), 16), cnt // 2)


def _gather_start(bufs, sems, spans=None):
    isend, irecv = sems[0], sems[1]
    x, y, c, chips, sib = _place()
    for i, buf in enumerate(bufs):
        blk = buf.at[2 * x + y, _span_half(c, spans and spans[i], buf.shape[1])]
        for k, chip in enumerate(chips):
            _remote(blk, blk, isend.at[i, k], irecv.at[i, k], (*chip, c)).start()


def _gather_finish(bufs, sems, spans=None):
    isend, irecv, fsend, frecv = sems
    x, y, c, chips, sib = _place()
    for i, buf in enumerate(bufs):
        mine = _span_half(c, spans and spans[i], buf.shape[1])
        for k, chip in enumerate(chips):
            blk = buf.at[2 * chip[0] + chip[1], mine]
            _remote(blk, blk, isend.at[i, k], irecv.at[i, k], (*chip, c)).wait_recv()
            _remote(blk, blk, fsend.at[i, k], frecv.at[i, k], sib).start()
    for i, buf in enumerate(bufs):
        span = spans and spans[i]
        mine, other = _span_half(c, span, buf.shape[1]), _span_half(1 - c, span, buf.shape[1])
        own = buf.at[2 * x + y, mine]
        for k, chip in enumerate(chips):
            got = buf.at[2 * chip[0] + chip[1], other]
            _remote(got, got, fsend.at[i, k], frecv.at[i, k], sib).wait_recv()
            _remote(own, own, isend.at[i, k], irecv.at[i, k], (*chip, c)).wait_send()
            blk = buf.at[2 * chip[0] + chip[1], mine]
            _remote(blk, blk, fsend.at[i, k], frecv.at[i, k], sib).wait_send()


def _gather_exchange(fulls, spans=None):
    n = len(fulls)
    return _Exchange(fulls, [_sds(a.shape, a.dtype) for a in fulls], {i: i for i in range(n)}, _gather_sems(n),
                     lambda ins, outs, sems: _gather_start(outs, sems, spans),
                     lambda ins, outs, sems: _gather_finish(outs, sems, spans))


def _swap_halves(grads):
    side = _swap_exchange(grads)
    n = len(grads)

    def body(*refs):
        side.start(refs[:n], refs[n:2 * n], refs[2 * n:])
        side.finish(refs[:n], refs[n:2 * n], refs[2 * n:])

    return pl.pallas_call(
        body, name="swap_halves", in_specs=[ANY] * n, out_specs=[ANY] * n, out_shape=side.outs,
        scratch_shapes=side.sems)(*grads)


def _swap_exchange(grads):
    n = len(grads)

    def copies(ins, outs, sems):
        x, y, c, chips, sib = _place()
        return [_remote(a.at[:, _half_rows(1 - c, a.shape[1])], b, sems[0].at[i], sems[1].at[i], sib)
                for i, (a, b) in enumerate(zip(ins, outs))]

    def start(ins, outs, sems):
        for cp in copies(ins, outs, sems):
            cp.start()

    def finish(ins, outs, sems):
        for cp in copies(ins, outs, sems):
            cp.wait()

    return _Exchange(grads, [_sds((N_SHARD, a.shape[1] // 2, a.shape[2]), a.dtype) for a in grads], {},
                     [pltpu.SemaphoreType.DMA((n,)) for _ in range(2)], start, finish)


def _both(a, b):
    i, o, s = len(a.ins), len(a.outs), len(a.sems)

    def start(ins, outs, sems):
        a.start(ins[:i], outs[:o], sems[:s])
        b.start(ins[i:], outs[o:], sems[s:])

    def finish(ins, outs, sems):
        a.finish(ins[:i], outs[:o], sems[:s])
        b.finish(ins[i:], outs[o:], sems[s:])

    aliases = dict(a.aliases)
    aliases.update({i + k: o + v for k, v in b.aliases.items()})
    return _Exchange(a.ins + b.ins, a.outs + b.outs, aliases, a.sems + b.sems, start, finish)


def _scatter_exchange(parts, spans=None, into=None):
    n = len(parts)

    def copies(ins, outs, sems):
        x, y, c, chips, sib = _place()
        cps = []
        for i in range(n):
            rows = pl.ds(*spans[i]) if spans and spans[i] else slice(None)
            for k, chip in enumerate(chips):
                cps.append(_remote(ins[i].at[2 * chip[0] + chip[1], rows], outs[i].at[k, rows],
                                   sems[0].at[i, k], sems[1].at[i, k], (*chip, c)))
        return cps

    def start(ins, outs, sems):
        for cp in copies(ins, outs, sems):
            cp.start()

    def finish(ins, outs, sems):
        for cp in copies(ins, outs, sems):
            cp.wait()

    return _Exchange(list(parts) + list(into or []), [_sds((3,) + a.shape[1:], a.dtype) for a in parts],
                     {n + i: i for i in range(n)} if into else {},
                     [pltpu.SemaphoreType.DMA((n, 3)) for _ in range(2)], start, finish)


def _join_halves(grads):
    n = len(grads)

    def body(*refs):
        bufs, send, recv = refs[n:2 * n], refs[2 * n], refs[2 * n + 1]
        x, y, c, chips, sib = _place()
        cps = []
        for i in range(n):
            blk = bufs[i].at[_half_rows(c, bufs[i].shape[0])]
            other = bufs[i].at[_half_rows(1 - c, bufs[i].shape[0])]
            cp = _remote(blk, blk, send.at[i], recv.at[i], sib)
            cp.start()
            cps.append((cp, _remote(other, other, send.at[i], recv.at[i], sib)))
        for cp, got in cps:
            cp.wait_send()
            got.wait_recv()

    dma = pltpu.SemaphoreType.DMA
    return pl.pallas_call(
        body, name="join_halves", in_specs=[ANY] * n, out_specs=[ANY] * n,
        out_shape=[_sds(a.shape, a.dtype) for a in grads], input_output_aliases={i: i for i in range(n)},
        scratch_shapes=[dma((n,)), dma((n,))])(*grads)


def _gather_rows(block):
    m_per, n = block.shape

    def body(x_ref, out_ref, send_sems, recv_sems, local_sem):
        x, y, c, chips, sibling = _place()
        me = (x, y, c)

        def rows(px, py, pc):
            return out_ref.at[pl.ds((4 * px + 2 * py + pc) * m_per, m_per), :]

        def copy(k, blk, to, src=None):
            return _remote(rows(*blk) if src is None else src, rows(*blk), send_sems.at[k], recv_sems.at[k], to)

        mine = pltpu.make_async_copy(x_ref, rows(*me), local_sem)
        mine.start()
        first = [copy(0, me, sibling, src=x_ref)]
        first += [copy(1 + j, me, (*chip, c), src=x_ref) for j, chip in enumerate(chips)]
        for cp in first:
            cp.start()
        passed = [copy(4 + j, (*chip, c), sibling) for j, chip in enumerate(chips)]
        for j, chip in enumerate(chips):
            copy(1 + j, (*chip, c), me).wait_recv()
            passed[j].start()
        copy(0, sibling, me).wait_recv()
        for j, chip in enumerate(chips):
            copy(4 + j, (*chip, 1 - c), me).wait_recv()
        for cp in first + passed:
            cp.wait_send()
        mine.wait()

    vmem = pl.BlockSpec(memory_space=pltpu.VMEM)
    dma = pltpu.SemaphoreType.DMA
    return pl.pallas_call(
        body, name="gather_rows", in_specs=[vmem], out_specs=vmem, out_shape=_sds((8 * m_per, n), block.dtype),
        scratch_shapes=[dma((7,)), dma((7,)), dma(())])(block)


def _row_tile(rows, cols):
    return _tile(rows, max(16, (1 << 20) // cols), 16)


def _sum_halves(grad, got, c_arr):
    _, hr, cols = got.shape
    tr = _row_tile(hr, cols)
    nb = hr // tr

    def body(c_ref, a_ref, b_ref, o_ref):
        o_ref[...] = (a_ref[...].astype(F32) + b_ref[...].astype(F32)).astype(BF16)

    blk = pl.BlockSpec((None, tr, cols), lambda s, i, c_ref: (s, i, 0))
    return pl.pallas_call(
        body, name="sum_halves",
        grid_spec=pltpu.PrefetchScalarGridSpec(
            num_scalar_prefetch=1, grid=(N_SHARD, nb),
            in_specs=[pl.BlockSpec((None, tr, cols), lambda s, i, c_ref: (s, c_ref[0] * nb + i, 0)), blk],
            out_specs=blk),
        out_shape=_sds(got.shape, BF16), compiler_params=_params(2, 0))(c_arr, grad, got)


def _sum_chips(part, got, sc_arr):
    _, hr, cols = got.shape
    tr = _row_tile(hr, cols)
    nb = hr // tr

    def body(s_ref, a_ref, b_ref, o_ref):
        o_ref[...] = ((a_ref[...].astype(F32) + b_ref[0].astype(F32)) + b_ref[1].astype(F32)) + b_ref[2].astype(F32)

    return pl.pallas_call(
        body, name="sum_chips",
        grid_spec=pltpu.PrefetchScalarGridSpec(
            num_scalar_prefetch=1, grid=(nb,),
            in_specs=[pl.BlockSpec((None, tr, cols), lambda i, s_ref: (s_ref[0], i, 0)),
                      pl.BlockSpec((3, tr, cols), lambda i, s_ref: (0, i, 0))],
            out_specs=pl.BlockSpec((tr, cols), lambda i, s_ref: (s_ref[1] * nb + i, 0))),
        out_shape=_sds((2 * hr, cols), F32), compiler_params=_params(1, 0))(sc_arr, part, got)


def _cast_into_slot(x2d, sc_arr):
    rows, cols = x2d.shape
    tr = _row_tile(rows, cols)

    def body(s_ref, x_ref, o_ref):
        o_ref[...] = x_ref[...].astype(BF16)

    return pl.pallas_call(
        body, name="cast_into_slot",
        grid_spec=pltpu.PrefetchScalarGridSpec(
            num_scalar_prefetch=1, grid=(rows // tr,),
            in_specs=[pl.BlockSpec((tr, cols), lambda i, s_ref: (i, 0))],
            out_specs=pl.BlockSpec((None, tr, cols), lambda i, s_ref: (s_ref[0], i, 0))),
        out_shape=_sds((N_SHARD, rows, cols), BF16), compiler_params=_params(1, 0))(sc_arr, x2d)


def _adam_math(w, g, m, v):
    m = ADAM_B1 * m + (1.0 - ADAM_B1) * g
    v = ADAM_B2 * v + (1.0 - ADAM_B2) * (g * g)
    m_hat = m / (1.0 - ADAM_B1 ** ADAM_STEP)
    v_hat = v / (1.0 - ADAM_B2 ** ADAM_STEP)
    delta = -ADAM_LR * (m_hat / (jnp.sqrt(v_hat) + ADAM_EPS) + ADAM_WD * w)
    return delta, m, v


def _adamw(w, g, m, v):
    rows, cols = w.shape
    tr = _tile(rows, max(8, (1 << 19) // cols), 8)

    def body(w_ref, g_ref, m_ref, v_ref, go_ref, d_ref, mo_ref, vo_ref):
        g_ = g_ref[...]
        go_ref[...] = g_
        d_ref[...], mo_ref[...], vo_ref[...] = _adam_math(w_ref[...], g_, m_ref[...], v_ref[...])

    blk = pl.BlockSpec((tr, cols), lambda i: (i, 0))
    return pl.pallas_call(
        body, name="adamw", grid=(rows // tr,), in_specs=[blk] * 4, out_specs=[blk] * 4,
        out_shape=[_sds((rows, cols), F32)] * 4, compiler_params=_params(1, 0))(w, g, m, v)


def _adamw_small(gathered, w, m, v):
    rows, cols = w.shape

    def body(a_ref, w_ref, m_ref, v_ref, go_ref, d_ref, mo_ref, vo_ref):
        g_ = a_ref[pl.ds(0, rows), :]
        for k in range(1, 8):
            g_ = g_ + a_ref[pl.ds(k * rows, rows), :]
        go_ref[...] = g_
        d_ref[...], mo_ref[...], vo_ref[...] = _adam_math(w_ref[...], g_, m_ref[...], v_ref[...])

    vmem = pl.BlockSpec(memory_space=pltpu.VMEM)
    return pl.pallas_call(
        body, name="adamw_small", in_specs=[vmem] * 4, out_specs=[vmem] * 4,
        out_shape=[_sds((rows, cols), F32)] * 4)(gathered, w, m, v)


BIG = ("ffn1_w_in", "ffn1_w_out", "mix_w_in", "w_branch_a", "w_branch_b", "mix_w_out", "ffn2_w_in", "ffn2_w_out")
BIG_KEY = {"ffn1_w_in": "ffn1_in", "ffn1_w_out": "ffn1_out", "mix_w_in": "mix_in", "w_branch_a": "wa",
           "w_branch_b": "wb", "mix_w_out": "mix_out", "ffn2_w_in": "ffn2_in", "ffn2_w_out": "ffn2_out"}
SMALL = ("ln1_g", "ln1_b", "hgrn_lb_fwd", "hgrn_lb_bwd", "hgrn_norm_g", "ln2_g", "ln2_b", "ln3_g", "ln3_b")
ORDER = ("ffn1_w_in", "ffn1_w_out", "ln1_g", "ln1_b", "mix_w_in", "hgrn_lb_fwd", "hgrn_lb_bwd", "hgrn_norm_g",
         "w_branch_a", "w_branch_b", "mix_w_out", "ln2_g", "ln2_b", "ffn2_w_in", "ffn2_w_out", "ln3_g", "ln3_b")
SMALL_ROWS = 16


def _pack_small(d):
    rows = jnp.concatenate([d[k].reshape(-1, d[k].shape[-1]) for k in SMALL], axis=0)
    return jnp.pad(rows, ((0, SMALL_ROWS - rows.shape[0]), (0, 0)))


def _unpack_small(a, like):
    out, r = {}, 0
    for k in SMALL:
        n = like[k].shape[0]
        out[k] = a[r:r + n].reshape(like[k].shape)
        r += n
    return out


def kernel(x, ffn1_w_in, ffn1_w_out, ln1_g, ln1_b, mix_w_in, hgrn_lb_fwd, hgrn_lb_bwd, hgrn_norm_g, w_branch_a, w_branch_b, mix_w_out, ln2_g, ln2_b, ffn2_w_in, ffn2_w_out, ln3_g, ln3_b, loss_target, m_ffn1_w_in, m_ffn1_w_out, m_ln1_g, m_ln1_b, m_mix_w_in, m_hgrn_lb_fwd, m_hgrn_lb_bwd, m_hgrn_norm_g, m_w_branch_a, m_w_branch_b, m_mix_w_out, m_ln2_g, m_ln2_b, m_ffn2_w_in, m_ffn2_w_out, m_ln3_g, m_ln3_b, v_ffn1_w_in, v_ffn1_w_out, v_ln1_g, v_ln1_b, v_mix_w_in, v_hgrn_lb_fwd, v_hgrn_lb_bwd, v_hgrn_norm_g, v_w_branch_a, v_w_branch_b, v_mix_w_out, v_ln2_g, v_ln2_b, v_ffn2_w_in, v_ffn2_w_out, v_ln3_g, v_ln3_b):
    wts = dict(ffn1_w_in=ffn1_w_in, ffn1_w_out=ffn1_w_out, ln1_g=ln1_g, ln1_b=ln1_b, mix_w_in=mix_w_in,
               hgrn_lb_fwd=hgrn_lb_fwd, hgrn_lb_bwd=hgrn_lb_bwd, hgrn_norm_g=hgrn_norm_g, w_branch_a=w_branch_a,
               w_branch_b=w_branch_b, mix_w_out=mix_w_out, ln2_g=ln2_g, ln2_b=ln2_b, ffn2_w_in=ffn2_w_in,
               ffn2_w_out=ffn2_w_out, ln3_g=ln3_g, ln3_b=ln3_b)
    mom = dict(ffn1_w_in=m_ffn1_w_in, ffn1_w_out=m_ffn1_w_out, ln1_g=m_ln1_g, ln1_b=m_ln1_b, mix_w_in=m_mix_w_in,
               hgrn_lb_fwd=m_hgrn_lb_fwd, hgrn_lb_bwd=m_hgrn_lb_bwd, hgrn_norm_g=m_hgrn_norm_g,
               w_branch_a=m_w_branch_a, w_branch_b=m_w_branch_b, mix_w_out=m_mix_w_out, ln2_g=m_ln2_g, ln2_b=m_ln2_b,
               ffn2_w_in=m_ffn2_w_in, ffn2_w_out=m_ffn2_w_out, ln3_g=m_ln3_g, ln3_b=m_ln3_b)
    var = dict(ffn1_w_in=v_ffn1_w_in, ffn1_w_out=v_ffn1_w_out, ln1_g=v_ln1_g, ln1_b=v_ln1_b, mix_w_in=v_mix_w_in,
               hgrn_lb_fwd=v_hgrn_lb_fwd, hgrn_lb_bwd=v_hgrn_lb_bwd, hgrn_norm_g=v_hgrn_norm_g,
               w_branch_a=v_w_branch_a, w_branch_b=v_w_branch_b, mix_w_out=v_mix_w_out, ln2_g=v_ln2_g, ln2_b=v_ln2_b,
               ffn2_w_in=v_ffn2_w_in, ffn2_w_out=v_ffn2_w_out, ln3_g=v_ln3_g, ln3_b=v_ln3_b)
    shard = (2 * lax.axis_index("x") + lax.axis_index("y")).astype(jnp.int32)
    sc_arr = jnp.stack([shard, lax.axis_index("c").astype(jnp.int32)])

    shard2d = {k: wts[k].reshape(wts[k].shape[1:]) for k in BIG}
    fulls = {BIG_KEY[k]: _cast_into_slot(shard2d[k], sc_arr) for k in BIG}
    p = {k: wts[k] for k in SMALL}

    loss, grad_x, grads, gp = _local_step(x, loss_target, fulls, p, sc_arr)
    loss = lax.psum(loss[0, 0], ("x", "y", "c"))

    out_g, out_d, out_m, out_v = {}, {}, {}, {}
    for k in BIG:
        g = grads[BIG_KEY[k]]
        shp = wts[k].shape
        res = _adamw(shard2d[k], g, mom[k].reshape(shp[1:]), var[k].reshape(shp[1:]))
        out_g[k], out_d[k], out_m[k], out_v[k] = [a.reshape(shp) for a in res]

    gathered = _gather_rows(_pack_small(gp))
    res = _adamw_small(gathered, _pack_small(wts), _pack_small(mom), _pack_small(var))
    for dst, a in zip((out_g, out_d, out_m, out_v), res):
        dst.update(_unpack_small(a, wts))

    return (loss, grad_x, *[out_g[k] for k in ORDER], *[out_d[k] for k in ORDER],
            *[out_m[k] for k in ORDER], *[out_v[k] for k in ORDER])
```

```python
import functools

import numpy as np
import jax
import jax.numpy as jnp
from jax import lax
from jax.experimental import pallas as pl
from jax.experimental.pallas import tpu as pltpu

F32 = jnp.float32
BF16 = jnp.bfloat16

HEAD = 128
ATTN_GROUPS = ((128, 1), (512, 4), (2048, 16))
ATTN_HEADS = 4
N_GROUPS = len(ATTN_GROUPS)
QKV_W = N_GROUPS * 3 * ATTN_HEADS * HEAD
ATTN_OUT = ATTN_HEADS * HEAD
ROPE_THETA = 500000.0
ROPE_DIM = HEAD // 4
HGRN_CHUNK = 32
HGRN_FWD_HEADS = 2
HGRN_BWD_HEADS = 2
HGRN_SUB = 4
ALPHA = 2.0 ** 0.25
LN_EPS = 1e-5
NEG_INF = -1e30
ADAM_LR, ADAM_B1, ADAM_B2, ADAM_EPS, ADAM_WD, ADAM_STEP = 0.001, 0.9, 0.999, 1e-08, 0.01, 10

N_SHARD = 4
VMEM_LIMIT = 56 * 1024 * 1024

NN = ((1,), (0,))
NT = ((1,), (1,))
TN = ((0,), (0,))


def _dot(a, b, dims, precision=None):
    return lax.dot_general(a, b, (dims, ((), ())), preferred_element_type=F32, precision=precision)


def _tile(n, pref, mult=128):
    best = None
    for t in range(mult, min(n, pref) + 1, mult):
        if n % t == 0:
            best = t
    return n if best is None else best


def _params(n_parallel, n_arbitrary):
    return pltpu.CompilerParams(
        dimension_semantics=("parallel",) * n_parallel + ("arbitrary",) * n_arbitrary,
        vmem_limit_bytes=VMEM_LIMIT)


def _sigmoid(x):
    return 1.0 / (1.0 + jnp.exp(-x))


def _silu(x):
    return x * _sigmoid(x)


class _Exchange:
    def __init__(self, ins, outs, aliases, sems, start, finish):
        self.ins, self.outs, self.aliases, self.sems = list(ins), list(outs), dict(aliases), list(sems)
        self.start, self.finish = start, finish


def _run(body, name, grid, ins, in_specs, outs, out_specs, scratch, n_arbitrary, side=None):
    n_in, n_out, n_scr = len(ins), len(outs), len(scratch)
    if side is None:
        return pl.pallas_call(
            body, name=name, grid=grid, in_specs=in_specs, out_specs=out_specs, out_shape=outs,
            scratch_shapes=scratch, compiler_params=_params(len(grid) - n_arbitrary, n_arbitrary))(*ins)
    s_in, s_out = len(side.ins), len(side.outs)
    i1 = n_in + s_in
    o1 = i1 + n_out
    o2 = o1 + s_out
    c1 = o2 + n_scr

    def wrapped(*refs):
        s_refs = (refs[n_in:i1], refs[o1:o2], refs[c1:])
        ids = [pl.program_id(a) for a in range(len(grid))]
        first = functools.reduce(jnp.logical_and, [i == 0 for i in ids])
        last = functools.reduce(jnp.logical_and, [i == g - 1 for i, g in zip(ids, grid)])

        @pl.when(first)
        def _():
            side.start(*s_refs)

        body(*refs[:n_in], *refs[i1:o1], *refs[o2:c1])

        @pl.when(last)
        def _():
            side.finish(*s_refs)

    res = pl.pallas_call(
        wrapped, name=name, grid=grid, in_specs=list(in_specs) + [ANY] * s_in,
        out_specs=list(out_specs) + [ANY] * s_out, out_shape=list(outs) + side.outs,
        scratch_shapes=list(scratch) + side.sems,
        input_output_aliases={n_in + a: n_out + b for a, b in side.aliases.items()},
        compiler_params=_params(0, len(grid)))(*ins, *side.ins)
    return res[:n_out], res[n_out:]


def _gemm(name, grid, ins, in_specs, outs, out_specs, accs, dot_fn, epi_fn, side=None):
    n_in, n_out, n_k = len(ins), len(outs), grid[-1]

    def body(*refs):
        in_refs, out_refs, acc_refs = refs[:n_in], refs[n_in:n_in + n_out], refs[n_in + n_out:]
        k = pl.program_id(len(grid) - 1)

        @pl.when(k == 0)
        def _():
            for a in acc_refs:
                a[...] = jnp.zeros(a.shape, F32)

        dot_fn(in_refs, acc_refs)

        @pl.when(k == n_k - 1)
        def _():
            epi_fn(in_refs, acc_refs, out_refs)

    return _run(body, name, grid, ins, in_specs, outs, out_specs, [pltpu.VMEM(s, F32) for s in accs], 1, side)


def _sds(shape, dtype):
    return jax.ShapeDtypeStruct(shape, dtype)


def _gemm_rows(name, grid, ins, in_specs, outs, out_specs, tm, n_sub, fn, side=None):
    n_in, sub = len(ins), tm // n_sub

    def body(*refs):
        for s in range(n_sub):
            fn(pl.ds(s * sub, sub), refs[:n_in], refs[n_in:])

    return _run(body, name, grid, ins, in_specs, outs, out_specs, [], 0, side)


def _ffn_up(xb, w3, side=None):
    t, d = xb.shape
    nf = w3.shape[2]
    f = 2 * nf
    tm = _tile(t, 512, 8)
    tn = nf // 2 if nf % 256 == 0 else nf
    ns = nf // tn

    def fn(rows, r, out):
        x = r[0][rows, :]
        g, u = _dot(x, r[1][...], NN), _dot(x, r[2][...], NN)
        out[0][0, rows, :] = g.astype(BF16)
        out[0][1, rows, :] = u.astype(BF16)
        out[1][rows, :] = (_silu(g) * u).astype(BF16)

    return _gemm_rows(
        "ffn_up", (t // tm, 2, ns), [xb, w3, w3],
        [pl.BlockSpec((tm, d), lambda i, j, h: (i, 0)),
         pl.BlockSpec((None, d, tn), lambda i, j, h: (j, 0, h)),
         pl.BlockSpec((None, d, tn), lambda i, j, h: (j + 2, 0, h))],
        [_sds((2, t, f), BF16), _sds((t, f), BF16)],
        [pl.BlockSpec((2, tm, tn), lambda i, j, h: (0, i, j * ns + h)),
         pl.BlockSpec((tm, tn), lambda i, j, h: (i, j * ns + h))],
        tm, 2, fn, side)


def _down_ln(a, w, resid, coef, g, b, side=None):
    t, kd = a.shape
    d = w.shape[1]
    tm = _tile(t, 256, 8)

    def fn(rows, r, out):
        v = ALPHA * r[2][rows, :] + coef * _dot(r[0][rows, :], r[1][...], NN)
        mu = jnp.mean(v, axis=-1, keepdims=True)
        c = v - mu
        var = jnp.mean(c * c, axis=-1, keepdims=True)
        rstd = lax.rsqrt(var + LN_EPS)
        xhat = c * rstd
        h = xhat * r[3][...] + r[4][...]
        out[0][rows, :] = h
        out[1][rows, :] = h.astype(BF16)
        out[2][rows, :] = xhat
        out[3][rows, :] = rstd

    row = pl.BlockSpec((tm, d), lambda i: (i, 0))
    vec = pl.BlockSpec((1, d), lambda i: (0, 0))
    return _gemm_rows(
        "down_ln", (t // tm,), [a, w, resid, g, b],
        [pl.BlockSpec((tm, kd), lambda i: (i, 0)),
         pl.BlockSpec((kd, d), lambda i: (0, 0), pipeline_mode=pl.Buffered(1)), row, vec, vec],
        [_sds((t, d), F32), _sds((t, d), BF16), _sds((t, d), F32), _sds((t, 1), F32)],
        [row, row, row, pl.BlockSpec((tm, 1), lambda i: (i, 0))], tm, 2, fn, side)


def _down_ln_loss(a, w, resid, coef, g, b, target):
    t, kd = a.shape
    d = w.shape[1]
    tm = _tile(t, 256, 8)
    n_sub = 2
    sub = tm // n_sub

    def body(a_ref, w_ref, r_ref, g_ref, b_ref, t_ref, dres_ref, dyb_ref, dg_ref, db_ref, loss_ref):
        @pl.when(pl.program_id(0) == 0)
        def _():
            dg_ref[...] = jnp.zeros(dg_ref.shape, F32)
            db_ref[...] = jnp.zeros(db_ref.shape, F32)
            loss_ref[...] = jnp.zeros(loss_ref.shape, F32)

        for s in range(n_sub):
            rows = pl.ds(s * sub, sub)
            v = ALPHA * r_ref[rows, :] + coef * _dot(a_ref[rows, :], w_ref[...], NN)
            mu = jnp.mean(v, axis=-1, keepdims=True)
            c = v - mu
            rstd = lax.rsqrt(jnp.mean(c * c, axis=-1, keepdims=True) + LN_EPS)
            xh = c * rstd
            e = xh * g_ref[...] + b_ref[...] - t_ref[rows, :]
            dh = e * (1.0 / d)
            dxh = dh * g_ref[...]
            m1 = jnp.mean(dxh, axis=-1, keepdims=True)
            m2 = jnp.mean(dxh * xh, axis=-1, keepdims=True)
            dv = rstd * (dxh - m1 - xh * m2)
            dres_ref[rows, :] = ALPHA * dv
            dyb_ref[rows, :] = (coef * dv).astype(BF16)
            dg_ref[...] += jnp.sum(dh * xh, axis=0, keepdims=True)
            db_ref[...] += jnp.sum(dh, axis=0, keepdims=True)
            part = 0.5 * jnp.sum(jnp.sum(e * e, axis=-1, keepdims=True) * (1.0 / d), axis=0, keepdims=True)
            loss_ref[...] += jnp.broadcast_to(part, loss_ref.shape)

    row = pl.BlockSpec((tm, d), lambda i: (i, 0))
    vec = pl.BlockSpec((1, d), lambda i: (0, 0))
    return _run(
        body, "down_ln_loss", (t // tm,), [a, w, resid, g, b, target],
        [pl.BlockSpec((tm, kd), lambda i: (i, 0)),
         pl.BlockSpec((kd, d), lambda i: (0, 0), pipeline_mode=pl.Buffered(1)), row, vec, vec, row],
        [_sds((t, d), F32), _sds((t, d), BF16), _sds((1, d), F32), _sds((1, d), F32), _sds((1, 128), F32)],
        [row, row, vec, vec, pl.BlockSpec((1, 128), lambda i: (0, 0))], [], 1)


def _only(res, side):
    return res[0] if side is None else (res[0][0], res[1])


def _mm_w3(a, w3, side=None):
    t, kd = a.shape
    n = w3.shape[2]
    tm = _tile(t, 512, 8)

    def fn(rows, r, out):
        out[0][rows, :] = _dot(r[0][rows, :], r[1][...], NN)

    return _only(_gemm_rows(
        "mm_w3", (N_SHARD, t // tm), [a, w3],
        [pl.BlockSpec((tm, kd), lambda j, i: (i, 0)),
         pl.BlockSpec((None, kd, n), lambda j, i: (j, 0, 0), pipeline_mode=pl.Buffered(1))],
        [_sds((t, N_SHARD * n), F32)], [pl.BlockSpec((tm, n), lambda j, i: (i, j))], tm, 1, fn, side), side)


def _mm_w2(a, w, out_dtype):
    t, kd = a.shape
    n = w.shape[1]
    tm, tn = _tile(t, 512, 8), _tile(n, 1024)

    def fn(rows, r, out):
        out[0][rows, :] = _dot(r[0][rows, :], r[1][...], NN).astype(out_dtype)

    return _gemm_rows(
        "mm_w2", (t // tm, n // tn), [a, w],
        [pl.BlockSpec((tm, kd), lambda i, j: (i, 0)), pl.BlockSpec((kd, tn), lambda i, j: (0, j))],
        [_sds((t, n), out_dtype)], [pl.BlockSpec((tm, tn), lambda i, j: (i, j))], tm, 1, fn)[0]


def _branch_gate(ob, wb3, ya, proj, gate_col):
    t, kd = ob.shape
    n = wb3.shape[2]
    d = N_SHARD * n
    tm = _tile(t, 512, 8)
    ga0, gb0 = gate_col // n, (gate_col + d) // n

    def fn(rows, r, out):
        yb = _dot(r[0][rows, :], r[1][...], NN)
        out[0][rows, :] = yb
        out[1][rows, :] = (_sigmoid(r[3][rows, :]) * r[2][rows, :] + _sigmoid(r[4][rows, :]) * yb).astype(BF16)

    blk = pl.BlockSpec((tm, n), lambda i, j: (i, j))
    return _gemm_rows(
        "branch_gate", (t // tm, N_SHARD), [ob, wb3, ya, proj, proj],
        [pl.BlockSpec((tm, kd), lambda i, j: (i, 0)), pl.BlockSpec((None, kd, n), lambda i, j: (j, 0, 0)), blk,
         pl.BlockSpec((tm, n), lambda i, j: (i, ga0 + j)), pl.BlockSpec((tm, n), lambda i, j: (i, gb0 + j))],
        [_sds((t, d), F32), _sds((t, d), BF16)], [blk, blk], tm, 2, fn)


def _swiglu_bwd(dyb, w, u3, side=None):
    t, d = dyb.shape
    f = w.shape[0]
    tm, tr = _tile(t, 1024, 8), _tile(f, 512)

    def fn(rows, r, out):
        da = _dot(r[0][rows, :], r[1][...], NT)
        g, u = r[2][0, rows, :].astype(F32), r[2][1, rows, :].astype(F32)
        s = _sigmoid(g)
        out[0][0, rows, :] = (da * u * s * (1.0 + g * (1.0 - s))).astype(BF16)
        out[0][1, rows, :] = (da * g * s).astype(BF16)

    ublk = pl.BlockSpec((2, tm, tr), lambda i, j: (0, i, j))
    return _only(_gemm_rows(
        "swiglu_bwd", (t // tm, f // tr), [dyb, w, u3],
        [pl.BlockSpec((tm, d), lambda i, j: (i, 0)), pl.BlockSpec((tr, d), lambda i, j: (j, 0)), ublk],
        [_sds((2, t, f), BF16)], [ublk], tm, 4, fn, side), side)


def _ffn_dx(du3, w3, resid, side=None):
    t = du3.shape[1]
    d, nf = w3.shape[1], w3.shape[2]
    tm, tr = _tile(t, 512, 8), _tile(d, 1024)

    def dot_fn(r, acc):
        acc[0][...] += _dot(r[0][...], r[1][...], NT)

    def epi_fn(r, acc, out):
        out[0][...] = acc[0][...] + r[2][...]

    blk = pl.BlockSpec((tm, tr), lambda i, j, k: (i, j))
    return _only(_gemm(
        "ffn_dx", (t // tm, d // tr, N_SHARD), [du3, w3, resid],
        [pl.BlockSpec((None, tm, nf), lambda i, j, k: (k // 2, i, k % 2)),
         pl.BlockSpec((None, tr, nf), lambda i, j, k: (k, j, 0)), blk],
        [_sds((t, d), F32)], [blk], [(tm, tr)], dot_fn, epi_fn, side), side)


def _nt_w3(dy, w3, resid, side=None):
    t = dy.shape[0]
    kd, n = w3.shape[1], w3.shape[2]
    tm, tr = _tile(t, 512, 8), _tile(kd, 1024)
    has_res = resid is not None

    def dot_fn(r, acc):
        acc[0][...] += _dot(r[0][...], r[1][...], NT)

    def epi_fn(r, acc, out):
        v = acc[0][...]
        if has_res:
            v = v + r[2][...]
        out[0][...] = v

    blk = pl.BlockSpec((tm, tr), lambda i, j, k: (i, j))
    return _only(_gemm(
        "nt_w3", (t // tm, kd // tr, N_SHARD), [dy, w3] + ([resid] if has_res else []),
        [pl.BlockSpec((tm, n), lambda i, j, k: (i, k)), pl.BlockSpec((None, tr, n), lambda i, j, k: (k, j, 0))]
        + ([blk] if has_res else []),
        [_sds((t, kd), F32)], [blk], [(tm, tr)], dot_fn, epi_fn, side), side)


def _nt_w2(dy, w, out_dtype):
    t, n = dy.shape
    r_ = w.shape[0]
    tm, tr = _tile(t, 512, 8), _tile(r_, 1024)

    def fn(rows, r, out):
        out[0][rows, :] = _dot(r[0][rows, :], r[1][...], NT).astype(out_dtype)

    return _gemm_rows(
        "nt_w2", (t // tm, r_ // tr), [dy, w],
        [pl.BlockSpec((tm, n), lambda i, j: (i, 0)), pl.BlockSpec((tr, n), lambda i, j: (j, 0))],
        [_sds((t, r_), out_dtype)], [pl.BlockSpec((tm, tr), lambda i, j: (i, j))], tm, 1, fn)[0]


def _gate_bwd(dmix, w, proj, ya, yb, gate_col):
    t, n = dmix.shape
    d = w.shape[0]
    tm, tr = _tile(t, 512, 8), _tile(d, 512)
    ga0, gb0 = gate_col // tr, (gate_col + d) // tr
    nb = d // tr

    def fn(rows, r, out):
        dz = _dot(r[0][rows, :], r[1][...], NT)
        ga, gb = _sigmoid(r[2][rows, :]), _sigmoid(r[3][rows, :])
        out[0][rows, :] = (dz * ga).astype(BF16)
        out[1][rows, :] = (dz * gb).astype(BF16)
        out[2][rows, :] = (dz * r[4][rows, :] * ga * (1.0 - ga)).astype(BF16)
        out[3][rows, :] = (dz * r[5][rows, :] * gb * (1.0 - gb)).astype(BF16)

    blk = pl.BlockSpec((tm, tr), lambda i, j: (i, j))
    return _gemm_rows(
        "gate_bwd", (t // tm, nb), [dmix, w, proj, proj, ya, yb],
        [pl.BlockSpec((tm, n), lambda i, j: (i, 0)), pl.BlockSpec((tr, n), lambda i, j: (j, 0)),
         pl.BlockSpec((tm, tr), lambda i, j: (i, ga0 + j)), pl.BlockSpec((tm, tr), lambda i, j: (i, gb0 + j)),
         blk, blk],
        [_sds((t, d), BF16), _sds((t, d), BF16), _sds((t, d), BF16), _sds((t, d), BF16)],
        [blk, blk, blk, blk], tm, 2, fn)


def _tn_w3(a, dy, dy_map, n, side=None):
    t, kd = a.shape
    tm, tkk = _tile(t, (6 << 20) // n, 8), _tile(kd, 512)

    def dot_fn(r, acc):
        acc[0][...] += _dot(r[0][...], r[1][...], TN)

    def epi_fn(r, acc, out):
        out[0][...] = acc[0][...].astype(BF16)

    dy_block = (tm, n) if dy.ndim == 2 else (None, tm, n)
    return _only(_gemm(
        "tn_w3", (kd // tkk, N_SHARD, t // tm), [a, dy],
        [pl.BlockSpec((tm, tkk), lambda i, j, m: (m, i)), pl.BlockSpec(dy_block, lambda i, j, m: dy_map(j, m))],
        [_sds((N_SHARD, kd, n), BF16)], [pl.BlockSpec((None, tkk, n), lambda i, j, m: (j, i, 0))],
        [(tkk, n)], dot_fn, epi_fn, side), side)


def _tn_w2(a, dy):
    t, kd = a.shape
    n = dy.shape[1]
    tkk, tn = _tile(kd, 512), _tile(n, 2048)
    tm = _tile(t, (4 << 20) // tn, 8)

    def dot_fn(r, acc):
        acc[0][...] += _dot(r[0][...], r[1][...], TN)

    def epi_fn(r, acc, out):
        out[0][...] = acc[0][...].astype(BF16)

    return _gemm(
        "tn_w2", (kd // tkk, n // tn, t // tm), [a, dy],
        [pl.BlockSpec((tm, tkk), lambda i, j, m: (m, i)), pl.BlockSpec((tm, tn), lambda i, j, m: (m, j))],
        [_sds((kd, n), BF16)], [pl.BlockSpec((tkk, tn), lambda i, j, m: (i, j))],
        [(tkk, tn)], dot_fn, epi_fn)[0]


def _ln_bwd(dh, xhat, rstd, g, coef):
    t, d = xhat.shape
    tm = _tile(t, 256, 8)

    def body(dh_ref, xh_ref, rstd_ref, g_ref, dres_ref, dyb_ref, dg_ref, db_ref):
        dh, xh = dh_ref[...], xh_ref[...]
        dxh = dh * g_ref[...]
        m1 = jnp.mean(dxh, axis=-1, keepdims=True)
        m2 = jnp.mean(dxh * xh, axis=-1, keepdims=True)
        dv = rstd_ref[...] * (dxh - m1 - xh * m2)
        dres_ref[...] = ALPHA * dv
        dyb_ref[...] = (coef * dv).astype(BF16)

        @pl.when(pl.program_id(0) == 0)
        def _():
            dg_ref[...] = jnp.zeros(dg_ref.shape, F32)
            db_ref[...] = jnp.zeros(db_ref.shape, F32)

        dg_ref[...] += jnp.sum(dh * xh, axis=0, keepdims=True)
        db_ref[...] += jnp.sum(dh, axis=0, keepdims=True)

    row = pl.BlockSpec((tm, d), lambda i: (i, 0))
    vec = pl.BlockSpec((1, d), lambda i: (0, 0))
    return pl.pallas_call(
        body, name="ln_bwd", grid=(t // tm,),
        in_specs=[row, row, pl.BlockSpec((tm, 1), lambda i: (i, 0)), vec], out_specs=[row, row, vec, vec],
        out_shape=[_sds((t, d), F32), _sds((t, d), BF16), _sds((1, d), F32), _sds((1, d), F32)],
        compiler_params=_params(0, 1))(dh, xhat, rstd, g)


def _cast_bf16(x2d):
    t, d = x2d.shape
    tm = _tile(t, 512, 8)

    def body(x_ref, o_ref):
        o_ref[...] = x_ref[...].astype(BF16)

    row = pl.BlockSpec((tm, d), lambda i: (i, 0))
    return pl.pallas_call(body, name="cast_bf16", grid=(t // tm,), in_specs=[row], out_specs=row,
                          out_shape=_sds((t, d), BF16), compiler_params=_params(1, 0))(x2d)


def _lower_bound(table):
    t0, t1 = table[0:1, :], table[1:2, :]
    m = jnp.maximum(t0, t1)
    e0, e1 = jnp.exp(t0 - m), jnp.exp(t1 - m)
    return e0 / (e0 + e1)


def _chunk_tri(rows, upper):
    r = lax.broadcasted_iota(jnp.int32, (rows, rows), 0)
    s = lax.broadcasted_iota(jnp.int32, (rows, rows), 1)
    shift = HGRN_CHUNK.bit_length() - 1
    same = lax.shift_right_logical(r, shift) == lax.shift_right_logical(s, shift)
    return same & ((r <= s) if upper else (r >= s))


def _tri_apply(x, upper):
    tri = _chunk_tri(x.shape[0], upper).astype(F32).astype(BF16)
    hi = x.astype(BF16)
    r1 = x - hi.astype(F32)
    mid = r1.astype(BF16)
    lo = (r1 - mid.astype(F32)).astype(BF16)
    return _dot(tri, hi, NN) + _dot(tri, mid, NN) + _dot(tri, lo, NN)


@functools.partial(jax.custom_vjp, nondiff_argnums=(1,))
def _chunk_cumsum(x, upper):
    return _tri_apply(x, upper)


def _chunk_cumsum_fwd(x, upper):
    return _tri_apply(x, upper), None


def _chunk_cumsum_bwd(upper, _, g):
    return (_tri_apply(g, not upper),)


_chunk_cumsum.defvjp(_chunk_cumsum_fwd, _chunk_cumsum_bwd)


def _hgrn_blocks(units, uppers):
    c = HGRN_CHUNK
    rows = units[0][0].shape[0]
    n_sub = rows // c
    ids = range(len(units))
    chunk = lax.shift_right_logical(lax.broadcasted_iota(jnp.int32, (rows, HEAD), 0), c.bit_length() - 1)
    zero = jnp.zeros((rows, HEAD), BF16)

    def expand(a):
        return jnp.concatenate([jnp.where(chunk == n, a, zero) for n in range(n_sub)], axis=1)

    fs = [u[3] + (1.0 - u[3]) * _sigmoid(u[1]) for u in units]
    lgs = [jnp.log(f) for f in fs]
    cums = [_chunk_cumsum(lgs[i], uppers[i]) for i in ids]
    tots = [[jnp.sum(lg[n * c:(n + 1) * c], axis=0, keepdims=True) for n in range(n_sub)] for lg in lgs]
    qd, kd, ke, vb = [], [], [], []
    for i in ids:
        totb = jnp.concatenate([jnp.broadcast_to(t, (c, HEAD)) for t in tots[i]], axis=0)
        kk = 1.0 - fs[i]
        qd.append((_silu(units[i][0]) * jnp.exp(cums[i])).astype(BF16))
        kd.append((kk * jnp.exp(-cums[i])).astype(BF16))
        ke.append((kk * jnp.exp(totb - cums[i])).astype(BF16))
        vb.append(units[i][2].astype(BF16))
    scores = [_dot(qd[i], kd[i], NT) for i in ids]
    kvs = [_dot(vb[i], expand(ke[i]), TN) for i in ids]
    outs = []
    for i in ids:
        a = jnp.where(_chunk_tri(rows, uppers[i]), scores[i], 0.0).astype(BF16)
        st = units[i][4]
        entering = [None] * n_sub
        for n in (range(n_sub - 1, -1, -1) if uppers[i] else range(n_sub)):
            entering[n] = st.astype(BF16)
            st = st * jnp.exp(tots[i][n]) + kvs[i][:, n * HEAD:(n + 1) * HEAD]
        outs.append((a, jnp.concatenate(entering, axis=1), st))
    res = []
    for i in ids:
        a, entering, st = outs[i]
        res.append((_dot(a, vb[i], NN) + _dot(expand(qd[i]), entering, NT), st))
    return res


def _hgrn_fwd(proj3, lb_f, lb_b, norm_g, d, side=None):
    b_, s_, _ = proj3.shape
    nh = d // HEAD
    hb = HGRN_FWD_HEADS
    wid = hb * HEAD
    rows = HGRN_CHUNK * HGRN_SUB
    n_blk = s_ // rows

    def body(hq_ref, hff_ref, hfb_ref, hi_ref, hog_ref, lbf_ref, lbb_ref, g_ref, oraw_ref, out_ref, of_ref, ob_ref):
        lbf, lbb = _lower_bound(lbf_ref[...]), _lower_bound(lbb_ref[...])

        def step(j, sts):
            rf = pl.ds(pl.multiple_of(j * rows, rows), rows)
            rb = pl.ds(pl.multiple_of((n_blk - 1 - j) * rows, rows), rows)
            units = []
            for hh in range(hb):
                cs = slice(hh * HEAD, (hh + 1) * HEAD)
                units.append((hq_ref[rf, cs], hff_ref[rf, cs], hi_ref[rf, cs], lbf[:, cs], sts[2 * hh]))
                units.append((hq_ref[rb, cs], hfb_ref[rb, cs], hi_ref[rb, cs], lbb[:, cs], sts[2 * hh + 1]))
            res = _hgrn_blocks(units, [False, True] * hb)
            for hh in range(hb):
                cs = slice(hh * HEAD, (hh + 1) * HEAD)
                of_ref[rf, cs] = res[2 * hh][0]
                ob_ref[rb, cs] = res[2 * hh + 1][0]
            return tuple(r[1] for r in res)

        z = jnp.zeros((HEAD, HEAD), F32)
        lax.fori_loop(0, n_blk, step, (z,) * (2 * hb))
        for hh in range(hb):
            cs = slice(hh * HEAD, (hh + 1) * HEAD)
            o = of_ref[:, cs] + ob_ref[:, cs]
            oraw_ref[:, cs] = o
            on = o * lax.rsqrt(jnp.mean(o * o, axis=-1, keepdims=True) + LN_EPS)
            out_ref[:, cs] = (on * g_ref[:, cs] * _silu(hog_ref[:, cs])).astype(BF16)

    def col(k):
        return pl.BlockSpec((None, s_, wid), lambda h, b: (b, 0, k * (nh // hb) + h))

    tab = pl.BlockSpec((2, wid), lambda h, b: (0, h))
    oblk = pl.BlockSpec((None, s_, wid), lambda h, b: (b, 0, h))
    return _run(
        body, "hgrn_fwd", (nh // hb, b_), [proj3, proj3, proj3, proj3, proj3, lb_f, lb_b, norm_g],
        [col(0), col(1), col(2), col(3), col(4), tab, tab, pl.BlockSpec((1, wid), lambda h, b: (0, h))],
        [_sds((b_, s_, d), F32), _sds((b_, s_, d), BF16)], [oblk, oblk],
        [pltpu.VMEM((s_, wid), F32), pltpu.VMEM((s_, wid), F32)], 0, side)


def _hgrn_bwd(proj3, lb_f, lb_b, norm_g, o_raw, do_a, d, side=None):
    b_, s_, _ = proj3.shape
    nh = d // HEAD
    hb = HGRN_BWD_HEADS
    wid = hb * HEAD
    rows = HGRN_CHUNK * HGRN_SUB
    n_blk = s_ // rows

    def body(hq_ref, hff_ref, hfb_ref, hi_ref, hog_ref, lbf_ref, lbb_ref, g_ref, oraw_ref, doa_ref,
             dq_ref, dff_ref, dfb_ref, di_ref, dog_ref, dlbf_ref, dlbb_ref, dg_ref,
             st_ref, dor_ref, dq2_ref, di2_ref):
        b = pl.program_id(1)
        tab_f, tab_b = lbf_ref[...], lbb_ref[...]
        lbf, lbb = _lower_bound(tab_f), _lower_bound(tab_b)

        dg_parts = []
        for hh in range(hb):
            cs = slice(hh * HEAD, (hh + 1) * HEAD)
            o, doa, hog, g = oraw_ref[:, cs], doa_ref[:, cs], hog_ref[:, cs], g_ref[:, cs]
            rs = lax.rsqrt(jnp.mean(o * o, axis=-1, keepdims=True) + LN_EPS)
            on = o * rs
            sg = _sigmoid(hog)
            gate = hog * sg
            dog_ref[:, cs] = (doa * on * g * sg * (1.0 + hog * (1.0 - sg))).astype(BF16)
            don = doa * g * gate
            dor_ref[:, cs] = rs * (don - on * jnp.mean(don * on, axis=-1, keepdims=True))
            dg_parts.append(jnp.sum(doa * on * gate, axis=0, keepdims=True))

        def fwd_step(j, carry):
            jb = n_blk - 1 - j
            rf = pl.ds(pl.multiple_of(j * rows, rows), rows)
            rb = pl.ds(pl.multiple_of(jb * rows, rows), rows)
            units = []
            for hh in range(hb):
                cs = slice(hh * HEAD, (hh + 1) * HEAD)
                st_ref[2 * hh, j] = carry[2 * hh]
                st_ref[2 * hh + 1, jb] = carry[2 * hh + 1]
                units.append((hq_ref[rf, cs], hff_ref[rf, cs], hi_ref[rf, cs], lbf[:, cs], carry[2 * hh]))
                units.append((hq_ref[rb, cs], hfb_ref[rb, cs], hi_ref[rb, cs], lbb[:, cs], carry[2 * hh + 1]))
            return tuple(r[1] for r in _hgrn_blocks(units, uppers))

        uppers = [False, True] * hb
        z = jnp.zeros((HEAD, HEAD), F32)
        lax.fori_loop(0, n_blk, fwd_step, (z,) * (2 * hb))

        def bwd_step(j, carry):
            gs, dls = carry
            jf = n_blk - 1 - j
            rf = pl.ds(pl.multiple_of(jf * rows, rows), rows)
            rb = pl.ds(pl.multiple_of(j * rows, rows), rows)
            flat, cots = [], []
            for hh in range(hb):
                cs = slice(hh * HEAD, (hh + 1) * HEAD)
                flat += [hq_ref[rf, cs], hff_ref[rf, cs], hi_ref[rf, cs], lbf[:, cs], st_ref[2 * hh, jf],
                         hq_ref[rb, cs], hfb_ref[rb, cs], hi_ref[rb, cs], lbb[:, cs], st_ref[2 * hh + 1, j]]
                cots += [(dor_ref[rf, cs], gs[2 * hh]), (dor_ref[rb, cs], gs[2 * hh + 1])]
            _, vjp = jax.vjp(lambda *a: _hgrn_blocks([a[5 * i:5 * i + 5] for i in range(2 * hb)], uppers), *flat)
            grads = vjp(cots)
            gs, dls = list(gs), list(dls)
            for hh in range(hb):
                cs = slice(hh * HEAD, (hh + 1) * HEAD)
                dq, df, di, dl, gs[2 * hh] = grads[10 * hh:10 * hh + 5]
                dq_ref[rf, cs] = dq.astype(BF16)
                dff_ref[rf, cs] = df.astype(BF16)
                di_ref[rf, cs] = di.astype(BF16)
                dls[2 * hh] = dls[2 * hh] + dl
                dq, df, di, dl, gs[2 * hh + 1] = grads[10 * hh + 5:10 * hh + 10]
                dq2_ref[rb, cs] = dq
                dfb_ref[rb, cs] = df.astype(BF16)
                di2_ref[rb, cs] = di
                dls[2 * hh + 1] = dls[2 * hh + 1] + dl
            return tuple(gs), tuple(dls)

        zl = jnp.zeros((1, HEAD), F32)
        _, dls = lax.fori_loop(0, n_blk, bwd_step, ((z,) * (2 * hb), (zl,) * (2 * hb)))
        dq_ref[...] = (dq_ref[...].astype(F32) + dq2_ref[...]).astype(BF16)
        di_ref[...] = (di_ref[...].astype(F32) + di2_ref[...]).astype(BF16)

        _, vjp_tf = jax.vjp(_lower_bound, tab_f)
        _, vjp_tb = jax.vjp(_lower_bound, tab_b)

        @pl.when(b == 0)
        def _():
            dlbf_ref[...] = jnp.zeros(dlbf_ref.shape, F32)
            dlbb_ref[...] = jnp.zeros(dlbb_ref.shape, F32)
            dg_ref[...] = jnp.zeros(dg_ref.shape, F32)

        dlbf_ref[...] += vjp_tf(jnp.concatenate(dls[0::2], axis=1))[0]
        dlbb_ref[...] += vjp_tb(jnp.concatenate(dls[1::2], axis=1))[0]
        dg_ref[...] += jnp.concatenate(dg_parts, axis=1)

    def col(k):
        return pl.BlockSpec((None, s_, wid), lambda h, b: (b, 0, k * (nh // hb) + h))

    tab = pl.BlockSpec((2, wid), lambda h, b: (0, h))
    vec = pl.BlockSpec((1, wid), lambda h, b: (0, h))
    oblk = pl.BlockSpec((None, s_, wid), lambda h, b: (b, 0, h))
    seq = _sds((b_, s_, d), BF16)
    return _run(
        body, "hgrn_bwd", (nh // hb, b_), [proj3, proj3, proj3, proj3, proj3, lb_f, lb_b, norm_g, o_raw, do_a],
        [col(0), col(1), col(2), col(3), col(4), tab, tab, vec, oblk, oblk],
        [seq, seq, seq, seq, seq, _sds((2, d), F32), _sds((2, d), F32), _sds((1, d), F32)],
        [oblk, oblk, oblk, oblk, oblk, tab, tab, vec],
        [pltpu.VMEM((2 * hb, n_blk, HEAD, HEAD), F32),
         pltpu.VMEM((s_, wid), F32), pltpu.VMEM((s_, wid), F32), pltpu.VMEM((s_, wid), F32)], 1, side)


def _rope_tables(s_):
    inv = ROPE_THETA ** (-jnp.arange(0, ROPE_DIM, 2, dtype=F32) / ROPE_DIM)
    ang = jnp.arange(s_, dtype=F32)[:, None] * inv
    cos, sin = jnp.cos(ang), jnp.sin(ang)
    rest = HEAD - ROPE_DIM
    ctab = jnp.concatenate([cos, cos, jnp.ones((s_, rest), F32)], axis=1)
    stab = jnp.concatenate([-sin, sin, jnp.zeros((s_, rest), F32)], axis=1)
    half = ROPE_DIM // 2
    perm = np.zeros((HEAD, HEAD), np.float32)
    for i in range(half):
        perm[i + half, i] = 1.0
        perm[i, i + half] = 1.0
    return ctab, stab, jnp.asarray(perm)


ATTN_GROUP = 8


def _attn_tiles(qkv, tabs, offs, perm, half):
    hi = lax.Precision.HIGHEST
    ids = range(len(qkv))
    qrot = [_dot(qkv[i][0], perm, NN, precision=hi) for i in ids]
    krot = [_dot(qkv[i][1], perm, NN, precision=hi) for i in ids]
    qs = [(qkv[i][0] * tabs[i][0] + qrot[i] * tabs[i][1]).astype(BF16) for i in ids]
    ks = [(qkv[i][1] * tabs[i][2] + krot[i] * tabs[i][3]).astype(BF16) for i in ids]
    ss = [_dot(qs[i], ks[i], NT) * (HEAD ** -0.5) for i in ids]
    ps, dens, lses = [], [], []
    for i in ids:
        qi = offs[i][0] + lax.broadcasted_iota(jnp.int32, ss[i].shape, 0)
        kj = offs[i][1] + lax.broadcasted_iota(jnp.int32, ss[i].shape, 1)
        s = jnp.where(jnp.abs(qi - kj) <= half, ss[i], NEG_INF)
        m = lax.stop_gradient(jnp.max(s, axis=-1, keepdims=True))
        p = jnp.exp(s - m)
        den = jnp.sum(p, axis=-1, keepdims=True)
        ps.append(p.astype(BF16))
        dens.append(den)
        lses.append(m + jnp.log(den))
    res = []
    for i in ids:
        o = _dot(ps[i], qkv[i][2].astype(BF16), NN) / dens[i]
        res.append((o, jnp.broadcast_to(lses[i], o.shape)))
    return res


def _attn_tiling(seg):
    tq = seg if seg <= 256 else 256
    kw = seg if seg <= 512 else 512
    tiles = []
    for i in range(seg // tq):
        ws = min(max(i * tq - (kw - tq) // 2, 0), seg - kw)
        tiles.append((i * tq, ws))
    return tq, kw, tiles


def _attn_specs(s_, d, g):
    c0 = (5 * d + g * 3 * ATTN_OUT) // HEAD

    def col(part):
        return pl.BlockSpec((None, s_, HEAD), lambda b, h: (b, 0, c0 + part * ATTN_HEADS + h))

    tab = pl.BlockSpec((s_, HEAD), lambda b, h: (0, 0))
    perm = pl.BlockSpec((HEAD, HEAD), lambda b, h: (0, 0))
    oblk = pl.BlockSpec((None, s_, HEAD), lambda b, h: (b, 0, h))
    return col, tab, perm, oblk


def _rows(r, first, count, dil):
    return pl.ds(r + dil * first, count, stride=dil) if dil > 1 else pl.ds(first, count)


def _attn_operands(group, dil, tq, kw, q_ref, k_ref, v_ref, c_ref, s_ref):
    qkv, tabs = [], []
    for r, q0, k0 in group:
        rq, rk = _rows(r, q0, tq, dil), _rows(r, k0, kw, dil)
        qkv.append((q_ref[rq, :], k_ref[rk, :], v_ref[rk, :]))
        tabs.append((c_ref[rq, :], s_ref[rq, :], c_ref[rk, :], s_ref[rk, :]))
    return qkv, tabs


def _attn_fwd(proj3, ctab, stab, perm, d, g):
    window, dil = ATTN_GROUPS[g]
    b_, s_, _ = proj3.shape
    seg = s_ // dil
    half = window // (2 * dil)
    tq, kw, tiles = _attn_tiling(seg)

    work = [(r, q0, k0) for r in range(dil) for q0, k0 in tiles]

    def body(q_ref, k_ref, v_ref, c_ref, s_ref, p_ref, o_ref, l_ref):
        pm = p_ref[...]
        for g0 in range(0, len(work), ATTN_GROUP):
            group = work[g0:g0 + ATTN_GROUP]
            qkv, tabs = _attn_operands(group, dil, tq, kw, q_ref, k_ref, v_ref, c_ref, s_ref)
            res = _attn_tiles(qkv, tabs, [(q0, k0) for _, q0, k0 in group], pm, half)
            for (r, q0, _), (o, l) in zip(group, res):
                o_ref[_rows(r, q0, tq, dil), :] = o
                l_ref[_rows(r, q0, tq, dil), :] = l

    col, tab, pspec, oblk = _attn_specs(s_, d, g)
    shp = _sds((b_, s_, ATTN_OUT), F32)
    return pl.pallas_call(
        body, name=f"attn_fwd_d{dil}", grid=(b_, ATTN_HEADS), in_specs=[col(0), col(1), col(2), tab, tab, pspec],
        out_specs=[oblk, oblk], out_shape=[shp, shp], compiler_params=_params(2, 0))(
            proj3, proj3, proj3, ctab, stab, perm)


def _attn_bwd(proj3, ctab, stab, perm, do, dl, d, g):
    window, dil = ATTN_GROUPS[g]
    b_, s_, _ = proj3.shape
    seg = s_ // dil
    half = window // (2 * dil)
    tq, kw, tiles = _attn_tiling(seg)

    work = [(r, q0, k0) for r in range(dil) for q0, k0 in tiles]

    def body(q_ref, k_ref, v_ref, c_ref, s_ref, p_ref, do_ref, dl_ref, dq_ref, dk_ref, dv_ref, dq_s, dk_s, dv_s):
        pm = p_ref[...]
        dk_s[...] = jnp.zeros(dk_s.shape, F32)
        dv_s[...] = jnp.zeros(dv_s.shape, F32)
        for g0 in range(0, len(work), ATTN_GROUP):
            group = work[g0:g0 + ATTN_GROUP]
            qkv, tabs = _attn_operands(group, dil, tq, kw, q_ref, k_ref, v_ref, c_ref, s_ref)
            offs = [(q0, k0) for _, q0, k0 in group]
            _, vjp = jax.vjp(
                lambda *a: _attn_tiles([a[3 * i:3 * i + 3] for i in range(len(group))], tabs, offs, pm, half),
                *[x for tile in qkv for x in tile])
            grads = vjp([(do_ref[_rows(r, q0, tq, dil), :], dl_ref[_rows(r, q0, tq, dil), :]) for r, q0, _ in group])
            for i, (r, q0, k0) in enumerate(group):
                rk = _rows(r, k0, kw, dil)
                dq_s[_rows(r, q0, tq, dil), :] = grads[3 * i]
                dk_s[rk, :] += grads[3 * i + 1]
                dv_s[rk, :] += grads[3 * i + 2]
        dq_ref[...] = dq_s[...].astype(BF16)
        dk_ref[...] = dk_s[...].astype(BF16)
        dv_ref[...] = dv_s[...].astype(BF16)

    col, tab, pspec, oblk = _attn_specs(s_, d, g)
    shp = _sds((b_, s_, ATTN_OUT), BF16)
    full = pltpu.VMEM((s_, HEAD), F32)
    return pl.pallas_call(
        body, name=f"attn_bwd_d{dil}", grid=(b_, ATTN_HEADS),
        in_specs=[col(0), col(1), col(2), tab, tab, pspec, oblk, oblk], out_specs=[oblk, oblk, oblk],
        out_shape=[shp, shp, shp], scratch_shapes=[full, full, full],
        compiler_params=_params(2, 0))(proj3, proj3, proj3, ctab, stab, perm, do, dl)


def _combine(os_, ls_):
    m = jnp.maximum(jnp.maximum(ls_[0], ls_[1]), ls_[2])
    es = [jnp.exp(l - m) for l in ls_]
    return (es[0] * os_[0] + es[1] * os_[1] + es[2] * os_[2]) / (es[0] + es[1] + es[2])


def _combine_fwd(os_, ls_):
    t, w = os_[0].shape
    tm = _tile(t, 512, 8)

    def body(*refs):
        refs[6][...] = _combine([r[...] for r in refs[:3]], [r[...] for r in refs[3:6]]).astype(BF16)

    row = pl.BlockSpec((tm, w), lambda i: (i, 0))
    return pl.pallas_call(body, name="combine_fwd", grid=(t // tm,), in_specs=[row] * 6, out_specs=row,
                          out_shape=_sds((t, w), BF16), compiler_params=_params(1, 0))(*os_, *ls_)


def _combine_bwd(os_, ls_, dob):
    t, w = os_[0].shape
    tm = _tile(t, 512, 8)

    def body(*refs):
        _, vjp = jax.vjp(lambda *a: _combine(a[:3], a[3:]), *[r[...] for r in refs[:6]])
        for r, g in zip(refs[7:], vjp(refs[6][...])):
            r[...] = g

    row = pl.BlockSpec((tm, w), lambda i: (i, 0))
    return pl.pallas_call(body, name="combine_bwd", grid=(t // tm,), in_specs=[row] * 7, out_specs=[row] * 6,
                          out_shape=[_sds((t, w), F32)] * 6, compiler_params=_params(1, 0))(*os_, *ls_, dob)


ROW_SHARDED = ("ffn1_out", "ffn2_out", "wa", "mix_out")


def _local_step(x, target, fulls, p, sc_arr):
    b_, s_, d = x.shape
    t = b_ * s_
    x2 = x.reshape(t, d)
    gate_col = 5 * d + QKV_W
    c_arr = sc_arr[1:2]
    w = {}

    def arrived(keys, arrays):
        for k, a in zip(keys, arrays):
            w[k] = a.reshape(a.shape[0] * a.shape[1], a.shape[2]) if k in ROW_SHARDED else a

    def reduce_begin(keys):
        g3 = [gw[k].reshape(fulls[k].shape) for k in keys]
        return [_sum_halves(a, b, c_arr) for a, b in zip(g3, _swap_halves(g3))]

    def reduce_end(keys, parts, got):
        for k, a, b in zip(keys, parts, got):
            reduced[k] = _sum_chips(a, b, sc_arr)

    arrived(["ffn1_in"], _gather_weights([fulls["ffn1_in"]]))
    xb = _cast_bf16(x2)
    mix_rows = fulls["mix_in"].shape[1]
    (u1, a1), got = _ffn_up(xb, w["ffn1_in"], _gather_exchange([fulls["ffn1_out"], fulls["mix_in"]],
                                                              [None, (0, mix_rows // 2)]))
    arrived(["ffn1_out"], got[:1])
    (h1, h1b, xh1, rs1), got = _down_ln(a1, w["ffn1_out"], x2, 0.5, p["ln1_g"], p["ln1_b"],
                                        _gather_exchange(got[1:], [(mix_rows // 2, mix_rows // 2)]))
    arrived(["mix_in"], got)

    keys = ["wa", "wb", "mix_out", "ffn2_in"]
    proj, got = _mm_w3(h1b, w["mix_in"], _gather_exchange([fulls[k] for k in keys]))
    arrived(keys, got)
    n_in = proj.shape[1]
    proj3 = proj.reshape(b_, s_, n_in)
    (o_raw, oa), got = _hgrn_fwd(proj3, p["hgrn_lb_fwd"], p["hgrn_lb_bwd"], p["hgrn_norm_g"], d,
                                 _gather_exchange([fulls["ffn2_out"]]))
    arrived(["ffn2_out"], got)
    oa2 = oa.reshape(t, d)

    ctab, stab, perm = _rope_tables(s_)
    os_, ls_ = [], []
    for g in range(N_GROUPS):
        o_g, l_g = _attn_fwd(proj3, ctab, stab, perm, d, g)
        os_.append(o_g.reshape(t, ATTN_OUT))
        ls_.append(l_g.reshape(t, ATTN_OUT))
    ob = _combine_fwd(os_, ls_)

    ya = _mm_w2(oa2, w["wa"], F32)
    yb, zb = _branch_gate(ob, w["wb"], ya, proj, gate_col)
    h2, h2b, xh2, rs2 = _down_ln(zb, w["mix_out"], h1, 1.0, p["ln2_g"], p["ln2_b"])

    u2, a2 = _ffn_up(h2b, w["ffn2_in"])

    gw, gp, reduced = {}, {}, {}
    nf = w["ffn2_in"].shape[2]

    def du_map(j, m):
        return (j // 2, m, j % 2)

    dres3, dy3, gp["ln3_g"], gp["ln3_b"], loss = _down_ln_loss(
        a2, w["ffn2_out"], h2, 0.5, p["ln3_g"], p["ln3_b"], target.reshape(t, d))
    du2 = _swiglu_bwd(dy3, w["ffn2_out"], u2)
    g_out = _tn_w2(a2, dy3).reshape(fulls["ffn2_out"].shape)
    g_in, got = _tn_w3(h2b, du2, du_map, nf, _swap_exchange([g_out]))
    parts_a = [_sum_halves(g_out, got[0], c_arr)]
    dh2, got = _ffn_dx(du2, w["ffn2_in"], dres3, _both(_scatter_exchange(parts_a), _swap_exchange([g_in])))
    reduce_end(["ffn2_out"], parts_a, got[:1])
    parts_a.append(_sum_halves(g_in, got[1], c_arr))

    dres2, dmix, gp["ln2_g"], gp["ln2_b"] = _ln_bwd(dh2, xh2, rs2, p["ln2_g"], 1.0)
    dya, dyb, dpga, dpgb = _gate_bwd(dmix, w["mix_out"], proj, ya, yb, gate_col)
    gw["mix_out"] = _tn_w2(zb, dmix)
    do_a = _nt_w2(dya, w["wa"], F32)
    gw["wa"] = _tn_w2(oa2, dya)
    nb = w["wb"].shape[2]
    do_b = _nt_w3(dyb, w["wb"], None)
    gw["wb"] = _tn_w3(ob, dyb, lambda j, m: (m, j), nb)
    keys = ["ffn2_in", "mix_out", "wa", "wb"]
    parts_b = parts_a[1:] + reduce_begin(keys[1:])

    (dq, dff, dfb, di, dog, gp["hgrn_lb_fwd"], gp["hgrn_lb_bwd"], gp["hgrn_norm_g"]), got = _hgrn_bwd(
        proj3, p["hgrn_lb_fwd"], p["hgrn_lb_bwd"], p["hgrn_norm_g"], o_raw, do_a.reshape(b_, s_, d), d,
        _scatter_exchange(parts_b))
    reduce_end(keys, parts_b, got)

    douts = _combine_bwd(os_, ls_, do_b)
    dqkv = []
    for g in range(N_GROUPS):
        grads_g = _attn_bwd(proj3, ctab, stab, perm, douts[g].reshape(b_, s_, ATTN_OUT),
                            douts[3 + g].reshape(b_, s_, ATTN_OUT), d, g)
        dqkv += [a.reshape(t, ATTN_OUT) for a in grads_g]

    dproj = jnp.concatenate(
        [a.reshape(t, d) for a in (dq, dff, dfb, di, dog)] + dqkv + [dpga, dpgb], axis=1)
    nm = w["mix_in"].shape[2]
    gw["mix_in"] = _tn_w3(h1b, dproj, lambda j, m: (m, j), nm)
    parts_c = reduce_begin(["mix_in"])
    cut = 3 * parts_c[0].shape[1] // 4
    dh1, got_c = _nt_w3(dproj, w["mix_in"], dres2, _scatter_exchange(parts_c, [(0, cut)]))

    dres1, dy1, gp["ln1_g"], gp["ln1_b"] = _ln_bwd(dh1, xh1, rs1, p["ln1_g"], 0.5)
    g_out = _tn_w2(a1, dy1).reshape(fulls["ffn1_out"].shape)
    du1, got = _swiglu_bwd(dy1, w["ffn1_out"], u1, _both(
        _swap_exchange([g_out]), _scatter_exchange(parts_c, [(cut, parts_c[0].shape[1] - cut)], got_c)))
    reduce_end(["mix_in"], parts_c, got[1:])
    parts_d = [_sum_halves(g_out, got[0], c_arr)]
    gw["ffn1_in"], got = _tn_w3(xb, du1, du_map, nf, _scatter_exchange(parts_d))
    reduce_end(["ffn1_out"], parts_d, got)
    parts_e = reduce_begin(["ffn1_in"])
    dx, got = _ffn_dx(du1, w["ffn1_in"], dres1, _scatter_exchange(parts_e))
    reduce_end(["ffn1_in"], parts_e, got)

    keys = list(reduced)
    grads = dict(zip(keys, _join_halves([reduced[k] for k in keys])))
    return loss, dx.reshape(b_, s_, d), grads, gp


MESH = pl.DeviceIdType.MESH
ANY = pl.BlockSpec(memory_space=pl.ANY)


def _place():
    x, y, c = lax.axis_index("x"), lax.axis_index("y"), lax.axis_index("c")
    chips = [(1 - x, y), (x, 1 - y), (1 - x, 1 - y)]
    return x, y, c, chips, (x, y, 1 - c)


def _half_rows(c, rows):
    hr = rows // 2
    return pl.ds(pl.multiple_of(c * hr, 16), hr)


def _remote(src, dst, send, recv, dev):
    return pltpu.make_async_remote_copy(src_ref=src, dst_ref=dst, send_sem=send, recv_sem=recv,
                                        device_id=dev, device_id_type=MESH)


def _gather_weights(fulls):
    n = len(fulls)

    def body(*refs):
        _gather_start(refs[n:2 * n], refs[2 * n:])
        _gather_finish(refs[n:2 * n], refs[2 * n:])

    return pl.pallas_call(
        body, name="gather_weights", in_specs=[ANY] * n, out_specs=[ANY] * n,
        out_shape=[_sds(a.shape, a.dtype) for a in fulls], input_output_aliases={i: i for i in range(n)},
        scratch_shapes=_gather_sems(n))(*fulls)


def _gather_sems(n):
    return [pltpu.SemaphoreType.DMA((n, 3)) for _ in range(4)]


def _span_half(c, span, rows):
    r0, cnt = (0, rows) if span is None else span
    return pl.ds(pl.multiple_of(r0 + c * (cnt // 2), 16), cnt // 2)


def _gather_start(bufs, sems, spans=None):
    isend, irecv = sems[0], sems[1]
    x, y, c, chips, sib = _place()
    for i, buf in enumerate(bufs):
        blk = buf.at[2 * x + y, _span_half(c, spans and spans[i], buf.shape[1])]
        for k, chip in enumerate(chips):
            _remote(blk, blk, isend.at[i, k], irecv.at[i, k], (*chip, c)).start()


def _gather_finish(bufs, sems, spans=None):
    isend, irecv, fsend, frecv = sems
    x, y, c, chips, sib = _place()
    for i, buf in enumerate(bufs):
        mine = _span_half(c, spans and spans[i], buf.shape[1])
        for k, chip in enumerate(chips):
            blk = buf.at[2 * chip[0] + chip[1], mine]
            _remote(blk, blk, isend.at[i, k], irecv.at[i, k], (*chip, c)).wait_recv()
            _remote(blk, blk, fsend.at[i, k], frecv.at[i, k], sib).start()
    for i, buf in enumerate(bufs):
        span = spans and spans[i]
        mine, other = _span_half(c, span, buf.shape[1]), _span_half(1 - c, span, buf.shape[1])
        own = buf.at[2 * x + y, mine]
        for k, chip in enumerate(chips):
            got = buf.at[2 * chip[0] + chip[1], other]
            _remote(got, got, fsend.at[i, k], frecv.at[i, k], sib).wait_recv()
            _remote(own, own, isend.at[i, k], irecv.at[i, k], (*chip, c)).wait_send()
            blk = buf.at[2 * chip[0] + chip[1], mine]
            _remote(blk, blk, fsend.at[i, k], frecv.at[i, k], sib).wait_send()


def _gather_exchange(fulls, spans=None):
    n = len(fulls)
    return _Exchange(fulls, [_sds(a.shape, a.dtype) for a in fulls], {i: i for i in range(n)}, _gather_sems(n),
                     lambda ins, outs, sems: _gather_start(outs, sems, spans),
                     lambda ins, outs, sems: _gather_finish(outs, sems, spans))


def _swap_halves(grads):
    side = _swap_exchange(grads)
    n = len(grads)

    def body(*refs):
        side.start(refs[:n], refs[n:2 * n], refs[2 * n:])
        side.finish(refs[:n], refs[n:2 * n], refs[2 * n:])

    return pl.pallas_call(
        body, name="swap_halves", in_specs=[ANY] * n, out_specs=[ANY] * n, out_shape=side.outs,
        scratch_shapes=side.sems)(*grads)


def _swap_exchange(grads):
    n = len(grads)

    def copies(ins, outs, sems):
        x, y, c, chips, sib = _place()
        return [_remote(a.at[:, _half_rows(1 - c, a.shape[1])], b, sems[0].at[i], sems[1].at[i], sib)
                for i, (a, b) in enumerate(zip(ins, outs))]

    def start(ins, outs, sems):
        for cp in copies(ins, outs, sems):
            cp.start()

    def finish(ins, outs, sems):
        for cp in copies(ins, outs, sems):
            cp.wait()

    return _Exchange(grads, [_sds((N_SHARD, a.shape[1] // 2, a.shape[2]), a.dtype) for a in grads], {},
                     [pltpu.SemaphoreType.DMA((n,)) for _ in range(2)], start, finish)


def _both(a, b):
    i, o, s = len(a.ins), len(a.outs), len(a.sems)

    def start(ins, outs, sems):
        a.start(ins[:i], outs[:o], sems[:s])
        b.start(ins[i:], outs[o:], sems[s:])

    def finish(ins, outs, sems):
        a.finish(ins[:i], outs[:o], sems[:s])
        b.finish(ins[i:], outs[o:], sems[s:])

    aliases = dict(a.aliases)
    aliases.update({i + k: o + v for k, v in b.aliases.items()})
    return _Exchange(a.ins + b.ins, a.outs + b.outs, aliases, a.sems + b.sems, start, finish)


def _scatter_exchange(parts, spans=None, into=None):
    n = len(parts)

    def copies(ins, outs, sems):
        x, y, c, chips, sib = _place()
        cps = []
        for i in range(n):
            rows = pl.ds(*spans[i]) if spans and spans[i] else slice(None)
            for k, chip in enumerate(chips):
                cps.append(_remote(ins[i].at[2 * chip[0] + chip[1], rows], outs[i].at[k, rows],
                                   sems[0].at[i, k], sems[1].at[i, k], (*chip, c)))
        return cps

    def start(ins, outs, sems):
        for cp in copies(ins, outs, sems):
            cp.start()

    def finish(ins, outs, sems):
        for cp in copies(ins, outs, sems):
            cp.wait()

    return _Exchange(list(parts) + list(into or []), [_sds((3,) + a.shape[1:], a.dtype) for a in parts],
                     {n + i: i for i in range(n)} if into else {},
                     [pltpu.SemaphoreType.DMA((n, 3)) for _ in range(2)], start, finish)


def _join_halves(grads):
    n = len(grads)

    def body(*refs):
        bufs, send, recv = refs[n:2 * n], refs[2 * n], refs[2 * n + 1]
        x, y, c, chips, sib = _place()
        cps = []
        for i in range(n):
            blk = bufs[i].at[_half_rows(c, bufs[i].shape[0])]
            other = bufs[i].at[_half_rows(1 - c, bufs[i].shape[0])]
            cp = _remote(blk, blk, send.at[i], recv.at[i], sib)
            cp.start()
            cps.append((cp, _remote(other, other, send.at[i], recv.at[i], sib)))
        for cp, got in cps:
            cp.wait_send()
            got.wait_recv()

    dma = pltpu.SemaphoreType.DMA
    return pl.pallas_call(
        body, name="join_halves", in_specs=[ANY] * n, out_specs=[ANY] * n,
        out_shape=[_sds(a.shape, a.dtype) for a in grads], input_output_aliases={i: i for i in range(n)},
        scratch_shapes=[dma((n,)), dma((n,))])(*grads)


def _gather_rows(block):
    m_per, n = block.shape

    def body(x_ref, out_ref, send_sems, recv_sems, local_sem):
        x, y, c, chips, sibling = _place()
        me = (x, y, c)

        def rows(px, py, pc):
            return out_ref.at[pl.ds((4 * px + 2 * py + pc) * m_per, m_per), :]

        def copy(k, blk, to, src=None):
            return _remote(rows(*blk) if src is None else src, rows(*blk), send_sems.at[k], recv_sems.at[k], to)

        mine = pltpu.make_async_copy(x_ref, rows(*me), local_sem)
        mine.start()
        first = [copy(0, me, sibling, src=x_ref)]
        first += [copy(1 + j, me, (*chip, c), src=x_ref) for j, chip in enumerate(chips)]
        for cp in first:
            cp.start()
        passed = [copy(4 + j, (*chip, c), sibling) for j, chip in enumerate(chips)]
        for j, chip in enumerate(chips):
            copy(1 + j, (*chip, c), me).wait_recv()
            passed[j].start()
        copy(0, sibling, me).wait_recv()
        for j, chip in enumerate(chips):
            copy(4 + j, (*chip, 1 - c), me).wait_recv()
        for cp in first + passed:
            cp.wait_send()
        mine.wait()

    vmem = pl.BlockSpec(memory_space=pltpu.VMEM)
    dma = pltpu.SemaphoreType.DMA
    return pl.pallas_call(
        body, name="gather_rows", in_specs=[vmem], out_specs=vmem, out_shape=_sds((8 * m_per, n), block.dtype),
        scratch_shapes=[dma((7,)), dma((7,)), dma(())])(block)


def _row_tile(rows, cols):
    return _tile(rows, max(16, (1 << 20) // cols), 16)


def _sum_halves(grad, got, c_arr):
    _, hr, cols = got.shape
    tr = _row_tile(hr, cols)
    nb = hr // tr

    def body(c_ref, a_ref, b_ref, o_ref):
        o_ref[...] = (a_ref[...].astype(F32) + b_ref[...].astype(F32)).astype(BF16)

    blk = pl.BlockSpec((None, tr, cols), lambda s, i, c_ref: (s, i, 0))
    return pl.pallas_call(
        body, name="sum_halves",
        grid_spec=pltpu.PrefetchScalarGridSpec(
            num_scalar_prefetch=1, grid=(N_SHARD, nb),
            in_specs=[pl.BlockSpec((None, tr, cols), lambda s, i, c_ref: (s, c_ref[0] * nb + i, 0)), blk],
            out_specs=blk),
        out_shape=_sds(got.shape, BF16), compiler_params=_params(2, 0))(c_arr, grad, got)


def _sum_chips(part, got, sc_arr):
    _, hr, cols = got.shape
    tr = _row_tile(hr, cols)
    nb = hr // tr

    def body(s_ref, a_ref, b_ref, o_ref):
        o_ref[...] = ((a_ref[...].astype(F32) + b_ref[0].astype(F32)) + b_ref[1].astype(F32)) + b_ref[2].astype(F32)

    return pl.pallas_call(
        body, name="sum_chips",
        grid_spec=pltpu.PrefetchScalarGridSpec(
            num_scalar_prefetch=1, grid=(nb,),
            in_specs=[pl.BlockSpec((None, tr, cols), lambda i, s_ref: (s_ref[0], i, 0)),
                      pl.BlockSpec((3, tr, cols), lambda i, s_ref: (0, i, 0))],
            out_specs=pl.BlockSpec((tr, cols), lambda i, s_ref: (s_ref[1] * nb + i, 0))),
        out_shape=_sds((2 * hr, cols), F32), compiler_params=_params(1, 0))(sc_arr, part, got)


def _cast_into_slot(x2d, sc_arr):
    rows, cols = x2d.shape
    tr = _row_tile(rows, cols)

    def body(s_ref, x_ref, o_ref):
        o_ref[...] = x_ref[...].astype(BF16)

    return pl.pallas_call(
        body, name="cast_into_slot",
        grid_spec=pltpu.PrefetchScalarGridSpec(
            num_scalar_prefetch=1, grid=(rows // tr,),
            in_specs=[pl.BlockSpec((tr, cols), lambda i, s_ref: (i, 0))],
            out_specs=pl.BlockSpec((None, tr, cols), lambda i, s_ref: (s_ref[0], i, 0))),
        out_shape=_sds((N_SHARD, rows, cols), BF16), compiler_params=_params(1, 0))(sc_arr, x2d)


def _adam_math(w, g, m, v):
    m = ADAM_B1 * m + (1.0 - ADAM_B1) * g
    v = ADAM_B2 * v + (1.0 - ADAM_B2) * (g * g)
    m_hat = m / (1.0 - ADAM_B1 ** ADAM_STEP)
    v_hat = v / (1.0 - ADAM_B2 ** ADAM_STEP)
    delta = -ADAM_LR * (m_hat / (jnp.sqrt(v_hat) + ADAM_EPS) + ADAM_WD * w)
    return delta, m, v


def _adamw(w, g, m, v):
    rows, cols = w.shape
    tr = _tile(rows, max(8, (3 << 18) // cols), 8)

    def body(w_ref, g_ref, m_ref, v_ref, go_ref, d_ref, mo_ref, vo_ref):
        g_ = g_ref[...]
        go_ref[...] = g_
        d_ref[...], mo_ref[...], vo_ref[...] = _adam_math(w_ref[...], g_, m_ref[...], v_ref[...])

    blk = pl.BlockSpec((tr, cols), lambda i: (i, 0))
    return pl.pallas_call(
        body, name="adamw", grid=(rows // tr,), in_specs=[blk] * 4, out_specs=[blk] * 4,
        out_shape=[_sds((rows, cols), F32)] * 4, compiler_params=_params(1, 0))(w, g, m, v)


def _adamw_small(gathered, w, m, v):
    rows, cols = w.shape

    def body(a_ref, w_ref, m_ref, v_ref, go_ref, d_ref, mo_ref, vo_ref):
        g_ = a_ref[pl.ds(0, rows), :]
        for k in range(1, 8):
            g_ = g_ + a_ref[pl.ds(k * rows, rows), :]
        go_ref[...] = g_
        d_ref[...], mo_ref[...], vo_ref[...] = _adam_math(w_ref[...], g_, m_ref[...], v_ref[...])

    vmem = pl.BlockSpec(memory_space=pltpu.VMEM)
    return pl.pallas_call(
        body, name="adamw_small", in_specs=[vmem] * 4, out_specs=[vmem] * 4,
        out_shape=[_sds((rows, cols), F32)] * 4)(gathered, w, m, v)


BIG = ("ffn1_w_in", "ffn1_w_out", "mix_w_in", "w_branch_a", "w_branch_b", "mix_w_out", "ffn2_w_in", "ffn2_w_out")
BIG_KEY = {"ffn1_w_in": "ffn1_in", "ffn1_w_out": "ffn1_out", "mix_w_in": "mix_in", "w_branch_a": "wa",
           "w_branch_b": "wb", "mix_w_out": "mix_out", "ffn2_w_in": "ffn2_in", "ffn2_w_out": "ffn2_out"}
SMALL = ("ln1_g", "ln1_b", "hgrn_lb_fwd", "hgrn_lb_bwd", "hgrn_norm_g", "ln2_g", "ln2_b", "ln3_g", "ln3_b")
ORDER = ("ffn1_w_in", "ffn1_w_out", "ln1_g", "ln1_b", "mix_w_in", "hgrn_lb_fwd", "hgrn_lb_bwd", "hgrn_norm_g",
         "w_branch_a", "w_branch_b", "mix_w_out", "ln2_g", "ln2_b", "ffn2_w_in", "ffn2_w_out", "ln3_g", "ln3_b")
SMALL_ROWS = 16


def _pack_small(d):
    rows = jnp.concatenate([d[k].reshape(-1, d[k].shape[-1]) for k in SMALL], axis=0)
    return jnp.pad(rows, ((0, SMALL_ROWS - rows.shape[0]), (0, 0)))


def _unpack_small(a, like):
    out, r = {}, 0
    for k in SMALL:
        n = like[k].shape[0]
        out[k] = a[r:r + n].reshape(like[k].shape)
        r += n
    return out


def kernel(x, ffn1_w_in, ffn1_w_out, ln1_g, ln1_b, mix_w_in, hgrn_lb_fwd, hgrn_lb_bwd, hgrn_norm_g, w_branch_a, w_branch_b, mix_w_out, ln2_g, ln2_b, ffn2_w_in, ffn2_w_out, ln3_g, ln3_b, loss_target, m_ffn1_w_in, m_ffn1_w_out, m_ln1_g, m_ln1_b, m_mix_w_in, m_hgrn_lb_fwd, m_hgrn_lb_bwd, m_hgrn_norm_g, m_w_branch_a, m_w_branch_b, m_mix_w_out, m_ln2_g, m_ln2_b, m_ffn2_w_in, m_ffn2_w_out, m_ln3_g, m_ln3_b, v_ffn1_w_in, v_ffn1_w_out, v_ln1_g, v_ln1_b, v_mix_w_in, v_hgrn_lb_fwd, v_hgrn_lb_bwd, v_hgrn_norm_g, v_w_branch_a, v_w_branch_b, v_mix_w_out, v_ln2_g, v_ln2_b, v_ffn2_w_in, v_ffn2_w_out, v_ln3_g, v_ln3_b):
    wts = dict(ffn1_w_in=ffn1_w_in, ffn1_w_out=ffn1_w_out, ln1_g=ln1_g, ln1_b=ln1_b, mix_w_in=mix_w_in,
               hgrn_lb_fwd=hgrn_lb_fwd, hgrn_lb_bwd=hgrn_lb_bwd, hgrn_norm_g=hgrn_norm_g, w_branch_a=w_branch_a,
               w_branch_b=w_branch_b, mix_w_out=mix_w_out, ln2_g=ln2_g, ln2_b=ln2_b, ffn2_w_in=ffn2_w_in,
               ffn2_w_out=ffn2_w_out, ln3_g=ln3_g, ln3_b=ln3_b)
    mom = dict(ffn1_w_in=m_ffn1_w_in, ffn1_w_out=m_ffn1_w_out, ln1_g=m_ln1_g, ln1_b=m_ln1_b, mix_w_in=m_mix_w_in,
               hgrn_lb_fwd=m_hgrn_lb_fwd, hgrn_lb_bwd=m_hgrn_lb_bwd, hgrn_norm_g=m_hgrn_norm_g,
               w_branch_a=m_w_branch_a, w_branch_b=m_w_branch_b, mix_w_out=m_mix_w_out, ln2_g=m_ln2_g, ln2_b=m_ln2_b,
               ffn2_w_in=m_ffn2_w_in, ffn2_w_out=m_ffn2_w_out, ln3_g=m_ln3_g, ln3_b=m_ln3_b)
    var = dict(ffn1_w_in=v_ffn1_w_in, ffn1_w_out=v_ffn1_w_out, ln1_g=v_ln1_g, ln1_b=v_ln1_b, mix_w_in=v_mix_w_in,
               hgrn_lb_fwd=v_hgrn_lb_fwd, hgrn_lb_bwd=v_hgrn_lb_bwd, hgrn_norm_g=v_hgrn_norm_g,
               w_branch_a=v_w_branch_a, w_branch_b=v_w_branch_b, mix_w_out=v_mix_w_out, ln2_g=v_ln2_g, ln2_b=v_ln2_b,
               ffn2_w_in=v_ffn2_w_in, ffn2_w_out=v_ffn2_w_out, ln3_g=v_ln3_g, ln3_b=v_ln3_b)
    shard = (2 * lax.axis_index("x") + lax.axis_index("y")).astype(jnp.int32)
    sc_arr = jnp.stack([shard, lax.axis_index("c").astype(jnp.int32)])

    shard2d = {k: wts[k].reshape(wts[k].shape[1:]) for k in BIG}
    fulls = {BIG_KEY[k]: _cast_into_slot(shard2d[k], sc_arr) for k in BIG}
    p = {k: wts[k] for k in SMALL}

    loss, grad_x, grads, gp = _local_step(x, loss_target, fulls, p, sc_arr)
    loss = lax.psum(loss[0, 0], ("x", "y", "c"))

    out_g, out_d, out_m, out_v = {}, {}, {}, {}
    for k in BIG:
        g = grads[BIG_KEY[k]]
        shp = wts[k].shape
        res = _adamw(shard2d[k], g, mom[k].reshape(shp[1:]), var[k].reshape(shp[1:]))
        out_g[k], out_d[k], out_m[k], out_v[k] = [a.reshape(shp) for a in res]

    gathered = _gather_rows(_pack_small(gp))
    res = _adamw_small(gathered, _pack_small(wts), _pack_small(mom), _pack_small(var))
    for dst, a in zip((out_g, out_d, out_m, out_v), res):
        dst.update(_unpack_small(a, wts))

    return (loss, grad_x, *[out_g[k] for k in ORDER], *[out_d[k] for k in ORDER],
            *[out_m[k] for k in ORDER], *[out_v[k] for k in ORDER])
```

```python
import functools

import numpy as np
import jax
import jax.numpy as jnp
from jax import lax
from jax.experimental import pallas as pl
from jax.experimental.pallas import tpu as pltpu

F32 = jnp.float32
BF16 = jnp.bfloat16

HEAD = 128
ATTN_GROUPS = ((128, 1), (512, 4), (2048, 16))
ATTN_HEADS = 4
N_GROUPS = len(ATTN_GROUPS)
QKV_W = N_GROUPS * 3 * ATTN_HEADS * HEAD
ATTN_OUT = ATTN_HEADS * HEAD
ROPE_THETA = 500000.0
ROPE_DIM = HEAD // 4
HGRN_CHUNK = 32
HGRN_FWD_HEADS = 2
HGRN_BWD_HEADS = 2
HGRN_SUB = 4
ALPHA = 2.0 ** 0.25
LN_EPS = 1e-5
NEG_INF = -1e30
ADAM_LR, ADAM_B1, ADAM_B2, ADAM_EPS, ADAM_WD, ADAM_STEP = 0.001, 0.9, 0.999, 1e-08, 0.01, 10

N_SHARD = 4
VMEM_LIMIT = 56 * 1024 * 1024

NN = ((1,), (0,))
NT = ((1,), (1,))
TN = ((0,), (0,))


def _dot(a, b, dims, precision=None):
    return lax.dot_general(a, b, (dims, ((), ())), preferred_element_type=F32, precision=precision)


def _tile(n, pref, mult=128):
    best = None
    for t in range(mult, min(n, pref) + 1, mult):
        if n % t == 0:
            best = t
    return n if best is None else best


def _params(n_parallel, n_arbitrary):
    return pltpu.CompilerParams(
        dimension_semantics=("parallel",) * n_parallel + ("arbitrary",) * n_arbitrary,
        vmem_limit_bytes=VMEM_LIMIT)


def _sigmoid(x):
    return 1.0 / (1.0 + jnp.exp(-x))


def _silu(x):
    return x * _sigmoid(x)


class _Exchange:
    def __init__(self, ins, outs, aliases, sems, start, finish):
        self.ins, self.outs, self.aliases, self.sems = list(ins), list(outs), dict(aliases), list(sems)
        self.start, self.finish = start, finish


def _run(body, name, grid, ins, in_specs, outs, out_specs, scratch, n_arbitrary, side=None):
    n_in, n_out, n_scr = len(ins), len(outs), len(scratch)
    if side is None:
        return pl.pallas_call(
            body, name=name, grid=grid, in_specs=in_specs, out_specs=out_specs, out_shape=outs,
            scratch_shapes=scratch, compiler_params=_params(len(grid) - n_arbitrary, n_arbitrary))(*ins)
    s_in, s_out = len(side.ins), len(side.outs)
    i1 = n_in + s_in
    o1 = i1 + n_out
    o2 = o1 + s_out
    c1 = o2 + n_scr

    def wrapped(*refs):
        s_refs = (refs[n_in:i1], refs[o1:o2], refs[c1:])
        ids = [pl.program_id(a) for a in range(len(grid))]
        first = functools.reduce(jnp.logical_and, [i == 0 for i in ids])
        last = functools.reduce(jnp.logical_and, [i == g - 1 for i, g in zip(ids, grid)])

        @pl.when(first)
        def _():
            side.start(*s_refs)

        body(*refs[:n_in], *refs[i1:o1], *refs[o2:c1])

        @pl.when(last)
        def _():
            side.finish(*s_refs)

    res = pl.pallas_call(
        wrapped, name=name, grid=grid, in_specs=list(in_specs) + [ANY] * s_in,
        out_specs=list(out_specs) + [ANY] * s_out, out_shape=list(outs) + side.outs,
        scratch_shapes=list(scratch) + side.sems,
        input_output_aliases={n_in + a: n_out + b for a, b in side.aliases.items()},
        compiler_params=_params(0, len(grid)))(*ins, *side.ins)
    return res[:n_out], res[n_out:]


def _gemm(name, grid, ins, in_specs, outs, out_specs, accs, dot_fn, epi_fn, side=None):
    n_in, n_out, n_k = len(ins), len(outs), grid[-1]

    def body(*refs):
        in_refs, out_refs, acc_refs = refs[:n_in], refs[n_in:n_in + n_out], refs[n_in + n_out:]
        k = pl.program_id(len(grid) - 1)

        @pl.when(k == 0)
        def _():
            for a in acc_refs:
                a[...] = jnp.zeros(a.shape, F32)

        dot_fn(in_refs, acc_refs)

        @pl.when(k == n_k - 1)
        def _():
            epi_fn(in_refs, acc_refs, out_refs)

    return _run(body, name, grid, ins, in_specs, outs, out_specs, [pltpu.VMEM(s, F32) for s in accs], 1, side)


def _sds(shape, dtype):
    return jax.ShapeDtypeStruct(shape, dtype)


def _gemm_rows(name, grid, ins, in_specs, outs, out_specs, tm, n_sub, fn, side=None):
    n_in, sub = len(ins), tm // n_sub

    def body(*refs):
        for s in range(n_sub):
            fn(pl.ds(s * sub, sub), refs[:n_in], refs[n_in:])

    return _run(body, name, grid, ins, in_specs, outs, out_specs, [], 0, side)


def _ffn_up(xb, w3, side=None):
    t, d = xb.shape
    nf = w3.shape[2]
    f = 2 * nf
    tm = _tile(t, 512, 8)
    tn = nf // 2 if nf % 256 == 0 else nf
    ns = nf // tn

    def fn(rows, r, out):
        x = r[0][rows, :]
        g, u = _dot(x, r[1][...], NN), _dot(x, r[2][...], NN)
        out[0][0, rows, :] = g.astype(BF16)
        out[0][1, rows, :] = u.astype(BF16)
        out[1][rows, :] = (_silu(g) * u).astype(BF16)

    return _gemm_rows(
        "ffn_up", (t // tm, 2, ns), [xb, w3, w3],
        [pl.BlockSpec((tm, d), lambda i, j, h: (i, 0)),
         pl.BlockSpec((None, d, tn), lambda i, j, h: (j, 0, h)),
         pl.BlockSpec((None, d, tn), lambda i, j, h: (j + 2, 0, h))],
        [_sds((2, t, f), BF16), _sds((t, f), BF16)],
        [pl.BlockSpec((2, tm, tn), lambda i, j, h: (0, i, j * ns + h)),
         pl.BlockSpec((tm, tn), lambda i, j, h: (i, j * ns + h))],
        tm, 2, fn, side)


def _down_ln(a, w, resid, coef, g, b, side=None):
    t, kd = a.shape
    d = w.shape[1]
    tm = _tile(t, 256, 8)

    def fn(rows, r, out):
        v = ALPHA * r[2][rows, :] + coef * _dot(r[0][rows, :], r[1][...], NN)
        mu = jnp.mean(v, axis=-1, keepdims=True)
        c = v - mu
        var = jnp.mean(c * c, axis=-1, keepdims=True)
        rstd = lax.rsqrt(var + LN_EPS)
        xhat = c * rstd
        h = xhat * r[3][...] + r[4][...]
        out[0][rows, :] = h
        out[1][rows, :] = h.astype(BF16)
        out[2][rows, :] = xhat
        out[3][rows, :] = rstd

    row = pl.BlockSpec((tm, d), lambda i: (i, 0))
    vec = pl.BlockSpec((1, d), lambda i: (0, 0))
    return _gemm_rows(
        "down_ln", (t // tm,), [a, w, resid, g, b],
        [pl.BlockSpec((tm, kd), lambda i: (i, 0)),
         pl.BlockSpec((kd, d), lambda i: (0, 0), pipeline_mode=pl.Buffered(1)), row, vec, vec],
        [_sds((t, d), F32), _sds((t, d), BF16), _sds((t, d), F32), _sds((t, 1), F32)],
        [row, row, row, pl.BlockSpec((tm, 1), lambda i: (i, 0))], tm, 2, fn, side)


def _down_ln_loss(a, w, resid, coef, g, b, target):
    t, kd = a.shape
    d = w.shape[1]
    tm = _tile(t, 256, 8)
    n_sub = 2
    sub = tm // n_sub

    def body(a_ref, w_ref, r_ref, g_ref, b_ref, t_ref, dres_ref, dyb_ref, dg_ref, db_ref, loss_ref):
        @pl.when(pl.program_id(0) == 0)
        def _():
            dg_ref[...] = jnp.zeros(dg_ref.shape, F32)
            db_ref[...] = jnp.zeros(db_ref.shape, F32)
            loss_ref[...] = jnp.zeros(loss_ref.shape, F32)

        for s in range(n_sub):
            rows = pl.ds(s * sub, sub)
            v = ALPHA * r_ref[rows, :] + coef * _dot(a_ref[rows, :], w_ref[...], NN)
            mu = jnp.mean(v, axis=-1, keepdims=True)
            c = v - mu
            rstd = lax.rsqrt(jnp.mean(c * c, axis=-1, keepdims=True) + LN_EPS)
            xh = c * rstd
            e = xh * g_ref[...] + b_ref[...] - t_ref[rows, :]
            dh = e * (1.0 / d)
            dxh = dh * g_ref[...]
            m1 = jnp.mean(dxh, axis=-1, keepdims=True)
            m2 = jnp.mean(dxh * xh, axis=-1, keepdims=True)
            dv = rstd * (dxh - m1 - xh * m2)
            dres_ref[rows, :] = ALPHA * dv
            dyb_ref[rows, :] = (coef * dv).astype(BF16)
            dg_ref[...] += jnp.sum(dh * xh, axis=0, keepdims=True)
            db_ref[...] += jnp.sum(dh, axis=0, keepdims=True)
            part = 0.5 * jnp.sum(jnp.sum(e * e, axis=-1, keepdims=True) * (1.0 / d), axis=0, keepdims=True)
            loss_ref[...] += jnp.broadcast_to(part, loss_ref.shape)

    row = pl.BlockSpec((tm, d), lambda i: (i, 0))
    vec = pl.BlockSpec((1, d), lambda i: (0, 0))
    return _run(
        body, "down_ln_loss", (t // tm,), [a, w, resid, g, b, target],
        [pl.BlockSpec((tm, kd), lambda i: (i, 0)),
         pl.BlockSpec((kd, d), lambda i: (0, 0), pipeline_mode=pl.Buffered(1)), row, vec, vec, row],
        [_sds((t, d), F32), _sds((t, d), BF16), _sds((1, d), F32), _sds((1, d), F32), _sds((1, 128), F32)],
        [row, row, vec, vec, pl.BlockSpec((1, 128), lambda i: (0, 0))], [], 1)


def _only(res, side):
    return res[0] if side is None else (res[0][0], res[1])


def _mm_w3(a, w3, side=None):
    t, kd = a.shape
    n = w3.shape[2]
    tm = _tile(t, 512, 8)

    def fn(rows, r, out):
        out[0][rows, :] = _dot(r[0][rows, :], r[1][...], NN)

    return _only(_gemm_rows(
        "mm_w3", (N_SHARD, t // tm), [a, w3],
        [pl.BlockSpec((tm, kd), lambda j, i: (i, 0)),
         pl.BlockSpec((None, kd, n), lambda j, i: (j, 0, 0), pipeline_mode=pl.Buffered(1))],
        [_sds((t, N_SHARD * n), F32)], [pl.BlockSpec((tm, n), lambda j, i: (i, j))], tm, 1, fn, side), side)


def _mm_w2(a, w, out_dtype):
    t, kd = a.shape
    n = w.shape[1]
    tm, tn = _tile(t, 512, 8), _tile(n, 1024)

    def fn(rows, r, out):
        out[0][rows, :] = _dot(r[0][rows, :], r[1][...], NN).astype(out_dtype)

    return _gemm_rows(
        "mm_w2", (t // tm, n // tn), [a, w],
        [pl.BlockSpec((tm, kd), lambda i, j: (i, 0)), pl.BlockSpec((kd, tn), lambda i, j: (0, j))],
        [_sds((t, n), out_dtype)], [pl.BlockSpec((tm, tn), lambda i, j: (i, j))], tm, 1, fn)[0]


def _branch_gate(ob, wb3, ya, proj, gate_col):
    t, kd = ob.shape
    n = wb3.shape[2]
    d = N_SHARD * n
    tm = _tile(t, 512, 8)
    ga0, gb0 = gate_col // n, (gate_col + d) // n

    def fn(rows, r, out):
        yb = _dot(r[0][rows, :], r[1][...], NN)
        out[0][rows, :] = yb
        out[1][rows, :] = (_sigmoid(r[3][rows, :]) * r[2][rows, :] + _sigmoid(r[4][rows, :]) * yb).astype(BF16)

    blk = pl.BlockSpec((tm, n), lambda i, j: (i, j))
    return _gemm_rows(
        "branch_gate", (t // tm, N_SHARD), [ob, wb3, ya, proj, proj],
        [pl.BlockSpec((tm, kd), lambda i, j: (i, 0)), pl.BlockSpec((None, kd, n), lambda i, j: (j, 0, 0)), blk,
         pl.BlockSpec((tm, n), lambda i, j: (i, ga0 + j)), pl.BlockSpec((tm, n), lambda i, j: (i, gb0 + j))],
        [_sds((t, d), F32), _sds((t, d), BF16)], [blk, blk], tm, 2, fn)


def _swiglu_bwd(dyb, w, u3, side=None):
    t, d = dyb.shape
    f = w.shape[0]
    tm, tr = _tile(t, 1024, 8), _tile(f, 512)

    def fn(rows, r, out):
        da = _dot(r[0][rows, :], r[1][...], NT)
        g, u = r[2][0, rows, :].astype(F32), r[2][1, rows, :].astype(F32)
        s = _sigmoid(g)
        out[0][0, rows, :] = (da * u * s * (1.0 + g * (1.0 - s))).astype(BF16)
        out[0][1, rows, :] = (da * g * s).astype(BF16)

    ublk = pl.BlockSpec((2, tm, tr), lambda i, j: (0, i, j))
    return _only(_gemm_rows(
        "swiglu_bwd", (t // tm, f // tr), [dyb, w, u3],
        [pl.BlockSpec((tm, d), lambda i, j: (i, 0)), pl.BlockSpec((tr, d), lambda i, j: (j, 0)), ublk],
        [_sds((2, t, f), BF16)], [ublk], tm, 4, fn, side), side)


def _ffn_dx(du3, w3, resid, side=None):
    t = du3.shape[1]
    d, nf = w3.shape[1], w3.shape[2]
    tm, tr = _tile(t, 512, 8), _tile(d, 1024)

    def dot_fn(r, acc):
        acc[0][...] += _dot(r[0][...], r[1][...], NT)

    def epi_fn(r, acc, out):
        out[0][...] = acc[0][...] + r[2][...]

    blk = pl.BlockSpec((tm, tr), lambda i, j, k: (i, j))
    return _only(_gemm(
        "ffn_dx", (t // tm, d // tr, N_SHARD), [du3, w3, resid],
        [pl.BlockSpec((None, tm, nf), lambda i, j, k: (k // 2, i, k % 2)),
         pl.BlockSpec((None, tr, nf), lambda i, j, k: (k, j, 0)), blk],
        [_sds((t, d), F32)], [blk], [(tm, tr)], dot_fn, epi_fn, side), side)


def _nt_w3(dy, w3, resid, side=None):
    t = dy.shape[0]
    kd, n = w3.shape[1], w3.shape[2]
    tm, tr = _tile(t, 512, 8), _tile(kd, 1024)
    has_res = resid is not None

    def dot_fn(r, acc):
        acc[0][...] += _dot(r[0][...], r[1][...], NT)

    def epi_fn(r, acc, out):
        v = acc[0][...]
        if has_res:
            v = v + r[2][...]
        out[0][...] = v

    blk = pl.BlockSpec((tm, tr), lambda i, j, k: (i, j))
    return _only(_gemm(
        "nt_w3", (t // tm, kd // tr, N_SHARD), [dy, w3] + ([resid] if has_res else []),
        [pl.BlockSpec((tm, n), lambda i, j, k: (i, k)), pl.BlockSpec((None, tr, n), lambda i, j, k: (k, j, 0))]
        + ([blk] if has_res else []),
        [_sds((t, kd), F32)], [blk], [(tm, tr)], dot_fn, epi_fn, side), side)


def _nt_w2(dy, w, out_dtype, side=None):
    t, n = dy.shape
    r_ = w.shape[0]
    tm, tr = _tile(t, 512, 8), _tile(r_, 1024)

    def fn(rows, r, out):
        out[0][rows, :] = _dot(r[0][rows, :], r[1][...], NT).astype(out_dtype)

    return _only(_gemm_rows(
        "nt_w2", (t // tm, r_ // tr), [dy, w],
        [pl.BlockSpec((tm, n), lambda i, j: (i, 0)), pl.BlockSpec((tr, n), lambda i, j: (j, 0))],
        [_sds((t, r_), out_dtype)], [pl.BlockSpec((tm, tr), lambda i, j: (i, j))], tm, 1, fn, side), side)


def _gate_bwd(dmix, w, proj, ya, yb, gate_col):
    t, n = dmix.shape
    d = w.shape[0]
    tm, tr = _tile(t, 512, 8), _tile(d, 512)
    ga0, gb0 = gate_col // tr, (gate_col + d) // tr
    nb = d // tr

    def fn(rows, r, out):
        dz = _dot(r[0][rows, :], r[1][...], NT)
        ga, gb = _sigmoid(r[2][rows, :]), _sigmoid(r[3][rows, :])
        out[0][rows, :] = (dz * ga).astype(BF16)
        out[1][rows, :] = (dz * gb).astype(BF16)
        out[2][rows, :] = (dz * r[4][rows, :] * ga * (1.0 - ga)).astype(BF16)
        out[3][rows, :] = (dz * r[5][rows, :] * gb * (1.0 - gb)).astype(BF16)

    blk = pl.BlockSpec((tm, tr), lambda i, j: (i, j))
    return _gemm_rows(
        "gate_bwd", (t // tm, nb), [dmix, w, proj, proj, ya, yb],
        [pl.BlockSpec((tm, n), lambda i, j: (i, 0)), pl.BlockSpec((tr, n), lambda i, j: (j, 0)),
         pl.BlockSpec((tm, tr), lambda i, j: (i, ga0 + j)), pl.BlockSpec((tm, tr), lambda i, j: (i, gb0 + j)),
         blk, blk],
        [_sds((t, d), BF16), _sds((t, d), BF16), _sds((t, d), BF16), _sds((t, d), BF16)],
        [blk, blk, blk, blk], tm, 2, fn)


def _tn_w3(a, dy, dy_map, n, side=None):
    t, kd = a.shape
    tm, tkk = _tile(t, (6 << 20) // n, 8), _tile(kd, 512)

    def dot_fn(r, acc):
        acc[0][...] += _dot(r[0][...], r[1][...], TN)

    def epi_fn(r, acc, out):
        out[0][...] = acc[0][...].astype(BF16)

    dy_block = (tm, n) if dy.ndim == 2 else (None, tm, n)
    return _only(_gemm(
        "tn_w3", (kd // tkk, N_SHARD, t // tm), [a, dy],
        [pl.BlockSpec((tm, tkk), lambda i, j, m: (m, i)), pl.BlockSpec(dy_block, lambda i, j, m: dy_map(j, m))],
        [_sds((N_SHARD, kd, n), BF16)], [pl.BlockSpec((None, tkk, n), lambda i, j, m: (j, i, 0))],
        [(tkk, n)], dot_fn, epi_fn, side), side)


def _tn_w2(a, dy):
    t, kd = a.shape
    n = dy.shape[1]
    tkk, tn = _tile(kd, 512), _tile(n, 2048)
    tm = _tile(t, (4 << 20) // tn, 8)

    def dot_fn(r, acc):
        acc[0][...] += _dot(r[0][...], r[1][...], TN)

    def epi_fn(r, acc, out):
        out[0][...] = acc[0][...].astype(BF16)

    return _gemm(
        "tn_w2", (kd // tkk, n // tn, t // tm), [a, dy],
        [pl.BlockSpec((tm, tkk), lambda i, j, m: (m, i)), pl.BlockSpec((tm, tn), lambda i, j, m: (m, j))],
        [_sds((kd, n), BF16)], [pl.BlockSpec((tkk, tn), lambda i, j, m: (i, j))],
        [(tkk, tn)], dot_fn, epi_fn)[0]


def _ln_bwd(dh, xhat, rstd, g, coef):
    t, d = xhat.shape
    tm = _tile(t, 256, 8)

    def body(dh_ref, xh_ref, rstd_ref, g_ref, dres_ref, dyb_ref, dg_ref, db_ref):
        dh, xh = dh_ref[...], xh_ref[...]
        dxh = dh * g_ref[...]
        m1 = jnp.mean(dxh, axis=-1, keepdims=True)
        m2 = jnp.mean(dxh * xh, axis=-1, keepdims=True)
        dv = rstd_ref[...] * (dxh - m1 - xh * m2)
        dres_ref[...] = ALPHA * dv
        dyb_ref[...] = (coef * dv).astype(BF16)

        @pl.when(pl.program_id(0) == 0)
        def _():
            dg_ref[...] = jnp.zeros(dg_ref.shape, F32)
            db_ref[...] = jnp.zeros(db_ref.shape, F32)

        dg_ref[...] += jnp.sum(dh * xh, axis=0, keepdims=True)
        db_ref[...] += jnp.sum(dh, axis=0, keepdims=True)

    row = pl.BlockSpec((tm, d), lambda i: (i, 0))
    vec = pl.BlockSpec((1, d), lambda i: (0, 0))
    return pl.pallas_call(
        body, name="ln_bwd", grid=(t // tm,),
        in_specs=[row, row, pl.BlockSpec((tm, 1), lambda i: (i, 0)), vec], out_specs=[row, row, vec, vec],
        out_shape=[_sds((t, d), F32), _sds((t, d), BF16), _sds((1, d), F32), _sds((1, d), F32)],
        compiler_params=_params(0, 1))(dh, xhat, rstd, g)


def _cast_bf16(x2d):
    t, d = x2d.shape
    tm = _tile(t, 512, 8)

    def body(x_ref, o_ref):
        o_ref[...] = x_ref[...].astype(BF16)

    row = pl.BlockSpec((tm, d), lambda i: (i, 0))
    return pl.pallas_call(body, name="cast_bf16", grid=(t // tm,), in_specs=[row], out_specs=row,
                          out_shape=_sds((t, d), BF16), compiler_params=_params(1, 0))(x2d)


def _lower_bound(table):
    t0, t1 = table[0:1, :], table[1:2, :]
    m = jnp.maximum(t0, t1)
    e0, e1 = jnp.exp(t0 - m), jnp.exp(t1 - m)
    return e0 / (e0 + e1)


def _chunk_tri(rows, upper):
    r = lax.broadcasted_iota(jnp.int32, (rows, rows), 0)
    s = lax.broadcasted_iota(jnp.int32, (rows, rows), 1)
    shift = HGRN_CHUNK.bit_length() - 1
    same = lax.shift_right_logical(r, shift) == lax.shift_right_logical(s, shift)
    return same & ((r <= s) if upper else (r >= s))


def _tri_apply(x, upper):
    tri = _chunk_tri(x.shape[0], upper).astype(F32).astype(BF16)
    hi = x.astype(BF16)
    r1 = x - hi.astype(F32)
    mid = r1.astype(BF16)
    lo = (r1 - mid.astype(F32)).astype(BF16)
    return _dot(tri, hi, NN) + _dot(tri, mid, NN) + _dot(tri, lo, NN)


@functools.partial(jax.custom_vjp, nondiff_argnums=(1,))
def _chunk_cumsum(x, upper):
    return _tri_apply(x, upper)


def _chunk_cumsum_fwd(x, upper):
    return _tri_apply(x, upper), None


def _chunk_cumsum_bwd(upper, _, g):
    return (_tri_apply(g, not upper),)


_chunk_cumsum.defvjp(_chunk_cumsum_fwd, _chunk_cumsum_bwd)


def _hgrn_blocks(units, uppers):
    c = HGRN_CHUNK
    rows = units[0][0].shape[0]
    n_sub = rows // c
    ids = range(len(units))
    chunk = lax.shift_right_logical(lax.broadcasted_iota(jnp.int32, (rows, HEAD), 0), c.bit_length() - 1)
    zero = jnp.zeros((rows, HEAD), BF16)

    def expand(a):
        return jnp.concatenate([jnp.where(chunk == n, a, zero) for n in range(n_sub)], axis=1)

    fs = [u[3] + (1.0 - u[3]) * _sigmoid(u[1]) for u in units]
    lgs = [jnp.log(f) for f in fs]
    cums = [_chunk_cumsum(lgs[i], uppers[i]) for i in ids]
    tots = [[jnp.sum(lg[n * c:(n + 1) * c], axis=0, keepdims=True) for n in range(n_sub)] for lg in lgs]
    qd, kd, ke, vb = [], [], [], []
    for i in ids:
        totb = jnp.concatenate([jnp.broadcast_to(t, (c, HEAD)) for t in tots[i]], axis=0)
        kk = 1.0 - fs[i]
        qd.append((_silu(units[i][0]) * jnp.exp(cums[i])).astype(BF16))
        kd.append((kk * jnp.exp(-cums[i])).astype(BF16))
        ke.append((kk * jnp.exp(totb - cums[i])).astype(BF16))
        vb.append(units[i][2].astype(BF16))
    scores = [_dot(qd[i], kd[i], NT) for i in ids]
    kvs = [_dot(vb[i], expand(ke[i]), TN) for i in ids]
    outs = []
    for i in ids:
        a = jnp.where(_chunk_tri(rows, uppers[i]), scores[i], 0.0).astype(BF16)
        st = units[i][4]
        entering = [None] * n_sub
        for n in (range(n_sub - 1, -1, -1) if uppers[i] else range(n_sub)):
            entering[n] = st.astype(BF16)
            st = st * jnp.exp(tots[i][n]) + kvs[i][:, n * HEAD:(n + 1) * HEAD]
        outs.append((a, jnp.concatenate(entering, axis=1), st))
    res = []
    for i in ids:
        a, entering, st = outs[i]
        res.append((_dot(a, vb[i], NN) + _dot(expand(qd[i]), entering, NT), st))
    return res


def _hgrn_fwd(proj3, lb_f, lb_b, norm_g, d, side=None):
    b_, s_, _ = proj3.shape
    nh = d // HEAD
    hb = HGRN_FWD_HEADS
    wid = hb * HEAD
    rows = HGRN_CHUNK * HGRN_SUB
    n_blk = s_ // rows

    def body(hq_ref, hff_ref, hfb_ref, hi_ref, hog_ref, lbf_ref, lbb_ref, g_ref, oraw_ref, out_ref, of_ref, ob_ref):
        lbf, lbb = _lower_bound(lbf_ref[...]), _lower_bound(lbb_ref[...])

        def step(j, sts):
            rf = pl.ds(pl.multiple_of(j * rows, rows), rows)
            rb = pl.ds(pl.multiple_of((n_blk - 1 - j) * rows, rows), rows)
            units = []
            for hh in range(hb):
                cs = slice(hh * HEAD, (hh + 1) * HEAD)
                units.append((hq_ref[rf, cs], hff_ref[rf, cs], hi_ref[rf, cs], lbf[:, cs], sts[2 * hh]))
                units.append((hq_ref[rb, cs], hfb_ref[rb, cs], hi_ref[rb, cs], lbb[:, cs], sts[2 * hh + 1]))
            res = _hgrn_blocks(units, [False, True] * hb)
            for hh in range(hb):
                cs = slice(hh * HEAD, (hh + 1) * HEAD)
                of_ref[rf, cs] = res[2 * hh][0]
                ob_ref[rb, cs] = res[2 * hh + 1][0]
            return tuple(r[1] for r in res)

        z = jnp.zeros((HEAD, HEAD), F32)
        lax.fori_loop(0, n_blk, step, (z,) * (2 * hb))
        for hh in range(hb):
            cs = slice(hh * HEAD, (hh + 1) * HEAD)
            o = of_ref[:, cs] + ob_ref[:, cs]
            oraw_ref[:, cs] = o
            on = o * lax.rsqrt(jnp.mean(o * o, axis=-1, keepdims=True) + LN_EPS)
            out_ref[:, cs] = (on * g_ref[:, cs] * _silu(hog_ref[:, cs])).astype(BF16)

    def col(k):
        return pl.BlockSpec((None, s_, wid), lambda h, b: (b, 0, k * (nh // hb) + h))

    tab = pl.BlockSpec((2, wid), lambda h, b: (0, h))
    oblk = pl.BlockSpec((None, s_, wid), lambda h, b: (b, 0, h))
    return _run(
        body, "hgrn_fwd", (nh // hb, b_), [proj3, proj3, proj3, proj3, proj3, lb_f, lb_b, norm_g],
        [col(0), col(1), col(2), col(3), col(4), tab, tab, pl.BlockSpec((1, wid), lambda h, b: (0, h))],
        [_sds((b_, s_, d), F32), _sds((b_, s_, d), BF16)], [oblk, oblk],
        [pltpu.VMEM((s_, wid), F32), pltpu.VMEM((s_, wid), F32)], 0, side)


def _hgrn_bwd(proj3, lb_f, lb_b, norm_g, o_raw, do_a, d, side=None):
    b_, s_, _ = proj3.shape
    nh = d // HEAD
    hb = HGRN_BWD_HEADS
    wid = hb * HEAD
    rows = HGRN_CHUNK * HGRN_SUB
    n_blk = s_ // rows

    def body(hq_ref, hff_ref, hfb_ref, hi_ref, hog_ref, lbf_ref, lbb_ref, g_ref, oraw_ref, doa_ref,
             dq_ref, dff_ref, dfb_ref, di_ref, dog_ref, dlbf_ref, dlbb_ref, dg_ref,
             st_ref, dor_ref, dq2_ref, di2_ref):
        b = pl.program_id(1)
        tab_f, tab_b = lbf_ref[...], lbb_ref[...]
        lbf, lbb = _lower_bound(tab_f), _lower_bound(tab_b)

        dg_parts = []
        for hh in range(hb):
            cs = slice(hh * HEAD, (hh + 1) * HEAD)
            o, doa, hog, g = oraw_ref[:, cs], doa_ref[:, cs], hog_ref[:, cs], g_ref[:, cs]
            rs = lax.rsqrt(jnp.mean(o * o, axis=-1, keepdims=True) + LN_EPS)
            on = o * rs
            sg = _sigmoid(hog)
            gate = hog * sg
            dog_ref[:, cs] = (doa * on * g * sg * (1.0 + hog * (1.0 - sg))).astype(BF16)
            don = doa * g * gate
            dor_ref[:, cs] = rs * (don - on * jnp.mean(don * on, axis=-1, keepdims=True))
            dg_parts.append(jnp.sum(doa * on * gate, axis=0, keepdims=True))

        def fwd_step(j, carry):
            jb = n_blk - 1 - j
            rf = pl.ds(pl.multiple_of(j * rows, rows), rows)
            rb = pl.ds(pl.multiple_of(jb * rows, rows), rows)
            units = []
            for hh in range(hb):
                cs = slice(hh * HEAD, (hh + 1) * HEAD)
                st_ref[2 * hh, j] = carry[2 * hh]
                st_ref[2 * hh + 1, jb] = carry[2 * hh + 1]
                units.append((hq_ref[rf, cs], hff_ref[rf, cs], hi_ref[rf, cs], lbf[:, cs], carry[2 * hh]))
                units.append((hq_ref[rb, cs], hfb_ref[rb, cs], hi_ref[rb, cs], lbb[:, cs], carry[2 * hh + 1]))
            return tuple(r[1] for r in _hgrn_blocks(units, uppers))

        uppers = [False, True] * hb
        z = jnp.zeros((HEAD, HEAD), F32)
        lax.fori_loop(0, n_blk, fwd_step, (z,) * (2 * hb))

        def bwd_step(j, carry):
            gs, dls = carry
            jf = n_blk - 1 - j
            rf = pl.ds(pl.multiple_of(jf * rows, rows), rows)
            rb = pl.ds(pl.multiple_of(j * rows, rows), rows)
            flat, cots = [], []
            for hh in range(hb):
                cs = slice(hh * HEAD, (hh + 1) * HEAD)
                flat += [hq_ref[rf, cs], hff_ref[rf, cs], hi_ref[rf, cs], lbf[:, cs], st_ref[2 * hh, jf],
                         hq_ref[rb, cs], hfb_ref[rb, cs], hi_ref[rb, cs], lbb[:, cs], st_ref[2 * hh + 1, j]]
                cots += [(dor_ref[rf, cs], gs[2 * hh]), (dor_ref[rb, cs], gs[2 * hh + 1])]
            _, vjp = jax.vjp(lambda *a: _hgrn_blocks([a[5 * i:5 * i + 5] for i in range(2 * hb)], uppers), *flat)
            grads = vjp(cots)
            gs, dls = list(gs), list(dls)
            for hh in range(hb):
                cs = slice(hh * HEAD, (hh + 1) * HEAD)
                dq, df, di, dl, gs[2 * hh] = grads[10 * hh:10 * hh + 5]
                dq_ref[rf, cs] = dq.astype(BF16)
                dff_ref[rf, cs] = df.astype(BF16)
                di_ref[rf, cs] = di.astype(BF16)
                dls[2 * hh] = dls[2 * hh] + dl
                dq, df, di, dl, gs[2 * hh + 1] = grads[10 * hh + 5:10 * hh + 10]
                dq2_ref[rb, cs] = dq
                dfb_ref[rb, cs] = df.astype(BF16)
                di2_ref[rb, cs] = di
                dls[2 * hh + 1] = dls[2 * hh + 1] + dl
            return tuple(gs), tuple(dls)

        zl = jnp.zeros((1, HEAD), F32)
        _, dls = lax.fori_loop(0, n_blk, bwd_step, ((z,) * (2 * hb), (zl,) * (2 * hb)))
        dq_ref[...] = (dq_ref[...].astype(F32) + dq2_ref[...]).astype(BF16)
        di_ref[...] = (di_ref[...].astype(F32) + di2_ref[...]).astype(BF16)

        _, vjp_tf = jax.vjp(_lower_bound, tab_f)
        _, vjp_tb = jax.vjp(_lower_bound, tab_b)

        @pl.when(b == 0)
        def _():
            dlbf_ref[...] = jnp.zeros(dlbf_ref.shape, F32)
            dlbb_ref[...] = jnp.zeros(dlbb_ref.shape, F32)
            dg_ref[...] = jnp.zeros(dg_ref.shape, F32)

        dlbf_ref[...] += vjp_tf(jnp.concatenate(dls[0::2], axis=1))[0]
        dlbb_ref[...] += vjp_tb(jnp.concatenate(dls[1::2], axis=1))[0]
        dg_ref[...] += jnp.concatenate(dg_parts, axis=1)

    def col(k):
        return pl.BlockSpec((None, s_, wid), lambda h, b: (b, 0, k * (nh // hb) + h))

    tab = pl.BlockSpec((2, wid), lambda h, b: (0, h))
    vec = pl.BlockSpec((1, wid), lambda h, b: (0, h))
    oblk = pl.BlockSpec((None, s_, wid), lambda h, b: (b, 0, h))
    seq = _sds((b_, s_, d), BF16)
    return _run(
        body, "hgrn_bwd", (nh // hb, b_), [proj3, proj3, proj3, proj3, proj3, lb_f, lb_b, norm_g, o_raw, do_a],
        [col(0), col(1), col(2), col(3), col(4), tab, tab, vec, oblk, oblk],
        [seq, seq, seq, seq, seq, _sds((2, d), F32), _sds((2, d), F32), _sds((1, d), F32)],
        [oblk, oblk, oblk, oblk, oblk, tab, tab, vec],
        [pltpu.VMEM((2 * hb, n_blk, HEAD, HEAD), F32),
         pltpu.VMEM((s_, wid), F32), pltpu.VMEM((s_, wid), F32), pltpu.VMEM((s_, wid), F32)], 1, side)


def _rope_tables(s_):
    inv = ROPE_THETA ** (-jnp.arange(0, ROPE_DIM, 2, dtype=F32) / ROPE_DIM)
    ang = jnp.arange(s_, dtype=F32)[:, None] * inv
    cos, sin = jnp.cos(ang), jnp.sin(ang)
    rest = HEAD - ROPE_DIM
    ctab = jnp.concatenate([cos, cos, jnp.ones((s_, rest), F32)], axis=1)
    stab = jnp.concatenate([-sin, sin, jnp.zeros((s_, rest), F32)], axis=1)
    half = ROPE_DIM // 2
    perm = np.zeros((HEAD, HEAD), np.float32)
    for i in range(half):
        perm[i + half, i] = 1.0
        perm[i, i + half] = 1.0
    return ctab, stab, jnp.asarray(perm)


ATTN_GROUP = 8


def _attn_tiles(qkv, tabs, offs, perm, half):
    hi = lax.Precision.HIGHEST
    ids = range(len(qkv))
    qrot = [_dot(qkv[i][0], perm, NN, precision=hi) for i in ids]
    krot = [_dot(qkv[i][1], perm, NN, precision=hi) for i in ids]
    qs = [(qkv[i][0] * tabs[i][0] + qrot[i] * tabs[i][1]).astype(BF16) for i in ids]
    ks = [(qkv[i][1] * tabs[i][2] + krot[i] * tabs[i][3]).astype(BF16) for i in ids]
    ss = [_dot(qs[i], ks[i], NT) * (HEAD ** -0.5) for i in ids]
    ps, dens, lses = [], [], []
    for i in ids:
        qi = offs[i][0] + lax.broadcasted_iota(jnp.int32, ss[i].shape, 0)
        kj = offs[i][1] + lax.broadcasted_iota(jnp.int32, ss[i].shape, 1)
        s = jnp.where(jnp.abs(qi - kj) <= half, ss[i], NEG_INF)
        m = lax.stop_gradient(jnp.max(s, axis=-1, keepdims=True))
        p = jnp.exp(s - m)
        den = jnp.sum(p, axis=-1, keepdims=True)
        ps.append(p.astype(BF16))
        dens.append(den)
        lses.append(m + jnp.log(den))
    res = []
    for i in ids:
        o = _dot(ps[i], qkv[i][2].astype(BF16), NN) / dens[i]
        res.append((o, jnp.broadcast_to(lses[i], o.shape)))
    return res


def _attn_tiling(seg):
    tq = seg if seg <= 256 else 256
    kw = seg if seg <= 512 else 512
    tiles = []
    for i in range(seg // tq):
        ws = min(max(i * tq - (kw - tq) // 2, 0), seg - kw)
        tiles.append((i * tq, ws))
    return tq, kw, tiles


def _attn_specs(s_, d, g):
    c0 = (5 * d + g * 3 * ATTN_OUT) // HEAD

    def col(part):
        return pl.BlockSpec((None, s_, HEAD), lambda b, h: (b, 0, c0 + part * ATTN_HEADS + h))

    tab = pl.BlockSpec((s_, HEAD), lambda b, h: (0, 0))
    perm = pl.BlockSpec((HEAD, HEAD), lambda b, h: (0, 0))
    oblk = pl.BlockSpec((None, s_, HEAD), lambda b, h: (b, 0, h))
    return col, tab, perm, oblk


def _rows(r, first, count, dil):
    return pl.ds(r + dil * first, count, stride=dil) if dil > 1 else pl.ds(first, count)


def _attn_operands(group, dil, tq, kw, q_ref, k_ref, v_ref, c_ref, s_ref):
    qkv, tabs = [], []
    for r, q0, k0 in group:
        rq, rk = _rows(r, q0, tq, dil), _rows(r, k0, kw, dil)
        qkv.append((q_ref[rq, :], k_ref[rk, :], v_ref[rk, :]))
        tabs.append((c_ref[rq, :], s_ref[rq, :], c_ref[rk, :], s_ref[rk, :]))
    return qkv, tabs


def _attn_fwd(proj3, ctab, stab, perm, d, g):
    window, dil = ATTN_GROUPS[g]
    b_, s_, _ = proj3.shape
    seg = s_ // dil
    half = window // (2 * dil)
    tq, kw, tiles = _attn_tiling(seg)

    work = [(r, q0, k0) for r in range(dil) for q0, k0 in tiles]

    def body(q_ref, k_ref, v_ref, c_ref, s_ref, p_ref, o_ref, l_ref):
        pm = p_ref[...]
        for g0 in range(0, len(work), ATTN_GROUP):
            group = work[g0:g0 + ATTN_GROUP]
            qkv, tabs = _attn_operands(group, dil, tq, kw, q_ref, k_ref, v_ref, c_ref, s_ref)
            res = _attn_tiles(qkv, tabs, [(q0, k0) for _, q0, k0 in group], pm, half)
            for (r, q0, _), (o, l) in zip(group, res):
                o_ref[_rows(r, q0, tq, dil), :] = o
                l_ref[_rows(r, q0, tq, dil), :] = l

    col, tab, pspec, oblk = _attn_specs(s_, d, g)
    shp = _sds((b_, s_, ATTN_OUT), F32)
    return pl.pallas_call(
        body, name=f"attn_fwd_d{dil}", grid=(b_, ATTN_HEADS), in_specs=[col(0), col(1), col(2), tab, tab, pspec],
        out_specs=[oblk, oblk], out_shape=[shp, shp], compiler_params=_params(2, 0))(
            proj3, proj3, proj3, ctab, stab, perm)


def _attn_bwd(proj3, ctab, stab, perm, do, dl, d, g):
    window, dil = ATTN_GROUPS[g]
    b_, s_, _ = proj3.shape
    seg = s_ // dil
    half = window // (2 * dil)
    tq, kw, tiles = _attn_tiling(seg)

    work = [(r, q0, k0) for r in range(dil) for q0, k0 in tiles]

    def body(q_ref, k_ref, v_ref, c_ref, s_ref, p_ref, do_ref, dl_ref, dq_ref, dk_ref, dv_ref, dq_s, dk_s, dv_s):
        pm = p_ref[...]
        dk_s[...] = jnp.zeros(dk_s.shape, F32)
        dv_s[...] = jnp.zeros(dv_s.shape, F32)
        for g0 in range(0, len(work), ATTN_GROUP):
            group = work[g0:g0 + ATTN_GROUP]
            qkv, tabs = _attn_operands(group, dil, tq, kw, q_ref, k_ref, v_ref, c_ref, s_ref)
            offs = [(q0, k0) for _, q0, k0 in group]
            _, vjp = jax.vjp(
                lambda *a: _attn_tiles([a[3 * i:3 * i + 3] for i in range(len(group))], tabs, offs, pm, half),
                *[x for tile in qkv for x in tile])
            grads = vjp([(do_ref[_rows(r, q0, tq, dil), :], dl_ref[_rows(r, q0, tq, dil), :]) for r, q0, _ in group])
            for i, (r, q0, k0) in enumerate(group):
                rk = _rows(r, k0, kw, dil)
                dq_s[_rows(r, q0, tq, dil), :] = grads[3 * i]
                dk_s[rk, :] += grads[3 * i + 1]
                dv_s[rk, :] += grads[3 * i + 2]
        dq_ref[...] = dq_s[...].astype(BF16)
        dk_ref[...] = dk_s[...].astype(BF16)
        dv_ref[...] = dv_s[...].astype(BF16)

    col, tab, pspec, oblk = _attn_specs(s_, d, g)
    shp = _sds((b_, s_, ATTN_OUT), BF16)
    full = pltpu.VMEM((s_, HEAD), F32)
    return pl.pallas_call(
        body, name=f"attn_bwd_d{dil}", grid=(b_, ATTN_HEADS),
        in_specs=[col(0), col(1), col(2), tab, tab, pspec, oblk, oblk], out_specs=[oblk, oblk, oblk],
        out_shape=[shp, shp, shp], scratch_shapes=[full, full, full],
        compiler_params=_params(2, 0))(proj3, proj3, proj3, ctab, stab, perm, do, dl)


def _combine(os_, ls_):
    m = jnp.maximum(jnp.maximum(ls_[0], ls_[1]), ls_[2])
    es = [jnp.exp(l - m) for l in ls_]
    return (es[0] * os_[0] + es[1] * os_[1] + es[2] * os_[2]) / (es[0] + es[1] + es[2])


def _combine_fwd(os_, ls_):
    t, w = os_[0].shape
    tm = _tile(t, 512, 8)

    def body(*refs):
        refs[6][...] = _combine([r[...] for r in refs[:3]], [r[...] for r in refs[3:6]]).astype(BF16)

    row = pl.BlockSpec((tm, w), lambda i: (i, 0))
    return pl.pallas_call(body, name="combine_fwd", grid=(t // tm,), in_specs=[row] * 6, out_specs=row,
                          out_shape=_sds((t, w), BF16), compiler_params=_params(1, 0))(*os_, *ls_)


def _combine_bwd(os_, ls_, dob):
    t, w = os_[0].shape
    tm = _tile(t, 512, 8)

    def body(*refs):
        _, vjp = jax.vjp(lambda *a: _combine(a[:3], a[3:]), *[r[...] for r in refs[:6]])
        for r, g in zip(refs[7:], vjp(refs[6][...])):
            r[...] = g

    row = pl.BlockSpec((tm, w), lambda i: (i, 0))
    return pl.pallas_call(body, name="combine_bwd", grid=(t // tm,), in_specs=[row] * 7, out_specs=[row] * 6,
                          out_shape=[_sds((t, w), F32)] * 6, compiler_params=_params(1, 0))(*os_, *ls_, dob)


ROW_SHARDED = ("ffn1_out", "ffn2_out", "wa", "mix_out")


def _local_step(x, target, fulls, p, sc_arr):
    b_, s_, d = x.shape
    t = b_ * s_
    x2 = x.reshape(t, d)
    gate_col = 5 * d + QKV_W
    c_arr = sc_arr[1:2]
    w = {}

    def arrived(keys, arrays):
        for k, a in zip(keys, arrays):
            w[k] = a.reshape(a.shape[0] * a.shape[1], a.shape[2]) if k in ROW_SHARDED else a

    def reduce_begin(keys):
        g3 = [gw[k].reshape(fulls[k].shape) for k in keys]
        return [_sum_halves(a, b, c_arr) for a, b in zip(g3, _swap_halves(g3))]

    def reduce_end(keys, parts, got):
        for k, a, b in zip(keys, parts, got):
            reduced[k] = _sum_chips(a, b, sc_arr)

    arrived(["ffn1_in"], _gather_weights([fulls["ffn1_in"]]))
    xb = _cast_bf16(x2)
    mix_rows = fulls["mix_in"].shape[1]
    (u1, a1), got = _ffn_up(xb, w["ffn1_in"], _gather_exchange([fulls["ffn1_out"], fulls["mix_in"]],
                                                              [None, (0, mix_rows // 2)]))
    arrived(["ffn1_out"], got[:1])
    (h1, h1b, xh1, rs1), got = _down_ln(a1, w["ffn1_out"], x2, 0.5, p["ln1_g"], p["ln1_b"],
                                        _gather_exchange(got[1:], [(mix_rows // 2, mix_rows // 2)]))
    arrived(["mix_in"], got)

    keys = ["wa", "wb", "mix_out", "ffn2_in"]
    proj, got = _mm_w3(h1b, w["mix_in"], _gather_exchange([fulls[k] for k in keys]))
    arrived(keys, got)
    n_in = proj.shape[1]
    proj3 = proj.reshape(b_, s_, n_in)
    (o_raw, oa), got = _hgrn_fwd(proj3, p["hgrn_lb_fwd"], p["hgrn_lb_bwd"], p["hgrn_norm_g"], d,
                                 _gather_exchange([fulls["ffn2_out"]]))
    arrived(["ffn2_out"], got)
    oa2 = oa.reshape(t, d)

    ctab, stab, perm = _rope_tables(s_)
    os_, ls_ = [], []
    for g in range(N_GROUPS):
        o_g, l_g = _attn_fwd(proj3, ctab, stab, perm, d, g)
        os_.append(o_g.reshape(t, ATTN_OUT))
        ls_.append(l_g.reshape(t, ATTN_OUT))
    ob = _combine_fwd(os_, ls_)

    ya = _mm_w2(oa2, w["wa"], F32)
    yb, zb = _branch_gate(ob, w["wb"], ya, proj, gate_col)
    h2, h2b, xh2, rs2 = _down_ln(zb, w["mix_out"], h1, 1.0, p["ln2_g"], p["ln2_b"])

    u2, a2 = _ffn_up(h2b, w["ffn2_in"])

    gw, gp, reduced = {}, {}, {}
    nf = w["ffn2_in"].shape[2]

    def du_map(j, m):
        return (j // 2, m, j % 2)

    dres3, dy3, gp["ln3_g"], gp["ln3_b"], loss = _down_ln_loss(
        a2, w["ffn2_out"], h2, 0.5, p["ln3_g"], p["ln3_b"], target.reshape(t, d))
    du2 = _swiglu_bwd(dy3, w["ffn2_out"], u2)
    g_out = _tn_w2(a2, dy3).reshape(fulls["ffn2_out"].shape)
    g_in, got = _tn_w3(h2b, du2, du_map, nf, _swap_exchange([g_out]))
    parts_a = [_sum_halves(g_out, got[0], c_arr)]
    dh2, got = _ffn_dx(du2, w["ffn2_in"], dres3, _both(_scatter_exchange(parts_a), _swap_exchange([g_in])))
    reduce_end(["ffn2_out"], parts_a, got[:1])
    parts_a.append(_sum_halves(g_in, got[1], c_arr))

    dres2, dmix, gp["ln2_g"], gp["ln2_b"] = _ln_bwd(dh2, xh2, rs2, p["ln2_g"], 1.0)
    dya, dyb, dpga, dpgb = _gate_bwd(dmix, w["mix_out"], proj, ya, yb, gate_col)
    g_mo = _tn_w2(zb, dmix).reshape(fulls["mix_out"].shape)
    do_a, got_mo = _nt_w2(dya, w["wa"], F32, _swap_exchange([g_mo]))
    g_wa = _tn_w2(oa2, dya).reshape(fulls["wa"].shape)
    nb = w["wb"].shape[2]
    do_b, got_wa = _nt_w3(dyb, w["wb"], None, _swap_exchange([g_wa]))
    gw["wb"] = _tn_w3(ob, dyb, lambda j, m: (m, j), nb)
    keys = ["ffn2_in", "mix_out", "wa", "wb"]
    parts_b = parts_a[1:] + [_sum_halves(g_mo, got_mo[0], c_arr), _sum_halves(g_wa, got_wa[0], c_arr)]
    parts_b += reduce_begin(["wb"])

    (dq, dff, dfb, di, dog, gp["hgrn_lb_fwd"], gp["hgrn_lb_bwd"], gp["hgrn_norm_g"]), got = _hgrn_bwd(
        proj3, p["hgrn_lb_fwd"], p["hgrn_lb_bwd"], p["hgrn_norm_g"], o_raw, do_a.reshape(b_, s_, d), d,
        _scatter_exchange(parts_b))
    reduce_end(keys, parts_b, got)

    douts = _combine_bwd(os_, ls_, do_b)
    dqkv = []
    for g in range(N_GROUPS):
        grads_g = _attn_bwd(proj3, ctab, stab, perm, douts[g].reshape(b_, s_, ATTN_OUT),
                            douts[3 + g].reshape(b_, s_, ATTN_OUT), d, g)
        dqkv += [a.reshape(t, ATTN_OUT) for a in grads_g]

    dproj = jnp.concatenate(
        [a.reshape(t, d) for a in (dq, dff, dfb, di, dog)] + dqkv + [dpga, dpgb], axis=1)
    nm = w["mix_in"].shape[2]
    gw["mix_in"] = _tn_w3(h1b, dproj, lambda j, m: (m, j), nm)
    parts_c = reduce_begin(["mix_in"])
    cut = 3 * parts_c[0].shape[1] // 4
    dh1, got_c = _nt_w3(dproj, w["mix_in"], dres2, _scatter_exchange(parts_c, [(0, cut)]))

    dres1, dy1, gp["ln1_g"], gp["ln1_b"] = _ln_bwd(dh1, xh1, rs1, p["ln1_g"], 0.5)
    g_out = _tn_w2(a1, dy1).reshape(fulls["ffn1_out"].shape)
    du1, got = _swiglu_bwd(dy1, w["ffn1_out"], u1, _both(
        _swap_exchange([g_out]), _scatter_exchange(parts_c, [(cut, parts_c[0].shape[1] - cut)], got_c)))
    reduce_end(["mix_in"], parts_c, got[1:])
    parts_d = [_sum_halves(g_out, got[0], c_arr)]
    gw["ffn1_in"], got = _tn_w3(xb, du1, du_map, nf, _scatter_exchange(parts_d))
    reduce_end(["ffn1_out"], parts_d, got)
    parts_e = reduce_begin(["ffn1_in"])
    dx, got = _ffn_dx(du1, w["ffn1_in"], dres1, _scatter_exchange(parts_e))
    reduce_end(["ffn1_in"], parts_e, got)

    keys = list(reduced)
    grads = dict(zip(keys, _join_halves([reduced[k] for k in keys])))
    return loss, dx.reshape(b_, s_, d), grads, gp


MESH = pl.DeviceIdType.MESH
ANY = pl.BlockSpec(memory_space=pl.ANY)


def _place():
    x, y, c = lax.axis_index("x"), lax.axis_index("y"), lax.axis_index("c")
    chips = [(1 - x, y), (x, 1 - y), (1 - x, 1 - y)]
    return x, y, c, chips, (x, y, 1 - c)


def _half_rows(c, rows):
    hr = rows // 2
    return pl.ds(pl.multiple_of(c * hr, 16), hr)


def _remote(src, dst, send, recv, dev):
    return pltpu.make_async_remote_copy(src_ref=src, dst_ref=dst, send_sem=send, recv_sem=recv,
                                        device_id=dev, device_id_type=MESH)


def _gather_weights(fulls):
    n = len(fulls)

    def body(*refs):
        _gather_start(refs[n:2 * n], refs[2 * n:])
        _gather_finish(refs[n:2 * n], refs[2 * n:])

    return pl.pallas_call(
        body, name="gather_weights", in_specs=[ANY] * n, out_specs=[ANY] * n,
        out_shape=[_sds(a.shape, a.dtype) for a in fulls], input_output_aliases={i: i for i in range(n)},
        scratch_shapes=_gather_sems(n))(*fulls)


def _gather_sems(n):
    return [pltpu.SemaphoreType.DMA((n, 3)) for _ in range(4)]


def _span_half(c, span, rows):
    r0, cnt = (0, rows) if span is None else span
    return pl.ds(pl.multiple_of(r0 + c * (cnt // 2), 16), cnt // 2)


def _gather_start(bufs, sems, spans=None):
    isend, irecv = sems[0], sems[1]
    x, y, c, chips, sib = _place()
    for i, buf in enumerate(bufs):
        blk = buf.at[2 * x + y, _span_half(c, spans and spans[i], buf.shape[1])]
        for k, chip in enumerate(chips):
            _remote(blk, blk, isend.at[i, k], irecv.at[i, k], (*chip, c)).start()


def _gather_finish(bufs, sems, spans=None):
    isend, irecv, fsend, frecv = sems
    x, y, c, chips, sib = _place()
    for i, buf in enumerate(bufs):
        mine = _span_half(c, spans and spans[i], buf.shape[1])
        for k, chip in enumerate(chips):
            blk = buf.at[2 * chip[0] + chip[1], mine]
            _remote(blk, blk, isend.at[i, k], irecv.at[i, k], (*chip, c)).wait_recv()
            _remote(blk, blk, fsend.at[i, k], frecv.at[i, k], sib).start()
    for i, buf in enumerate(bufs):
        span = spans and spans[i]
        mine, other = _span_half(c, span, buf.shape[1]), _span_half(1 - c, span, buf.shape[1])
        own = buf.at[2 * x + y, mine]
        for k, chip in enumerate(chips):
            got = buf.at[2 * chip[0] + chip[1], other]
            _remote(got, got, fsend.at[i, k], frecv.at[i, k], sib).wait_recv()
            _remote(own, own, isend.at[i, k], irecv.at[i, k], (*chip, c)).wait_send()
            blk = buf.at[2 * chip[0] + chip[1], mine]
            _remote(blk, blk, fsend.at[i, k], frecv.at[i, k], sib).wait_send()


def _gather_exchange(fulls, spans=None):
    n = len(fulls)
    return _Exchange(fulls, [_sds(a.shape, a.dtype) for a in fulls], {i: i for i in range(n)}, _gather_sems(n),
                     lambda ins, outs, sems: _gather_start(outs, sems, spans),
                     lambda ins, outs, sems: _gather_finish(outs, sems, spans))


def _swap_halves(grads):
    side = _swap_exchange(grads)
    n = len(grads)

    def body(*refs):
        side.start(refs[:n], refs[n:2 * n], refs[2 * n:])
        side.finish(refs[:n], refs[n:2 * n], refs[2 * n:])

    return pl.pallas_call(
        body, name="swap_halves", in_specs=[ANY] * n, out_specs=[ANY] * n, out_shape=side.outs,
        scratch_shapes=side.sems)(*grads)


def _swap_exchange(grads):
    n = len(grads)

    def copies(ins, outs, sems):
        x, y, c, chips, sib = _place()
        return [_remote(a.at[:, _half_rows(1 - c, a.shape[1])], b, sems[0].at[i], sems[1].at[i], sib)
                for i, (a, b) in enumerate(zip(ins, outs))]

    def start(ins, outs, sems):
        for cp in copies(ins, outs, sems):
            cp.start()

    def finish(ins, outs, sems):
        for cp in copies(ins, outs, sems):
            cp.wait()

    return _Exchange(grads, [_sds((N_SHARD, a.shape[1] // 2, a.shape[2]), a.dtype) for a in grads], {},
                     [pltpu.SemaphoreType.DMA((n,)) for _ in range(2)], start, finish)


def _both(a, b):
    i, o, s = len(a.ins), len(a.outs), len(a.sems)

    def start(ins, outs, sems):
        a.start(ins[:i], outs[:o], sems[:s])
        b.start(ins[i:], outs[o:], sems[s:])

    def finish(ins, outs, sems):
        a.finish(ins[:i], outs[:o], sems[:s])
        b.finish(ins[i:], outs[o:], sems[s:])

    aliases = dict(a.aliases)
    aliases.update({i + k: o + v for k, v in b.aliases.items()})
    return _Exchange(a.ins + b.ins, a.outs + b.outs, aliases, a.sems + b.sems, start, finish)


def _scatter_exchange(parts, spans=None, into=None):
    n = len(parts)

    def copies(ins, outs, sems):
        x, y, c, chips, sib = _place()
        cps = []
        for i in range(n):
            rows = pl.ds(*spans[i]) if spans and spans[i] else slice(None)
            for k, chip in enumerate(chips):
                cps.append(_remote(ins[i].at[2 * chip[0] + chip[1], rows], outs[i].at[k, rows],
                                   sems[0].at[i, k], sems[1].at[i, k], (*chip, c)))
        return cps

    def start(ins, outs, sems):
        for cp in copies(ins, outs, sems):
            cp.start()

    def finish(ins, outs, sems):
        for cp in copies(ins, outs, sems):
            cp.wait()

    return _Exchange(list(parts) + list(into or []), [_sds((3,) + a.shape[1:], a.dtype) for a in parts],
                     {n + i: i for i in range(n)} if into else {},
                     [pltpu.SemaphoreType.DMA((n, 3)) for _ in range(2)], start, finish)


def _join_halves(grads):
    n = len(grads)

    def body(*refs):
        bufs, send, recv = refs[n:2 * n], refs[2 * n], refs[2 * n + 1]
        x, y, c, chips, sib = _place()
        cps = []
        for i in range(n):
            blk = bufs[i].at[_half_rows(c, bufs[i].shape[0])]
            other = bufs[i].at[_half_rows(1 - c, bufs[i].shape[0])]
            cp = _remote(blk, blk, send.at[i], recv.at[i], sib)
            cp.start()
            cps.append((cp, _remote(other, other, send.at[i], recv.at[i], sib)))
        for cp, got in cps:
            cp.wait_send()
            got.wait_recv()

    dma = pltpu.SemaphoreType.DMA
    return pl.pallas_call(
        body, name="join_halves", in_specs=[ANY] * n, out_specs=[ANY] * n,
        out_shape=[_sds(a.shape, a.dtype) for a in grads], input_output_aliases={i: i for i in range(n)},
        scratch_shapes=[dma((n,)), dma((n,))])(*grads)


def _gather_rows(block):
    m_per, n = block.shape

    def body(x_ref, out_ref, send_sems, recv_sems, local_sem):
        x, y, c, chips, sibling = _place()
        me = (x, y, c)

        def rows(px, py, pc):
            return out_ref.at[pl.ds((4 * px + 2 * py + pc) * m_per, m_per), :]

        def copy(k, blk, to, src=None):
            return _remote(rows(*blk) if src is None else src, rows(*blk), send_sems.at[k], recv_sems.at[k], to)

        mine = pltpu.make_async_copy(x_ref, rows(*me), local_sem)
        mine.start()
        first = [copy(0, me, sibling, src=x_ref)]
        first += [copy(1 + j, me, (*chip, c), src=x_ref) for j, chip in enumerate(chips)]
        for cp in first:
            cp.start()
        passed = [copy(4 + j, (*chip, c), sibling) for j, chip in enumerate(chips)]
        for j, chip in enumerate(chips):
            copy(1 + j, (*chip, c), me).wait_recv()
            passed[j].start()
        copy(0, sibling, me).wait_recv()
        for j, chip in enumerate(chips):
            copy(4 + j, (*chip, 1 - c), me).wait_recv()
        for cp in first + passed:
            cp.wait_send()
        mine.wait()

    vmem = pl.BlockSpec(memory_space=pltpu.VMEM)
    dma = pltpu.SemaphoreType.DMA
    return pl.pallas_call(
        body, name="gather_rows", in_specs=[vmem], out_specs=vmem, out_shape=_sds((8 * m_per, n), block.dtype),
        scratch_shapes=[dma((7,)), dma((7,)), dma(())])(block)


def _row_tile(rows, cols):
    return _tile(rows, max(16, (1 << 20) // cols), 16)


def _sum_halves(grad, got, c_arr):
    _, hr, cols = got.shape
    tr = _row_tile(hr, cols)
    nb = hr // tr

    def body(c_ref, a_ref, b_ref, o_ref):
        o_ref[...] = (a_ref[...].astype(F32) + b_ref[...].astype(F32)).astype(BF16)

    blk = pl.BlockSpec((None, tr, cols), lambda s, i, c_ref: (s, i, 0))
    return pl.pallas_call(
        body, name="sum_halves",
        grid_spec=pltpu.PrefetchScalarGridSpec(
            num_scalar_prefetch=1, grid=(N_SHARD, nb),
            in_specs=[pl.BlockSpec((None, tr, cols), lambda s, i, c_ref: (s, c_ref[0] * nb + i, 0)), blk],
            out_specs=blk),
        out_shape=_sds(got.shape, BF16), compiler_params=_params(2, 0))(c_arr, grad, got)


def _sum_chips(part, got, sc_arr):
    _, hr, cols = got.shape
    tr = _row_tile(hr, cols)
    nb = hr // tr

    def body(s_ref, a_ref, b_ref, o_ref):
        o_ref[...] = ((a_ref[...].astype(F32) + b_ref[0].astype(F32)) + b_ref[1].astype(F32)) + b_ref[2].astype(F32)

    return pl.pallas_call(
        body, name="sum_chips",
        grid_spec=pltpu.PrefetchScalarGridSpec(
            num_scalar_prefetch=1, grid=(nb,),
            in_specs=[pl.BlockSpec((None, tr, cols), lambda i, s_ref: (s_ref[0], i, 0)),
                      pl.BlockSpec((3, tr, cols), lambda i, s_ref: (0, i, 0))],
            out_specs=pl.BlockSpec((tr, cols), lambda i, s_ref: (s_ref[1] * nb + i, 0))),
        out_shape=_sds((2 * hr, cols), F32), compiler_params=_params(1, 0))(sc_arr, part, got)


def _cast_into_slot(x2d, sc_arr):
    rows, cols = x2d.shape
    tr = _row_tile(rows, cols)

    def body(s_ref, x_ref, o_ref):
        o_ref[...] = x_ref[...].astype(BF16)

    return pl.pallas_call(
        body, name="cast_into_slot",
        grid_spec=pltpu.PrefetchScalarGridSpec(
            num_scalar_prefetch=1, grid=(rows // tr,),
            in_specs=[pl.BlockSpec((tr, cols), lambda i, s_ref: (i, 0))],
            out_specs=pl.BlockSpec((None, tr, cols), lambda i, s_ref: (s_ref[0], i, 0))),
        out_shape=_sds((N_SHARD, rows, cols), BF16), compiler_params=_params(1, 0))(sc_arr, x2d)


def _adam_math(w, g, m, v):
    m = ADAM_B1 * m + (1.0 - ADAM_B1) * g
    v = ADAM_B2 * v + (1.0 - ADAM_B2) * (g * g)
    m_hat = m / (1.0 - ADAM_B1 ** ADAM_STEP)
    v_hat = v / (1.0 - ADAM_B2 ** ADAM_STEP)
    delta = -ADAM_LR * (m_hat / (jnp.sqrt(v_hat) + ADAM_EPS) + ADAM_WD * w)
    return delta, m, v


def _adamw(w, g, m, v):
    rows, cols = w.shape
    tr = _tile(rows, max(8, (3 << 18) // cols), 8)

    def body(w_ref, g_ref, m_ref, v_ref, go_ref, d_ref, mo_ref, vo_ref):
        g_ = g_ref[...]
        go_ref[...] = g_
        d_ref[...], mo_ref[...], vo_ref[...] = _adam_math(w_ref[...], g_, m_ref[...], v_ref[...])

    blk = pl.BlockSpec((tr, cols), lambda i: (i, 0))
    return pl.pallas_call(
        body, name="adamw", grid=(rows // tr,), in_specs=[blk] * 4, out_specs=[blk] * 4,
        out_shape=[_sds((rows, cols), F32)] * 4, compiler_params=_params(1, 0))(w, g, m, v)


def _adamw_small(gathered, w, m, v):
    rows, cols = w.shape

    def body(a_ref, w_ref, m_ref, v_ref, go_ref, d_ref, mo_ref, vo_ref):
        g_ = a_ref[pl.ds(0, rows), :]
        for k in range(1, 8):
            g_ = g_ + a_ref[pl.ds(k * rows, rows), :]
        go_ref[...] = g_
        d_ref[...], mo_ref[...], vo_ref[...] = _adam_math(w_ref[...], g_, m_ref[...], v_ref[...])

    vmem = pl.BlockSpec(memory_space=pltpu.VMEM)
    return pl.pallas_call(
        body, name="adamw_small", in_specs=[vmem] * 4, out_specs=[vmem] * 4,
        out_shape=[_sds((rows, cols), F32)] * 4)(gathered, w, m, v)


BIG = ("ffn1_w_in", "ffn1_w_out", "mix_w_in", "w_branch_a", "w_branch_b", "mix_w_out", "ffn2_w_in", "ffn2_w_out")
BIG_KEY = {"ffn1_w_in": "ffn1_in", "ffn1_w_out": "ffn1_out", "mix_w_in": "mix_in", "w_branch_a": "wa",
           "w_branch_b": "wb", "mix_w_out": "mix_out", "ffn2_w_in": "ffn2_in", "ffn2_w_out": "ffn2_out"}
SMALL = ("ln1_g", "ln1_b", "hgrn_lb_fwd", "hgrn_lb_bwd", "hgrn_norm_g", "ln2_g", "ln2_b", "ln3_g", "ln3_b")
ORDER = ("ffn1_w_in", "ffn1_w_out", "ln1_g", "ln1_b", "mix_w_in", "hgrn_lb_fwd", "hgrn_lb_bwd", "hgrn_norm_g",
         "w_branch_a", "w_branch_b", "mix_w_out", "ln2_g", "ln2_b", "ffn2_w_in", "ffn2_w_out", "ln3_g", "ln3_b")
SMALL_ROWS = 16


def _pack_small(d):
    rows = jnp.concatenate([d[k].reshape(-1, d[k].shape[-1]) for k in SMALL], axis=0)
    return jnp.pad(rows, ((0, SMALL_ROWS - rows.shape[0]), (0, 0)))


def _unpack_small(a, like):
    out, r = {}, 0
    for k in SMALL:
        n = like[k].shape[0]
        out[k] = a[r:r + n].reshape(like[k].shape)
        r += n
    return out


def kernel(x, ffn1_w_in, ffn1_w_out, ln1_g, ln1_b, mix_w_in, hgrn_lb_fwd, hgrn_lb_bwd, hgrn_norm_g, w_branch_a, w_branch_b, mix_w_out, ln2_g, ln2_b, ffn2_w_in, ffn2_w_out, ln3_g, ln3_b, loss_target, m_ffn1_w_in, m_ffn1_w_out, m_ln1_g, m_ln1_b, m_mix_w_in, m_hgrn_lb_fwd, m_hgrn_lb_bwd, m_hgrn_norm_g, m_w_branch_a, m_w_branch_b, m_mix_w_out, m_ln2_g, m_ln2_b, m_ffn2_w_in, m_ffn2_w_out, m_ln3_g, m_ln3_b, v_ffn1_w_in, v_ffn1_w_out, v_ln1_g, v_ln1_b, v_mix_w_in, v_hgrn_lb_fwd, v_hgrn_lb_bwd, v_hgrn_norm_g, v_w_branch_a, v_w_branch_b, v_mix_w_out, v_ln2_g, v_ln2_b, v_ffn2_w_in, v_ffn2_w_out, v_ln3_g, v_ln3_b):
    wts = dict(ffn1_w_in=ffn1_w_in, ffn1_w_out=ffn1_w_out, ln1_g=ln1_g, ln1_b=ln1_b, mix_w_in=mix_w_in,
               hgrn_lb_fwd=hgrn_lb_fwd, hgrn_lb_bwd=hgrn_lb_bwd, hgrn_norm_g=hgrn_norm_g, w_branch_a=w_branch_a,
               w_branch_b=w_branch_b, mix_w_out=mix_w_out, ln2_g=ln2_g, ln2_b=ln2_b, ffn2_w_in=ffn2_w_in,
               ffn2_w_out=ffn2_w_out, ln3_g=ln3_g, ln3_b=ln3_b)
    mom = dict(ffn1_w_in=m_ffn1_w_in, ffn1_w_out=m_ffn1_w_out, ln1_g=m_ln1_g, ln1_b=m_ln1_b, mix_w_in=m_mix_w_in,
               hgrn_lb_fwd=m_hgrn_lb_fwd, hgrn_lb_bwd=m_hgrn_lb_bwd, hgrn_norm_g=m_hgrn_norm_g,
               w_branch_a=m_w_branch_a, w_branch_b=m_w_branch_b, mix_w_out=m_mix_w_out, ln2_g=m_ln2_g, ln2_b=m_ln2_b,
               ffn2_w_in=m_ffn2_w_in, ffn2_w_out=m_ffn2_w_out, ln3_g=m_ln3_g, ln3_b=m_ln3_b)
    var = dict(ffn1_w_in=v_ffn1_w_in, ffn1_w_out=v_ffn1_w_out, ln1_g=v_ln1_g, ln1_b=v_ln1_b, mix_w_in=v_mix_w_in,
               hgrn_lb_fwd=v_hgrn_lb_fwd, hgrn_lb_bwd=v_hgrn_lb_bwd, hgrn_norm_g=v_hgrn_norm_g,
               w_branch_a=v_w_branch_a, w_branch_b=v_w_branch_b, mix_w_out=v_mix_w_out, ln2_g=v_ln2_g, ln2_b=v_ln2_b,
               ffn2_w_in=v_ffn2_w_in, ffn2_w_out=v_ffn2_w_out, ln3_g=v_ln3_g, ln3_b=v_ln3_b)
    shard = (2 * lax.axis_index("x") + lax.axis_index("y")).astype(jnp.int32)
    sc_arr = jnp.stack([shard, lax.axis_index("c").astype(jnp.int32)])

    shard2d = {k: wts[k].reshape(wts[k].shape[1:]) for k in BIG}
    fulls = {BIG_KEY[k]: _cast_into_slot(shard2d[k], sc_arr) for k in BIG}
    p = {k: wts[k] for k in SMALL}

    loss, grad_x, grads, gp = _local_step(x, loss_target, fulls, p, sc_arr)
    loss = lax.psum(loss[0, 0], ("x", "y", "c"))

    out_g, out_d, out_m, out_v = {}, {}, {}, {}
    for k in BIG:
        g = grads[BIG_KEY[k]]
        shp = wts[k].shape
        res = _adamw(shard2d[k], g, mom[k].reshape(shp[1:]), var[k].reshape(shp[1:]))
        out_g[k], out_d[k], out_m[k], out_v[k] = [a.reshape(shp) for a in res]

    gathered = _gather_rows(_pack_small(gp))
    res = _adamw_small(gathered, _pack_small(wts), _pack_small(mom), _pack_small(var))
    for dst, a in zip((out_g, out_d, out_m, out_v), res):
        dst.update(_unpack_small(a, wts))

    return (loss, grad_x, *[out_g[k] for k in ORDER], *[out_d[k] for k in ORDER],
            *[out_m[k] for k in ORDER], *[out_v[k] for k in ORDER])
```
